```python
import numpy as np
import jax
import jax.numpy as jnp
from jax import lax

D_MODEL = 1024
BATCH = 8
SEQ = 8192
DEPTH = 1

RET_HEADS = 4
RET_QK_DIM = 128
RET_V_DIM = 256
MLSTM_HEADS = 4
MLSTM_QK_DIM = 128
MLSTM_V_DIM = 256
CONV_WIDTH = 4
CHUNK = 128
ROPE_BASE = 10000.0
N_GROUPS = 4
EXPERTS_PER_GROUP = 8
N_EXPERTS = N_GROUPS * EXPERTS_PER_GROUP
TOP_K = 2
D_EXPERT = 512
MOE_BLOCK = 128
NORM_EPS = 1e-6

RET_QK = RET_HEADS * RET_QK_DIM
RET_V = RET_HEADS * RET_V_DIM
M_QK = MLSTM_HEADS * MLSTM_QK_DIM
M_V = MLSTM_HEADS * MLSTM_V_DIM
IN_SPLITS = (RET_QK, RET_QK, RET_V, RET_V, M_QK, M_QK, M_V, M_V, MLSTM_HEADS, MLSTM_HEADS, D_MODEL, D_MODEL)
D_IN = RET_QK * 2 + RET_V * 2 + M_QK * 2 + M_V * 2 + MLSTM_HEADS * 2 + D_MODEL * 2

kernel_name = "hybrid_retention_mlstm_hmoe"


def split_points(widths):
    pts, acc = [], 0
    for w in widths[:-1]:
        acc += w
        pts.append(acc)
    return pts


def rms_norm(x, w):
    xf = x.astype(jnp.float32)
    y = xf * lax.rsqrt(jnp.mean(xf * xf, axis=-1, keepdims=True) + NORM_EPS)
    return (y * w.astype(jnp.float32)).astype(x.dtype)


def head_rms_norm(xh):
    B, S, H, d = xh.shape
    y = xh * lax.rsqrt(jnp.mean(xh * xh, axis=-1, keepdims=True) + NORM_EPS)
    return y.reshape(B, S, H * d)


def rotary(x, positions):
    d = x.shape[-1]
    inv_freq = 1.0 / (ROPE_BASE ** (jnp.arange(0, d, 2, dtype=jnp.float32) / d))
    ang = positions.astype(jnp.float32)[:, None] * inv_freq[None, :]
    cos = jnp.cos(ang)[None, :, None, :]
    sin = jnp.sin(ang)[None, :, None, :]
    x1, x2 = jnp.split(x, 2, axis=-1)
    return jnp.concatenate([x1 * cos - x2 * sin, x2 * cos + x1 * sin], axis=-1)


def to_chunks(t):
    B, S, H, d = t.shape
    return t.reshape(B, S // CHUNK, CHUNK, H, d).transpose(1, 0, 3, 2, 4)


def from_chunks(t):
    nc, B, H, L, d = t.shape
    return t.transpose(1, 0, 3, 2, 4).reshape(B, nc * L, H, d)


def retention_chunkwise(q, k, v, log_gamma):
    B, S, H, dk = q.shape
    dv = v.shape[-1]
    idx = jnp.arange(CHUNK, dtype=jnp.float32)
    rel = idx[:, None] - idx[None, :]
    causal = rel >= 0
    decay_mask = jnp.where(causal[None], jnp.exp(log_gamma[:, None, None] * jnp.where(causal, rel, 0.0)[None]), 0.0)
    q_decay = jnp.exp(log_gamma[:, None] * (idx + 1.0)[None, :])
    k_decay = jnp.exp(log_gamma[:, None] * (CHUNK - 1.0 - idx)[None, :])
    chunk_decay = jnp.exp(log_gamma * CHUNK)

    def step(R, inp):
        qi, ki, vi = inp
        s = jnp.einsum('bhld,bhmd->bhlm', qi, ki) * decay_mask[None]
        o = (jnp.einsum('bhlm,bhme->bhle', s, vi)
             + jnp.einsum('bhld,bhde->bhle', qi, R) * q_decay[None, :, :, None])
        R = (R * chunk_decay[None, :, None, None]
             + jnp.einsum('bhld,bhle->bhde', ki * k_decay[None, :, :, None], vi))
        return R, o

    R0 = jnp.zeros((B, H, dk, dv), jnp.float32)
    _, o = lax.scan(step, R0, (to_chunks(q), to_chunks(k), to_chunks(v)))
    return from_chunks(o)


def mlstm_chunkwise(q, k, v, i_pre, f_pre):
    B, S, H, dk = q.shape
    dv = v.shape[-1]
    nc = S // CHUNK
    log_f = jax.nn.log_sigmoid(f_pre)
    gate_chunks = lambda g: g.reshape(B, nc, CHUNK, H).transpose(1, 0, 3, 2)
    causal = jnp.tril(jnp.ones((CHUNK, CHUNK), dtype=bool))

    def step(carry, inp):
        C, n, m = carry
        qi, ki, vi, ii, lfi = inp
        b = jnp.cumsum(lfi, axis=-1)
        a = b + m[..., None]
        dmat = jnp.where(causal, b[..., :, None] - b[..., None, :] + ii[..., None, :], -jnp.inf)
        m_t = jnp.maximum(a, jnp.max(dmat, axis=-1))
        w_inter = jnp.exp(a - m_t)
        s = jnp.einsum('bhld,bhmd->bhlm', qi, ki) * jnp.exp(dmat - m_t[..., None])
        num = jnp.einsum('bhlm,bhme->bhle', s, vi) + jnp.einsum('bhld,bhde->bhle', qi, C) * w_inter[..., None]
        den = jnp.sum(s, axis=-1) + jnp.einsum('bhld,bhd->bhl', qi, n) * w_inter
        h = num / jnp.maximum(jnp.abs(den), jnp.exp(-m_t))[..., None]
        b_last = b[..., -1]
        g = b_last[..., None] - b + ii
        m_new = jnp.maximum(b_last + m, jnp.max(g, axis=-1))
        keep = jnp.exp(b_last + m - m_new)
        wk = ki * jnp.exp(g - m_new[..., None])[..., None]
        C = C * keep[..., None, None] + jnp.einsum('bhld,bhle->bhde', wk, vi)
        n = n * keep[..., None] + jnp.sum(wk, axis=-2)
        return (C, n, m_new), h

    carry0 = (jnp.zeros((B, H, dk, dv), jnp.float32), jnp.zeros((B, H, dk), jnp.float32),
              jnp.zeros((B, H), jnp.float32))
    _, h = lax.scan(step, carry0, (to_chunks(q), to_chunks(k), to_chunks(v), gate_chunks(i_pre), gate_chunks(log_f)))
    return from_chunks(h)


def causal_depthwise_conv(u, w, b):
    y = lax.conv_general_dilated(u, w.astype(u.dtype), window_strides=(1,), padding=[(w.shape[0] - 1, 0)],
                                 dimension_numbers=('NWC', 'WIO', 'NWC'), feature_group_count=u.shape[-1])
    return y + b.astype(u.dtype)


def hybrid_mixer(h, positions, w_in, ret_gn_w, w_ret_branch, conv_w, conv_b, b_i, b_f, mlstm_gn_w,
                 w_mlstm_branch, w_out):
    B, S, _ = h.shape
    f32 = jnp.float32
    proj = h @ w_in
    (r_q, r_k, r_v, r_g, m_q, m_k, m_v, m_o, m_i, m_f, gate_r, gate_m) = jnp.split(proj, split_points(IN_SPLITS), axis=-1)

    log_gamma = jnp.log(1.0 - 2.0 ** (-5.0 - jnp.arange(RET_HEADS, dtype=f32)))
    rq = rotary(r_q.reshape(B, S, RET_HEADS, RET_QK_DIM).astype(f32), positions)
    rk = rotary(r_k.reshape(B, S, RET_HEADS, RET_QK_DIM).astype(f32), positions) * (RET_QK_DIM ** -0.5)
    rv = r_v.reshape(B, S, RET_HEADS, RET_V_DIM).astype(f32)
    ret = head_rms_norm(retention_chunkwise(rq, rk, rv, log_gamma))
    ret = ret * ret_gn_w.astype(f32) * jax.nn.silu(r_g.astype(f32))
    y_ret = ret.astype(h.dtype) @ w_ret_branch

    qk = jax.nn.silu(causal_depthwise_conv(jnp.concatenate([m_q, m_k], axis=-1), conv_w, conv_b))
    mq, mk = jnp.split(qk, 2, axis=-1)
    mq = mq.reshape(B, S, MLSTM_HEADS, MLSTM_QK_DIM).astype(f32)
    mk = mk.reshape(B, S, MLSTM_HEADS, MLSTM_QK_DIM).astype(f32) * (MLSTM_QK_DIM ** -0.5)
    mv = m_v.reshape(B, S, MLSTM_HEADS, MLSTM_V_DIM).astype(f32)
    i_pre = m_i.astype(f32) + b_i.astype(f32)
    f_pre = m_f.astype(f32) + b_f.astype(f32)
    hm = mlstm_chunkwise(mq, mk, mv, i_pre, f_pre).reshape(B, S, M_V) * jax.nn.sigmoid(m_o.astype(f32))
    hm = head_rms_norm(hm.reshape(B, S, MLSTM_HEADS, MLSTM_V_DIM)) * mlstm_gn_w.astype(f32)
    y_m = hm.astype(h.dtype) @ w_mlstm_branch

    merged = jax.nn.sigmoid(gate_r) * y_ret + jax.nn.sigmoid(gate_m) * y_m
    return merged @ w_out


def hierarchical_moe(h, w_group, b_group, w_expert_router, b_expert_router, w_gate, w_up, w_down):
    B, S, D = h.shape
    T = B * S
    xt = h.reshape(T, D)
    xf = xt.astype(jnp.float32)
    group_prob = jax.nn.softmax(xf @ w_group.astype(jnp.float32) + b_group.astype(jnp.float32), axis=-1)
    g_w, g_idx = lax.top_k(group_prob, 1)
    e_logits = (xf @ w_expert_router.astype(jnp.float32) + b_expert_router.astype(jnp.float32)).reshape(T, N_GROUPS, EXPERTS_PER_GROUP)
    sel_logits = e_logits[jnp.arange(T), g_idx[:, 0]]
    top_logits, e_idx = lax.top_k(sel_logits, TOP_K)
    e_w = jax.nn.softmax(top_logits, axis=-1)
    weights = g_w * e_w
    expert_id = g_idx * EXPERTS_PER_GROUP + e_idx

    A = T * TOP_K
    P = A + N_EXPERTS * MOE_BLOCK
    n_blocks = P // MOE_BLOCK
    flat_e = expert_id.reshape(A).astype(jnp.int32)
    flat_tok = jnp.repeat(jnp.arange(T, dtype=jnp.int32), TOP_K)
    flat_w = weights.reshape(A)
    order = jnp.argsort(flat_e)
    se = flat_e[order]
    counts = jnp.bincount(flat_e, length=N_EXPERTS)
    padded = ((counts + MOE_BLOCK - 1) // MOE_BLOCK) * MOE_BLOCK
    start = jnp.cumsum(counts) - counts
    pend = jnp.cumsum(padded)
    pstart = pend - padded
    dest = pstart[se] + (jnp.arange(A, dtype=jnp.int32) - start[se])
    slot_tok = jnp.full((P,), T, dtype=jnp.int32).at[dest].set(flat_tok[order])
    slot_w = jnp.zeros((P,), jnp.float32).at[dest].set(flat_w[order])
    blk_start = jnp.arange(n_blocks, dtype=jnp.int32) * MOE_BLOCK
    blk_e = jnp.minimum(jnp.sum(blk_start[:, None] >= pend[None, :], axis=-1), N_EXPERTS - 1)
    x_pad = jnp.concatenate([xt, jnp.zeros((1, D), xt.dtype)], axis=0)
    xs = x_pad[slot_tok].reshape(n_blocks, MOE_BLOCK, D)

    def expert_block(args):
        xb, e = args
        return (jax.nn.silu(xb @ w_gate[e]) * (xb @ w_up[e])) @ w_down[e]

    ys = lax.map(expert_block, (xs, blk_e)).reshape(P, D)
    ys = ys * slot_w[:, None].astype(ys.dtype)
    out = jax.ops.segment_sum(ys, slot_tok, num_segments=T + 1)[:T]
    return out.reshape(B, S, D)


def setup_inputs(seed: int = 0) -> dict:
    key = jax.random.key(seed)
    ks = jax.random.split(key, 24)
    L, D = DEPTH, D_MODEL

    def nrm(k, shape, scale):
        return jax.random.normal(k, shape, jnp.float32) * scale

    return {
        "x": nrm(ks[0], (BATCH, SEQ, D), 1.0),
        "norm_mix_w": 1.0 + nrm(ks[1], (L, D), 0.02),
        "w_in": nrm(ks[2], (L, D, D_IN), D ** -0.5),
        "ret_gn_w": 1.0 + nrm(ks[3], (L, RET_V), 0.02),
        "w_ret_branch": nrm(ks[4], (L, RET_V, D), RET_V ** -0.5),
        "mlstm_conv_w": nrm(ks[5], (L, CONV_WIDTH, 1, 2 * M_QK), CONV_WIDTH ** -0.5),
        "mlstm_conv_b": nrm(ks[6], (L, 2 * M_QK), 0.01),
        "b_igate": nrm(ks[7], (L, MLSTM_HEADS), 0.1),
        "b_fgate": jnp.linspace(3.0, 6.0, MLSTM_HEADS, dtype=jnp.float32)[None, :] + nrm(ks[8], (L, MLSTM_HEADS), 0.1),
        "mlstm_gn_w": 1.0 + nrm(ks[9], (L, M_V), 0.02),
        "w_mlstm_branch": nrm(ks[10], (L, M_V, D), M_V ** -0.5),
        "w_out": nrm(ks[11], (L, D, D), D ** -0.5),
        "norm_ffn_w": 1.0 + nrm(ks[12], (L, D), 0.02),
        "w_group": nrm(ks[13], (L, D, N_GROUPS), D ** -0.5),
        "b_group": nrm(ks[14], (L, N_GROUPS), 0.01),
        "w_expert_router": nrm(ks[15], (L, D, N_EXPERTS), D ** -0.5),
        "b_expert_router": nrm(ks[16], (L, N_EXPERTS), 0.01),
        "w_gate": nrm(ks[17], (L, N_EXPERTS, D, D_EXPERT), D ** -0.5),
        "w_up": nrm(ks[18], (L, N_EXPERTS, D, D_EXPERT), D ** -0.5),
        "w_down": nrm(ks[19], (L, N_EXPERTS, D_EXPERT, D), D_EXPERT ** -0.5),
        "norm_final_w": 1.0 + nrm(ks[20], (D,), 0.02),
    }


def reference(x, norm_mix_w, w_in, ret_gn_w, w_ret_branch, mlstm_conv_w, mlstm_conv_b, b_igate, b_fgate,
              mlstm_gn_w, w_mlstm_branch, w_out, norm_ffn_w, w_group, b_group, w_expert_router,
              b_expert_router, w_gate, w_up, w_down, norm_final_w):
    positions = jnp.arange(x.shape[1], dtype=jnp.int32)
    for layer in range(DEPTH):
        h = rms_norm(x, norm_mix_w[layer])
        x = x + hybrid_mixer(h, positions, w_in[layer], ret_gn_w[layer], w_ret_branch[layer], mlstm_conv_w[layer],
                             mlstm_conv_b[layer], b_igate[layer], b_fgate[layer], mlstm_gn_w[layer],
                             w_mlstm_branch[layer], w_out[layer])
        h = rms_norm(x, norm_ffn_w[layer])
        x = x + hierarchical_moe(h, w_group[layer], b_group[layer], w_expert_router[layer], b_expert_router[layer],
                                 w_gate[layer], w_up[layer], w_down[layer])
    return rms_norm(x, norm_final_w)
```

```python
import functools

import numpy as np
import jax
import jax.numpy as jnp
from jax import lax
from jax.experimental import pallas as pl
from jax.experimental.pallas import tpu as pltpu

F32 = jnp.float32
BF16 = jnp.bfloat16
U32 = jnp.uint32

D_MODEL = 1024
N_HEADS = 4
QK_DIM = 128
V_DIM = 256
CHUNK = 128
CONV_WIDTH = 4
ROPE_BASE = 10000.0
N_GROUPS = 4
EXPERTS_PER_GROUP = 8
N_EXPERTS = N_GROUPS * EXPERTS_PER_GROUP
D_EXPERT = 512
NORM_EPS = 1e-6
QK_ALL = N_HEADS * QK_DIM
V_ALL = N_HEADS * V_DIM

MIX_COLS = 2 * QK_ALL + 2 * V_ALL
N_BIG = 2 * MIX_COLS + 2 * D_MODEL
LANES = 128
PACKED = D_MODEL // 2

ROWS_PROJ = 1024
COLS_PROJ = 1024
CHUNKS_PER_STEP = 2
ROWS_MERGE = 512
ROWS_DISPATCH = 2048
ROWS_COMBINE = 512
MOE_ROWS = 512
VMEM_LIMIT = 56 * 1024 * 1024


def _rms(x, eps=NORM_EPS):
    return x * lax.rsqrt(jnp.mean(x * x, axis=-1, keepdims=True) + eps)


def _pack_rows(x):
    lo = lax.bitcast_convert_type(x[:, :PACKED].astype(BF16).astype(F32), U32)
    hi = lax.bitcast_convert_type(x[:, PACKED:].astype(BF16).astype(F32), U32)
    return (hi & jnp.uint32(0xFFFF0000)) | (lo >> 16)


def _unpack_rows(w):
    lo = lax.bitcast_convert_type(w << 16, F32)
    hi = lax.bitcast_convert_type(w & jnp.uint32(0xFFFF0000), F32)
    return jnp.concatenate([lo, hi], axis=1)


def _in_proj_kernel(x_ref, nw_ref, w_ref, wif_ref, o_ref, gates_ref, h_scr):
    @pl.when(pl.program_id(1) == 0)
    def _():
        h = (_rms(x_ref[...]) * nw_ref[...]).astype(BF16)
        h_scr[...] = h
        gates_ref[...] = jnp.dot(h, wif_ref[...], preferred_element_type=F32)

    o_ref[...] = jnp.dot(h_scr[...], w_ref[...], preferred_element_type=F32).astype(o_ref.dtype)


def _in_projection(x2d, norm_w, w_big, w_if):
    t = x2d.shape[0]
    tm = min(ROWS_PROJ, t)
    tn = COLS_PROJ
    return pl.pallas_call(
        _in_proj_kernel,
        grid=(t // tm, N_BIG // tn),
        in_specs=[
            pl.BlockSpec((tm, D_MODEL), lambda i, j: (i, 0)),
            pl.BlockSpec((1, D_MODEL), lambda i, j: (0, 0)),
            pl.BlockSpec((D_MODEL, tn), lambda i, j: (0, j)),
            pl.BlockSpec((D_MODEL, LANES), lambda i, j: (0, 0)),
        ],
        out_specs=[
            pl.BlockSpec((tm, tn), lambda i, j: (i, j)),
            pl.BlockSpec((tm, LANES), lambda i, j: (i, 0)),
        ],
        out_shape=[
            jax.ShapeDtypeStruct((t, N_BIG), BF16),
            jax.ShapeDtypeStruct((t, LANES), F32),
        ],
        scratch_shapes=[pltpu.VMEM((tm, D_MODEL), BF16)],
        compiler_params=pltpu.CompilerParams(
            dimension_semantics=("arbitrary", "arbitrary"), vmem_limit_bytes=VMEM_LIMIT),
        name="in_projection",
    )(x2d, norm_w, w_big, w_if)


def _retention_kernel(p_ref, cos_ref, sin_ref, dq_ref, dk_ref, gn_ref, o_ref, state_scr, *, chunk_decay):
    L = CHUNK

    @pl.when(pl.program_id(1) == 0)
    def _():
        state_scr[...] = jnp.zeros_like(state_scr)

    row = lax.broadcasted_iota(jnp.int32, (L, L), 0)
    col = lax.broadcasted_iota(jnp.int32, (L, L), 1)
    causal = row >= col
    n_chunks = p_ref.shape[0] // L
    for ci in range(n_chunks):
        r0 = ci * L
        cosf = cos_ref[r0:r0 + L, :]
        sinf = sin_ref[r0:r0 + L, :]
        for h in range(N_HEADS):
            q = p_ref[r0:r0 + L, h * QK_DIM:(h + 1) * QK_DIM].astype(F32)
            k = p_ref[r0:r0 + L, QK_ALL + h * QK_DIM:QK_ALL + (h + 1) * QK_DIM].astype(F32)
            v = p_ref[r0:r0 + L, 2 * QK_ALL + h * V_DIM:2 * QK_ALL + (h + 1) * V_DIM]
            g = p_ref[r0:r0 + L, 2 * QK_ALL + V_ALL + h * V_DIM:2 * QK_ALL + V_ALL + (h + 1) * V_DIM].astype(F32)
            qt = ((q * cosf + pltpu.roll(q, QK_DIM // 2, 1) * sinf) * dq_ref[h]).astype(BF16)
            kt = ((k * cosf + pltpu.roll(k, QK_DIM // 2, 1) * sinf) * dk_ref[h]).astype(BF16)
            s = lax.dot_general(qt, kt, (((1,), (1,)), ((), ())), preferred_element_type=F32)
            s = jnp.where(causal, s, 0.0).astype(BF16)
            state = state_scr[h]
            lhs = jnp.concatenate([s, qt], axis=1)
            rhs = jnp.concatenate([v, state.astype(BF16)], axis=0)
            o = jnp.dot(lhs, rhs, preferred_element_type=F32)
            kv = lax.dot_general(kt, v, (((0,), (0,)), ((), ())), preferred_element_type=F32)
            state_scr[h] = (state + kv) * chunk_decay[h]
            y = _rms(o) * gn_ref[:, h * V_DIM:(h + 1) * V_DIM] * (g * jax.nn.sigmoid(g))
            o_ref[r0:r0 + L, h * V_DIM:(h + 1) * V_DIM] = y.astype(o_ref.dtype)


def _retention(proj3, cosf, sinf, dq, dk, gn_w, chunk_decay):
    b, s, _ = proj3.shape
    lb = CHUNK * min(CHUNKS_PER_STEP, s // CHUNK)
    return pl.pallas_call(
        functools.partial(_retention_kernel, chunk_decay=chunk_decay),
        grid=(b, s // lb),
        in_specs=[
            pl.BlockSpec((None, lb, MIX_COLS), lambda i, c: (i, c, 0)),
            pl.BlockSpec((lb, QK_DIM), lambda i, c: (c, 0)),
            pl.BlockSpec((lb, QK_DIM), lambda i, c: (c, 0)),
            pl.BlockSpec((N_HEADS, CHUNK, QK_DIM), lambda i, c: (0, 0, 0)),
            pl.BlockSpec((N_HEADS, CHUNK, QK_DIM), lambda i, c: (0, 0, 0)),
            pl.BlockSpec((1, V_ALL), lambda i, c: (0, 0)),
        ],
        out_specs=pl.BlockSpec((None, lb, V_ALL), lambda i, c: (i, c, 0)),
        out_shape=jax.ShapeDtypeStruct((b, s, V_ALL), BF16),
        scratch_shapes=[pltpu.VMEM((N_HEADS, QK_DIM, V_DIM), F32)],
        compiler_params=pltpu.CompilerParams(
            dimension_semantics=("arbitrary", "arbitrary"), vmem_limit_bytes=VMEM_LIMIT),
        name="retention",
    )(proj3, cosf, sinf, dq, dk, gn_w)


def _mlstm_kernel(p_ref, gates_ref, cw_ref, cb_ref, gb_ref, gn_ref, tril_ref, o_ref,
                  c_scr, n_scr, m_scr, tail_scr, act_scr):
    L = CHUNK
    lb = p_ref.shape[0]

    @pl.when(pl.program_id(1) == 0)
    def _():
        c_scr[...] = jnp.zeros_like(c_scr)
        n_scr[...] = jnp.zeros_like(n_scr)
        m_scr[...] = jnp.zeros_like(m_scr)
        tail_scr[...] = jnp.zeros_like(tail_scr)

    u = p_ref[:, 0:2 * QK_ALL].astype(F32)
    ext = jnp.concatenate([tail_scr[...], u], axis=0)
    acc = u * cw_ref[CONV_WIDTH - 1:CONV_WIDTH, :] + cb_ref[...]
    for d in range(1, CONV_WIDTH):
        acc = acc + ext[8 - d:8 - d + lb, :] * cw_ref[CONV_WIDTH - 1 - d:CONV_WIDTH - d, :]
    tail_scr[...] = u[lb - 8:lb, :]
    act_scr[...] = acc * jax.nn.sigmoid(acc)

    row = lax.broadcasted_iota(jnp.int32, (L, L), 0)
    col = lax.broadcasted_iota(jnp.int32, (L, L), 1)
    causal = row >= col
    k_scale = QK_DIM ** -0.5
    for ci in range(lb // L):
        r0 = ci * L
        pre = gates_ref[r0:r0 + L, :] + gb_ref[...]
        log_f = jnp.minimum(pre, 0.0) - jnp.log1p(jnp.exp(-jnp.abs(pre)))
        bcum = jnp.dot(tril_ref[...], log_f, preferred_element_type=F32, precision=lax.Precision.HIGHEST)
        for h in range(N_HEADS):
            q = act_scr[r0:r0 + L, h * QK_DIM:(h + 1) * QK_DIM]
            k = act_scr[r0:r0 + L, QK_ALL + h * QK_DIM:QK_ALL + (h + 1) * QK_DIM] * k_scale
            v = p_ref[r0:r0 + L, 2 * QK_ALL + h * V_DIM:2 * QK_ALL + (h + 1) * V_DIM]
            og = p_ref[r0:r0 + L, 2 * QK_ALL + V_ALL + h * V_DIM:2 * QK_ALL + V_ALL + (h + 1) * V_DIM].astype(F32)
            b_h = bcum[:, N_HEADS + h:N_HEADS + h + 1]
            i_h = pre[:, h:h + 1]
            m_prev = m_scr[h:h + 1, 0:1]
            n_prev = n_scr[h:h + 1, :]
            c_prev = c_scr[h]

            a = b_h + m_prev
            src = jnp.transpose(jnp.broadcast_to(i_h - b_h, (L, L)))
            dmat = jnp.where(causal, b_h + src, -jnp.inf)
            m_t = jnp.maximum(a, jnp.max(dmat, axis=-1, keepdims=True))
            w_inter = jnp.exp(a - m_t)
            qb = q.astype(BF16)
            kb = k.astype(BF16)
            s = lax.dot_general(qb, kb, (((1,), (1,)), ((), ())), preferred_element_type=F32) * jnp.exp(dmat - m_t)
            lhs = jnp.concatenate([s.astype(BF16), (q * w_inter).astype(BF16)], axis=1)
            rhs = jnp.concatenate([v, c_prev.astype(BF16)], axis=0)
            num = jnp.dot(lhs, rhs, preferred_element_type=F32)
            den = (jnp.sum(s, axis=-1, keepdims=True)
                   + jnp.sum(q * n_prev, axis=-1, keepdims=True) * w_inter)
            hh = num / jnp.maximum(jnp.abs(den), jnp.exp(-m_t))

            b_last = b_h[L - 1:L, :]
            gk = b_last - b_h + i_h
            m_new = jnp.maximum(b_last + m_prev, jnp.max(gk, axis=0, keepdims=True))
            keep = jnp.exp(b_last + m_prev - m_new)
            wk = k * jnp.exp(gk - m_new)
            c_scr[h] = c_prev * keep + lax.dot_general(
                wk.astype(BF16), v, (((0,), (0,)), ((), ())), preferred_element_type=F32)
            n_scr[h:h + 1, :] = n_prev * keep + jnp.sum(wk, axis=0, keepdims=True)
            m_scr[h:h + 1, :] = jnp.broadcast_to(m_new, (1, LANES))

            y = _rms(hh * jax.nn.sigmoid(og)) * gn_ref[:, h * V_DIM:(h + 1) * V_DIM]
            o_ref[r0:r0 + L, h * V_DIM:(h + 1) * V_DIM] = y.astype(o_ref.dtype)


def _mlstm(proj3, gates3, conv_w, conv_b, gate_bias, gn_w, tril):
    b, s, _ = proj3.shape
    lb = CHUNK * min(CHUNKS_PER_STEP, s // CHUNK)
    return pl.pallas_call(
        _mlstm_kernel,
        grid=(b, s // lb),
        in_specs=[
            pl.BlockSpec((None, lb, MIX_COLS), lambda i, c: (i, c, 1)),
            pl.BlockSpec((None, lb, LANES), lambda i, c: (i, c, 0)),
            pl.BlockSpec((CONV_WIDTH, 2 * QK_ALL), lambda i, c: (0, 0)),
            pl.BlockSpec((1, 2 * QK_ALL), lambda i, c: (0, 0)),
            pl.BlockSpec((1, LANES), lambda i, c: (0, 0)),
            pl.BlockSpec((1, V_ALL), lambda i, c: (0, 0)),
            pl.BlockSpec((CHUNK, CHUNK), lambda i, c: (0, 0)),
        ],
        out_specs=pl.BlockSpec((None, lb, V_ALL), lambda i, c: (i, c, 0)),
        out_shape=jax.ShapeDtypeStruct((b, s, V_ALL), BF16),
        scratch_shapes=[
            pltpu.VMEM((N_HEADS, QK_DIM, V_DIM), F32),
            pltpu.VMEM((8, QK_DIM), F32),
            pltpu.VMEM((8, LANES), F32),
            pltpu.VMEM((8, 2 * QK_ALL), F32),
            pltpu.VMEM((lb, 2 * QK_ALL), F32),
        ],
        compiler_params=pltpu.CompilerParams(
            dimension_semantics=("arbitrary", "arbitrary"), vmem_limit_bytes=VMEM_LIMIT),
        name="mlstm",
    )(proj3, gates3, conv_w, conv_b, gate_bias, gn_w, tril)


def _merge_route_kernel(ret_ref, hm_ref, gr_ref, gm_ref, x_ref, wr_ref, wm_ref, wo_ref, nw_ref,
                        wrt_ref, brt_ref, lower_ref, x1_ref, h2_ref, route_ref, cnt_ref, carry_scr):
    @pl.when(pl.program_id(0) == 0)
    def _():
        carry_scr[...] = jnp.zeros_like(carry_scr)

    y_ret = jnp.dot(ret_ref[...], wr_ref[...], preferred_element_type=F32)
    y_m = jnp.dot(hm_ref[...], wm_ref[...], preferred_element_type=F32)
    merged = (jax.nn.sigmoid(gr_ref[...].astype(F32)) * y_ret
              + jax.nn.sigmoid(gm_ref[...].astype(F32)) * y_m)
    x1 = x_ref[...] + jnp.dot(merged.astype(BF16), wo_ref[...], preferred_element_type=F32)
    x1_ref[...] = x1
    h2 = _rms(x1) * nw_ref[...]
    h2_ref[...] = _pack_rows(h2)

    logits = jnp.dot(h2.astype(BF16), wrt_ref[...], preferred_element_type=F32) + brt_ref[...]
    tm = logits.shape[0]
    lane = lax.broadcasted_iota(jnp.int32, (tm, LANES), 1)
    neg = -jnp.inf
    big = jnp.int32(LANES)
    is_group = (lane >= N_EXPERTS) & (lane < N_EXPERTS + N_GROUPS)
    gl = jnp.where(is_group, logits, neg)
    g_max = jnp.max(gl, axis=-1, keepdims=True)
    g_idx = jnp.min(jnp.where(gl == g_max, lane, big), axis=-1, keepdims=True) - N_EXPERTS
    g_w = 1.0 / jnp.sum(jnp.exp(gl - g_max), axis=-1, keepdims=True)
    in_group = (lane >= g_idx * EXPERTS_PER_GROUP) & (lane < (g_idx + 1) * EXPERTS_PER_GROUP)
    el = jnp.where(in_group, logits, neg)
    l1 = jnp.max(el, axis=-1, keepdims=True)
    e1 = jnp.min(jnp.where(el == l1, lane, big), axis=-1, keepdims=True)
    el2 = jnp.where(lane == e1, neg, el)
    l2 = jnp.max(el2, axis=-1, keepdims=True)
    e2 = jnp.min(jnp.where(el2 == l2, lane, big), axis=-1, keepdims=True)
    t21 = jnp.exp(l2 - l1)
    w1 = g_w / (1.0 + t21)
    w2 = g_w * t21 / (1.0 + t21)

    hit1 = lane == e1
    hit2 = lane == e2
    cnt = jnp.where(hit1 | hit2, 1.0, 0.0)
    before = jnp.dot(lower_ref[...], cnt.astype(BF16), preferred_element_type=F32) + carry_scr[...]
    r1 = jnp.sum(jnp.where(hit1, before, 0.0), axis=-1, keepdims=True)
    r2 = jnp.sum(jnp.where(hit2, before, 0.0), axis=-1, keepdims=True)
    carry = carry_scr[...] + jnp.sum(cnt, axis=0, keepdims=True)
    carry_scr[...] = carry
    cnt_ref[...] = carry

    fields = (e1.astype(F32), e2.astype(F32), r1, r2, w1, w2)
    packed = jnp.zeros((tm, LANES), F32)
    for idx, val in enumerate(fields):
        packed = jnp.where(lane == idx, val, packed)
    route_ref[...] = packed


def _merge_route(ret, hm, proj, x2d, w_ret, w_m, w_out, norm_w, w_router, b_router, lower):
    t = x2d.shape[0]
    tm = min(ROWS_MERGE, t)
    gate_r_blk = 2 * MIX_COLS // D_MODEL
    row_blk = lambda i: (i, 0)
    const = lambda i: (0, 0)
    return pl.pallas_call(
        _merge_route_kernel,
        grid=(t // tm,),
        in_specs=[
            pl.BlockSpec((tm, V_ALL), row_blk),
            pl.BlockSpec((tm, V_ALL), row_blk),
            pl.BlockSpec((tm, D_MODEL), lambda i: (i, gate_r_blk)),
            pl.BlockSpec((tm, D_MODEL), lambda i: (i, gate_r_blk + 1)),
            pl.BlockSpec((tm, D_MODEL), row_blk),
            pl.BlockSpec((V_ALL, D_MODEL), const),
            pl.BlockSpec((V_ALL, D_MODEL), const),
            pl.BlockSpec((D_MODEL, D_MODEL), const),
            pl.BlockSpec((1, D_MODEL), const),
            pl.BlockSpec((D_MODEL, LANES), const),
            pl.BlockSpec((1, LANES), const),
            pl.BlockSpec((tm, tm), const),
        ],
        out_specs=[
            pl.BlockSpec((tm, D_MODEL), row_blk),
            pl.BlockSpec((tm, PACKED), row_blk),
            pl.BlockSpec((tm, LANES), row_blk),
            pl.BlockSpec((1, LANES), const),
        ],
        out_shape=[
            jax.ShapeDtypeStruct((t, D_MODEL), F32),
            jax.ShapeDtypeStruct((t, PACKED), U32),
            jax.ShapeDtypeStruct((t, LANES), F32),
            jax.ShapeDtypeStruct((1, LANES), F32),
        ],
        scratch_shapes=[pltpu.VMEM((1, LANES), F32)],
        compiler_params=pltpu.CompilerParams(
            dimension_semantics=("arbitrary",), vmem_limit_bytes=VMEM_LIMIT),
        name="merge_route",
    )(ret, hm, proj, proj, x2d, w_ret, w_m, w_out, norm_w, w_router, b_router, lower)


def _row_copy(src_hbm, src_row, dst, dst_row, sem):
    return pltpu.make_async_copy(src_hbm.at[pl.ds(src_row, 1)], dst.at[pl.ds(dst_row, 1)], sem)


def _dispatch_kernel(dest_ref, h2_hbm, xs_in_hbm, xs_hbm, sem):
    del xs_in_hbm
    tm = dest_ref.shape[0] // 2
    base = pl.program_id(0) * tm

    def start(r, carry):
        for slot in range(2):
            _row_copy(h2_hbm, base + r, xs_hbm, dest_ref[2 * r + slot], sem).start()
        return carry

    lax.fori_loop(0, tm, start, 0)

    def wait(r, carry):
        for slot in range(2):
            _row_copy(h2_hbm, 0, xs_hbm, 0, sem).wait()
        return carry

    lax.fori_loop(0, tm, wait, 0)


def _dispatch(dest, h2p, xs_zero):
    t = h2p.shape[0]
    tm = min(ROWS_DISPATCH, t)
    return pl.pallas_call(
        _dispatch_kernel,
        grid=(t // tm,),
        in_specs=[
            pl.BlockSpec((2 * tm,), lambda i: (i,), memory_space=pltpu.SMEM),
            pl.BlockSpec(memory_space=pl.ANY),
            pl.BlockSpec(memory_space=pl.ANY),
        ],
        out_specs=pl.BlockSpec(memory_space=pl.ANY),
        out_shape=jax.ShapeDtypeStruct(xs_zero.shape, xs_zero.dtype),
        scratch_shapes=[pltpu.SemaphoreType.DMA(())],
        input_output_aliases={2: 0},
        compiler_params=pltpu.CompilerParams(dimension_semantics=("arbitrary",)),
        name="dispatch",
    )(dest, h2p, xs_zero)


def _experts_kernel(blk_e_ref, n_used_ref, xs_ref, wgu_ref, wd_ref, ys_ref):
    del blk_e_ref
    i = pl.program_id(0)

    @pl.when(i < n_used_ref[0])
    def _():
        xb = _unpack_rows(xs_ref[...]).astype(BF16)
        gu = jnp.dot(xb, wgu_ref[...], preferred_element_type=F32)
        g = gu[:, :D_EXPERT]
        act = (g * jax.nn.sigmoid(g) * gu[:, D_EXPERT:]).astype(BF16)
        ys_ref[...] = _pack_rows(jnp.dot(act, wd_ref[...], preferred_element_type=F32))

    @pl.when(i >= n_used_ref[0])
    def _():
        ys_ref[...] = jnp.zeros_like(ys_ref)


def _experts(blk_e, n_used, xs, w_gate_up, w_down):
    p = xs.shape[0]
    grid_spec = pltpu.PrefetchScalarGridSpec(
        num_scalar_prefetch=2,
        grid=(p // MOE_ROWS,),
        in_specs=[
            pl.BlockSpec((MOE_ROWS, PACKED), lambda i, be, nu: (i, 0)),
            pl.BlockSpec((None, D_MODEL, 2 * D_EXPERT), lambda i, be, nu: (be[i], 0, 0)),
            pl.BlockSpec((None, D_EXPERT, D_MODEL), lambda i, be, nu: (be[i], 0, 0)),
        ],
        out_specs=pl.BlockSpec((MOE_ROWS, PACKED), lambda i, be, nu: (i, 0)),
    )
    return pl.pallas_call(
        _experts_kernel,
        grid_spec=grid_spec,
        out_shape=jax.ShapeDtypeStruct((p, PACKED), U32),
        compiler_params=pltpu.CompilerParams(
            dimension_semantics=("arbitrary",), vmem_limit_bytes=VMEM_LIMIT),
        name="experts",
    )(blk_e, n_used, xs, w_gate_up, w_down)


def _combine_kernel(dest_ref, route_ref, x1_ref, nw_ref, ys_hbm, o_ref, buf_a, buf_b, sem):
    tm = x1_ref.shape[0]

    def start(r, carry):
        _row_copy(ys_hbm, dest_ref[2 * r], buf_a, r, sem).start()
        _row_copy(ys_hbm, dest_ref[2 * r + 1], buf_b, r, sem).start()
        return carry

    lax.fori_loop(0, tm, start, 0)

    def wait(r, carry):
        _row_copy(ys_hbm, 0, buf_a, 0, sem).wait()
        _row_copy(ys_hbm, 0, buf_b, 0, sem).wait()
        return carry

    lax.fori_loop(0, tm, wait, 0)

    w1 = route_ref[:, 4:5]
    w2 = route_ref[:, 5:6]
    x2 = x1_ref[...] + w1 * _unpack_rows(buf_a[...]) + w2 * _unpack_rows(buf_b[...])
    o_ref[...] = _rms(x2) * nw_ref[...]


def _combine(dest, route, x1, norm_w, ys):
    t = x1.shape[0]
    tm = min(ROWS_COMBINE, t)
    return pl.pallas_call(
        _combine_kernel,
        grid=(t // tm,),
        in_specs=[
            pl.BlockSpec((2 * tm,), lambda i: (i,), memory_space=pltpu.SMEM),
            pl.BlockSpec((tm, LANES), lambda i: (i, 0)),
            pl.BlockSpec((tm, D_MODEL), lambda i: (i, 0)),
            pl.BlockSpec((1, D_MODEL), lambda i: (0, 0)),
            pl.BlockSpec(memory_space=pl.ANY),
        ],
        out_specs=pl.BlockSpec((tm, D_MODEL), lambda i: (i, 0)),
        out_shape=jax.ShapeDtypeStruct((t, D_MODEL), F32),
        scratch_shapes=[
            pltpu.VMEM((tm, PACKED), U32),
            pltpu.VMEM((tm, PACKED), U32),
            pltpu.SemaphoreType.DMA(()),
        ],
        compiler_params=pltpu.CompilerParams(
            dimension_semantics=("arbitrary",), vmem_limit_bytes=VMEM_LIMIT),
        name="combine",
    )(dest, route, x1, norm_w, ys)


def _rotary_tables(seq):
    inv_freq = 1.0 / (ROPE_BASE ** (jnp.arange(0, QK_DIM, 2, dtype=F32) / QK_DIM))
    ang = jnp.arange(seq, dtype=F32)[:, None] * inv_freq[None, :]
    cos, sin = jnp.cos(ang), jnp.sin(ang)
    return jnp.concatenate([cos, cos], axis=1), jnp.concatenate([-sin, sin], axis=1)


def _retention_decay_tables():
    gamma = 1.0 - 2.0 ** (-5.0 - np.arange(N_HEADS, dtype=np.float64))
    idx = np.arange(CHUNK, dtype=np.float64) + 1.0
    dq = gamma[:, None] ** idx[None, :]
    dk = gamma[:, None] ** (-idx[None, :]) * QK_DIM ** -0.5
    bcast = lambda a: jnp.asarray(np.broadcast_to(a[:, :, None], (N_HEADS, CHUNK, QK_DIM)), F32)
    return bcast(dq), bcast(dk), tuple(float(g) for g in gamma ** CHUNK)


def kernel(x, norm_mix_w, w_in, ret_gn_w, w_ret_branch, mlstm_conv_w, mlstm_conv_b, b_igate, b_fgate,
           mlstm_gn_w, w_mlstm_branch, w_out, norm_ffn_w, w_group, b_group, w_expert_router,
           b_expert_router, w_gate, w_up, w_down, norm_final_w):
    assert norm_mix_w.shape[0] == 1, "one layer"
    b, s, d = x.shape
    t = b * s
    assert d == D_MODEL and s % CHUNK == 0

    wi = w_in[0]
    n_pre = 2 * MIX_COLS
    w_big = jnp.concatenate([wi[:, :n_pre], wi[:, n_pre + 2 * N_HEADS:]], axis=1).astype(BF16)
    w_if = jnp.pad(wi[:, n_pre:n_pre + 2 * N_HEADS], ((0, 0), (0, LANES - 2 * N_HEADS))).astype(BF16)
    gate_bias = jnp.pad(jnp.concatenate([b_igate[0], b_fgate[0]]), (0, LANES - 2 * N_HEADS))[None, :]
    w_router = jnp.pad(jnp.concatenate([w_expert_router[0], w_group[0]], axis=1),
                       ((0, 0), (0, LANES - N_EXPERTS - N_GROUPS))).astype(BF16)
    b_router = jnp.pad(jnp.concatenate([b_expert_router[0], b_group[0]]),
                       (0, LANES - N_EXPERTS - N_GROUPS))[None, :]
    w_gate_up = jnp.concatenate([w_gate[0], w_up[0]], axis=2).astype(BF16)
    w_dn = w_down[0].astype(BF16)

    cosf, sinf = _rotary_tables(s)
    dq, dk, chunk_decay = _retention_decay_tables()
    tril = jnp.tril(jnp.ones((CHUNK, CHUNK), F32))
    tm_merge = min(ROWS_MERGE, t)
    lower = jnp.tril(jnp.ones((tm_merge, tm_merge), F32), -1).astype(BF16)

    x2d = x.reshape(t, d)
    proj, gates = _in_projection(x2d, norm_mix_w, w_big, w_if)
    proj3 = proj.reshape(b, s, N_BIG)
    ret = _retention(proj3, cosf, sinf, dq, dk, ret_gn_w, chunk_decay)
    hm = _mlstm(proj3, gates.reshape(b, s, LANES), mlstm_conv_w[0, :, 0, :], mlstm_conv_b,
                gate_bias, mlstm_gn_w, tril)
    x1, h2p, route, counts = _merge_route(
        ret.reshape(t, V_ALL), hm.reshape(t, V_ALL), proj, x2d, w_ret_branch[0].astype(BF16),
        w_mlstm_branch[0].astype(BF16), w_out[0].astype(BF16), norm_ffn_w, w_router, b_router, lower)

    n_slots = 2 * t + N_EXPERTS * MOE_ROWS
    cnt = counts[0, :N_EXPERTS].astype(jnp.int32)
    padded = (cnt + MOE_ROWS - 1) // MOE_ROWS * MOE_ROWS
    pend = jnp.cumsum(padded)
    pstart = pend - padded
    eid = route[:, 0:2].astype(jnp.int32)
    dest = (pstart[eid] + route[:, 2:4].astype(jnp.int32)).reshape(2 * t)
    blk_start = jnp.arange(n_slots // MOE_ROWS, dtype=jnp.int32) * MOE_ROWS
    blk_e = jnp.minimum(jnp.sum(blk_start[:, None] >= pend[None, :], axis=-1), N_EXPERTS - 1).astype(jnp.int32)
    n_used = (pend[-1:] // MOE_ROWS).astype(jnp.int32)

    xs = _dispatch(dest, h2p, jnp.zeros((n_slots, PACKED), U32))
    ys = _experts(blk_e, n_used, xs, w_gate_up, w_dn)
    out = _combine(dest, route, x1, norm_final_w[None, :], ys)
    return out.reshape(b, s, d)
```

```python
import functools

import numpy as np
import jax
import jax.numpy as jnp
from jax import lax
from jax.experimental import pallas as pl
from jax.experimental.pallas import tpu as pltpu

F32 = jnp.float32
BF16 = jnp.bfloat16
U32 = jnp.uint32

D_MODEL = 1024
N_HEADS = 4
QK_DIM = 128
V_DIM = 256
CHUNK = 128
CONV_WIDTH = 4
ROPE_BASE = 10000.0
N_GROUPS = 4
EXPERTS_PER_GROUP = 8
N_EXPERTS = N_GROUPS * EXPERTS_PER_GROUP
D_EXPERT = 512
NORM_EPS = 1e-6
QK_ALL = N_HEADS * QK_DIM
V_ALL = N_HEADS * V_DIM

MIX_COLS = 2 * QK_ALL + 2 * V_ALL
N_BIG = 2 * MIX_COLS + 2 * D_MODEL
LANES = 128
PACKED = D_MODEL // 2

ROWS_PROJ = 1024
COLS_PROJ = 1024
CHUNKS_PER_STEP = 2
ROWS_MERGE = 512
ROWS_DISPATCH = 2048
ROWS_COMBINE = 512
ROUTE_ROWS = 8
MOE_ROWS = 512
VMEM_LIMIT = 56 * 1024 * 1024


def _rms(x, eps=NORM_EPS):
    return x * lax.rsqrt(jnp.mean(x * x, axis=-1, keepdims=True) + eps)


def _pack_rows(x):
    lo = lax.bitcast_convert_type(x[:, :PACKED].astype(BF16).astype(F32), U32)
    hi = lax.bitcast_convert_type(x[:, PACKED:].astype(BF16).astype(F32), U32)
    return (hi & jnp.uint32(0xFFFF0000)) | (lo >> 16)


def _unpack_rows(w):
    lo = lax.bitcast_convert_type(w << 16, F32)
    hi = lax.bitcast_convert_type(w & jnp.uint32(0xFFFF0000), F32)
    return jnp.concatenate([lo, hi], axis=1)


def _in_proj_kernel(x_ref, nw_ref, w_ref, wif_ref, o_ref, gates_ref, h_scr):
    @pl.when(pl.program_id(1) == 0)
    def _():
        h = (_rms(x_ref[...]) * nw_ref[...]).astype(BF16)
        h_scr[...] = h
        gates_ref[...] = jnp.dot(h, wif_ref[...], preferred_element_type=F32)

    o_ref[...] = jnp.dot(h_scr[...], w_ref[...], preferred_element_type=F32).astype(o_ref.dtype)


def _in_projection(x2d, norm_w, w_big, w_if):
    t = x2d.shape[0]
    tm = min(ROWS_PROJ, t)
    tn = COLS_PROJ
    return pl.pallas_call(
        _in_proj_kernel,
        grid=(t // tm, N_BIG // tn),
        in_specs=[
            pl.BlockSpec((tm, D_MODEL), lambda i, j: (i, 0)),
            pl.BlockSpec((1, D_MODEL), lambda i, j: (0, 0)),
            pl.BlockSpec((D_MODEL, tn), lambda i, j: (0, j)),
            pl.BlockSpec((D_MODEL, LANES), lambda i, j: (0, 0)),
        ],
        out_specs=[
            pl.BlockSpec((tm, tn), lambda i, j: (i, j)),
            pl.BlockSpec((tm, LANES), lambda i, j: (i, 0)),
        ],
        out_shape=[
            jax.ShapeDtypeStruct((t, N_BIG), BF16),
            jax.ShapeDtypeStruct((t, LANES), F32),
        ],
        scratch_shapes=[pltpu.VMEM((tm, D_MODEL), BF16)],
        compiler_params=pltpu.CompilerParams(
            dimension_semantics=("arbitrary", "arbitrary"), vmem_limit_bytes=VMEM_LIMIT),
        name="in_projection",
    )(x2d, norm_w, w_big, w_if)


def _retention_kernel(p_ref, cos_ref, sin_ref, dq_ref, dk_ref, gn_ref, o_ref, state_scr, *, chunk_decay):
    L = CHUNK

    @pl.when(pl.program_id(1) == 0)
    def _():
        state_scr[...] = jnp.zeros_like(state_scr)

    row = lax.broadcasted_iota(jnp.int32, (L, L), 0)
    col = lax.broadcasted_iota(jnp.int32, (L, L), 1)
    causal = row >= col
    n_chunks = p_ref.shape[0] // L
    for ci in range(n_chunks):
        r0 = ci * L
        cosf = cos_ref[r0:r0 + L, :]
        sinf = sin_ref[r0:r0 + L, :]
        for h in range(N_HEADS):
            q = p_ref[r0:r0 + L, h * QK_DIM:(h + 1) * QK_DIM].astype(F32)
            k = p_ref[r0:r0 + L, QK_ALL + h * QK_DIM:QK_ALL + (h + 1) * QK_DIM].astype(F32)
            v = p_ref[r0:r0 + L, 2 * QK_ALL + h * V_DIM:2 * QK_ALL + (h + 1) * V_DIM]
            g = p_ref[r0:r0 + L, 2 * QK_ALL + V_ALL + h * V_DIM:2 * QK_ALL + V_ALL + (h + 1) * V_DIM].astype(F32)
            qt = ((q * cosf + pltpu.roll(q, QK_DIM // 2, 1) * sinf) * dq_ref[h]).astype(BF16)
            kt = ((k * cosf + pltpu.roll(k, QK_DIM // 2, 1) * sinf) * dk_ref[h]).astype(BF16)
            s = lax.dot_general(qt, kt, (((1,), (1,)), ((), ())), preferred_element_type=F32)
            s = jnp.where(causal, s, 0.0).astype(BF16)
            state = state_scr[h]
            lhs = jnp.concatenate([s, qt], axis=1)
            rhs = jnp.concatenate([v, state.astype(BF16)], axis=0)
            o = jnp.dot(lhs, rhs, preferred_element_type=F32)
            kv = lax.dot_general(kt, v, (((0,), (0,)), ((), ())), preferred_element_type=F32)
            state_scr[h] = (state + kv) * chunk_decay[h]
            y = _rms(o) * gn_ref[:, h * V_DIM:(h + 1) * V_DIM] * (g * jax.nn.sigmoid(g))
            o_ref[r0:r0 + L, h * V_DIM:(h + 1) * V_DIM] = y.astype(o_ref.dtype)


def _retention(proj3, cosf, sinf, dq, dk, gn_w, chunk_decay):
    b, s, _ = proj3.shape
    lb = CHUNK * min(CHUNKS_PER_STEP, s // CHUNK)
    return pl.pallas_call(
        functools.partial(_retention_kernel, chunk_decay=chunk_decay),
        grid=(b, s // lb),
        in_specs=[
            pl.BlockSpec((None, lb, MIX_COLS), lambda i, c: (i, c, 0)),
            pl.BlockSpec((lb, QK_DIM), lambda i, c: (c, 0)),
            pl.BlockSpec((lb, QK_DIM), lambda i, c: (c, 0)),
            pl.BlockSpec((N_HEADS, CHUNK, QK_DIM), lambda i, c: (0, 0, 0)),
            pl.BlockSpec((N_HEADS, CHUNK, QK_DIM), lambda i, c: (0, 0, 0)),
            pl.BlockSpec((1, V_ALL), lambda i, c: (0, 0)),
        ],
        out_specs=pl.BlockSpec((None, lb, V_ALL), lambda i, c: (i, c, 0)),
        out_shape=jax.ShapeDtypeStruct((b, s, V_ALL), BF16),
        scratch_shapes=[pltpu.VMEM((N_HEADS, QK_DIM, V_DIM), F32)],
        compiler_params=pltpu.CompilerParams(
            dimension_semantics=("arbitrary", "arbitrary"), vmem_limit_bytes=VMEM_LIMIT),
        name="retention",
    )(proj3, cosf, sinf, dq, dk, gn_w)


def _mlstm_kernel(p_ref, gates_ref, cw_ref, cb_ref, gb_ref, gn_ref, tril_ref, o_ref,
                  c_scr, n_scr, m_scr, tail_scr, act_scr):
    L = CHUNK
    lb = p_ref.shape[0]

    @pl.when(pl.program_id(1) == 0)
    def _():
        c_scr[...] = jnp.zeros_like(c_scr)
        n_scr[...] = jnp.zeros_like(n_scr)
        m_scr[...] = jnp.zeros_like(m_scr)
        tail_scr[...] = jnp.zeros_like(tail_scr)

    u = p_ref[:, 0:2 * QK_ALL].astype(F32)
    ext = jnp.concatenate([tail_scr[...], u], axis=0)
    acc = u * cw_ref[CONV_WIDTH - 1:CONV_WIDTH, :] + cb_ref[...]
    for d in range(1, CONV_WIDTH):
        acc = acc + ext[8 - d:8 - d + lb, :] * cw_ref[CONV_WIDTH - 1 - d:CONV_WIDTH - d, :]
    tail_scr[...] = u[lb - 8:lb, :]
    act_scr[...] = acc * jax.nn.sigmoid(acc)

    row = lax.broadcasted_iota(jnp.int32, (L, L), 0)
    col = lax.broadcasted_iota(jnp.int32, (L, L), 1)
    causal = row >= col
    k_scale = QK_DIM ** -0.5
    for ci in range(lb // L):
        r0 = ci * L
        pre = gates_ref[r0:r0 + L, :] + gb_ref[...]
        log_f = jnp.minimum(pre, 0.0) - jnp.log1p(jnp.exp(-jnp.abs(pre)))
        bcum = jnp.dot(tril_ref[...], log_f, preferred_element_type=F32, precision=lax.Precision.HIGHEST)
        for h in range(N_HEADS):
            q = act_scr[r0:r0 + L, h * QK_DIM:(h + 1) * QK_DIM]
            k = act_scr[r0:r0 + L, QK_ALL + h * QK_DIM:QK_ALL + (h + 1) * QK_DIM] * k_scale
            v = p_ref[r0:r0 + L, 2 * QK_ALL + h * V_DIM:2 * QK_ALL + (h + 1) * V_DIM]
            og = p_ref[r0:r0 + L, 2 * QK_ALL + V_ALL + h * V_DIM:2 * QK_ALL + V_ALL + (h + 1) * V_DIM].astype(F32)
            b_h = bcum[:, N_HEADS + h:N_HEADS + h + 1]
            i_h = pre[:, h:h + 1]
            m_prev = m_scr[h:h + 1, 0:1]
            n_prev = n_scr[h:h + 1, :]
            c_prev = c_scr[h]

            a = b_h + m_prev
            src = jnp.transpose(jnp.broadcast_to(i_h - b_h, (L, L)))
            dmat = jnp.where(causal, b_h + src, -jnp.inf)
            m_t = jnp.maximum(a, jnp.max(dmat, axis=-1, keepdims=True))
            w_inter = jnp.exp(a - m_t)
            qb = q.astype(BF16)
            kb = k.astype(BF16)
            s = lax.dot_general(qb, kb, (((1,), (1,)), ((), ())), preferred_element_type=F32) * jnp.exp(dmat - m_t)
            lhs = jnp.concatenate([s.astype(BF16), (q * w_inter).astype(BF16)], axis=1)
            rhs = jnp.concatenate([v, c_prev.astype(BF16)], axis=0)
            num = jnp.dot(lhs, rhs, preferred_element_type=F32)
            den = (jnp.sum(s, axis=-1, keepdims=True)
                   + jnp.sum(q * n_prev, axis=-1, keepdims=True) * w_inter)
            hh = num / jnp.maximum(jnp.abs(den), jnp.exp(-m_t))

            b_last = b_h[L - 1:L, :]
            gk = b_last - b_h + i_h
            m_new = jnp.maximum(b_last + m_prev, jnp.max(gk, axis=0, keepdims=True))
            keep = jnp.exp(b_last + m_prev - m_new)
            wk = k * jnp.exp(gk - m_new)
            c_scr[h] = c_prev * keep + lax.dot_general(
                wk.astype(BF16), v, (((0,), (0,)), ((), ())), preferred_element_type=F32)
            n_scr[h:h + 1, :] = n_prev * keep + jnp.sum(wk, axis=0, keepdims=True)
            m_scr[h:h + 1, :] = jnp.broadcast_to(m_new, (1, LANES))

            y = _rms(hh * jax.nn.sigmoid(og)) * gn_ref[:, h * V_DIM:(h + 1) * V_DIM]
            o_ref[r0:r0 + L, h * V_DIM:(h + 1) * V_DIM] = y.astype(o_ref.dtype)


def _mlstm(proj3, gates3, conv_w, conv_b, gate_bias, gn_w, tril):
    b, s, _ = proj3.shape
    lb = CHUNK * min(CHUNKS_PER_STEP, s // CHUNK)
    return pl.pallas_call(
        _mlstm_kernel,
        grid=(b, s // lb),
        in_specs=[
            pl.BlockSpec((None, lb, MIX_COLS), lambda i, c: (i, c, 1)),
            pl.BlockSpec((None, lb, LANES), lambda i, c: (i, c, 0)),
            pl.BlockSpec((CONV_WIDTH, 2 * QK_ALL), lambda i, c: (0, 0)),
            pl.BlockSpec((1, 2 * QK_ALL), lambda i, c: (0, 0)),
            pl.BlockSpec((1, LANES), lambda i, c: (0, 0)),
            pl.BlockSpec((1, V_ALL), lambda i, c: (0, 0)),
            pl.BlockSpec((CHUNK, CHUNK), lambda i, c: (0, 0)),
        ],
        out_specs=pl.BlockSpec((None, lb, V_ALL), lambda i, c: (i, c, 0)),
        out_shape=jax.ShapeDtypeStruct((b, s, V_ALL), BF16),
        scratch_shapes=[
            pltpu.VMEM((N_HEADS, QK_DIM, V_DIM), F32),
            pltpu.VMEM((8, QK_DIM), F32),
            pltpu.VMEM((8, LANES), F32),
            pltpu.VMEM((8, 2 * QK_ALL), F32),
            pltpu.VMEM((lb, 2 * QK_ALL), F32),
        ],
        compiler_params=pltpu.CompilerParams(
            dimension_semantics=("arbitrary", "arbitrary"), vmem_limit_bytes=VMEM_LIMIT),
        name="mlstm",
    )(proj3, gates3, conv_w, conv_b, gate_bias, gn_w, tril)


def _merge_route_kernel(ret_ref, hm_ref, gr_ref, gm_ref, x_ref, wr_ref, wm_ref, wo_ref, nw_ref,
                        wrt_ref, brt_ref, lower_ref, x1_ref, h2_ref, route_ref, route_t_ref, cnt_ref,
                        carry_scr):
    @pl.when(pl.program_id(0) == 0)
    def _():
        carry_scr[...] = jnp.zeros_like(carry_scr)

    y_ret = jnp.dot(ret_ref[...], wr_ref[...], preferred_element_type=F32)
    y_m = jnp.dot(hm_ref[...], wm_ref[...], preferred_element_type=F32)
    merged = (jax.nn.sigmoid(gr_ref[...].astype(F32)) * y_ret
              + jax.nn.sigmoid(gm_ref[...].astype(F32)) * y_m)
    x1 = x_ref[...] + jnp.dot(merged.astype(BF16), wo_ref[...], preferred_element_type=F32)
    x1_ref[...] = x1
    h2 = _rms(x1) * nw_ref[...]
    h2_ref[...] = _pack_rows(h2)

    logits = jnp.dot(h2.astype(BF16), wrt_ref[...], preferred_element_type=F32) + brt_ref[...]
    tm = logits.shape[0]
    lane = lax.broadcasted_iota(jnp.int32, (tm, LANES), 1)
    neg = -jnp.inf
    big = jnp.int32(LANES)
    is_group = (lane >= N_EXPERTS) & (lane < N_EXPERTS + N_GROUPS)
    gl = jnp.where(is_group, logits, neg)
    g_max = jnp.max(gl, axis=-1, keepdims=True)
    g_idx = jnp.min(jnp.where(gl == g_max, lane, big), axis=-1, keepdims=True) - N_EXPERTS
    g_w = 1.0 / jnp.sum(jnp.exp(gl - g_max), axis=-1, keepdims=True)
    in_group = (lane >= g_idx * EXPERTS_PER_GROUP) & (lane < (g_idx + 1) * EXPERTS_PER_GROUP)
    el = jnp.where(in_group, logits, neg)
    l1 = jnp.max(el, axis=-1, keepdims=True)
    e1 = jnp.min(jnp.where(el == l1, lane, big), axis=-1, keepdims=True)
    el2 = jnp.where(lane == e1, neg, el)
    l2 = jnp.max(el2, axis=-1, keepdims=True)
    e2 = jnp.min(jnp.where(el2 == l2, lane, big), axis=-1, keepdims=True)
    t21 = jnp.exp(l2 - l1)
    w1 = g_w / (1.0 + t21)
    w2 = g_w * t21 / (1.0 + t21)

    hit1 = lane == e1
    hit2 = lane == e2
    cnt = jnp.where(hit1 | hit2, 1.0, 0.0)
    before = jnp.dot(lower_ref[...], cnt.astype(BF16), preferred_element_type=F32) + carry_scr[...]
    r1 = jnp.sum(jnp.where(hit1, before, 0.0), axis=-1, keepdims=True)
    r2 = jnp.sum(jnp.where(hit2, before, 0.0), axis=-1, keepdims=True)
    carry = carry_scr[...] + jnp.sum(cnt, axis=0, keepdims=True)
    carry_scr[...] = carry
    cnt_ref[...] = carry

    fields = (e1.astype(F32), e2.astype(F32), r1, r2, w1, w2)
    packed = jnp.zeros((tm, LANES), F32)
    for idx, val in enumerate(fields):
        packed = jnp.where(lane == idx, val, packed)
    route_ref[...] = packed
    route_t_ref[...] = jnp.transpose(packed)[0:ROUTE_ROWS, :]


def _merge_route(ret, hm, proj, x2d, w_ret, w_m, w_out, norm_w, w_router, b_router, lower):
    t = x2d.shape[0]
    tm = min(ROWS_MERGE, t)
    gate_r_blk = 2 * MIX_COLS // D_MODEL
    row_blk = lambda i: (i, 0)
    const = lambda i: (0, 0)
    return pl.pallas_call(
        _merge_route_kernel,
        grid=(t // tm,),
        in_specs=[
            pl.BlockSpec((tm, V_ALL), row_blk),
            pl.BlockSpec((tm, V_ALL), row_blk),
            pl.BlockSpec((tm, D_MODEL), lambda i: (i, gate_r_blk)),
            pl.BlockSpec((tm, D_MODEL), lambda i: (i, gate_r_blk + 1)),
            pl.BlockSpec((tm, D_MODEL), row_blk),
            pl.BlockSpec((V_ALL, D_MODEL), const),
            pl.BlockSpec((V_ALL, D_MODEL), const),
            pl.BlockSpec((D_MODEL, D_MODEL), const),
            pl.BlockSpec((1, D_MODEL), const),
            pl.BlockSpec((D_MODEL, LANES), const),
            pl.BlockSpec((1, LANES), const),
            pl.BlockSpec((tm, tm), const),
        ],
        out_specs=[
            pl.BlockSpec((tm, D_MODEL), row_blk),
            pl.BlockSpec((tm, PACKED), row_blk),
            pl.BlockSpec((tm, LANES), row_blk),
            pl.BlockSpec((ROUTE_ROWS, tm), lambda i: (0, i)),
            pl.BlockSpec((1, LANES), const),
        ],
        out_shape=[
            jax.ShapeDtypeStruct((t, D_MODEL), F32),
            jax.ShapeDtypeStruct((t, PACKED), U32),
            jax.ShapeDtypeStruct((t, LANES), F32),
            jax.ShapeDtypeStruct((ROUTE_ROWS, t), F32),
            jax.ShapeDtypeStruct((1, LANES), F32),
        ],
        scratch_shapes=[pltpu.VMEM((1, LANES), F32)],
        compiler_params=pltpu.CompilerParams(
            dimension_semantics=("arbitrary",), vmem_limit_bytes=VMEM_LIMIT),
        name="merge_route",
    )(ret, hm, proj, proj, x2d, w_ret, w_m, w_out, norm_w, w_router, b_router, lower)


SUBLANES = 8


def _for_row_groups(n_rows, body):
    def step(g, carry):
        for sub in range(SUBLANES):
            body(g, sub)
        return carry

    lax.fori_loop(0, n_rows // SUBLANES, step, 0)


def _dispatch_kernel(fill_ref, dest_a_ref, dest_b_ref, h2_ref, xs_hbm, zero_scr, sem):
    tm = h2_ref.shape[0] * SUBLANES

    @pl.when(pl.program_id(0) == 0)
    def _():
        zero_scr[...] = jnp.zeros_like(zero_scr)

        def fill_copy(j):
            row0 = pl.multiple_of(fill_ref[j], MOE_ROWS)
            return pltpu.make_async_copy(zero_scr, xs_hbm.at[pl.ds(row0, MOE_ROWS)], sem)

        def start_fill(j, carry):
            @pl.when(fill_ref[j] >= 0)
            def _():
                fill_copy(j).start()
            return carry

        def wait_fill(j, carry):
            @pl.when(fill_ref[j] >= 0)
            def _():
                fill_copy(j).wait()
            return carry

        lax.fori_loop(0, fill_ref.shape[0], start_fill, 0)
        lax.fori_loop(0, fill_ref.shape[0], wait_fill, 0)

    def start(g, sub):
        src = h2_ref.at[g, pl.ds(sub, 1)]
        r = g * SUBLANES + sub
        pltpu.make_async_copy(src, xs_hbm.at[pl.ds(dest_a_ref[r], 1)], sem).start()
        pltpu.make_async_copy(src, xs_hbm.at[pl.ds(dest_b_ref[r], 1)], sem).start()

    _for_row_groups(tm, start)
    for _ in range(2):
        pltpu.make_async_copy(xs_hbm.at[pl.ds(0, tm)], xs_hbm.at[pl.ds(0, tm)], sem).wait()


def _dispatch(fill_rows, dest_a, dest_b, h2p, n_slots):
    t = h2p.shape[0]
    tm = min(ROWS_DISPATCH, t)
    return pl.pallas_call(
        _dispatch_kernel,
        grid=(t // tm,),
        in_specs=[
            pl.BlockSpec(memory_space=pltpu.SMEM),
            pl.BlockSpec((tm,), lambda i: (i,), memory_space=pltpu.SMEM),
            pl.BlockSpec((tm,), lambda i: (i,), memory_space=pltpu.SMEM),
            pl.BlockSpec((tm // SUBLANES, SUBLANES, PACKED), lambda i: (i, 0, 0)),
        ],
        out_specs=pl.BlockSpec(memory_space=pl.ANY),
        out_shape=jax.ShapeDtypeStruct((n_slots, PACKED), U32),
        scratch_shapes=[pltpu.VMEM((MOE_ROWS, PACKED), U32), pltpu.SemaphoreType.DMA(())],
        compiler_params=pltpu.CompilerParams(
            dimension_semantics=("arbitrary",), vmem_limit_bytes=VMEM_LIMIT),
        name="dispatch",
    )(fill_rows, dest_a, dest_b, h2p.reshape(t // SUBLANES, SUBLANES, PACKED))


def _experts_kernel(blk_e_ref, blk_valid_ref, xs_ref, wgu_ref, wd_ref, ys_ref):
    del blk_e_ref
    valid = blk_valid_ref[pl.program_id(0)]

    @pl.when(valid > 0)
    def _():
        xb = _unpack_rows(xs_ref[...]).astype(BF16)
        gu = jnp.dot(xb, wgu_ref[...], preferred_element_type=F32)
        g = gu[:, :D_EXPERT]
        act = (g * jax.nn.sigmoid(g) * gu[:, D_EXPERT:]).astype(BF16)
        ys_ref[...] = _pack_rows(jnp.dot(act, wd_ref[...], preferred_element_type=F32))

    @pl.when(valid <= 0)
    def _():
        ys_ref[...] = jnp.zeros_like(ys_ref)


def _experts(blk_e, blk_valid, xs, w_gate_up, w_down):
    p = xs.shape[0]
    grid_spec = pltpu.PrefetchScalarGridSpec(
        num_scalar_prefetch=2,
        grid=(p // MOE_ROWS,),
        in_specs=[
            pl.BlockSpec((MOE_ROWS, PACKED), lambda i, be, bv: (i, 0)),
            pl.BlockSpec((None, D_MODEL, 2 * D_EXPERT), lambda i, be, bv: (be[i], 0, 0)),
            pl.BlockSpec((None, D_EXPERT, D_MODEL), lambda i, be, bv: (be[i], 0, 0)),
        ],
        out_specs=pl.BlockSpec((MOE_ROWS, PACKED), lambda i, be, bv: (i, 0)),
    )
    return pl.pallas_call(
        _experts_kernel,
        grid_spec=grid_spec,
        out_shape=jax.ShapeDtypeStruct((p, PACKED), U32),
        compiler_params=pltpu.CompilerParams(
            dimension_semantics=("arbitrary",), vmem_limit_bytes=VMEM_LIMIT),
        name="experts",
    )(blk_e, blk_valid, xs, w_gate_up, w_down)


def _combine_kernel(dest_a_ref, dest_b_ref, route_ref, x1_ref, nw_ref, ys_hbm, o_ref, buf_a, buf_b, sem):
    tm = x1_ref.shape[0]

    def start(g, sub):
        r = g * SUBLANES + sub
        pltpu.make_async_copy(ys_hbm.at[pl.ds(dest_a_ref[r], 1)], buf_a.at[g, pl.ds(sub, 1)], sem).start()
        pltpu.make_async_copy(ys_hbm.at[pl.ds(dest_b_ref[r], 1)], buf_b.at[g, pl.ds(sub, 1)], sem).start()

    _for_row_groups(tm, start)
    for _ in range(2):
        pltpu.make_async_copy(ys_hbm.at[pl.ds(0, tm)], ys_hbm.at[pl.ds(0, tm)], sem).wait()

    w1 = route_ref[:, 4:5]
    w2 = route_ref[:, 5:6]
    ya = _unpack_rows(buf_a[...].reshape(tm, PACKED))
    yb = _unpack_rows(buf_b[...].reshape(tm, PACKED))
    x2 = x1_ref[...] + w1 * ya + w2 * yb
    o_ref[...] = _rms(x2) * nw_ref[...]


def _combine(dest_a, dest_b, route, x1, norm_w, ys):
    t = x1.shape[0]
    tm = min(ROWS_COMBINE, t)
    return pl.pallas_call(
        _combine_kernel,
        grid=(t // tm,),
        in_specs=[
            pl.BlockSpec((tm,), lambda i: (i,), memory_space=pltpu.SMEM),
            pl.BlockSpec((tm,), lambda i: (i,), memory_space=pltpu.SMEM),
            pl.BlockSpec((tm, LANES), lambda i: (i, 0)),
            pl.BlockSpec((tm, D_MODEL), lambda i: (i, 0)),
            pl.BlockSpec((1, D_MODEL), lambda i: (0, 0)),
            pl.BlockSpec(memory_space=pl.ANY),
        ],
        out_specs=pl.BlockSpec((tm, D_MODEL), lambda i: (i, 0)),
        out_shape=jax.ShapeDtypeStruct((t, D_MODEL), F32),
        scratch_shapes=[
            pltpu.VMEM((tm // SUBLANES, SUBLANES, PACKED), U32),
            pltpu.VMEM((tm // SUBLANES, SUBLANES, PACKED), U32),
            pltpu.SemaphoreType.DMA(()),
        ],
        compiler_params=pltpu.CompilerParams(
            dimension_semantics=("arbitrary",), vmem_limit_bytes=VMEM_LIMIT),
        name="combine",
    )(dest_a, dest_b, route, x1, norm_w, ys)


def _rotary_tables(seq):
    inv_freq = 1.0 / (ROPE_BASE ** (jnp.arange(0, QK_DIM, 2, dtype=F32) / QK_DIM))
    ang = jnp.arange(seq, dtype=F32)[:, None] * inv_freq[None, :]
    cos, sin = jnp.cos(ang), jnp.sin(ang)
    return jnp.concatenate([cos, cos], axis=1), jnp.concatenate([-sin, sin], axis=1)


def _retention_decay_tables():
    gamma = 1.0 - 2.0 ** (-5.0 - np.arange(N_HEADS, dtype=np.float64))
    idx = np.arange(CHUNK, dtype=np.float64) + 1.0
    dq = gamma[:, None] ** idx[None, :]
    dk = gamma[:, None] ** (-idx[None, :]) * QK_DIM ** -0.5
    bcast = lambda a: jnp.asarray(np.broadcast_to(a[:, :, None], (N_HEADS, CHUNK, QK_DIM)), F32)
    return bcast(dq), bcast(dk), tuple(float(g) for g in gamma ** CHUNK)


def kernel(x, norm_mix_w, w_in, ret_gn_w, w_ret_branch, mlstm_conv_w, mlstm_conv_b, b_igate, b_fgate,
           mlstm_gn_w, w_mlstm_branch, w_out, norm_ffn_w, w_group, b_group, w_expert_router,
           b_expert_router, w_gate, w_up, w_down, norm_final_w):
    assert norm_mix_w.shape[0] == 1, "one layer"
    b, s, d = x.shape
    t = b * s
    assert d == D_MODEL and s % CHUNK == 0

    wi = w_in[0]
    n_pre = 2 * MIX_COLS
    w_big = jnp.concatenate([wi[:, :n_pre], wi[:, n_pre + 2 * N_HEADS:]], axis=1).astype(BF16)
    w_if = jnp.pad(wi[:, n_pre:n_pre + 2 * N_HEADS], ((0, 0), (0, LANES - 2 * N_HEADS))).astype(BF16)
    gate_bias = jnp.pad(jnp.concatenate([b_igate[0], b_fgate[0]]), (0, LANES - 2 * N_HEADS))[None, :]
    w_router = jnp.pad(jnp.concatenate([w_expert_router[0], w_group[0]], axis=1),
                       ((0, 0), (0, LANES - N_EXPERTS - N_GROUPS))).astype(BF16)
    b_router = jnp.pad(jnp.concatenate([b_expert_router[0], b_group[0]]),
                       (0, LANES - N_EXPERTS - N_GROUPS))[None, :]
    w_gate_up = jnp.concatenate([w_gate[0], w_up[0]], axis=2).astype(BF16)
    w_dn = w_down[0].astype(BF16)

    cosf, sinf = _rotary_tables(s)
    dq, dk, chunk_decay = _retention_decay_tables()
    tril = jnp.tril(jnp.ones((CHUNK, CHUNK), F32))
    tm_merge = min(ROWS_MERGE, t)
    lower = jnp.tril(jnp.ones((tm_merge, tm_merge), F32), -1).astype(BF16)

    x2d = x.reshape(t, d)
    proj, gates = _in_projection(x2d, norm_mix_w, w_big, w_if)
    proj3 = proj.reshape(b, s, N_BIG)
    ret = _retention(proj3, cosf, sinf, dq, dk, ret_gn_w, chunk_decay)
    hm = _mlstm(proj3, gates.reshape(b, s, LANES), mlstm_conv_w[0, :, 0, :], mlstm_conv_b,
                gate_bias, mlstm_gn_w, tril)
    x1, h2p, route, route_t, counts = _merge_route(
        ret.reshape(t, V_ALL), hm.reshape(t, V_ALL), proj, x2d, w_ret_branch[0].astype(BF16),
        w_mlstm_branch[0].astype(BF16), w_out[0].astype(BF16), norm_ffn_w, w_router, b_router, lower)

    n_slots = 2 * t + N_EXPERTS * MOE_ROWS
    cnt = counts[0, :N_EXPERTS].astype(jnp.int32)
    padded = (cnt + MOE_ROWS - 1) // MOE_ROWS * MOE_ROWS
    pend = jnp.cumsum(padded)
    pstart = pend - padded
    eid = route_t[0:2].astype(jnp.int32)
    experts = jnp.arange(N_EXPERTS, dtype=jnp.int32)[:, None, None]
    slot0 = jnp.sum(jnp.where(eid[None] == experts, pstart[:, None, None], 0), axis=0)
    dest = slot0 + route_t[2:4].astype(jnp.int32)
    blk_start = jnp.arange(n_slots // MOE_ROWS, dtype=jnp.int32) * MOE_ROWS
    blk_e = jnp.minimum(jnp.sum(blk_start[:, None] >= pend[None, :], axis=-1), N_EXPERTS - 1).astype(jnp.int32)
    blk_valid = jnp.clip(pstart[blk_e] + cnt[blk_e] - blk_start, 0, MOE_ROWS).astype(jnp.int32)
    tail = pend[-1] + jnp.arange(N_EXPERTS, dtype=jnp.int32) * MOE_ROWS
    fill_rows = jnp.concatenate([jnp.where(padded > 0, pend - MOE_ROWS, -1),
                                 jnp.where(tail < n_slots, tail, -1)]).astype(jnp.int32)

    xs = _dispatch(fill_rows, dest[0], dest[1], h2p, n_slots)
    ys = _experts(blk_e, blk_valid, xs, w_gate_up, w_dn)
    out = _combine(dest[0], dest[1], route, x1, norm_final_w[None, :], ys)
    return out.reshape(b, s, d)
```

```python
import functools

import numpy as np
import jax
import jax.numpy as jnp
from jax import lax
from jax.experimental import pallas as pl
from jax.experimental.pallas import tpu as pltpu

F32 = jnp.float32
BF16 = jnp.bfloat16
U32 = jnp.uint32

D_MODEL = 1024
N_HEADS = 4
QK_DIM = 128
V_DIM = 256
CHUNK = 128
CONV_WIDTH = 4
ROPE_BASE = 10000.0
N_GROUPS = 4
EXPERTS_PER_GROUP = 8
N_EXPERTS = N_GROUPS * EXPERTS_PER_GROUP
D_EXPERT = 512
NORM_EPS = 1e-6
QK_ALL = N_HEADS * QK_DIM
V_ALL = N_HEADS * V_DIM

MIX_COLS = 2 * QK_ALL + 2 * V_ALL
N_BIG = 2 * MIX_COLS + 2 * D_MODEL
LANES = 128
PACKED = D_MODEL // 2

ROWS_PROJ = 2048
COLS_PROJ = 1024
RET_CHUNKS_PER_STEP = 4
MLSTM_CHUNKS_PER_STEP = 2
ROWS_MERGE = 512
ROWS_DISPATCH = 2048
ROWS_COMBINE = 512
ROUTE_ROWS = 8
MOE_ROWS = 512
VMEM_LIMIT = 56 * 1024 * 1024


def _rms(x, eps=NORM_EPS):
    return x * lax.rsqrt(jnp.mean(x * x, axis=-1, keepdims=True) + eps)


def _pack_rows(x):
    lo = lax.bitcast_convert_type(x[:, :PACKED].astype(BF16).astype(F32), U32)
    hi = lax.bitcast_convert_type(x[:, PACKED:].astype(BF16).astype(F32), U32)
    return (hi & jnp.uint32(0xFFFF0000)) | (lo >> 16)


def _unpack_rows(w):
    lo = lax.bitcast_convert_type(w << 16, F32)
    hi = lax.bitcast_convert_type(w & jnp.uint32(0xFFFF0000), F32)
    return jnp.concatenate([lo, hi], axis=1)


def _in_proj_kernel(x_ref, nw_ref, w_ref, wif_ref, o_ref, gates_ref, h_scr):
    @pl.when(pl.program_id(1) == 0)
    def _():
        h = (_rms(x_ref[...]) * nw_ref[...]).astype(BF16)
        h_scr[...] = h
        gates_ref[...] = jnp.dot(h, wif_ref[...], preferred_element_type=F32)

    o_ref[...] = jnp.dot(h_scr[...], w_ref[...], preferred_element_type=F32).astype(o_ref.dtype)


def _in_projection(x2d, norm_w, w_big, w_if):
    t = x2d.shape[0]
    tm = min(ROWS_PROJ, t)
    tn = COLS_PROJ
    return pl.pallas_call(
        _in_proj_kernel,
        grid=(t // tm, N_BIG // tn),
        in_specs=[
            pl.BlockSpec((tm, D_MODEL), lambda i, j: (i, 0)),
            pl.BlockSpec((1, D_MODEL), lambda i, j: (0, 0)),
            pl.BlockSpec((D_MODEL, tn), lambda i, j: (0, j)),
            pl.BlockSpec((D_MODEL, LANES), lambda i, j: (0, 0)),
        ],
        out_specs=[
            pl.BlockSpec((tm, tn), lambda i, j: (i, j)),
            pl.BlockSpec((tm, LANES), lambda i, j: (i, 0)),
        ],
        out_shape=[
            jax.ShapeDtypeStruct((t, N_BIG), BF16),
            jax.ShapeDtypeStruct((t, LANES), F32),
        ],
        scratch_shapes=[pltpu.VMEM((tm, D_MODEL), BF16)],
        compiler_params=pltpu.CompilerParams(
            dimension_semantics=("arbitrary", "arbitrary"), vmem_limit_bytes=VMEM_LIMIT),
        name="in_projection",
    )(x2d, norm_w, w_big, w_if)


def _retention_kernel(p_ref, cos_ref, sin_ref, dq_ref, dk_ref, gn_ref, o_ref, state_scr, *, chunk_decay):
    L = CHUNK

    @pl.when(pl.program_id(1) == 0)
    def _():
        state_scr[...] = jnp.zeros_like(state_scr)

    row = lax.broadcasted_iota(jnp.int32, (L, L), 0)
    col = lax.broadcasted_iota(jnp.int32, (L, L), 1)
    causal = row >= col
    n_chunks = p_ref.shape[0] // L
    units = [(ci * L, h) for ci in range(n_chunks) for h in range(N_HEADS)]

    qts, kts, scores = [], [], []
    for r0, h in units:
        cosf = cos_ref[r0:r0 + L, :]
        sinf = sin_ref[r0:r0 + L, :]
        q = p_ref[r0:r0 + L, h * QK_DIM:(h + 1) * QK_DIM].astype(F32)
        k = p_ref[r0:r0 + L, QK_ALL + h * QK_DIM:QK_ALL + (h + 1) * QK_DIM].astype(F32)
        qt = ((q * cosf + pltpu.roll(q, QK_DIM // 2, 1) * sinf) * dq_ref[h]).astype(BF16)
        kt = ((k * cosf + pltpu.roll(k, QK_DIM // 2, 1) * sinf) * dk_ref[h]).astype(BF16)
        s = lax.dot_general(qt, kt, (((1,), (1,)), ((), ())), preferred_element_type=F32)
        qts.append(qt)
        kts.append(kt)
        scores.append(jnp.where(causal, s, 0.0).astype(BF16))

    states = [state_scr[h] for h in range(N_HEADS)]
    for (r0, h), qt, kt, s in zip(units, qts, kts, scores):
        v = p_ref[r0:r0 + L, 2 * QK_ALL + h * V_DIM:2 * QK_ALL + (h + 1) * V_DIM]
        g = p_ref[r0:r0 + L, 2 * QK_ALL + V_ALL + h * V_DIM:2 * QK_ALL + V_ALL + (h + 1) * V_DIM].astype(F32)
        lhs = jnp.concatenate([s, qt], axis=1)
        rhs = jnp.concatenate([v, states[h].astype(BF16)], axis=0)
        o = jnp.dot(lhs, rhs, preferred_element_type=F32)
        kv = lax.dot_general(kt, v, (((0,), (0,)), ((), ())), preferred_element_type=F32)
        states[h] = (states[h] + kv) * chunk_decay[h]
        y = _rms(o) * gn_ref[:, h * V_DIM:(h + 1) * V_DIM] * (g * jax.nn.sigmoid(g))
        o_ref[r0:r0 + L, h * V_DIM:(h + 1) * V_DIM] = y.astype(o_ref.dtype)
    for h in range(N_HEADS):
        state_scr[h] = states[h]


def _retention(proj3, cosf, sinf, dq, dk, gn_w, chunk_decay):
    b, s, _ = proj3.shape
    lb = CHUNK * min(RET_CHUNKS_PER_STEP, s // CHUNK)
    return pl.pallas_call(
        functools.partial(_retention_kernel, chunk_decay=chunk_decay),
        grid=(b, s // lb),
        in_specs=[
            pl.BlockSpec((None, lb, MIX_COLS), lambda i, c: (i, c, 0)),
            pl.BlockSpec((lb, QK_DIM), lambda i, c: (c, 0)),
            pl.BlockSpec((lb, QK_DIM), lambda i, c: (c, 0)),
            pl.BlockSpec((N_HEADS, CHUNK, QK_DIM), lambda i, c: (0, 0, 0)),
            pl.BlockSpec((N_HEADS, CHUNK, QK_DIM), lambda i, c: (0, 0, 0)),
            pl.BlockSpec((1, V_ALL), lambda i, c: (0, 0)),
        ],
        out_specs=pl.BlockSpec((None, lb, V_ALL), lambda i, c: (i, c, 0)),
        out_shape=jax.ShapeDtypeStruct((b, s, V_ALL), BF16),
        scratch_shapes=[pltpu.VMEM((N_HEADS, QK_DIM, V_DIM), F32)],
        compiler_params=pltpu.CompilerParams(
            dimension_semantics=("arbitrary", "arbitrary"), vmem_limit_bytes=VMEM_LIMIT),
        name="retention",
    )(proj3, cosf, sinf, dq, dk, gn_w)


def _mlstm_kernel(p_ref, gates_ref, cw_ref, cb_ref, gb_ref, gn_ref, o_ref,
                  c_scr, n_scr, m_scr, tail_scr, act_scr):
    L = CHUNK
    lb = p_ref.shape[0]

    @pl.when(pl.program_id(1) == 0)
    def _():
        c_scr[...] = jnp.zeros_like(c_scr)
        n_scr[...] = jnp.zeros_like(n_scr)
        m_scr[...] = jnp.zeros_like(m_scr)
        tail_scr[...] = jnp.zeros_like(tail_scr)

    u = p_ref[:, 0:2 * QK_ALL].astype(F32)
    ext = jnp.concatenate([tail_scr[...], u], axis=0)
    acc = u * cw_ref[CONV_WIDTH - 1:CONV_WIDTH, :] + cb_ref[...]
    for d in range(1, CONV_WIDTH):
        acc = acc + ext[8 - d:8 - d + lb, :] * cw_ref[CONV_WIDTH - 1 - d:CONV_WIDTH - d, :]
    tail_scr[...] = u[lb - 8:lb, :]
    act_scr[...] = acc * jax.nn.sigmoid(acc)

    row = lax.broadcasted_iota(jnp.int32, (L, L), 0)
    col = lax.broadcasted_iota(jnp.int32, (L, L), 1)
    causal = row >= col
    k_scale = QK_DIM ** -0.5
    units = [(ci * L, h) for ci in range(lb // L) for h in range(N_HEADS)]

    def load_qk(r0, h):
        q = act_scr[r0:r0 + L, h * QK_DIM:(h + 1) * QK_DIM]
        k = act_scr[r0:r0 + L, QK_ALL + h * QK_DIM:QK_ALL + (h + 1) * QK_DIM] * k_scale
        return q, k

    gate_terms, src_rows = [], []
    lane_t = lax.broadcasted_iota(jnp.int32, (2 * N_HEADS, L), 1)
    for ci in range(lb // L):
        pre = gates_ref[ci * L:(ci + 1) * L, :] + gb_ref[...]
        pre_rows = jnp.transpose(pre)[0:2 * N_HEADS, :]
        b_rows = jnp.minimum(pre_rows, 0.0) - jnp.log1p(jnp.exp(-jnp.abs(pre_rows)))
        shift = 1
        while shift < L:
            b_rows = b_rows + jnp.where(lane_t >= shift, pltpu.roll(b_rows, shift, 1), 0.0)
            shift *= 2
        src_rows.append(pre_rows[0:N_HEADS, :] - b_rows[N_HEADS:2 * N_HEADS, :])
        bcum = jnp.transpose(jnp.concatenate([b_rows, jnp.zeros((L - 2 * N_HEADS, L), F32)], axis=0))
        gate_terms.append((pre, bcum))
    scores = []
    for r0, h in units:
        q, k = load_qk(r0, h)
        scores.append(lax.dot_general(q.astype(BF16), k.astype(BF16), (((1,), (1,)), ((), ())),
                                      preferred_element_type=F32))

    m_state = [m_scr[h:h + 1, :] for h in range(N_HEADS)]
    terms = []
    for (r0, h), qk in zip(units, scores):
        pre, bcum = gate_terms[r0 // L]
        q, k = load_qk(r0, h)
        b_t = jnp.broadcast_to(bcum[:, N_HEADS + h:N_HEADS + h + 1], (L, L))
        i_t = jnp.broadcast_to(pre[:, h:h + 1], (L, L))
        src = jnp.broadcast_to(src_rows[r0 // L][h:h + 1, :], (L, L))
        m_prev = m_state[h]
        a = b_t + m_prev
        dmat = jnp.where(causal, b_t + src, -jnp.inf)
        m_t = jnp.maximum(a, jnp.max(dmat, axis=-1, keepdims=True))
        w_inter = jnp.exp(a - m_t)
        s = qk * jnp.exp(dmat - m_t)
        lhs = jnp.concatenate([s.astype(BF16), (q * w_inter).astype(BF16)], axis=1)
        b_last = b_t[L - 1:L, :]
        gk = b_last - b_t + i_t
        m_new = jnp.maximum(b_last + m_prev, jnp.max(gk, axis=0, keepdims=True))
        wk = k * jnp.exp(gk - m_new)
        m_state[h] = m_new
        terms.append(dict(
            lhs=lhs, s_sum=jnp.sum(s, axis=-1, keepdims=True), w_inter=w_inter, floor=jnp.exp(-m_t),
            keep=jnp.exp(b_last + m_prev - m_new), wk=wk.astype(BF16), wk_sum=jnp.sum(wk, axis=0, keepdims=True)))

    n_state = [n_scr[h:h + 1, :] for h in range(N_HEADS)]
    c_state = [c_scr[h] for h in range(N_HEADS)]
    v_cols = lambda h: slice(2 * QK_ALL + h * V_DIM, 2 * QK_ALL + (h + 1) * V_DIM)
    for ci in range(lb // L):
        r0 = ci * L
        chunk_terms = terms[ci * N_HEADS:(ci + 1) * N_HEADS]
        q_dot_n = [jnp.sum(load_qk(r0, h)[0] * n_state[h], axis=-1, keepdims=True) for h in range(N_HEADS)]
        nums = []
        for h, t in enumerate(chunk_terms):
            rhs = jnp.concatenate([p_ref[r0:r0 + L, v_cols(h)], c_state[h].astype(BF16)], axis=0)
            nums.append(jnp.dot(t["lhs"], rhs, preferred_element_type=F32))
        for h, t in enumerate(chunk_terms):
            c_state[h] = c_state[h] * t["keep"][:, 0:1] + lax.dot_general(
                t["wk"], p_ref[r0:r0 + L, v_cols(h)], (((0,), (0,)), ((), ())), preferred_element_type=F32)
            n_state[h] = n_state[h] * t["keep"] + t["wk_sum"]
        for h, t in enumerate(chunk_terms):
            og = p_ref[r0:r0 + L, V_ALL + v_cols(h).start:V_ALL + v_cols(h).stop].astype(F32)
            den = t["s_sum"] + q_dot_n[h] * t["w_inter"]
            inv = 1.0 / jnp.maximum(jnp.abs(den), t["floor"])
            hh = nums[h] * jnp.concatenate([inv] * (V_DIM // LANES), axis=1)
            y = _rms(hh * jax.nn.sigmoid(og)) * gn_ref[:, h * V_DIM:(h + 1) * V_DIM]
            o_ref[r0:r0 + L, h * V_DIM:(h + 1) * V_DIM] = y.astype(o_ref.dtype)

    for h in range(N_HEADS):
        c_scr[h] = c_state[h]
        n_scr[h:h + 1, :] = n_state[h]
        m_scr[h:h + 1, :] = m_state[h]


def _mlstm(proj3, gates3, conv_w, conv_b, gate_bias, gn_w):
    b, s, _ = proj3.shape
    lb = CHUNK * min(MLSTM_CHUNKS_PER_STEP, s // CHUNK)
    return pl.pallas_call(
        _mlstm_kernel,
        grid=(b, s // lb),
        in_specs=[
            pl.BlockSpec((None, lb, MIX_COLS), lambda i, c: (i, c, 1)),
            pl.BlockSpec((None, lb, LANES), lambda i, c: (i, c, 0)),
            pl.BlockSpec((CONV_WIDTH, 2 * QK_ALL), lambda i, c: (0, 0)),
            pl.BlockSpec((1, 2 * QK_ALL), lambda i, c: (0, 0)),
            pl.BlockSpec((1, LANES), lambda i, c: (0, 0)),
            pl.BlockSpec((1, V_ALL), lambda i, c: (0, 0)),
        ],
        out_specs=pl.BlockSpec((None, lb, V_ALL), lambda i, c: (i, c, 0)),
        out_shape=jax.ShapeDtypeStruct((b, s, V_ALL), BF16),
        scratch_shapes=[
            pltpu.VMEM((N_HEADS, QK_DIM, V_DIM), F32),
            pltpu.VMEM((8, QK_DIM), F32),
            pltpu.VMEM((8, LANES), F32),
            pltpu.VMEM((8, 2 * QK_ALL), F32),
            pltpu.VMEM((lb, 2 * QK_ALL), F32),
        ],
        compiler_params=pltpu.CompilerParams(
            dimension_semantics=("arbitrary", "arbitrary"), vmem_limit_bytes=VMEM_LIMIT),
        name="mlstm",
    )(proj3, gates3, conv_w, conv_b, gate_bias, gn_w)


def _merge_route_kernel(ret_ref, hm_ref, gr_ref, gm_ref, x_ref, wr_ref, wm_ref, wo_ref, nw_ref,
                        wrt_ref, brt_ref, lower_ref, x1_ref, h2_ref, route_ref, route_t_ref, cnt_ref,
                        carry_scr, logits_scr):
    step = pl.program_id(0)

    @pl.when(step == 0)
    def _():
        carry_scr[...] = jnp.zeros_like(carry_scr)
        logits_scr[...] = jnp.zeros_like(logits_scr)

    logits = logits_scr[...]

    y_ret = jnp.dot(ret_ref[...], wr_ref[...], preferred_element_type=F32)
    y_m = jnp.dot(hm_ref[...], wm_ref[...], preferred_element_type=F32)
    merged = (jax.nn.sigmoid(gr_ref[...].astype(F32)) * y_ret
              + jax.nn.sigmoid(gm_ref[...].astype(F32)) * y_m)
    x1 = x_ref[...] + jnp.dot(merged.astype(BF16), wo_ref[...], preferred_element_type=F32)
    x1_ref[...] = x1
    h2 = _rms(x1) * nw_ref[...]
    h2_ref[...] = _pack_rows(h2)
    new_logits = jnp.dot(h2.astype(BF16), wrt_ref[...], preferred_element_type=F32) + brt_ref[...]

    live = jnp.where(step > 0, 1.0, 0.0)
    tm = logits.shape[0]
    lane = lax.broadcasted_iota(jnp.int32, (tm, LANES), 1)
    neg = -jnp.inf
    big = jnp.int32(LANES)
    is_group = (lane >= N_EXPERTS) & (lane < N_EXPERTS + N_GROUPS)
    gl = jnp.where(is_group, logits, neg)
    g_max = jnp.max(gl, axis=-1, keepdims=True)
    g_idx = jnp.min(jnp.where(gl == g_max, lane, big), axis=-1, keepdims=True) - N_EXPERTS
    g_w = 1.0 / jnp.sum(jnp.exp(gl - g_max), axis=-1, keepdims=True)
    in_group = (lane >= g_idx * EXPERTS_PER_GROUP) & (lane < (g_idx + 1) * EXPERTS_PER_GROUP)
    el = jnp.where(in_group, logits, neg)
    l1 = jnp.max(el, axis=-1, keepdims=True)
    e1 = jnp.min(jnp.where(el == l1, lane, big), axis=-1, keepdims=True)
    el2 = jnp.where(lane == e1, neg, el)
    l2 = jnp.max(el2, axis=-1, keepdims=True)
    e2 = jnp.min(jnp.where(el2 == l2, lane, big), axis=-1, keepdims=True)
    t21 = jnp.exp(l2 - l1)
    w1 = g_w / (1.0 + t21)
    w2 = g_w * t21 / (1.0 + t21)

    hit1 = lane == e1
    hit2 = lane == e2
    cnt = jnp.where(hit1 | hit2, live, 0.0)
    before = jnp.dot(lower_ref[...], cnt.astype(BF16), preferred_element_type=F32) + carry_scr[...]
    r1 = jnp.sum(jnp.where(hit1, before, 0.0), axis=-1, keepdims=True)
    r2 = jnp.sum(jnp.where(hit2, before, 0.0), axis=-1, keepdims=True)
    carry = carry_scr[...] + jnp.sum(cnt, axis=0, keepdims=True)
    carry_scr[...] = carry
    cnt_ref[...] = carry

    fields = (e1.astype(F32), e2.astype(F32), r1, r2, w1, w2)
    packed = jnp.zeros((tm, LANES), F32)
    for idx, val in enumerate(fields):
        packed = jnp.where(lane == idx, val, packed)
    route_ref[...] = packed
    route_t_ref[...] = jnp.transpose(packed)[0:ROUTE_ROWS, :]
    logits_scr[...] = new_logits


def _merge_route(ret, hm, proj, x2d, w_ret, w_m, w_out, norm_w, w_router, b_router, lower):
    t = x2d.shape[0]
    tm = min(ROWS_MERGE, t)
    n_tiles = t // tm
    gate_r_blk = 2 * MIX_COLS // D_MODEL
    tile = lambda i: jnp.minimum(i, n_tiles - 1)
    routed = lambda i: jnp.maximum(i - 1, 0)
    row_blk = lambda i: (tile(i), 0)
    const = lambda i: (0, 0)
    return pl.pallas_call(
        _merge_route_kernel,
        grid=(n_tiles + 1,),
        in_specs=[
            pl.BlockSpec((tm, V_ALL), row_blk),
            pl.BlockSpec((tm, V_ALL), row_blk),
            pl.BlockSpec((tm, D_MODEL), lambda i: (tile(i), gate_r_blk)),
            pl.BlockSpec((tm, D_MODEL), lambda i: (tile(i), gate_r_blk + 1)),
            pl.BlockSpec((tm, D_MODEL), row_blk),
            pl.BlockSpec((V_ALL, D_MODEL), const),
            pl.BlockSpec((V_ALL, D_MODEL), const),
            pl.BlockSpec((D_MODEL, D_MODEL), const),
            pl.BlockSpec((1, D_MODEL), const),
            pl.BlockSpec((D_MODEL, LANES), const),
            pl.BlockSpec((1, LANES), const),
            pl.BlockSpec((tm, tm), const),
        ],
        out_specs=[
            pl.BlockSpec((tm, D_MODEL), row_blk),
            pl.BlockSpec((tm, PACKED), row_blk),
            pl.BlockSpec((tm, LANES), lambda i: (routed(i), 0)),
            pl.BlockSpec((ROUTE_ROWS, tm), lambda i: (0, routed(i))),
            pl.BlockSpec((1, LANES), const),
        ],
        out_shape=[
            jax.ShapeDtypeStruct((t, D_MODEL), F32),
            jax.ShapeDtypeStruct((t, PACKED), U32),
            jax.ShapeDtypeStruct((t, LANES), F32),
            jax.ShapeDtypeStruct((ROUTE_ROWS, t), F32),
            jax.ShapeDtypeStruct((1, LANES), F32),
        ],
        scratch_shapes=[pltpu.VMEM((1, LANES), F32), pltpu.VMEM((tm, LANES), F32)],
        compiler_params=pltpu.CompilerParams(
            dimension_semantics=("arbitrary",), vmem_limit_bytes=VMEM_LIMIT),
        name="merge_route",
    )(ret, hm, proj, proj, x2d, w_ret, w_m, w_out, norm_w, w_router, b_router, lower)


SUBLANES = 8


def _for_row_groups(n_rows, body):
    def step(g, carry):
        for sub in range(SUBLANES):
            body(g, sub)
        return carry

    lax.fori_loop(0, n_rows // SUBLANES, step, 0)


def _dispatch_kernel(fill_ref, dest_a_ref, dest_b_ref, h2_ref, xs_hbm, zero_scr, sem):
    tm = h2_ref.shape[0] * SUBLANES

    @pl.when(pl.program_id(0) == 0)
    def _():
        zero_scr[...] = jnp.zeros_like(zero_scr)

        def fill_copy(j):
            row0 = pl.multiple_of(fill_ref[j], MOE_ROWS)
            return pltpu.make_async_copy(zero_scr, xs_hbm.at[pl.ds(row0, MOE_ROWS)], sem)

        def start_fill(j, carry):
            @pl.when(fill_ref[j] >= 0)
            def _():
                fill_copy(j).start()
            return carry

        def wait_fill(j, carry):
            @pl.when(fill_ref[j] >= 0)
            def _():
                fill_copy(j).wait()
            return carry

        lax.fori_loop(0, fill_ref.shape[0], start_fill, 0)
        lax.fori_loop(0, fill_ref.shape[0], wait_fill, 0)

    def start(g, sub):
        src = h2_ref.at[g, pl.ds(sub, 1)]
        r = g * SUBLANES + sub
        pltpu.make_async_copy(src, xs_hbm.at[pl.ds(dest_a_ref[r], 1)], sem).start()
        pltpu.make_async_copy(src, xs_hbm.at[pl.ds(dest_b_ref[r], 1)], sem).start()

    _for_row_groups(tm, start)
    for _ in range(2):
        pltpu.make_async_copy(xs_hbm.at[pl.ds(0, tm)], xs_hbm.at[pl.ds(0, tm)], sem).wait()


def _dispatch(fill_rows, dest_a, dest_b, h2p, n_slots):
    t = h2p.shape[0]
    tm = min(ROWS_DISPATCH, t)
    return pl.pallas_call(
        _dispatch_kernel,
        grid=(t // tm,),
        in_specs=[
            pl.BlockSpec(memory_space=pltpu.SMEM),
            pl.BlockSpec((tm,), lambda i: (i,), memory_space=pltpu.SMEM),
            pl.BlockSpec((tm,), lambda i: (i,), memory_space=pltpu.SMEM),
            pl.BlockSpec((tm // SUBLANES, SUBLANES, PACKED), lambda i: (i, 0, 0)),
        ],
        out_specs=pl.BlockSpec(memory_space=pl.ANY),
        out_shape=jax.ShapeDtypeStruct((n_slots, PACKED), U32),
        scratch_shapes=[pltpu.VMEM((MOE_ROWS, PACKED), U32), pltpu.SemaphoreType.DMA(())],
        compiler_params=pltpu.CompilerParams(
            dimension_semantics=("arbitrary",), vmem_limit_bytes=VMEM_LIMIT),
        name="dispatch",
    )(fill_rows, dest_a, dest_b, h2p.reshape(t // SUBLANES, SUBLANES, PACKED))


def _experts_kernel(blk_e_ref, blk_valid_ref, xs_ref, wgu_ref, wd_ref, ys_ref):
    del blk_e_ref
    valid = blk_valid_ref[pl.program_id(0)]

    @pl.when(valid > 0)
    def _():
        xb = _unpack_rows(xs_ref[...]).astype(BF16)
        gu = jnp.dot(xb, wgu_ref[...], preferred_element_type=F32)
        g = gu[:, :D_EXPERT]
        act = (g * jax.nn.sigmoid(g) * gu[:, D_EXPERT:]).astype(BF16)
        ys_ref[...] = _pack_rows(jnp.dot(act, wd_ref[...], preferred_element_type=F32))

    @pl.when(valid <= 0)
    def _():
        ys_ref[...] = jnp.zeros_like(ys_ref)


def _experts(blk_e, blk_valid, xs, w_gate_up, w_down):
    p = xs.shape[0]
    grid_spec = pltpu.PrefetchScalarGridSpec(
        num_scalar_prefetch=2,
        grid=(p // MOE_ROWS,),
        in_specs=[
            pl.BlockSpec((MOE_ROWS, PACKED), lambda i, be, bv: (i, 0)),
            pl.BlockSpec((None, D_MODEL, 2 * D_EXPERT), lambda i, be, bv: (be[i], 0, 0)),
            pl.BlockSpec((None, D_EXPERT, D_MODEL), lambda i, be, bv: (be[i], 0, 0)),
        ],
        out_specs=pl.BlockSpec((MOE_ROWS, PACKED), lambda i, be, bv: (i, 0)),
    )
    return pl.pallas_call(
        _experts_kernel,
        grid_spec=grid_spec,
        out_shape=jax.ShapeDtypeStruct((p, PACKED), U32),
        compiler_params=pltpu.CompilerParams(
            dimension_semantics=("arbitrary",), vmem_limit_bytes=VMEM_LIMIT),
        name="experts",
    )(blk_e, blk_valid, xs, w_gate_up, w_down)


def _combine_kernel(dest_a_ref, dest_b_ref, route_ref, x1_ref, nw_ref, ys_hbm, o_ref, buf_a, buf_b, sem):
    tm = x1_ref.shape[0]

    def start(g, sub):
        r = g * SUBLANES + sub
        pltpu.make_async_copy(ys_hbm.at[pl.ds(dest_a_ref[r], 1)], buf_a.at[g, pl.ds(sub, 1)], sem).start()
        pltpu.make_async_copy(ys_hbm.at[pl.ds(dest_b_ref[r], 1)], buf_b.at[g, pl.ds(sub, 1)], sem).start()

    _for_row_groups(tm, start)
    for _ in range(2):
        pltpu.make_async_copy(ys_hbm.at[pl.ds(0, tm)], ys_hbm.at[pl.ds(0, tm)], sem).wait()

    w1 = route_ref[:, 4:5]
    w2 = route_ref[:, 5:6]
    ya = _unpack_rows(buf_a[...].reshape(tm, PACKED))
    yb = _unpack_rows(buf_b[...].reshape(tm, PACKED))
    x2 = x1_ref[...] + w1 * ya + w2 * yb
    o_ref[...] = _rms(x2) * nw_ref[...]


def _combine(dest_a, dest_b, route, x1, norm_w, ys):
    t = x1.shape[0]
    tm = min(ROWS_COMBINE, t)
    return pl.pallas_call(
        _combine_kernel,
        grid=(t // tm,),
        in_specs=[
            pl.BlockSpec((tm,), lambda i: (i,), memory_space=pltpu.SMEM),
            pl.BlockSpec((tm,), lambda i: (i,), memory_space=pltpu.SMEM),
            pl.BlockSpec((tm, LANES), lambda i: (i, 0)),
            pl.BlockSpec((tm, D_MODEL), lambda i: (i, 0)),
            pl.BlockSpec((1, D_MODEL), lambda i: (0, 0)),
            pl.BlockSpec(memory_space=pl.ANY),
        ],
        out_specs=pl.BlockSpec((tm, D_MODEL), lambda i: (i, 0)),
        out_shape=jax.ShapeDtypeStruct((t, D_MODEL), F32),
        scratch_shapes=[
            pltpu.VMEM((tm // SUBLANES, SUBLANES, PACKED), U32),
            pltpu.VMEM((tm // SUBLANES, SUBLANES, PACKED), U32),
            pltpu.SemaphoreType.DMA(()),
        ],
        compiler_params=pltpu.CompilerParams(
            dimension_semantics=("arbitrary",), vmem_limit_bytes=VMEM_LIMIT),
        name="combine",
    )(dest_a, dest_b, route, x1, norm_w, ys)


def _rotary_tables(seq):
    inv_freq = 1.0 / (ROPE_BASE ** (jnp.arange(0, QK_DIM, 2, dtype=F32) / QK_DIM))
    ang = jnp.arange(seq, dtype=F32)[:, None] * inv_freq[None, :]
    cos, sin = jnp.cos(ang), jnp.sin(ang)
    return jnp.concatenate([cos, cos], axis=1), jnp.concatenate([-sin, sin], axis=1)


def _retention_decay_tables():
    gamma = 1.0 - 2.0 ** (-5.0 - np.arange(N_HEADS, dtype=np.float64))
    idx = np.arange(CHUNK, dtype=np.float64) + 1.0
    dq = gamma[:, None] ** idx[None, :]
    dk = gamma[:, None] ** (-idx[None, :]) * QK_DIM ** -0.5
    bcast = lambda a: jnp.asarray(np.broadcast_to(a[:, :, None], (N_HEADS, CHUNK, QK_DIM)), F32)
    return bcast(dq), bcast(dk), tuple(float(g) for g in gamma ** CHUNK)


def kernel(x, norm_mix_w, w_in, ret_gn_w, w_ret_branch, mlstm_conv_w, mlstm_conv_b, b_igate, b_fgate,
           mlstm_gn_w, w_mlstm_branch, w_out, norm_ffn_w, w_group, b_group, w_expert_router,
           b_expert_router, w_gate, w_up, w_down, norm_final_w):
    assert norm_mix_w.shape[0] == 1, "one layer"
    b, s, d = x.shape
    t = b * s
    assert d == D_MODEL and s % CHUNK == 0

    wi = w_in[0]
    n_pre = 2 * MIX_COLS
    w_big = jnp.concatenate([wi[:, :n_pre], wi[:, n_pre + 2 * N_HEADS:]], axis=1).astype(BF16)
    w_if = jnp.pad(wi[:, n_pre:n_pre + 2 * N_HEADS], ((0, 0), (0, LANES - 2 * N_HEADS))).astype(BF16)
    gate_bias = jnp.pad(jnp.concatenate([b_igate[0], b_fgate[0]]), (0, LANES - 2 * N_HEADS))[None, :]
    w_router = jnp.pad(jnp.concatenate([w_expert_router[0], w_group[0]], axis=1),
                       ((0, 0), (0, LANES - N_EXPERTS - N_GROUPS))).astype(BF16)
    b_router = jnp.pad(jnp.concatenate([b_expert_router[0], b_group[0]]),
                       (0, LANES - N_EXPERTS - N_GROUPS))[None, :]
    w_gate_up = jnp.concatenate([w_gate[0], w_up[0]], axis=2).astype(BF16)
    w_dn = w_down[0].astype(BF16)

    cosf, sinf = _rotary_tables(s)
    dq, dk, chunk_decay = _retention_decay_tables()
    tm_merge = min(ROWS_MERGE, t)
    lower = jnp.tril(jnp.ones((tm_merge, tm_merge), F32), -1).astype(BF16)

    x2d = x.reshape(t, d)
    proj, gates = _in_projection(x2d, norm_mix_w, w_big, w_if)
    proj3 = proj.reshape(b, s, N_BIG)
    ret = _retention(proj3, cosf, sinf, dq, dk, ret_gn_w, chunk_decay)
    hm = _mlstm(proj3, gates.reshape(b, s, LANES), mlstm_conv_w[0, :, 0, :], mlstm_conv_b,
                gate_bias, mlstm_gn_w)
    x1, h2p, route, route_t, counts = _merge_route(
        ret.reshape(t, V_ALL), hm.reshape(t, V_ALL), proj, x2d, w_ret_branch[0].astype(BF16),
        w_mlstm_branch[0].astype(BF16), w_out[0].astype(BF16), norm_ffn_w, w_router, b_router, lower)

    n_slots = 2 * t + N_EXPERTS * MOE_ROWS
    cnt = counts[0, :N_EXPERTS].astype(jnp.int32)
    padded = (cnt + MOE_ROWS - 1) // MOE_ROWS * MOE_ROWS
    pend = jnp.cumsum(padded)
    pstart = pend - padded
    eid = route_t[0:2].astype(jnp.int32)
    experts = jnp.arange(N_EXPERTS, dtype=jnp.int32)[:, None, None]
    slot0 = jnp.sum(jnp.where(eid[None] == experts, pstart[:, None, None], 0), axis=0)
    dest = slot0 + route_t[2:4].astype(jnp.int32)
    blk_start = jnp.arange(n_slots // MOE_ROWS, dtype=jnp.int32) * MOE_ROWS
    blk_e = jnp.minimum(jnp.sum(blk_start[:, None] >= pend[None, :], axis=-1), N_EXPERTS - 1).astype(jnp.int32)
    blk_valid = jnp.clip(pstart[blk_e] + cnt[blk_e] - blk_start, 0, MOE_ROWS).astype(jnp.int32)
    tail = pend[-1] + jnp.arange(N_EXPERTS, dtype=jnp.int32) * MOE_ROWS
    fill_rows = jnp.concatenate([jnp.where(padded > 0, pend - MOE_ROWS, -1),
                                 jnp.where(tail < n_slots, tail, -1)]).astype(jnp.int32)

    xs = _dispatch(fill_rows, dest[0], dest[1], h2p, n_slots)
    ys = _experts(blk_e, blk_valid, xs, w_gate_up, w_dn)
    out = _combine(dest[0], dest[1], route, x1, norm_final_w[None, :], ys)
    return out.reshape(b, s, d)
```

```python
import functools

import numpy as np
import jax
import jax.numpy as jnp
from jax import lax
from jax.experimental import pallas as pl
from jax.experimental.pallas import tpu as pltpu

F32 = jnp.float32
BF16 = jnp.bfloat16
U32 = jnp.uint32

D_MODEL = 1024
N_HEADS = 4
QK_DIM = 128
V_DIM = 256
CHUNK = 128
CONV_WIDTH = 4
ROPE_BASE = 10000.0
N_GROUPS = 4
EXPERTS_PER_GROUP = 8
N_EXPERTS = N_GROUPS * EXPERTS_PER_GROUP
D_EXPERT = 512
NORM_EPS = 1e-6
QK_ALL = N_HEADS * QK_DIM
V_ALL = N_HEADS * V_DIM

MIX_COLS = 2 * QK_ALL + 2 * V_ALL
N_BIG = 2 * MIX_COLS + 2 * D_MODEL
LANES = 128
PACKED = D_MODEL // 2

ROWS_PROJ = 2048
COLS_PROJ = 1024
RET_CHUNKS_PER_STEP = 4
MLSTM_CHUNKS_PER_STEP = 2
ROWS_MERGE = 512
ROWS_DISPATCH = 2048
ROWS_COMBINE = 512
COMBINE_ROWS_PER_ITER = 32
ROUTE_ROWS = 8
MOE_ROWS = 512
VMEM_LIMIT = 56 * 1024 * 1024


def _rms(x, eps=NORM_EPS):
    return x * lax.rsqrt(jnp.mean(x * x, axis=-1, keepdims=True) + eps)


def _pack_rows(x):
    lo = lax.bitcast_convert_type(x[:, :PACKED].astype(BF16).astype(F32), U32)
    hi = lax.bitcast_convert_type(x[:, PACKED:].astype(BF16).astype(F32), U32)
    return (hi & jnp.uint32(0xFFFF0000)) | (lo >> 16)


def _unpack_rows(w):
    lo = lax.bitcast_convert_type(w << 16, F32)
    hi = lax.bitcast_convert_type(w & jnp.uint32(0xFFFF0000), F32)
    return jnp.concatenate([lo, hi], axis=1)


def _in_proj_kernel(x_ref, nw_ref, w_ref, wif_ref, o_ref, gates_ref, h_scr):
    @pl.when(pl.program_id(1) == 0)
    def _():
        h = (_rms(x_ref[...]) * nw_ref[...]).astype(BF16)
        h_scr[...] = h
        gates_ref[...] = jnp.dot(h, wif_ref[...], preferred_element_type=F32)

    o_ref[...] = jnp.dot(h_scr[...], w_ref[...], preferred_element_type=F32).astype(o_ref.dtype)


def _in_projection(x2d, norm_w, w_big, w_if):
    t = x2d.shape[0]
    tm = min(ROWS_PROJ, t)
    tn = COLS_PROJ
    return pl.pallas_call(
        _in_proj_kernel,
        grid=(t // tm, N_BIG // tn),
        in_specs=[
            pl.BlockSpec((tm, D_MODEL), lambda i, j: (i, 0)),
            pl.BlockSpec((1, D_MODEL), lambda i, j: (0, 0)),
            pl.BlockSpec((D_MODEL, tn), lambda i, j: (0, j)),
            pl.BlockSpec((D_MODEL, LANES), lambda i, j: (0, 0)),
        ],
        out_specs=[
            pl.BlockSpec((tm, tn), lambda i, j: (i, j)),
            pl.BlockSpec((tm, LANES), lambda i, j: (i, 0)),
        ],
        out_shape=[
            jax.ShapeDtypeStruct((t, N_BIG), BF16),
            jax.ShapeDtypeStruct((t, LANES), F32),
        ],
        scratch_shapes=[pltpu.VMEM((tm, D_MODEL), BF16)],
        compiler_params=pltpu.CompilerParams(
            dimension_semantics=("arbitrary", "arbitrary"), vmem_limit_bytes=VMEM_LIMIT),
        name="in_projection",
    )(x2d, norm_w, w_big, w_if)


def _retention_kernel(p_ref, cos_ref, sin_ref, dq_ref, dk_ref, gn_ref, o_ref, state_scr, *, chunk_decay):
    L = CHUNK

    @pl.when(pl.program_id(1) == 0)
    def _():
        state_scr[...] = jnp.zeros_like(state_scr)

    row = lax.broadcasted_iota(jnp.int32, (L, L), 0)
    col = lax.broadcasted_iota(jnp.int32, (L, L), 1)
    causal = row >= col
    n_chunks = p_ref.shape[0] // L
    units = [(ci * L, h) for ci in range(n_chunks) for h in range(N_HEADS)]

    qts, kts, scores = [], [], []
    for r0, h in units:
        cosf = cos_ref[r0:r0 + L, :]
        sinf = sin_ref[r0:r0 + L, :]
        q = p_ref[r0:r0 + L, h * QK_DIM:(h + 1) * QK_DIM].astype(F32)
        k = p_ref[r0:r0 + L, QK_ALL + h * QK_DIM:QK_ALL + (h + 1) * QK_DIM].astype(F32)
        qt = ((q * cosf + pltpu.roll(q, QK_DIM // 2, 1) * sinf) * dq_ref[h]).astype(BF16)
        kt = ((k * cosf + pltpu.roll(k, QK_DIM // 2, 1) * sinf) * dk_ref[h]).astype(BF16)
        s = lax.dot_general(qt, kt, (((1,), (1,)), ((), ())), preferred_element_type=F32)
        qts.append(qt)
        kts.append(kt)
        scores.append(jnp.where(causal, s, 0.0).astype(BF16))

    states = [state_scr[h] for h in range(N_HEADS)]
    for (r0, h), qt, kt, s in zip(units, qts, kts, scores):
        v = p_ref[r0:r0 + L, 2 * QK_ALL + h * V_DIM:2 * QK_ALL + (h + 1) * V_DIM]
        g = p_ref[r0:r0 + L, 2 * QK_ALL + V_ALL + h * V_DIM:2 * QK_ALL + V_ALL + (h + 1) * V_DIM].astype(F32)
        lhs = jnp.concatenate([s, qt], axis=1)
        rhs = jnp.concatenate([v, states[h].astype(BF16)], axis=0)
        o = jnp.dot(lhs, rhs, preferred_element_type=F32)
        kv = lax.dot_general(kt, v, (((0,), (0,)), ((), ())), preferred_element_type=F32)
        states[h] = (states[h] + kv) * chunk_decay[h]
        y = _rms(o) * gn_ref[:, h * V_DIM:(h + 1) * V_DIM] * (g * jax.nn.sigmoid(g))
        o_ref[r0:r0 + L, h * V_DIM:(h + 1) * V_DIM] = y.astype(o_ref.dtype)
    for h in range(N_HEADS):
        state_scr[h] = states[h]


def _retention(proj3, cosf, sinf, dq, dk, gn_w, chunk_decay):
    b, s, _ = proj3.shape
    lb = CHUNK * min(RET_CHUNKS_PER_STEP, s // CHUNK)
    return pl.pallas_call(
        functools.partial(_retention_kernel, chunk_decay=chunk_decay),
        grid=(b, s // lb),
        in_specs=[
            pl.BlockSpec((None, lb, MIX_COLS), lambda i, c: (i, c, 0)),
            pl.BlockSpec((lb, QK_DIM), lambda i, c: (c, 0)),
            pl.BlockSpec((lb, QK_DIM), lambda i, c: (c, 0)),
            pl.BlockSpec((N_HEADS, CHUNK, QK_DIM), lambda i, c: (0, 0, 0)),
            pl.BlockSpec((N_HEADS, CHUNK, QK_DIM), lambda i, c: (0, 0, 0)),
            pl.BlockSpec((1, V_ALL), lambda i, c: (0, 0)),
        ],
        out_specs=pl.BlockSpec((None, lb, V_ALL), lambda i, c: (i, c, 0)),
        out_shape=jax.ShapeDtypeStruct((b, s, V_ALL), BF16),
        scratch_shapes=[pltpu.VMEM((N_HEADS, QK_DIM, V_DIM), F32)],
        compiler_params=pltpu.CompilerParams(
            dimension_semantics=("arbitrary", "arbitrary"), vmem_limit_bytes=VMEM_LIMIT),
        name="retention",
    )(proj3, cosf, sinf, dq, dk, gn_w)


def _mlstm_kernel(p_ref, gates_ref, cw_ref, cb_ref, gb_ref, gn_ref, o_ref,
                  c_scr, n_scr, m_scr, tail_scr, act_scr):
    L = CHUNK
    lb = p_ref.shape[0]

    @pl.when(pl.program_id(1) == 0)
    def _():
        c_scr[...] = jnp.zeros_like(c_scr)
        n_scr[...] = jnp.zeros_like(n_scr)
        m_scr[...] = jnp.zeros_like(m_scr)
        tail_scr[...] = jnp.zeros_like(tail_scr)

    u = p_ref[:, 0:2 * QK_ALL].astype(F32)
    ext = jnp.concatenate([tail_scr[...], u], axis=0)
    acc = u * cw_ref[CONV_WIDTH - 1:CONV_WIDTH, :] + cb_ref[...]
    for d in range(1, CONV_WIDTH):
        acc = acc + ext[8 - d:8 - d + lb, :] * cw_ref[CONV_WIDTH - 1 - d:CONV_WIDTH - d, :]
    tail_scr[...] = u[lb - 8:lb, :]
    act_scr[...] = acc * jax.nn.sigmoid(acc)

    row = lax.broadcasted_iota(jnp.int32, (L, L), 0)
    col = lax.broadcasted_iota(jnp.int32, (L, L), 1)
    causal = row >= col
    k_scale = QK_DIM ** -0.5
    units = [(ci * L, h) for ci in range(lb // L) for h in range(N_HEADS)]

    def load_qk(r0, h):
        q = act_scr[r0:r0 + L, h * QK_DIM:(h + 1) * QK_DIM]
        k = act_scr[r0:r0 + L, QK_ALL + h * QK_DIM:QK_ALL + (h + 1) * QK_DIM] * k_scale
        return q, k

    gate_terms, src_rows = [], []
    lane_t = lax.broadcasted_iota(jnp.int32, (2 * N_HEADS, L), 1)
    for ci in range(lb // L):
        pre = gates_ref[ci * L:(ci + 1) * L, :] + gb_ref[...]
        pre_rows = jnp.transpose(pre)[0:2 * N_HEADS, :]
        b_rows = jnp.minimum(pre_rows, 0.0) - jnp.log1p(jnp.exp(-jnp.abs(pre_rows)))
        shift = 1
        while shift < L:
            b_rows = b_rows + jnp.where(lane_t >= shift, pltpu.roll(b_rows, shift, 1), 0.0)
            shift *= 2
        src_rows.append(pre_rows[0:N_HEADS, :] - b_rows[N_HEADS:2 * N_HEADS, :])
        bcum = jnp.transpose(jnp.concatenate([b_rows, jnp.zeros((L - 2 * N_HEADS, L), F32)], axis=0))
        gate_terms.append((pre, bcum))
    scores = []
    for r0, h in units:
        q, k = load_qk(r0, h)
        scores.append(lax.dot_general(q.astype(BF16), k.astype(BF16), (((1,), (1,)), ((), ())),
                                      preferred_element_type=F32))

    m_state = [m_scr[h:h + 1, :] for h in range(N_HEADS)]
    terms = []
    for (r0, h), qk in zip(units, scores):
        pre, bcum = gate_terms[r0 // L]
        q, k = load_qk(r0, h)
        b_t = jnp.broadcast_to(bcum[:, N_HEADS + h:N_HEADS + h + 1], (L, L))
        i_t = jnp.broadcast_to(pre[:, h:h + 1], (L, L))
        src = jnp.broadcast_to(src_rows[r0 // L][h:h + 1, :], (L, L))
        m_prev = m_state[h]
        a = b_t + m_prev
        dmat = jnp.where(causal, b_t + src, -jnp.inf)
        m_t = jnp.maximum(a, jnp.max(dmat, axis=-1, keepdims=True))
        w_inter = jnp.exp(a - m_t)
        s = qk * jnp.exp(dmat - m_t)
        lhs = jnp.concatenate([s.astype(BF16), (q * w_inter).astype(BF16)], axis=1)
        b_last = b_t[L - 1:L, :]
        gk = b_last - b_t + i_t
        m_new = jnp.maximum(b_last + m_prev, jnp.max(gk, axis=0, keepdims=True))
        wk = k * jnp.exp(gk - m_new)
        m_state[h] = m_new
        terms.append(dict(
            lhs=lhs, s_sum=jnp.sum(s, axis=-1, keepdims=True), w_inter=w_inter, floor=jnp.exp(-m_t),
            keep=jnp.exp(b_last + m_prev - m_new), wk=wk.astype(BF16), wk_sum=jnp.sum(wk, axis=0, keepdims=True)))

    n_state = [n_scr[h:h + 1, :] for h in range(N_HEADS)]
    c_state = [c_scr[h] for h in range(N_HEADS)]
    v_cols = lambda h: slice(2 * QK_ALL + h * V_DIM, 2 * QK_ALL + (h + 1) * V_DIM)
    for ci in range(lb // L):
        r0 = ci * L
        chunk_terms = terms[ci * N_HEADS:(ci + 1) * N_HEADS]
        q_dot_n = [jnp.sum(load_qk(r0, h)[0] * n_state[h], axis=-1, keepdims=True) for h in range(N_HEADS)]
        nums = []
        for h, t in enumerate(chunk_terms):
            rhs = jnp.concatenate([p_ref[r0:r0 + L, v_cols(h)], c_state[h].astype(BF16)], axis=0)
            nums.append(jnp.dot(t["lhs"], rhs, preferred_element_type=F32))
        for h, t in enumerate(chunk_terms):
            c_state[h] = c_state[h] * t["keep"][:, 0:1] + lax.dot_general(
                t["wk"], p_ref[r0:r0 + L, v_cols(h)], (((0,), (0,)), ((), ())), preferred_element_type=F32)
            n_state[h] = n_state[h] * t["keep"] + t["wk_sum"]
        for h, t in enumerate(chunk_terms):
            og = p_ref[r0:r0 + L, V_ALL + v_cols(h).start:V_ALL + v_cols(h).stop].astype(F32)
            den = t["s_sum"] + q_dot_n[h] * t["w_inter"]
            inv = 1.0 / jnp.maximum(jnp.abs(den), t["floor"])
            hh = nums[h] * jnp.concatenate([inv] * (V_DIM // LANES), axis=1)
            y = _rms(hh * jax.nn.sigmoid(og)) * gn_ref[:, h * V_DIM:(h + 1) * V_DIM]
            o_ref[r0:r0 + L, h * V_DIM:(h + 1) * V_DIM] = y.astype(o_ref.dtype)

    for h in range(N_HEADS):
        c_scr[h] = c_state[h]
        n_scr[h:h + 1, :] = n_state[h]
        m_scr[h:h + 1, :] = m_state[h]


def _mlstm(proj3, gates3, conv_w, conv_b, gate_bias, gn_w):
    b, s, _ = proj3.shape
    lb = CHUNK * min(MLSTM_CHUNKS_PER_STEP, s // CHUNK)
    return pl.pallas_call(
        _mlstm_kernel,
        grid=(b, s // lb),
        in_specs=[
            pl.BlockSpec((None, lb, MIX_COLS), lambda i, c: (i, c, 1)),
            pl.BlockSpec((None, lb, LANES), lambda i, c: (i, c, 0)),
            pl.BlockSpec((CONV_WIDTH, 2 * QK_ALL), lambda i, c: (0, 0)),
            pl.BlockSpec((1, 2 * QK_ALL), lambda i, c: (0, 0)),
            pl.BlockSpec((1, LANES), lambda i, c: (0, 0)),
            pl.BlockSpec((1, V_ALL), lambda i, c: (0, 0)),
        ],
        out_specs=pl.BlockSpec((None, lb, V_ALL), lambda i, c: (i, c, 0)),
        out_shape=jax.ShapeDtypeStruct((b, s, V_ALL), BF16),
        scratch_shapes=[
            pltpu.VMEM((N_HEADS, QK_DIM, V_DIM), F32),
            pltpu.VMEM((8, QK_DIM), F32),
            pltpu.VMEM((8, LANES), F32),
            pltpu.VMEM((8, 2 * QK_ALL), F32),
            pltpu.VMEM((lb, 2 * QK_ALL), F32),
        ],
        compiler_params=pltpu.CompilerParams(
            dimension_semantics=("arbitrary", "arbitrary"), vmem_limit_bytes=VMEM_LIMIT),
        name="mlstm",
    )(proj3, gates3, conv_w, conv_b, gate_bias, gn_w)


def _merge_route_kernel(ret_ref, hm_ref, gr_ref, gm_ref, x_ref, wr_ref, wm_ref, wo_ref, nw_ref,
                        wrt_ref, brt_ref, lower_ref, x1_ref, h2_ref, route_ref, route_t_ref, cnt_ref,
                        carry_scr, logits_scr):
    step = pl.program_id(0)

    @pl.when(step == 0)
    def _():
        carry_scr[...] = jnp.zeros_like(carry_scr)
        logits_scr[...] = jnp.zeros_like(logits_scr)

    logits = logits_scr[...]

    y_ret = jnp.dot(ret_ref[...], wr_ref[...], preferred_element_type=F32)
    y_m = jnp.dot(hm_ref[...], wm_ref[...], preferred_element_type=F32)
    merged = (jax.nn.sigmoid(gr_ref[...].astype(F32)) * y_ret
              + jax.nn.sigmoid(gm_ref[...].astype(F32)) * y_m)
    x1 = x_ref[...] + jnp.dot(merged.astype(BF16), wo_ref[...], preferred_element_type=F32)
    x1_ref[...] = x1
    h2 = _rms(x1) * nw_ref[...]
    h2_ref[...] = _pack_rows(h2)
    new_logits = jnp.dot(h2.astype(BF16), wrt_ref[...], preferred_element_type=F32) + brt_ref[...]

    live = jnp.where(step > 0, 1.0, 0.0)
    tm = logits.shape[0]
    lane = lax.broadcasted_iota(jnp.int32, (tm, LANES), 1)
    neg = -jnp.inf
    big = jnp.int32(LANES)
    is_group = (lane >= N_EXPERTS) & (lane < N_EXPERTS + N_GROUPS)
    gl = jnp.where(is_group, logits, neg)
    g_max = jnp.max(gl, axis=-1, keepdims=True)
    g_idx = jnp.min(jnp.where(gl == g_max, lane, big), axis=-1, keepdims=True) - N_EXPERTS
    g_w = 1.0 / jnp.sum(jnp.exp(gl - g_max), axis=-1, keepdims=True)
    in_group = (lane >= g_idx * EXPERTS_PER_GROUP) & (lane < (g_idx + 1) * EXPERTS_PER_GROUP)
    el = jnp.where(in_group, logits, neg)
    l1 = jnp.max(el, axis=-1, keepdims=True)
    e1 = jnp.min(jnp.where(el == l1, lane, big), axis=-1, keepdims=True)
    el2 = jnp.where(lane == e1, neg, el)
    l2 = jnp.max(el2, axis=-1, keepdims=True)
    e2 = jnp.min(jnp.where(el2 == l2, lane, big), axis=-1, keepdims=True)
    t21 = jnp.exp(l2 - l1)
    w1 = g_w / (1.0 + t21)
    w2 = g_w * t21 / (1.0 + t21)

    hit1 = lane == e1
    hit2 = lane == e2
    cnt = jnp.where(hit1 | hit2, live, 0.0)
    before = jnp.dot(lower_ref[...], cnt.astype(BF16), preferred_element_type=F32) + carry_scr[...]
    r1 = jnp.sum(jnp.where(hit1, before, 0.0), axis=-1, keepdims=True)
    r2 = jnp.sum(jnp.where(hit2, before, 0.0), axis=-1, keepdims=True)
    carry = carry_scr[...] + jnp.sum(cnt, axis=0, keepdims=True)
    carry_scr[...] = carry
    cnt_ref[...] = carry

    fields = (e1.astype(F32), e2.astype(F32), r1, r2, w1, w2)
    packed = jnp.zeros((tm, LANES), F32)
    for idx, val in enumerate(fields):
        packed = jnp.where(lane == idx, val, packed)
    route_ref[...] = packed
    route_t_ref[...] = jnp.transpose(packed)[0:ROUTE_ROWS, :]
    logits_scr[...] = new_logits


def _merge_route(ret, hm, proj, x2d, w_ret, w_m, w_out, norm_w, w_router, b_router, lower):
    t = x2d.shape[0]
    tm = min(ROWS_MERGE, t)
    n_tiles = t // tm
    gate_r_blk = 2 * MIX_COLS // D_MODEL
    tile = lambda i: jnp.minimum(i, n_tiles - 1)
    routed = lambda i: jnp.maximum(i - 1, 0)
    row_blk = lambda i: (tile(i), 0)
    const = lambda i: (0, 0)
    return pl.pallas_call(
        _merge_route_kernel,
        grid=(n_tiles + 1,),
        in_specs=[
            pl.BlockSpec((tm, V_ALL), row_blk),
            pl.BlockSpec((tm, V_ALL), row_blk),
            pl.BlockSpec((tm, D_MODEL), lambda i: (tile(i), gate_r_blk)),
            pl.BlockSpec((tm, D_MODEL), lambda i: (tile(i), gate_r_blk + 1)),
            pl.BlockSpec((tm, D_MODEL), row_blk),
            pl.BlockSpec((V_ALL, D_MODEL), const),
            pl.BlockSpec((V_ALL, D_MODEL), const),
            pl.BlockSpec((D_MODEL, D_MODEL), const),
            pl.BlockSpec((1, D_MODEL), const),
            pl.BlockSpec((D_MODEL, LANES), const),
            pl.BlockSpec((1, LANES), const),
            pl.BlockSpec((tm, tm), const),
        ],
        out_specs=[
            pl.BlockSpec((tm, D_MODEL), row_blk),
            pl.BlockSpec((tm, PACKED), row_blk),
            pl.BlockSpec((tm, LANES), lambda i: (routed(i), 0)),
            pl.BlockSpec((ROUTE_ROWS, tm), lambda i: (0, routed(i))),
            pl.BlockSpec((1, LANES), const),
        ],
        out_shape=[
            jax.ShapeDtypeStruct((t, D_MODEL), F32),
            jax.ShapeDtypeStruct((t, PACKED), U32),
            jax.ShapeDtypeStruct((t, LANES), F32),
            jax.ShapeDtypeStruct((ROUTE_ROWS, t), F32),
            jax.ShapeDtypeStruct((1, LANES), F32),
        ],
        scratch_shapes=[pltpu.VMEM((1, LANES), F32), pltpu.VMEM((tm, LANES), F32)],
        compiler_params=pltpu.CompilerParams(
            dimension_semantics=("arbitrary",), vmem_limit_bytes=VMEM_LIMIT),
        name="merge_route",
    )(ret, hm, proj, proj, x2d, w_ret, w_m, w_out, norm_w, w_router, b_router, lower)


SUBLANES = 8


def _for_row_groups(n_rows, body):
    def step(g, carry):
        for sub in range(SUBLANES):
            body(g, sub)
        return carry

    lax.fori_loop(0, n_rows // SUBLANES, step, 0)


def _dispatch_kernel(fill_ref, dest_a_ref, dest_b_ref, h2_ref, xs_hbm, zero_scr, sem):
    tm = h2_ref.shape[0] * SUBLANES

    @pl.when(pl.program_id(0) == 0)
    def _():
        zero_scr[...] = jnp.zeros_like(zero_scr)

        def fill_copy(j):
            row0 = pl.multiple_of(fill_ref[j], MOE_ROWS)
            return pltpu.make_async_copy(zero_scr, xs_hbm.at[pl.ds(row0, MOE_ROWS)], sem)

        def start_fill(j, carry):
            @pl.when(fill_ref[j] >= 0)
            def _():
                fill_copy(j).start()
            return carry

        def wait_fill(j, carry):
            @pl.when(fill_ref[j] >= 0)
            def _():
                fill_copy(j).wait()
            return carry

        lax.fori_loop(0, fill_ref.shape[0], start_fill, 0)
        lax.fori_loop(0, fill_ref.shape[0], wait_fill, 0)

    def start(g, sub):
        src = h2_ref.at[g, pl.ds(sub, 1)]
        r = g * SUBLANES + sub
        pltpu.make_async_copy(src, xs_hbm.at[pl.ds(dest_a_ref[r], 1)], sem).start()
        pltpu.make_async_copy(src, xs_hbm.at[pl.ds(dest_b_ref[r], 1)], sem).start()

    _for_row_groups(tm, start)
    for _ in range(2):
        pltpu.make_async_copy(xs_hbm.at[pl.ds(0, tm)], xs_hbm.at[pl.ds(0, tm)], sem).wait()


def _dispatch(fill_rows, dest_a, dest_b, h2p, n_slots):
    t = h2p.shape[0]
    tm = min(ROWS_DISPATCH, t)
    return pl.pallas_call(
        _dispatch_kernel,
        grid=(t // tm,),
        in_specs=[
            pl.BlockSpec(memory_space=pltpu.SMEM),
            pl.BlockSpec((tm,), lambda i: (i,), memory_space=pltpu.SMEM),
            pl.BlockSpec((tm,), lambda i: (i,), memory_space=pltpu.SMEM),
            pl.BlockSpec((tm // SUBLANES, SUBLANES, PACKED), lambda i: (i, 0, 0)),
        ],
        out_specs=pl.BlockSpec(memory_space=pl.ANY),
        out_shape=jax.ShapeDtypeStruct((n_slots, PACKED), U32),
        scratch_shapes=[pltpu.VMEM((MOE_ROWS, PACKED), U32), pltpu.SemaphoreType.DMA(())],
        compiler_params=pltpu.CompilerParams(
            dimension_semantics=("arbitrary",), vmem_limit_bytes=VMEM_LIMIT),
        name="dispatch",
    )(fill_rows, dest_a, dest_b, h2p.reshape(t // SUBLANES, SUBLANES, PACKED))


def _experts_kernel(blk_e_ref, blk_valid_ref, xs_ref, wg_ref, wu_ref, wd_ref, ys_ref, wgu_scr, wd_scr):
    i = pl.program_id(0)
    valid = blk_valid_ref[i]

    @pl.when((i == 0) | (blk_e_ref[i] != blk_e_ref[jnp.maximum(i - 1, 0)]))
    def _():
        wgu_scr[:, :D_EXPERT] = wg_ref[...].astype(BF16)
        wgu_scr[:, D_EXPERT:] = wu_ref[...].astype(BF16)
        wd_scr[...] = wd_ref[...].astype(BF16)

    @pl.when(valid > 0)
    def _():
        xb = _unpack_rows(xs_ref[...]).astype(BF16)
        gu = jnp.dot(xb, wgu_scr[...], preferred_element_type=F32)
        g = gu[:, :D_EXPERT]
        act = (g * jax.nn.sigmoid(g) * gu[:, D_EXPERT:]).astype(BF16)
        ys_ref[...] = _pack_rows(jnp.dot(act, wd_scr[...], preferred_element_type=F32))

    @pl.when(valid <= 0)
    def _():
        ys_ref[...] = jnp.zeros_like(ys_ref)


def _experts(blk_e, blk_valid, xs, w_gate, w_up, w_down):
    p = xs.shape[0]
    per_expert = lambda i, be, bv: (be[i], 0, 0)
    grid_spec = pltpu.PrefetchScalarGridSpec(
        num_scalar_prefetch=2,
        grid=(p // MOE_ROWS,),
        in_specs=[
            pl.BlockSpec((MOE_ROWS, PACKED), lambda i, be, bv: (i, 0)),
            pl.BlockSpec((None, D_MODEL, D_EXPERT), per_expert),
            pl.BlockSpec((None, D_MODEL, D_EXPERT), per_expert),
            pl.BlockSpec((None, D_EXPERT, D_MODEL), per_expert),
        ],
        out_specs=pl.BlockSpec((MOE_ROWS, PACKED), lambda i, be, bv: (i, 0)),
        scratch_shapes=[pltpu.VMEM((D_MODEL, 2 * D_EXPERT), BF16), pltpu.VMEM((D_EXPERT, D_MODEL), BF16)],
    )
    return pl.pallas_call(
        _experts_kernel,
        grid_spec=grid_spec,
        out_shape=jax.ShapeDtypeStruct((p, PACKED), U32),
        compiler_params=pltpu.CompilerParams(
            dimension_semantics=("arbitrary",), vmem_limit_bytes=VMEM_LIMIT),
        name="experts",
    )(blk_e, blk_valid, xs, w_gate, w_up, w_down)


def _combine_kernel(dest_a_ref, dest_b_ref, route_ref, x1_ref, nw_ref, ys_hbm, o_ref, buf_a, buf_b, sems):
    step = pl.program_id(0)
    n_tiles = pl.num_programs(0) - 1
    tm = x1_ref.shape[0]
    slot = step % 2
    prev = 1 - slot
    groups = COMBINE_ROWS_PER_ITER // SUBLANES

    def issue(it):
        for gi in range(groups):
            g = it * groups + gi
            for sub in range(SUBLANES):
                r = g * SUBLANES + sub
                pltpu.make_async_copy(ys_hbm.at[pl.ds(dest_a_ref[r], 1)],
                                      buf_a.at[slot, g, pl.ds(sub, 1)], sems.at[slot]).start()
                pltpu.make_async_copy(ys_hbm.at[pl.ds(dest_b_ref[r], 1)],
                                      buf_b.at[slot, g, pl.ds(sub, 1)], sems.at[slot]).start()

    def rows_of(it):
        return pl.ds(pl.multiple_of(it * COMBINE_ROWS_PER_ITER, COMBINE_ROWS_PER_ITER), COMBINE_ROWS_PER_ITER)

    def finish(it):
        rows = rows_of(it)
        grp = pl.ds(it * groups, groups)
        ya = _unpack_rows(buf_a[prev, grp].reshape(COMBINE_ROWS_PER_ITER, PACKED))
        yb = _unpack_rows(buf_b[prev, grp].reshape(COMBINE_ROWS_PER_ITER, PACKED))
        x2 = x1_ref[rows, :] + route_ref[rows, 4:5] * ya + route_ref[rows, 5:6] * yb
        return _rms(x2) * nw_ref[...]

    def wait_prev():
        for _ in range(2):
            pltpu.make_async_copy(ys_hbm.at[pl.ds(0, tm)], ys_hbm.at[pl.ds(0, tm)], sems.at[prev]).wait()

    def loop(body):
        def step_fn(it, carry):
            body(it)
            return carry
        lax.fori_loop(0, tm // COMBINE_ROWS_PER_ITER, step_fn, 0)

    @pl.when(step == 0)
    def _():
        loop(issue)

    @pl.when((step > 0) & (step < n_tiles))
    def _():
        wait_prev()

        def both(it):
            y = finish(it)
            issue(it)
            o_ref[rows_of(it), :] = y
        loop(both)

    @pl.when(step == n_tiles)
    def _():
        wait_prev()

        def last(it):
            o_ref[rows_of(it), :] = finish(it)
        loop(last)


def _combine(dest_a, dest_b, route, x1, norm_w, ys):
    t = x1.shape[0]
    tm = min(ROWS_COMBINE, t)
    n_tiles = t // tm
    gathered = lambda i: (jnp.minimum(i, n_tiles - 1),)
    finished = lambda i: (jnp.maximum(i - 1, 0), 0)
    return pl.pallas_call(
        _combine_kernel,
        grid=(n_tiles + 1,),
        in_specs=[
            pl.BlockSpec((tm,), gathered, memory_space=pltpu.SMEM),
            pl.BlockSpec((tm,), gathered, memory_space=pltpu.SMEM),
            pl.BlockSpec((tm, LANES), finished),
            pl.BlockSpec((tm, D_MODEL), finished),
            pl.BlockSpec((1, D_MODEL), lambda i: (0, 0)),
            pl.BlockSpec(memory_space=pl.ANY),
        ],
        out_specs=pl.BlockSpec((tm, D_MODEL), finished),
        out_shape=jax.ShapeDtypeStruct((t, D_MODEL), F32),
        scratch_shapes=[
            pltpu.VMEM((2, tm // SUBLANES, SUBLANES, PACKED), U32),
            pltpu.VMEM((2, tm // SUBLANES, SUBLANES, PACKED), U32),
            pltpu.SemaphoreType.DMA((2,)),
        ],
        compiler_params=pltpu.CompilerParams(
            dimension_semantics=("arbitrary",), vmem_limit_bytes=VMEM_LIMIT),
        name="combine",
    )(dest_a, dest_b, route, x1, norm_w, ys)


def _rotary_tables(seq):
    inv_freq = 1.0 / (ROPE_BASE ** (jnp.arange(0, QK_DIM, 2, dtype=F32) / QK_DIM))
    ang = jnp.arange(seq, dtype=F32)[:, None] * inv_freq[None, :]
    cos, sin = jnp.cos(ang), jnp.sin(ang)
    return jnp.concatenate([cos, cos], axis=1), jnp.concatenate([-sin, sin], axis=1)


def _retention_decay_tables():
    gamma = 1.0 - 2.0 ** (-5.0 - np.arange(N_HEADS, dtype=np.float64))
    idx = np.arange(CHUNK, dtype=np.float64) + 1.0
    dq = gamma[:, None] ** idx[None, :]
    dk = gamma[:, None] ** (-idx[None, :]) * QK_DIM ** -0.5
    bcast = lambda a: jnp.asarray(np.broadcast_to(a[:, :, None], (N_HEADS, CHUNK, QK_DIM)), F32)
    return bcast(dq), bcast(dk), tuple(float(g) for g in gamma ** CHUNK)


def kernel(x, norm_mix_w, w_in, ret_gn_w, w_ret_branch, mlstm_conv_w, mlstm_conv_b, b_igate, b_fgate,
           mlstm_gn_w, w_mlstm_branch, w_out, norm_ffn_w, w_group, b_group, w_expert_router,
           b_expert_router, w_gate, w_up, w_down, norm_final_w):
    assert norm_mix_w.shape[0] == 1, "one layer"
    b, s, d = x.shape
    t = b * s
    assert d == D_MODEL and s % CHUNK == 0

    wi = w_in[0]
    n_pre = 2 * MIX_COLS
    w_big = jnp.concatenate([wi[:, :n_pre], wi[:, n_pre + 2 * N_HEADS:]], axis=1).astype(BF16)
    w_if = jnp.pad(wi[:, n_pre:n_pre + 2 * N_HEADS], ((0, 0), (0, LANES - 2 * N_HEADS))).astype(BF16)
    gate_bias = jnp.pad(jnp.concatenate([b_igate[0], b_fgate[0]]), (0, LANES - 2 * N_HEADS))[None, :]
    w_router = jnp.pad(jnp.concatenate([w_expert_router[0], w_group[0]], axis=1),
                       ((0, 0), (0, LANES - N_EXPERTS - N_GROUPS))).astype(BF16)
    b_router = jnp.pad(jnp.concatenate([b_expert_router[0], b_group[0]]),
                       (0, LANES - N_EXPERTS - N_GROUPS))[None, :]

    cosf, sinf = _rotary_tables(s)
    dq, dk, chunk_decay = _retention_decay_tables()
    tm_merge = min(ROWS_MERGE, t)
    lower = jnp.tril(jnp.ones((tm_merge, tm_merge), F32), -1).astype(BF16)

    x2d = x.reshape(t, d)
    proj, gates = _in_projection(x2d, norm_mix_w, w_big, w_if)
    proj3 = proj.reshape(b, s, N_BIG)
    ret = _retention(proj3, cosf, sinf, dq, dk, ret_gn_w, chunk_decay)
    hm = _mlstm(proj3, gates.reshape(b, s, LANES), mlstm_conv_w[0, :, 0, :], mlstm_conv_b,
                gate_bias, mlstm_gn_w)
    x1, h2p, route, route_t, counts = _merge_route(
        ret.reshape(t, V_ALL), hm.reshape(t, V_ALL), proj, x2d, w_ret_branch[0].astype(BF16),
        w_mlstm_branch[0].astype(BF16), w_out[0].astype(BF16), norm_ffn_w, w_router, b_router, lower)

    n_slots = 2 * t + N_EXPERTS * MOE_ROWS
    cnt = counts[0, :N_EXPERTS].astype(jnp.int32)
    padded = (cnt + MOE_ROWS - 1) // MOE_ROWS * MOE_ROWS
    expert_ids = jnp.arange(N_EXPERTS, dtype=jnp.int32)
    pstart = jnp.sum(jnp.where(expert_ids[None, :] < expert_ids[:, None], padded[None, :], 0), axis=1)
    pend = pstart + padded
    eid = route_t[0:2].astype(jnp.int32)
    slot0 = jnp.sum(jnp.where(eid[None] == expert_ids[:, None, None], pstart[:, None, None], 0), axis=0)
    dest = slot0 + route_t[2:4].astype(jnp.int32)
    blk_start = jnp.arange(n_slots // MOE_ROWS, dtype=jnp.int32) * MOE_ROWS
    blk_e = jnp.minimum(jnp.sum(blk_start[:, None] >= pend[None, :], axis=-1), N_EXPERTS - 1).astype(jnp.int32)
    blk_valid = jnp.clip(pstart[blk_e] + cnt[blk_e] - blk_start, 0, MOE_ROWS).astype(jnp.int32)
    tail = pend[-1] + expert_ids * MOE_ROWS
    fill_rows = jnp.concatenate([jnp.where(padded > 0, pend - MOE_ROWS, -1),
                                 jnp.where(tail < n_slots, tail, -1)]).astype(jnp.int32)

    xs = _dispatch(fill_rows, dest[0], dest[1], h2p, n_slots)
    ys = _experts(blk_e, blk_valid, xs, w_gate[0], w_up[0], w_down[0])
    out = _combine(dest[0], dest[1], route, x1, norm_final_w[None, :], ys)
    return out.reshape(b, s, d)
```

```python
import functools

import numpy as np
import jax
import jax.numpy as jnp
from jax import lax
from jax.experimental import pallas as pl
from jax.experimental.pallas import tpu as pltpu

F32 = jnp.float32
BF16 = jnp.bfloat16
U32 = jnp.uint32

D_MODEL = 1024
N_HEADS = 4
QK_DIM = 128
V_DIM = 256
CHUNK = 128
CONV_WIDTH = 4
ROPE_BASE = 10000.0
N_GROUPS = 4
EXPERTS_PER_GROUP = 8
N_EXPERTS = N_GROUPS * EXPERTS_PER_GROUP
D_EXPERT = 512
NORM_EPS = 1e-6
QK_ALL = N_HEADS * QK_DIM
V_ALL = N_HEADS * V_DIM

MIX_COLS = 2 * QK_ALL + 2 * V_ALL
N_BIG = 2 * MIX_COLS + 2 * D_MODEL
LANES = 128
PACKED = D_MODEL // 2

ROWS_PROJ = 2048
COLS_PROJ = 2048
RET_CHUNKS_PER_STEP = 4
MLSTM_CHUNKS_PER_STEP = 4
ROWS_MERGE = 512
ROWS_DISPATCH = 2048
ROWS_COMBINE = 512
COMBINE_ROWS_PER_ITER = 32
ROUTE_ROWS = 8
MOE_ROWS = 512
V7X_VMEM_BYTES = 64 * 1024 * 1024
VMEM_LIMIT = V7X_VMEM_BYTES - 8 * 1024 * 1024
VMEM_LIMIT_PROJ = V7X_VMEM_BYTES - 4 * 1024 * 1024


def _rms(x, eps=NORM_EPS):
    return x * lax.rsqrt(jnp.mean(x * x, axis=-1, keepdims=True) + eps)


def _pack_rows(x):
    lo = lax.bitcast_convert_type(x[:, :PACKED].astype(BF16).astype(F32), U32)
    hi = lax.bitcast_convert_type(x[:, PACKED:].astype(BF16).astype(F32), U32)
    return (hi & jnp.uint32(0xFFFF0000)) | (lo >> 16)


def _unpack_rows(w):
    lo = lax.bitcast_convert_type(w << 16, F32)
    hi = lax.bitcast_convert_type(w & jnp.uint32(0xFFFF0000), F32)
    return jnp.concatenate([lo, hi], axis=1)


def _in_proj_kernel(x_ref, nw_ref, w_ref, wif_ref, o_ref, gates_ref, h_scr):
    @pl.when(pl.program_id(1) == 0)
    def _():
        h = (_rms(x_ref[...]) * nw_ref[...]).astype(BF16)
        h_scr[...] = h
        gates_ref[...] = jnp.dot(h, wif_ref[...], preferred_element_type=F32)

    o_ref[...] = jnp.dot(h_scr[...], w_ref[...], preferred_element_type=F32).astype(o_ref.dtype)


def _in_projection(x2d, norm_w, w_big, w_if):
    t = x2d.shape[0]
    tm = min(ROWS_PROJ, t)
    tn = COLS_PROJ
    return pl.pallas_call(
        _in_proj_kernel,
        grid=(t // tm, N_BIG // tn),
        in_specs=[
            pl.BlockSpec((tm, D_MODEL), lambda i, j: (i, 0)),
            pl.BlockSpec((1, D_MODEL), lambda i, j: (0, 0)),
            pl.BlockSpec((D_MODEL, tn), lambda i, j: (0, j)),
            pl.BlockSpec((D_MODEL, LANES), lambda i, j: (0, 0)),
        ],
        out_specs=[
            pl.BlockSpec((tm, tn), lambda i, j: (i, j)),
            pl.BlockSpec((tm, LANES), lambda i, j: (i, 0)),
        ],
        out_shape=[
            jax.ShapeDtypeStruct((t, N_BIG), BF16),
            jax.ShapeDtypeStruct((t, LANES), F32),
        ],
        scratch_shapes=[pltpu.VMEM((tm, D_MODEL), BF16)],
        compiler_params=pltpu.CompilerParams(
            dimension_semantics=("arbitrary", "arbitrary"), vmem_limit_bytes=VMEM_LIMIT_PROJ),
        name="in_projection",
    )(x2d, norm_w, w_big, w_if)


def _retention_kernel(p_ref, cos_ref, sin_ref, dq_ref, dk_ref, gn_ref, o_ref, state_scr, *, chunk_decay):
    L = CHUNK

    @pl.when(pl.program_id(1) == 0)
    def _():
        state_scr[...] = jnp.zeros_like(state_scr)

    row = lax.broadcasted_iota(jnp.int32, (L, L), 0)
    col = lax.broadcasted_iota(jnp.int32, (L, L), 1)
    causal = row >= col
    n_chunks = p_ref.shape[0] // L
    units = [(ci * L, h) for ci in range(n_chunks) for h in range(N_HEADS)]

    qts, kts, scores = [], [], []
    for r0, h in units:
        cosf = cos_ref[r0:r0 + L, :]
        sinf = sin_ref[r0:r0 + L, :]
        q = p_ref[r0:r0 + L, h * QK_DIM:(h + 1) * QK_DIM].astype(F32)
        k = p_ref[r0:r0 + L, QK_ALL + h * QK_DIM:QK_ALL + (h + 1) * QK_DIM].astype(F32)
        qt = ((q * cosf + pltpu.roll(q, QK_DIM // 2, 1) * sinf) * dq_ref[h]).astype(BF16)
        kt = ((k * cosf + pltpu.roll(k, QK_DIM // 2, 1) * sinf) * dk_ref[h]).astype(BF16)
        s = lax.dot_general(qt, kt, (((1,), (1,)), ((), ())), preferred_element_type=F32)
        qts.append(qt)
        kts.append(kt)
        scores.append(jnp.where(causal, s, 0.0).astype(BF16))

    states = [state_scr[h] for h in range(N_HEADS)]
    for (r0, h), qt, kt, s in zip(units, qts, kts, scores):
        v = p_ref[r0:r0 + L, 2 * QK_ALL + h * V_DIM:2 * QK_ALL + (h + 1) * V_DIM]
        g = p_ref[r0:r0 + L, 2 * QK_ALL + V_ALL + h * V_DIM:2 * QK_ALL + V_ALL + (h + 1) * V_DIM].astype(F32)
        lhs = jnp.concatenate([s, qt], axis=1)
        rhs = jnp.concatenate([v, states[h].astype(BF16)], axis=0)
        o = jnp.dot(lhs, rhs, preferred_element_type=F32)
        kv = lax.dot_general(kt, v, (((0,), (0,)), ((), ())), preferred_element_type=F32)
        states[h] = (states[h] + kv) * chunk_decay[h]
        y = _rms(o) * gn_ref[:, h * V_DIM:(h + 1) * V_DIM] * (g * jax.nn.sigmoid(g))
        o_ref[r0:r0 + L, h * V_DIM:(h + 1) * V_DIM] = y.astype(o_ref.dtype)
    for h in range(N_HEADS):
        state_scr[h] = states[h]


def _retention(proj3, cosf, sinf, dq, dk, gn_w, chunk_decay):
    b, s, _ = proj3.shape
    lb = CHUNK * min(RET_CHUNKS_PER_STEP, s // CHUNK)
    return pl.pallas_call(
        functools.partial(_retention_kernel, chunk_decay=chunk_decay),
        grid=(b, s // lb),
        in_specs=[
            pl.BlockSpec((None, lb, MIX_COLS), lambda i, c: (i, c, 0)),
            pl.BlockSpec((lb, QK_DIM), lambda i, c: (c, 0)),
            pl.BlockSpec((lb, QK_DIM), lambda i, c: (c, 0)),
            pl.BlockSpec((N_HEADS, CHUNK, QK_DIM), lambda i, c: (0, 0, 0)),
            pl.BlockSpec((N_HEADS, CHUNK, QK_DIM), lambda i, c: (0, 0, 0)),
            pl.BlockSpec((1, V_ALL), lambda i, c: (0, 0)),
        ],
        out_specs=pl.BlockSpec((None, lb, V_ALL), lambda i, c: (i, c, 0)),
        out_shape=jax.ShapeDtypeStruct((b, s, V_ALL), BF16),
        scratch_shapes=[pltpu.VMEM((N_HEADS, QK_DIM, V_DIM), F32)],
        compiler_params=pltpu.CompilerParams(
            dimension_semantics=("arbitrary", "arbitrary"), vmem_limit_bytes=VMEM_LIMIT),
        name="retention",
    )(proj3, cosf, sinf, dq, dk, gn_w)


def _mlstm_kernel(p_ref, gates_ref, cw_ref, cb_ref, shift_ref, gb_ref, gn_ref, o_ref,
                  c_scr, n_scr, m_scr, tail_scr, act_scr):
    L = CHUNK
    lb = p_ref.shape[0]

    @pl.when(pl.program_id(1) == 0)
    def _():
        c_scr[...] = jnp.zeros_like(c_scr)
        n_scr[...] = jnp.zeros_like(n_scr)
        m_scr[...] = jnp.zeros_like(m_scr)
        tail_scr[...] = jnp.zeros_like(tail_scr)

    ub = p_ref[:, 0:2 * QK_ALL]
    u = ub.astype(F32)
    tail = tail_scr[...]
    row8 = lax.broadcasted_iota(jnp.int32, tail.shape, 0)
    acc = u * cw_ref[CONV_WIDTH - 1:CONV_WIDTH, :] + cb_ref[...]
    head = jnp.zeros_like(tail)
    for d in range(1, CONV_WIDTH):
        w_d = cw_ref[CONV_WIDTH - 1 - d:CONV_WIDTH - d, :]
        acc = acc + jnp.dot(shift_ref[d - 1], ub, preferred_element_type=F32) * w_d
        head = head + jnp.where(row8 < d, pltpu.roll(tail, d, 0), 0.0) * w_d
    acc = jnp.concatenate([acc[0:SUBLANES, :] + head, acc[SUBLANES:, :]], axis=0)
    tail_scr[...] = u[lb - SUBLANES:lb, :]
    act_scr[...] = acc * jax.nn.sigmoid(acc)

    row = lax.broadcasted_iota(jnp.int32, (L, L), 0)
    col = lax.broadcasted_iota(jnp.int32, (L, L), 1)
    causal = row >= col
    k_scale = QK_DIM ** -0.5
    units = [(ci * L, h) for ci in range(lb // L) for h in range(N_HEADS)]

    def load_qk(r0, h):
        q = act_scr[r0:r0 + L, h * QK_DIM:(h + 1) * QK_DIM]
        k = act_scr[r0:r0 + L, QK_ALL + h * QK_DIM:QK_ALL + (h + 1) * QK_DIM] * k_scale
        return q, k

    gate_terms, src_rows = [], []
    lane_t = lax.broadcasted_iota(jnp.int32, (2 * N_HEADS, L), 1)
    for ci in range(lb // L):
        pre = gates_ref[ci * L:(ci + 1) * L, :] + gb_ref[...]
        pre_rows = jnp.transpose(pre)[0:2 * N_HEADS, :]
        b_rows = jnp.minimum(pre_rows, 0.0) - jnp.log1p(jnp.exp(-jnp.abs(pre_rows)))
        shift = 1
        while shift < L:
            b_rows = b_rows + jnp.where(lane_t >= shift, pltpu.roll(b_rows, shift, 1), 0.0)
            shift *= 2
        src_rows.append(pre_rows[0:N_HEADS, :] - b_rows[N_HEADS:2 * N_HEADS, :])
        bcum = jnp.transpose(jnp.concatenate([b_rows, jnp.zeros((L - 2 * N_HEADS, L), F32)], axis=0))
        gate_terms.append((pre, bcum))
    scores = []
    for r0, h in units:
        q, k = load_qk(r0, h)
        scores.append(lax.dot_general(q.astype(BF16), k.astype(BF16), (((1,), (1,)), ((), ())),
                                      preferred_element_type=F32))

    m_state = [m_scr[h:h + 1, :] for h in range(N_HEADS)]
    terms = []
    for (r0, h), qk in zip(units, scores):
        pre, bcum = gate_terms[r0 // L]
        q, k = load_qk(r0, h)
        b_t = jnp.broadcast_to(bcum[:, N_HEADS + h:N_HEADS + h + 1], (L, L))
        i_t = jnp.broadcast_to(pre[:, h:h + 1], (L, L))
        src = jnp.broadcast_to(src_rows[r0 // L][h:h + 1, :], (L, L))
        m_prev = m_state[h]
        a = b_t + m_prev
        dmat = jnp.where(causal, b_t + src, -jnp.inf)
        m_t = jnp.maximum(a, jnp.max(dmat, axis=-1, keepdims=True))
        w_inter = jnp.exp(a - m_t)
        s = qk * jnp.exp(dmat - m_t)
        lhs = jnp.concatenate([s.astype(BF16), (q * w_inter).astype(BF16)], axis=1)
        b_last = b_t[L - 1:L, :]
        gk = b_last - b_t + i_t
        m_new = jnp.maximum(b_last + m_prev, jnp.max(gk, axis=0, keepdims=True))
        wk = k * jnp.exp(gk - m_new)
        m_state[h] = m_new
        terms.append(dict(
            lhs=lhs, s_sum=jnp.sum(s, axis=-1, keepdims=True), w_inter=w_inter, floor=jnp.exp(-m_t),
            keep=jnp.exp(b_last + m_prev - m_new), wk=wk.astype(BF16), wk_sum=jnp.sum(wk, axis=0, keepdims=True)))

    n_state = [n_scr[h:h + 1, :] for h in range(N_HEADS)]
    c_state = [c_scr[h] for h in range(N_HEADS)]
    v_cols = lambda h: slice(2 * QK_ALL + h * V_DIM, 2 * QK_ALL + (h + 1) * V_DIM)
    for ci in range(lb // L):
        r0 = ci * L
        chunk_terms = terms[ci * N_HEADS:(ci + 1) * N_HEADS]
        q_dot_n = [jnp.sum(load_qk(r0, h)[0] * n_state[h], axis=-1, keepdims=True) for h in range(N_HEADS)]
        nums = []
        for h, t in enumerate(chunk_terms):
            rhs = jnp.concatenate([p_ref[r0:r0 + L, v_cols(h)], c_state[h].astype(BF16)], axis=0)
            nums.append(jnp.dot(t["lhs"], rhs, preferred_element_type=F32))
        for h, t in enumerate(chunk_terms):
            c_state[h] = c_state[h] * t["keep"][:, 0:1] + lax.dot_general(
                t["wk"], p_ref[r0:r0 + L, v_cols(h)], (((0,), (0,)), ((), ())), preferred_element_type=F32)
            n_state[h] = n_state[h] * t["keep"] + t["wk_sum"]
        for h, t in enumerate(chunk_terms):
            og = p_ref[r0:r0 + L, V_ALL + v_cols(h).start:V_ALL + v_cols(h).stop].astype(F32)
            den = t["s_sum"] + q_dot_n[h] * t["w_inter"]
            inv = 1.0 / jnp.maximum(jnp.abs(den), t["floor"])
            hh = nums[h] * jnp.concatenate([inv] * (V_DIM // LANES), axis=1)
            y = _rms(hh * jax.nn.sigmoid(og)) * gn_ref[:, h * V_DIM:(h + 1) * V_DIM]
            o_ref[r0:r0 + L, h * V_DIM:(h + 1) * V_DIM] = y.astype(o_ref.dtype)

    for h in range(N_HEADS):
        c_scr[h] = c_state[h]
        n_scr[h:h + 1, :] = n_state[h]
        m_scr[h:h + 1, :] = m_state[h]


def _mlstm(proj3, gates3, conv_w, conv_b, gate_bias, gn_w):
    b, s, _ = proj3.shape
    lb = CHUNK * min(MLSTM_CHUNKS_PER_STEP, s // CHUNK)
    shifts = jnp.stack([jnp.eye(lb, k=-d, dtype=BF16) for d in range(1, CONV_WIDTH)])
    return pl.pallas_call(
        _mlstm_kernel,
        grid=(b, s // lb),
        in_specs=[
            pl.BlockSpec((None, lb, MIX_COLS), lambda i, c: (i, c, 1)),
            pl.BlockSpec((None, lb, LANES), lambda i, c: (i, c, 0)),
            pl.BlockSpec((CONV_WIDTH, 2 * QK_ALL), lambda i, c: (0, 0)),
            pl.BlockSpec((1, 2 * QK_ALL), lambda i, c: (0, 0)),
            pl.BlockSpec((CONV_WIDTH - 1, lb, lb), lambda i, c: (0, 0, 0)),
            pl.BlockSpec((1, LANES), lambda i, c: (0, 0)),
            pl.BlockSpec((1, V_ALL), lambda i, c: (0, 0)),
        ],
        out_specs=pl.BlockSpec((None, lb, V_ALL), lambda i, c: (i, c, 0)),
        out_shape=jax.ShapeDtypeStruct((b, s, V_ALL), BF16),
        scratch_shapes=[
            pltpu.VMEM((N_HEADS, QK_DIM, V_DIM), F32),
            pltpu.VMEM((8, QK_DIM), F32),
            pltpu.VMEM((8, LANES), F32),
            pltpu.VMEM((8, 2 * QK_ALL), F32),
            pltpu.VMEM((lb, 2 * QK_ALL), F32),
        ],
        compiler_params=pltpu.CompilerParams(
            dimension_semantics=("arbitrary", "arbitrary"), vmem_limit_bytes=VMEM_LIMIT),
        name="mlstm",
    )(proj3, gates3, conv_w, conv_b, shifts, gate_bias, gn_w)


def _merge_route_kernel(ret_ref, hm_ref, gr_ref, gm_ref, x_ref, wr_ref, wm_ref, wo_ref, nw_ref,
                        wrt_ref, brt_ref, lower_ref, x1_ref, h2_ref, route_ref, route_t_ref, cnt_ref,
                        carry_scr, logits_scr):
    step = pl.program_id(0)

    @pl.when(step == 0)
    def _():
        carry_scr[...] = jnp.zeros_like(carry_scr)
        logits_scr[...] = jnp.zeros_like(logits_scr)

    logits = logits_scr[...]

    y_ret = jnp.dot(ret_ref[...], wr_ref[...], preferred_element_type=F32)
    y_m = jnp.dot(hm_ref[...], wm_ref[...], preferred_element_type=F32)
    merged = (jax.nn.sigmoid(gr_ref[...].astype(F32)) * y_ret
              + jax.nn.sigmoid(gm_ref[...].astype(F32)) * y_m)
    x1 = x_ref[...] + jnp.dot(merged.astype(BF16), wo_ref[...], preferred_element_type=F32)
    x1_ref[...] = x1
    h2 = _rms(x1) * nw_ref[...]
    h2_ref[...] = _pack_rows(h2)
    new_logits = jnp.dot(h2.astype(BF16), wrt_ref[...], preferred_element_type=F32) + brt_ref[...]

    live = jnp.where(step > 0, 1.0, 0.0)
    tm = logits.shape[0]
    lane = lax.broadcasted_iota(jnp.int32, (tm, LANES), 1)
    neg = -jnp.inf
    big = jnp.int32(LANES)
    is_group = (lane >= N_EXPERTS) & (lane < N_EXPERTS + N_GROUPS)
    gl = jnp.where(is_group, logits, neg)
    g_max = jnp.max(gl, axis=-1, keepdims=True)
    g_idx = jnp.min(jnp.where(gl == g_max, lane, big), axis=-1, keepdims=True) - N_EXPERTS
    g_w = 1.0 / jnp.sum(jnp.exp(gl - g_max), axis=-1, keepdims=True)
    in_group = (lane >= g_idx * EXPERTS_PER_GROUP) & (lane < (g_idx + 1) * EXPERTS_PER_GROUP)
    el = jnp.where(in_group, logits, neg)
    l1 = jnp.max(el, axis=-1, keepdims=True)
    e1 = jnp.min(jnp.where(el == l1, lane, big), axis=-1, keepdims=True)
    el2 = jnp.where(lane == e1, neg, el)
    l2 = jnp.max(el2, axis=-1, keepdims=True)
    e2 = jnp.min(jnp.where(el2 == l2, lane, big), axis=-1, keepdims=True)
    t21 = jnp.exp(l2 - l1)
    w1 = g_w / (1.0 + t21)
    w2 = g_w * t21 / (1.0 + t21)

    hit1 = lane == e1
    hit2 = lane == e2
    cnt = jnp.where(hit1 | hit2, live, 0.0)
    before = jnp.dot(lower_ref[...], cnt.astype(BF16), preferred_element_type=F32) + carry_scr[...]
    r1 = jnp.sum(jnp.where(hit1, before, 0.0), axis=-1, keepdims=True)
    r2 = jnp.sum(jnp.where(hit2, before, 0.0), axis=-1, keepdims=True)
    carry = carry_scr[...] + jnp.sum(cnt, axis=0, keepdims=True)
    carry_scr[...] = carry
    cnt_ref[...] = carry

    fields = (e1.astype(F32), e2.astype(F32), r1, r2, w1, w2)
    packed = jnp.zeros((tm, LANES), F32)
    for idx, val in enumerate(fields):
        packed = jnp.where(lane == idx, val, packed)
    route_ref[...] = packed
    route_t_ref[...] = jnp.transpose(packed)[0:ROUTE_ROWS, :]
    logits_scr[...] = new_logits


def _merge_route(ret, hm, proj, x2d, w_ret, w_m, w_out, norm_w, w_router, b_router, lower):
    t = x2d.shape[0]
    tm = min(ROWS_MERGE, t)
    n_tiles = t // tm
    gate_r_blk = 2 * MIX_COLS // D_MODEL
    tile = lambda i: jnp.minimum(i, n_tiles - 1)
    routed = lambda i: jnp.maximum(i - 1, 0)
    row_blk = lambda i: (tile(i), 0)
    const = lambda i: (0, 0)
    return pl.pallas_call(
        _merge_route_kernel,
        grid=(n_tiles + 1,),
        in_specs=[
            pl.BlockSpec((tm, V_ALL), row_blk),
            pl.BlockSpec((tm, V_ALL), row_blk),
            pl.BlockSpec((tm, D_MODEL), lambda i: (tile(i), gate_r_blk)),
            pl.BlockSpec((tm, D_MODEL), lambda i: (tile(i), gate_r_blk + 1)),
            pl.BlockSpec((tm, D_MODEL), row_blk),
            pl.BlockSpec((V_ALL, D_MODEL), const),
            pl.BlockSpec((V_ALL, D_MODEL), const),
            pl.BlockSpec((D_MODEL, D_MODEL), const),
            pl.BlockSpec((1, D_MODEL), const),
            pl.BlockSpec((D_MODEL, LANES), const),
            pl.BlockSpec((1, LANES), const),
            pl.BlockSpec((tm, tm), const),
        ],
        out_specs=[
            pl.BlockSpec((tm, D_MODEL), row_blk),
            pl.BlockSpec((tm, PACKED), row_blk),
            pl.BlockSpec((tm, LANES), lambda i: (routed(i), 0)),
            pl.BlockSpec((ROUTE_ROWS, tm), lambda i: (0, routed(i))),
            pl.BlockSpec((1, LANES), const),
        ],
        out_shape=[
            jax.ShapeDtypeStruct((t, D_MODEL), F32),
            jax.ShapeDtypeStruct((t, PACKED), U32),
            jax.ShapeDtypeStruct((t, LANES), F32),
            jax.ShapeDtypeStruct((ROUTE_ROWS, t), F32),
            jax.ShapeDtypeStruct((1, LANES), F32),
        ],
        scratch_shapes=[pltpu.VMEM((1, LANES), F32), pltpu.VMEM((tm, LANES), F32)],
        compiler_params=pltpu.CompilerParams(
            dimension_semantics=("arbitrary",), vmem_limit_bytes=VMEM_LIMIT),
        name="merge_route",
    )(ret, hm, proj, proj, x2d, w_ret, w_m, w_out, norm_w, w_router, b_router, lower)


SUBLANES = 8


def _for_row_groups(n_rows, body):
    def step(g, carry):
        for sub in range(SUBLANES):
            body(g, sub)
        return carry

    lax.fori_loop(0, n_rows // SUBLANES, step, 0)


def _dispatch_kernel(fill_ref, dest_a_ref, dest_b_ref, h2_ref, xs_hbm, zero_scr, sem):
    tm = h2_ref.shape[0] * SUBLANES

    @pl.when(pl.program_id(0) == 0)
    def _():
        zero_scr[...] = jnp.zeros_like(zero_scr)

        def fill_copy(j):
            row0 = pl.multiple_of(fill_ref[j], MOE_ROWS)
            return pltpu.make_async_copy(zero_scr, xs_hbm.at[pl.ds(row0, MOE_ROWS)], sem)

        def start_fill(j, carry):
            @pl.when(fill_ref[j] >= 0)
            def _():
                fill_copy(j).start()
            return carry

        def wait_fill(j, carry):
            @pl.when(fill_ref[j] >= 0)
            def _():
                fill_copy(j).wait()
            return carry

        lax.fori_loop(0, fill_ref.shape[0], start_fill, 0)
        lax.fori_loop(0, fill_ref.shape[0], wait_fill, 0)

    def start(g, sub):
        src = h2_ref.at[g, pl.ds(sub, 1)]
        r = g * SUBLANES + sub
        pltpu.make_async_copy(src, xs_hbm.at[pl.ds(dest_a_ref[r], 1)], sem).start()
        pltpu.make_async_copy(src, xs_hbm.at[pl.ds(dest_b_ref[r], 1)], sem).start()

    _for_row_groups(tm, start)
    for _ in range(2):
        pltpu.make_async_copy(xs_hbm.at[pl.ds(0, tm)], xs_hbm.at[pl.ds(0, tm)], sem).wait()


def _dispatch(fill_rows, dest_a, dest_b, h2p, n_slots):
    t = h2p.shape[0]
    tm = min(ROWS_DISPATCH, t)
    return pl.pallas_call(
        _dispatch_kernel,
        grid=(t // tm,),
        in_specs=[
            pl.BlockSpec(memory_space=pltpu.SMEM),
            pl.BlockSpec((tm,), lambda i: (i,), memory_space=pltpu.SMEM),
            pl.BlockSpec((tm,), lambda i: (i,), memory_space=pltpu.SMEM),
            pl.BlockSpec((tm // SUBLANES, SUBLANES, PACKED), lambda i: (i, 0, 0)),
        ],
        out_specs=pl.BlockSpec(memory_space=pl.ANY),
        out_shape=jax.ShapeDtypeStruct((n_slots, PACKED), U32),
        scratch_shapes=[pltpu.VMEM((MOE_ROWS, PACKED), U32), pltpu.SemaphoreType.DMA(())],
        compiler_params=pltpu.CompilerParams(
            dimension_semantics=("arbitrary",), vmem_limit_bytes=VMEM_LIMIT),
        name="dispatch",
    )(fill_rows, dest_a, dest_b, h2p.reshape(t // SUBLANES, SUBLANES, PACKED))


def _experts_kernel(blk_e_ref, blk_valid_ref, xs_ref, wg_ref, wu_ref, wd_ref, ys_ref, wgu_scr, wd_scr):
    i = pl.program_id(0)
    valid = blk_valid_ref[i]

    @pl.when((i == 0) | (blk_e_ref[i] != blk_e_ref[jnp.maximum(i - 1, 0)]))
    def _():
        wgu_scr[:, :D_EXPERT] = wg_ref[...].astype(BF16)
        wgu_scr[:, D_EXPERT:] = wu_ref[...].astype(BF16)
        wd_scr[...] = wd_ref[...].astype(BF16)

    @pl.when(valid > 0)
    def _():
        xb = _unpack_rows(xs_ref[...]).astype(BF16)
        gu = jnp.dot(xb, wgu_scr[...], preferred_element_type=F32)
        g = gu[:, :D_EXPERT]
        act = (g * jax.nn.sigmoid(g) * gu[:, D_EXPERT:]).astype(BF16)
        ys_ref[...] = _pack_rows(jnp.dot(act, wd_scr[...], preferred_element_type=F32))

    @pl.when(valid <= 0)
    def _():
        ys_ref[...] = jnp.zeros_like(ys_ref)


def _experts(blk_e, blk_valid, xs, w_gate, w_up, w_down):
    p = xs.shape[0]
    per_expert = lambda i, be, bv: (be[i], 0, 0)
    grid_spec = pltpu.PrefetchScalarGridSpec(
        num_scalar_prefetch=2,
        grid=(p // MOE_ROWS,),
        in_specs=[
            pl.BlockSpec((MOE_ROWS, PACKED), lambda i, be, bv: (i, 0)),
            pl.BlockSpec((None, D_MODEL, D_EXPERT), per_expert),
            pl.BlockSpec((None, D_MODEL, D_EXPERT), per_expert),
            pl.BlockSpec((None, D_EXPERT, D_MODEL), per_expert),
        ],
        out_specs=pl.BlockSpec((MOE_ROWS, PACKED), lambda i, be, bv: (i, 0)),
        scratch_shapes=[pltpu.VMEM((D_MODEL, 2 * D_EXPERT), BF16), pltpu.VMEM((D_EXPERT, D_MODEL), BF16)],
    )
    return pl.pallas_call(
        _experts_kernel,
        grid_spec=grid_spec,
        out_shape=jax.ShapeDtypeStruct((p, PACKED), U32),
        compiler_params=pltpu.CompilerParams(
            dimension_semantics=("arbitrary",), vmem_limit_bytes=VMEM_LIMIT),
        name="experts",
    )(blk_e, blk_valid, xs, w_gate, w_up, w_down)


def _combine_kernel(dest_a_ref, dest_b_ref, route_ref, x1_ref, nw_ref, ys_hbm, o_ref, buf_a, buf_b, sems):
    step = pl.program_id(0)
    n_tiles = pl.num_programs(0) - 1
    tm = x1_ref.shape[0]
    slot = step % 2
    prev = 1 - slot
    groups = COMBINE_ROWS_PER_ITER // SUBLANES

    def issue(it):
        for gi in range(groups):
            g = it * groups + gi
            for sub in range(SUBLANES):
                r = g * SUBLANES + sub
                pltpu.make_async_copy(ys_hbm.at[pl.ds(dest_a_ref[r], 1)],
                                      buf_a.at[slot, g, pl.ds(sub, 1)], sems.at[slot]).start()
                pltpu.make_async_copy(ys_hbm.at[pl.ds(dest_b_ref[r], 1)],
                                      buf_b.at[slot, g, pl.ds(sub, 1)], sems.at[slot]).start()

    def rows_of(it):
        return pl.ds(pl.multiple_of(it * COMBINE_ROWS_PER_ITER, COMBINE_ROWS_PER_ITER), COMBINE_ROWS_PER_ITER)

    def finish(it):
        rows = rows_of(it)
        grp = pl.ds(it * groups, groups)
        ya = _unpack_rows(buf_a[prev, grp].reshape(COMBINE_ROWS_PER_ITER, PACKED))
        yb = _unpack_rows(buf_b[prev, grp].reshape(COMBINE_ROWS_PER_ITER, PACKED))
        x2 = x1_ref[rows, :] + route_ref[rows, 4:5] * ya + route_ref[rows, 5:6] * yb
        return _rms(x2) * nw_ref[...]

    def wait_prev():
        for _ in range(2):
            pltpu.make_async_copy(ys_hbm.at[pl.ds(0, tm)], ys_hbm.at[pl.ds(0, tm)], sems.at[prev]).wait()

    def loop(body):
        def step_fn(it, carry):
            body(it)
            return carry
        lax.fori_loop(0, tm // COMBINE_ROWS_PER_ITER, step_fn, 0)

    @pl.when(step == 0)
    def _():
        loop(issue)

    @pl.when((step > 0) & (step < n_tiles))
    def _():
        wait_prev()

        def both(it):
            y = finish(it)
            issue(it)
            o_ref[rows_of(it), :] = y
        loop(both)

    @pl.when(step == n_tiles)
    def _():
        wait_prev()

        def last(it):
            o_ref[rows_of(it), :] = finish(it)
        loop(last)


def _combine(dest_a, dest_b, route, x1, norm_w, ys):
    t = x1.shape[0]
    tm = min(ROWS_COMBINE, t)
    n_tiles = t // tm
    gathered = lambda i: (jnp.minimum(i, n_tiles - 1),)
    finished = lambda i: (jnp.maximum(i - 1, 0), 0)
    return pl.pallas_call(
        _combine_kernel,
        grid=(n_tiles + 1,),
        in_specs=[
            pl.BlockSpec((tm,), gathered, memory_space=pltpu.SMEM),
            pl.BlockSpec((tm,), gathered, memory_space=pltpu.SMEM),
            pl.BlockSpec((tm, LANES), finished),
            pl.BlockSpec((tm, D_MODEL), finished),
            pl.BlockSpec((1, D_MODEL), lambda i: (0, 0)),
            pl.BlockSpec(memory_space=pl.ANY),
        ],
        out_specs=pl.BlockSpec((tm, D_MODEL), finished),
        out_shape=jax.ShapeDtypeStruct((t, D_MODEL), F32),
        scratch_shapes=[
            pltpu.VMEM((2, tm // SUBLANES, SUBLANES, PACKED), U32),
            pltpu.VMEM((2, tm // SUBLANES, SUBLANES, PACKED), U32),
            pltpu.SemaphoreType.DMA((2,)),
        ],
        compiler_params=pltpu.CompilerParams(
            dimension_semantics=("arbitrary",), vmem_limit_bytes=VMEM_LIMIT),
        name="combine",
    )(dest_a, dest_b, route, x1, norm_w, ys)


def _rotary_tables(seq):
    inv_freq = 1.0 / (ROPE_BASE ** (jnp.arange(0, QK_DIM, 2, dtype=F32) / QK_DIM))
    ang = jnp.arange(seq, dtype=F32)[:, None] * inv_freq[None, :]
    cos, sin = jnp.cos(ang), jnp.sin(ang)
    return jnp.concatenate([cos, cos], axis=1), jnp.concatenate([-sin, sin], axis=1)


def _retention_decay_tables():
    gamma = 1.0 - 2.0 ** (-5.0 - np.arange(N_HEADS, dtype=np.float64))
    idx = np.arange(CHUNK, dtype=np.float64) + 1.0
    dq = gamma[:, None] ** idx[None, :]
    dk = gamma[:, None] ** (-idx[None, :]) * QK_DIM ** -0.5
    bcast = lambda a: jnp.asarray(np.broadcast_to(a[:, :, None], (N_HEADS, CHUNK, QK_DIM)), F32)
    return bcast(dq), bcast(dk), tuple(float(g) for g in gamma ** CHUNK)


def kernel(x, norm_mix_w, w_in, ret_gn_w, w_ret_branch, mlstm_conv_w, mlstm_conv_b, b_igate, b_fgate,
           mlstm_gn_w, w_mlstm_branch, w_out, norm_ffn_w, w_group, b_group, w_expert_router,
           b_expert_router, w_gate, w_up, w_down, norm_final_w):
    assert norm_mix_w.shape[0] == 1, "one layer"
    b, s, d = x.shape
    t = b * s
    assert d == D_MODEL and s % CHUNK == 0

    wi = w_in[0]
    n_pre = 2 * MIX_COLS
    w_big = jnp.concatenate([wi[:, :n_pre], wi[:, n_pre + 2 * N_HEADS:]], axis=1).astype(BF16)
    w_if = jnp.pad(wi[:, n_pre:n_pre + 2 * N_HEADS], ((0, 0), (0, LANES - 2 * N_HEADS))).astype(BF16)
    gate_bias = jnp.pad(jnp.concatenate([b_igate[0], b_fgate[0]]), (0, LANES - 2 * N_HEADS))[None, :]
    w_router = jnp.pad(jnp.concatenate([w_expert_router[0], w_group[0]], axis=1),
                       ((0, 0), (0, LANES - N_EXPERTS - N_GROUPS))).astype(BF16)
    b_router = jnp.pad(jnp.concatenate([b_expert_router[0], b_group[0]]),
                       (0, LANES - N_EXPERTS - N_GROUPS))[None, :]

    cosf, sinf = _rotary_tables(s)
    dq, dk, chunk_decay = _retention_decay_tables()
    tm_merge = min(ROWS_MERGE, t)
    lower = jnp.tril(jnp.ones((tm_merge, tm_merge), F32), -1).astype(BF16)

    x2d = x.reshape(t, d)
    proj, gates = _in_projection(x2d, norm_mix_w, w_big, w_if)
    proj3 = proj.reshape(b, s, N_BIG)
    ret = _retention(proj3, cosf, sinf, dq, dk, ret_gn_w, chunk_decay)
    hm = _mlstm(proj3, gates.reshape(b, s, LANES), mlstm_conv_w[0, :, 0, :], mlstm_conv_b,
                gate_bias, mlstm_gn_w)
    x1, h2p, route, route_t, counts = _merge_route(
        ret.reshape(t, V_ALL), hm.reshape(t, V_ALL), proj, x2d, w_ret_branch[0].astype(BF16),
        w_mlstm_branch[0].astype(BF16), w_out[0].astype(BF16), norm_ffn_w, w_router, b_router, lower)

    n_slots = 2 * t + N_EXPERTS * MOE_ROWS
    cnt = counts[0, :N_EXPERTS].astype(jnp.int32)
    padded = (cnt + MOE_ROWS - 1) // MOE_ROWS * MOE_ROWS
    expert_ids = jnp.arange(N_EXPERTS, dtype=jnp.int32)
    pstart = jnp.sum(jnp.where(expert_ids[None, :] < expert_ids[:, None], padded[None, :], 0), axis=1)
    pend = pstart + padded
    eid = route_t[0:2].astype(jnp.int32)
    slot0 = jnp.sum(jnp.where(eid[None] == expert_ids[:, None, None], pstart[:, None, None], 0), axis=0)
    dest = slot0 + route_t[2:4].astype(jnp.int32)
    blk_start = jnp.arange(n_slots // MOE_ROWS, dtype=jnp.int32) * MOE_ROWS
    blk_e = jnp.minimum(jnp.sum(blk_start[:, None] >= pend[None, :], axis=-1), N_EXPERTS - 1).astype(jnp.int32)
    blk_valid = jnp.clip(pstart[blk_e] + cnt[blk_e] - blk_start, 0, MOE_ROWS).astype(jnp.int32)
    tail = pend[-1] + expert_ids * MOE_ROWS
    fill_rows = jnp.concatenate([jnp.where(padded > 0, pend - MOE_ROWS, -1),
                                 jnp.where(tail < n_slots, tail, -1)]).astype(jnp.int32)

    xs = _dispatch(fill_rows, dest[0], dest[1], h2p, n_slots)
    ys = _experts(blk_e, blk_valid, xs, w_gate[0], w_up[0], w_down[0])
    out = _combine(dest[0], dest[1], route, x1, norm_final_w[None, :], ys)
    return out.reshape(b, s, d)
```

```python
import functools

import numpy as np
import jax
import jax.numpy as jnp
from jax import lax
from jax.experimental import pallas as pl
from jax.experimental.pallas import tpu as pltpu

F32 = jnp.float32
BF16 = jnp.bfloat16
U32 = jnp.uint32

D_MODEL = 1024
N_HEADS = 4
QK_DIM = 128
V_DIM = 256
CHUNK = 128
CONV_WIDTH = 4
ROPE_BASE = 10000.0
N_GROUPS = 4
EXPERTS_PER_GROUP = 8
N_EXPERTS = N_GROUPS * EXPERTS_PER_GROUP
D_EXPERT = 512
NORM_EPS = 1e-6
QK_ALL = N_HEADS * QK_DIM
V_ALL = N_HEADS * V_DIM

MIX_COLS = 2 * QK_ALL + 2 * V_ALL
N_BIG = 2 * MIX_COLS + 2 * D_MODEL
LANES = 128
PACKED = D_MODEL // 2

ROWS_PROJ = 2048
COLS_PROJ = 2048
RET_CHUNKS_PER_STEP = 4
MLSTM_CHUNKS_PER_STEP = 4
ROWS_MERGE = 512
ROUTE_ROWS = 8
SUBLANES = 8
LOCAL_ROWS = 2 * ROWS_MERGE + 256
assert LOCAL_ROWS >= 2 * ROWS_MERGE + N_EXPERTS * (SUBLANES - 1) and LOCAL_ROWS % LANES == 0
LOCAL_GROUPS = LOCAL_ROWS // SUBLANES
GROUP_TABLE = 256
assert GROUP_TABLE >= LOCAL_GROUPS
MOE_ROWS = 512
V7X_VMEM_BYTES = 64 * 1024 * 1024
VMEM_LIMIT = V7X_VMEM_BYTES - 8 * 1024 * 1024
VMEM_LIMIT_PROJ = V7X_VMEM_BYTES - 4 * 1024 * 1024


def _rms(x, eps=NORM_EPS):
    return x * lax.rsqrt(jnp.mean(x * x, axis=-1, keepdims=True) + eps)


def _pack_rows(x):
    lo = lax.bitcast_convert_type(x[:, :PACKED].astype(BF16).astype(F32), U32)
    hi = lax.bitcast_convert_type(x[:, PACKED:].astype(BF16).astype(F32), U32)
    return (hi & jnp.uint32(0xFFFF0000)) | (lo >> 16)


def _unpack_rows(w):
    lo = lax.bitcast_convert_type(w << 16, F32)
    hi = lax.bitcast_convert_type(w & jnp.uint32(0xFFFF0000), F32)
    return jnp.concatenate([lo, hi], axis=1)


def _in_proj_kernel(x_ref, nw_ref, w_ref, wif_ref, o_ref, gates_ref, h_scr):
    @pl.when(pl.program_id(1) == 0)
    def _():
        h = (_rms(x_ref[...]) * nw_ref[...]).astype(BF16)
        h_scr[...] = h
        gates_ref[...] = jnp.dot(h, wif_ref[...], preferred_element_type=F32)

    o_ref[...] = jnp.dot(h_scr[...], w_ref[...], preferred_element_type=F32).astype(o_ref.dtype)


def _in_projection(x2d, norm_w, w_big, w_if):
    t = x2d.shape[0]
    tm = min(ROWS_PROJ, t)
    tn = COLS_PROJ
    return pl.pallas_call(
        _in_proj_kernel,
        grid=(t // tm, N_BIG // tn),
        in_specs=[
            pl.BlockSpec((tm, D_MODEL), lambda i, j: (i, 0)),
            pl.BlockSpec((1, D_MODEL), lambda i, j: (0, 0)),
            pl.BlockSpec((D_MODEL, tn), lambda i, j: (0, j)),
            pl.BlockSpec((D_MODEL, LANES), lambda i, j: (0, 0)),
        ],
        out_specs=[
            pl.BlockSpec((tm, tn), lambda i, j: (i, j)),
            pl.BlockSpec((tm, LANES), lambda i, j: (i, 0)),
        ],
        out_shape=[
            jax.ShapeDtypeStruct((t, N_BIG), BF16),
            jax.ShapeDtypeStruct((t, LANES), F32),
        ],
        scratch_shapes=[pltpu.VMEM((tm, D_MODEL), BF16)],
        compiler_params=pltpu.CompilerParams(
            dimension_semantics=("arbitrary", "arbitrary"), vmem_limit_bytes=VMEM_LIMIT_PROJ),
        name="in_projection",
    )(x2d, norm_w, w_big, w_if)


def _retention_kernel(p_ref, cos_ref, sin_ref, dq_ref, dk_ref, gn_ref, o_ref, state_scr, *, chunk_decay):
    L = CHUNK

    @pl.when(pl.program_id(1) == 0)
    def _():
        state_scr[...] = jnp.zeros_like(state_scr)

    row = lax.broadcasted_iota(jnp.int32, (L, L), 0)
    col = lax.broadcasted_iota(jnp.int32, (L, L), 1)
    causal = row >= col
    n_chunks = p_ref.shape[0] // L
    units = [(ci * L, h) for ci in range(n_chunks) for h in range(N_HEADS)]

    qts, kts, scores = [], [], []
    for r0, h in units:
        cosf = cos_ref[r0:r0 + L, :]
        sinf = sin_ref[r0:r0 + L, :]
        q = p_ref[r0:r0 + L, h * QK_DIM:(h + 1) * QK_DIM].astype(F32)
        k = p_ref[r0:r0 + L, QK_ALL + h * QK_DIM:QK_ALL + (h + 1) * QK_DIM].astype(F32)
        qt = ((q * cosf + pltpu.roll(q, QK_DIM // 2, 1) * sinf) * dq_ref[h]).astype(BF16)
        kt = ((k * cosf + pltpu.roll(k, QK_DIM // 2, 1) * sinf) * dk_ref[h]).astype(BF16)
        s = lax.dot_general(qt, kt, (((1,), (1,)), ((), ())), preferred_element_type=F32)
        qts.append(qt)
        kts.append(kt)
        scores.append(jnp.where(causal, s, 0.0).astype(BF16))

    states = [state_scr[h] for h in range(N_HEADS)]
    for (r0, h), qt, kt, s in zip(units, qts, kts, scores):
        v = p_ref[r0:r0 + L, 2 * QK_ALL + h * V_DIM:2 * QK_ALL + (h + 1) * V_DIM]
        g = p_ref[r0:r0 + L, 2 * QK_ALL + V_ALL + h * V_DIM:2 * QK_ALL + V_ALL + (h + 1) * V_DIM].astype(F32)
        lhs = jnp.concatenate([s, qt], axis=1)
        rhs = jnp.concatenate([v, states[h].astype(BF16)], axis=0)
        o = jnp.dot(lhs, rhs, preferred_element_type=F32)
        kv = lax.dot_general(kt, v, (((0,), (0,)), ((), ())), preferred_element_type=F32)
        states[h] = (states[h] + kv) * chunk_decay[h]
        y = _rms(o) * gn_ref[:, h * V_DIM:(h + 1) * V_DIM] * (g * jax.nn.sigmoid(g))
        o_ref[r0:r0 + L, h * V_DIM:(h + 1) * V_DIM] = y.astype(o_ref.dtype)
    for h in range(N_HEADS):
        state_scr[h] = states[h]


def _retention(proj3, cosf, sinf, dq, dk, gn_w, chunk_decay):
    b, s, _ = proj3.shape
    lb = CHUNK * min(RET_CHUNKS_PER_STEP, s // CHUNK)
    return pl.pallas_call(
        functools.partial(_retention_kernel, chunk_decay=chunk_decay),
        grid=(b, s // lb),
        in_specs=[
            pl.BlockSpec((None, lb, MIX_COLS), lambda i, c: (i, c, 0)),
            pl.BlockSpec((lb, QK_DIM), lambda i, c: (c, 0)),
            pl.BlockSpec((lb, QK_DIM), lambda i, c: (c, 0)),
            pl.BlockSpec((N_HEADS, CHUNK, QK_DIM), lambda i, c: (0, 0, 0)),
            pl.BlockSpec((N_HEADS, CHUNK, QK_DIM), lambda i, c: (0, 0, 0)),
            pl.BlockSpec((1, V_ALL), lambda i, c: (0, 0)),
        ],
        out_specs=pl.BlockSpec((None, lb, V_ALL), lambda i, c: (i, c, 0)),
        out_shape=jax.ShapeDtypeStruct((b, s, V_ALL), BF16),
        scratch_shapes=[pltpu.VMEM((N_HEADS, QK_DIM, V_DIM), F32)],
        compiler_params=pltpu.CompilerParams(
            dimension_semantics=("arbitrary", "arbitrary"), vmem_limit_bytes=VMEM_LIMIT),
        name="retention",
    )(proj3, cosf, sinf, dq, dk, gn_w)


def _mlstm_kernel(p_ref, gates_ref, cw_ref, cb_ref, shift_ref, gb_ref, gn_ref, o_ref,
                  c_scr, n_scr, m_scr, tail_scr, act_scr):
    L = CHUNK
    lb = p_ref.shape[0]

    @pl.when(pl.program_id(1) == 0)
    def _():
        c_scr[...] = jnp.zeros_like(c_scr)
        n_scr[...] = jnp.zeros_like(n_scr)
        m_scr[...] = jnp.zeros_like(m_scr)
        tail_scr[...] = jnp.zeros_like(tail_scr)

    ub = p_ref[:, 0:2 * QK_ALL]
    u = ub.astype(F32)
    tail = tail_scr[...]
    row8 = lax.broadcasted_iota(jnp.int32, tail.shape, 0)
    acc = u * cw_ref[CONV_WIDTH - 1:CONV_WIDTH, :] + cb_ref[...]
    head = jnp.zeros_like(tail)
    for d in range(1, CONV_WIDTH):
        w_d = cw_ref[CONV_WIDTH - 1 - d:CONV_WIDTH - d, :]
        acc = acc + jnp.dot(shift_ref[d - 1], ub, preferred_element_type=F32) * w_d
        head = head + jnp.where(row8 < d, pltpu.roll(tail, d, 0), 0.0) * w_d
    acc = jnp.concatenate([acc[0:SUBLANES, :] + head, acc[SUBLANES:, :]], axis=0)
    tail_scr[...] = u[lb - SUBLANES:lb, :]
    act_scr[...] = acc * jax.nn.sigmoid(acc)

    row = lax.broadcasted_iota(jnp.int32, (L, L), 0)
    col = lax.broadcasted_iota(jnp.int32, (L, L), 1)
    causal = row >= col
    k_scale = QK_DIM ** -0.5
    units = [(ci * L, h) for ci in range(lb // L) for h in range(N_HEADS)]

    def load_qk(r0, h):
        q = act_scr[r0:r0 + L, h * QK_DIM:(h + 1) * QK_DIM]
        k = act_scr[r0:r0 + L, QK_ALL + h * QK_DIM:QK_ALL + (h + 1) * QK_DIM] * k_scale
        return q, k

    gate_terms, src_rows = [], []
    lane_t = lax.broadcasted_iota(jnp.int32, (2 * N_HEADS, L), 1)
    for ci in range(lb // L):
        pre = gates_ref[ci * L:(ci + 1) * L, :] + gb_ref[...]
        pre_rows = jnp.transpose(pre)[0:2 * N_HEADS, :]
        b_rows = jnp.minimum(pre_rows, 0.0) - jnp.log1p(jnp.exp(-jnp.abs(pre_rows)))
        shift = 1
        while shift < L:
            b_rows = b_rows + jnp.where(lane_t >= shift, pltpu.roll(b_rows, shift, 1), 0.0)
            shift *= 2
        src_rows.append(pre_rows[0:N_HEADS, :] - b_rows[N_HEADS:2 * N_HEADS, :])
        bcum = jnp.transpose(jnp.concatenate([b_rows, jnp.zeros((L - 2 * N_HEADS, L), F32)], axis=0))
        gate_terms.append((pre, bcum))
    scores = []
    for r0, h in units:
        q, k = load_qk(r0, h)
        scores.append(lax.dot_general(q.astype(BF16), k.astype(BF16), (((1,), (1,)), ((), ())),
                                      preferred_element_type=F32))

    m_state = [m_scr[h:h + 1, :] for h in range(N_HEADS)]
    terms = []
    for (r0, h), qk in zip(units, scores):
        pre, bcum = gate_terms[r0 // L]
        q, k = load_qk(r0, h)
        b_t = jnp.broadcast_to(bcum[:, N_HEADS + h:N_HEADS + h + 1], (L, L))
        i_t = jnp.broadcast_to(pre[:, h:h + 1], (L, L))
        src = jnp.broadcast_to(src_rows[r0 // L][h:h + 1, :], (L, L))
        m_prev = m_state[h]
        a = b_t + m_prev
        dmat = jnp.where(causal, b_t + src, -jnp.inf)
        m_t = jnp.maximum(a, jnp.max(dmat, axis=-1, keepdims=True))
        w_inter = jnp.exp(a - m_t)
        s = qk * jnp.exp(dmat - m_t)
        lhs = jnp.concatenate([s.astype(BF16), (q * w_inter).astype(BF16)], axis=1)
        b_last = b_t[L - 1:L, :]
        gk = b_last - b_t + i_t
        m_new = jnp.maximum(b_last + m_prev, jnp.max(gk, axis=0, keepdims=True))
        wk = k * jnp.exp(gk - m_new)
        m_state[h] = m_new
        terms.append(dict(
            lhs=lhs, s_sum=jnp.sum(s, axis=-1, keepdims=True), w_inter=w_inter, floor=jnp.exp(-m_t),
            keep=jnp.exp(b_last + m_prev - m_new), wk=wk.astype(BF16), wk_sum=jnp.sum(wk, axis=0, keepdims=True)))

    n_state = [n_scr[h:h + 1, :] for h in range(N_HEADS)]
    c_state = [c_scr[h] for h in range(N_HEADS)]
    v_cols = lambda h: slice(2 * QK_ALL + h * V_DIM, 2 * QK_ALL + (h + 1) * V_DIM)
    for ci in range(lb // L):
        r0 = ci * L
        chunk_terms = terms[ci * N_HEADS:(ci + 1) * N_HEADS]
        q_dot_n = [jnp.sum(load_qk(r0, h)[0] * n_state[h], axis=-1, keepdims=True) for h in range(N_HEADS)]
        nums = []
        for h, t in enumerate(chunk_terms):
            rhs = jnp.concatenate([p_ref[r0:r0 + L, v_cols(h)], c_state[h].astype(BF16)], axis=0)
            nums.append(jnp.dot(t["lhs"], rhs, preferred_element_type=F32))
        for h, t in enumerate(chunk_terms):
            c_state[h] = c_state[h] * t["keep"][:, 0:1] + lax.dot_general(
                t["wk"], p_ref[r0:r0 + L, v_cols(h)], (((0,), (0,)), ((), ())), preferred_element_type=F32)
            n_state[h] = n_state[h] * t["keep"] + t["wk_sum"]
        for h, t in enumerate(chunk_terms):
            og = p_ref[r0:r0 + L, V_ALL + v_cols(h).start:V_ALL + v_cols(h).stop].astype(F32)
            den = t["s_sum"] + q_dot_n[h] * t["w_inter"]
            inv = 1.0 / jnp.maximum(jnp.abs(den), t["floor"])
            hh = nums[h] * jnp.concatenate([inv] * (V_DIM // LANES), axis=1)
            y = _rms(hh * jax.nn.sigmoid(og)) * gn_ref[:, h * V_DIM:(h + 1) * V_DIM]
            o_ref[r0:r0 + L, h * V_DIM:(h + 1) * V_DIM] = y.astype(o_ref.dtype)

    for h in range(N_HEADS):
        c_scr[h] = c_state[h]
        n_scr[h:h + 1, :] = n_state[h]
        m_scr[h:h + 1, :] = m_state[h]


def _mlstm(proj3, gates3, conv_w, conv_b, gate_bias, gn_w):
    b, s, _ = proj3.shape
    lb = CHUNK * min(MLSTM_CHUNKS_PER_STEP, s // CHUNK)
    shifts = jnp.stack([jnp.eye(lb, k=-d, dtype=BF16) for d in range(1, CONV_WIDTH)])
    return pl.pallas_call(
        _mlstm_kernel,
        grid=(b, s // lb),
        in_specs=[
            pl.BlockSpec((None, lb, MIX_COLS), lambda i, c: (i, c, 1)),
            pl.BlockSpec((None, lb, LANES), lambda i, c: (i, c, 0)),
            pl.BlockSpec((CONV_WIDTH, 2 * QK_ALL), lambda i, c: (0, 0)),
            pl.BlockSpec((1, 2 * QK_ALL), lambda i, c: (0, 0)),
            pl.BlockSpec((CONV_WIDTH - 1, lb, lb), lambda i, c: (0, 0, 0)),
            pl.BlockSpec((1, LANES), lambda i, c: (0, 0)),
            pl.BlockSpec((1, V_ALL), lambda i, c: (0, 0)),
        ],
        out_specs=pl.BlockSpec((None, lb, V_ALL), lambda i, c: (i, c, 0)),
        out_shape=jax.ShapeDtypeStruct((b, s, V_ALL), BF16),
        scratch_shapes=[
            pltpu.VMEM((N_HEADS, QK_DIM, V_DIM), F32),
            pltpu.VMEM((8, QK_DIM), F32),
            pltpu.VMEM((8, LANES), F32),
            pltpu.VMEM((8, 2 * QK_ALL), F32),
            pltpu.VMEM((lb, 2 * QK_ALL), F32),
        ],
        compiler_params=pltpu.CompilerParams(
            dimension_semantics=("arbitrary", "arbitrary"), vmem_limit_bytes=VMEM_LIMIT),
        name="mlstm",
    )(proj3, gates3, conv_w, conv_b, shifts, gate_bias, gn_w)


def _merge_route_kernel(ret_ref, hm_ref, gr_ref, gm_ref, x_ref, wr_ref, wm_ref, wo_ref, nw_ref,
                        wrt_ref, brt_ref, lower_ref, x1_ref, h2_ref, route_ref, route_t_ref, tile_ref,
                        logits_scr):
    step = pl.program_id(0)

    @pl.when(step == 0)
    def _():
        logits_scr[...] = jnp.zeros_like(logits_scr)

    logits = logits_scr[...]

    y_ret = jnp.dot(ret_ref[...], wr_ref[...], preferred_element_type=F32)
    y_m = jnp.dot(hm_ref[...], wm_ref[...], preferred_element_type=F32)
    merged = (jax.nn.sigmoid(gr_ref[...].astype(F32)) * y_ret
              + jax.nn.sigmoid(gm_ref[...].astype(F32)) * y_m)
    x1 = x_ref[...] + jnp.dot(merged.astype(BF16), wo_ref[...], preferred_element_type=F32)
    x1_ref[...] = x1
    h2 = _rms(x1) * nw_ref[...]
    h2_ref[...] = _pack_rows(h2)
    new_logits = jnp.dot(h2.astype(BF16), wrt_ref[...], preferred_element_type=F32) + brt_ref[...]

    live = jnp.where(step > 0, 1.0, 0.0)
    tm = logits.shape[0]
    lane = lax.broadcasted_iota(jnp.int32, (tm, LANES), 1)
    neg = -jnp.inf
    big = jnp.int32(LANES)
    is_group = (lane >= N_EXPERTS) & (lane < N_EXPERTS + N_GROUPS)
    gl = jnp.where(is_group, logits, neg)
    g_max = jnp.max(gl, axis=-1, keepdims=True)
    g_idx = jnp.min(jnp.where(gl == g_max, lane, big), axis=-1, keepdims=True) - N_EXPERTS
    g_w = 1.0 / jnp.sum(jnp.exp(gl - g_max), axis=-1, keepdims=True)
    in_group = (lane >= g_idx * EXPERTS_PER_GROUP) & (lane < (g_idx + 1) * EXPERTS_PER_GROUP)
    el = jnp.where(in_group, logits, neg)
    l1 = jnp.max(el, axis=-1, keepdims=True)
    e1 = jnp.min(jnp.where(el == l1, lane, big), axis=-1, keepdims=True)
    el2 = jnp.where(lane == e1, neg, el)
    l2 = jnp.max(el2, axis=-1, keepdims=True)
    e2 = jnp.min(jnp.where(el2 == l2, lane, big), axis=-1, keepdims=True)
    t21 = jnp.exp(l2 - l1)
    w1 = g_w / (1.0 + t21)
    w2 = g_w * t21 / (1.0 + t21)

    hit1 = lane == e1
    hit2 = lane == e2
    cnt = jnp.where(hit1 | hit2, live, 0.0)
    before = jnp.dot(lower_ref[...], cnt.astype(BF16), preferred_element_type=F32)
    count = jnp.sum(cnt, axis=0, keepdims=True)
    run = jnp.floor((count + (SUBLANES - 1)) * (1.0 / SUBLANES)) * SUBLANES
    lane1 = lax.broadcasted_iota(jnp.int32, (1, LANES), 1)
    run_end = run
    shift = 1
    while shift < N_EXPERTS:
        run_end = run_end + jnp.where(lane1 >= shift, pltpu.roll(run_end, shift, 1), 0.0)
        shift *= 2
    run_start = run_end - run
    local = before + run_start
    r1 = jnp.sum(jnp.where(hit1, local, 0.0), axis=-1, keepdims=True)
    r2 = jnp.sum(jnp.where(hit2, local, 0.0), axis=-1, keepdims=True)
    sub8 = lax.broadcasted_iota(jnp.int32, (SUBLANES, LANES), 0)
    tile_ref[...] = jnp.where(sub8 == 0, count, jnp.where(sub8 == 1, run, jnp.where(sub8 == 2, run_start, 0.0)))

    fields = (e1.astype(F32), e2.astype(F32), r1, r2, w1, w2)
    packed = jnp.zeros((tm, LANES), F32)
    for idx, val in enumerate(fields):
        packed = jnp.where(lane == idx, val, packed)
    route_ref[...] = packed
    route_t_ref[...] = jnp.transpose(packed)[0:ROUTE_ROWS, :]
    logits_scr[...] = new_logits


def _merge_route(ret, hm, proj, x2d, w_ret, w_m, w_out, norm_w, w_router, b_router, lower):
    t = x2d.shape[0]
    tm = min(ROWS_MERGE, t)
    n_tiles = t // tm
    gate_r_blk = 2 * MIX_COLS // D_MODEL
    tile = lambda i: jnp.minimum(i, n_tiles - 1)
    routed = lambda i: jnp.maximum(i - 1, 0)
    row_blk = lambda i: (tile(i), 0)
    const = lambda i: (0, 0)
    return pl.pallas_call(
        _merge_route_kernel,
        grid=(n_tiles + 1,),
        in_specs=[
            pl.BlockSpec((tm, V_ALL), row_blk),
            pl.BlockSpec((tm, V_ALL), row_blk),
            pl.BlockSpec((tm, D_MODEL), lambda i: (tile(i), gate_r_blk)),
            pl.BlockSpec((tm, D_MODEL), lambda i: (tile(i), gate_r_blk + 1)),
            pl.BlockSpec((tm, D_MODEL), row_blk),
            pl.BlockSpec((V_ALL, D_MODEL), const),
            pl.BlockSpec((V_ALL, D_MODEL), const),
            pl.BlockSpec((D_MODEL, D_MODEL), const),
            pl.BlockSpec((1, D_MODEL), const),
            pl.BlockSpec((D_MODEL, LANES), const),
            pl.BlockSpec((1, LANES), const),
            pl.BlockSpec((tm, tm), const),
        ],
        out_specs=[
            pl.BlockSpec((tm, D_MODEL), row_blk),
            pl.BlockSpec((tm, PACKED), row_blk),
            pl.BlockSpec((tm, LANES), lambda i: (routed(i), 0)),
            pl.BlockSpec((ROUTE_ROWS, tm), lambda i: (0, routed(i))),
            pl.BlockSpec((None, SUBLANES, LANES), lambda i: (routed(i), 0, 0)),
        ],
        out_shape=[
            jax.ShapeDtypeStruct((t, D_MODEL), F32),
            jax.ShapeDtypeStruct((t, PACKED), U32),
            jax.ShapeDtypeStruct((t, LANES), F32),
            jax.ShapeDtypeStruct((ROUTE_ROWS, t), F32),
            jax.ShapeDtypeStruct((n_tiles, SUBLANES, LANES), F32),
        ],
        scratch_shapes=[pltpu.VMEM((tm, LANES), F32)],
        compiler_params=pltpu.CompilerParams(
            dimension_semantics=("arbitrary",), vmem_limit_bytes=VMEM_LIMIT),
        name="merge_route",
    )(ret, hm, proj, proj, x2d, w_ret, w_m, w_out, norm_w, w_router, b_router, lower)


def _dispatch_kernel(fill_ref, n_groups_ref, gdst_ref, route_t_ref, h2_ref, xs_hbm, sorted_scr, zero_scr, sem):
    step = pl.program_id(0)
    tm = h2_ref.shape[0]

    @pl.when(step == 0)
    def _():
        zero_scr[...] = jnp.zeros_like(zero_scr)

        def fill_copy(j):
            g0 = pl.multiple_of(fill_ref[j], MOE_ROWS // SUBLANES)
            return pltpu.make_async_copy(zero_scr, xs_hbm.at[pl.ds(g0, MOE_ROWS // SUBLANES)], sem)

        def start_fill(j, carry):
            @pl.when(fill_ref[j] >= 0)
            def _():
                fill_copy(j).start()
            return carry

        def wait_fill(j, carry):
            @pl.when(fill_ref[j] >= 0)
            def _():
                fill_copy(j).wait()
            return carry

        lax.fori_loop(0, fill_ref.shape[0], start_fill, 0)
        lax.fori_loop(0, fill_ref.shape[0], wait_fill, 0)

    slot = lax.broadcasted_iota(jnp.int32, (LOCAL_ROWS, tm), 0)
    row1 = route_t_ref[2:3, :].astype(jnp.int32)
    row2 = route_t_ref[3:4, :].astype(jnp.int32)
    pick = jnp.where((slot == row1) | (slot == row2), 1.0, 0.0).astype(BF16)
    tokens = _unpack_rows(h2_ref[...]).astype(BF16)
    ordered = jnp.dot(pick, tokens, preferred_element_type=F32)
    sorted_scr[...] = _pack_rows(ordered).reshape(LOCAL_GROUPS, SUBLANES, PACKED)

    n_groups = n_groups_ref[step]

    def group_copy(j):
        return pltpu.make_async_copy(sorted_scr.at[pl.ds(j, 1)], xs_hbm.at[pl.ds(gdst_ref[j], 1)], sem)

    def start(j, carry):
        group_copy(j).start()
        return carry

    def wait(j, carry):
        group_copy(j).wait()
        return carry

    lax.fori_loop(0, n_groups, start, 0)
    lax.fori_loop(0, n_groups, wait, 0)


def _dispatch(fill_groups, n_groups, gdst, route_t, h2p, n_slots):
    t = h2p.shape[0]
    tm = min(ROWS_MERGE, t)
    return pl.pallas_call(
        _dispatch_kernel,
        grid=(t // tm,),
        in_specs=[
            pl.BlockSpec(memory_space=pltpu.SMEM),
            pl.BlockSpec(memory_space=pltpu.SMEM),
            pl.BlockSpec((GROUP_TABLE,), lambda i: (i,), memory_space=pltpu.SMEM),
            pl.BlockSpec((ROUTE_ROWS, tm), lambda i: (0, i)),
            pl.BlockSpec((tm, PACKED), lambda i: (i, 0)),
        ],
        out_specs=pl.BlockSpec(memory_space=pl.ANY),
        out_shape=jax.ShapeDtypeStruct((n_slots // SUBLANES, SUBLANES, PACKED), U32),
        scratch_shapes=[
            pltpu.VMEM((LOCAL_GROUPS, SUBLANES, PACKED), U32),
            pltpu.VMEM((MOE_ROWS // SUBLANES, SUBLANES, PACKED), U32),
            pltpu.SemaphoreType.DMA(()),
        ],
        compiler_params=pltpu.CompilerParams(
            dimension_semantics=("arbitrary",), vmem_limit_bytes=VMEM_LIMIT),
        name="dispatch",
    )(fill_groups, n_groups, gdst, route_t, h2p)


def _experts_kernel(blk_e_ref, blk_valid_ref, xs_ref, wg_ref, wu_ref, wd_ref, ys_ref, wgu_scr, wd_scr):
    i = pl.program_id(0)
    valid = blk_valid_ref[i]

    @pl.when((i == 0) | (blk_e_ref[i] != blk_e_ref[jnp.maximum(i - 1, 0)]))
    def _():
        wgu_scr[:, :D_EXPERT] = wg_ref[...].astype(BF16)
        wgu_scr[:, D_EXPERT:] = wu_ref[...].astype(BF16)
        wd_scr[...] = wd_ref[...].astype(BF16)

    @pl.when(valid > 0)
    def _():
        xb = _unpack_rows(xs_ref[...]).astype(BF16)
        gu = jnp.dot(xb, wgu_scr[...], preferred_element_type=F32)
        g = gu[:, :D_EXPERT]
        act = (g * jax.nn.sigmoid(g) * gu[:, D_EXPERT:]).astype(BF16)
        ys_ref[...] = _pack_rows(jnp.dot(act, wd_scr[...], preferred_element_type=F32))

    @pl.when(valid <= 0)
    def _():
        ys_ref[...] = jnp.zeros_like(ys_ref)


def _experts(blk_e, blk_valid, xs, w_gate, w_up, w_down):
    p = xs.shape[0]
    per_expert = lambda i, be, bv: (be[i], 0, 0)
    grid_spec = pltpu.PrefetchScalarGridSpec(
        num_scalar_prefetch=2,
        grid=(p // MOE_ROWS,),
        in_specs=[
            pl.BlockSpec((MOE_ROWS, PACKED), lambda i, be, bv: (i, 0)),
            pl.BlockSpec((None, D_MODEL, D_EXPERT), per_expert),
            pl.BlockSpec((None, D_MODEL, D_EXPERT), per_expert),
            pl.BlockSpec((None, D_EXPERT, D_MODEL), per_expert),
        ],
        out_specs=pl.BlockSpec((MOE_ROWS, PACKED), lambda i, be, bv: (i, 0)),
        scratch_shapes=[pltpu.VMEM((D_MODEL, 2 * D_EXPERT), BF16), pltpu.VMEM((D_EXPERT, D_MODEL), BF16)],
    )
    return pl.pallas_call(
        _experts_kernel,
        grid_spec=grid_spec,
        out_shape=jax.ShapeDtypeStruct((p, PACKED), U32),
        compiler_params=pltpu.CompilerParams(
            dimension_semantics=("arbitrary",), vmem_limit_bytes=VMEM_LIMIT),
        name="experts",
    )(blk_e, blk_valid, xs, w_gate, w_up, w_down)


def _combine_kernel(n_groups_ref, gdst_ref, route_ref, x1_ref, nw_ref, ys_hbm, o_ref, buf, sems):
    step = pl.program_id(0)
    n_tiles = pl.num_programs(0) - 1
    tm = x1_ref.shape[0]
    slot = step % 2
    prev = 1 - slot

    @pl.when(step == 0)
    def _():
        buf[...] = jnp.zeros_like(buf)

    def group_copy(which, j):
        return pltpu.make_async_copy(ys_hbm.at[pl.ds(gdst_ref[j], 1)], buf.at[which, pl.ds(j, 1)], sems.at[which])

    @pl.when(step < n_tiles)
    def _():
        def start(j, carry):
            group_copy(slot, j).start()
            return carry
        lax.fori_loop(0, n_groups_ref[step], start, 0)

    @pl.when(step > 0)
    def _():
        def wait(j, carry):
            group_copy(prev, 0).wait()
            return carry
        lax.fori_loop(0, n_groups_ref[step - 1], wait, 0)

        rows = _unpack_rows(buf[prev].reshape(LOCAL_ROWS, PACKED)).astype(BF16)
        pos = lax.broadcasted_iota(jnp.int32, (tm, LOCAL_ROWS), 1)
        row1 = route_ref[:, 2:3].astype(jnp.int32)
        row2 = route_ref[:, 3:4].astype(jnp.int32)
        mix = (jnp.where(pos == row1, route_ref[:, 4:5], 0.0)
               + jnp.where(pos == row2, route_ref[:, 5:6], 0.0)).astype(BF16)
        x2 = x1_ref[...] + jnp.dot(mix, rows, preferred_element_type=F32)
        o_ref[...] = _rms(x2) * nw_ref[...]


def _combine(n_groups, gdst, route, x1, norm_w, ys3):
    t = x1.shape[0]
    tm = min(ROWS_MERGE, t)
    n_tiles = t // tm
    gathered = lambda i: (jnp.minimum(i, n_tiles - 1),)
    finished = lambda i: (jnp.maximum(i - 1, 0), 0)
    return pl.pallas_call(
        _combine_kernel,
        grid=(n_tiles + 1,),
        in_specs=[
            pl.BlockSpec(memory_space=pltpu.SMEM),
            pl.BlockSpec((GROUP_TABLE,), gathered, memory_space=pltpu.SMEM),
            pl.BlockSpec((tm, LANES), finished),
            pl.BlockSpec((tm, D_MODEL), finished),
            pl.BlockSpec((1, D_MODEL), lambda i: (0, 0)),
            pl.BlockSpec(memory_space=pl.ANY),
        ],
        out_specs=pl.BlockSpec((tm, D_MODEL), finished),
        out_shape=jax.ShapeDtypeStruct((t, D_MODEL), F32),
        scratch_shapes=[
            pltpu.VMEM((2, LOCAL_GROUPS, SUBLANES, PACKED), U32),
            pltpu.SemaphoreType.DMA((2,)),
        ],
        compiler_params=pltpu.CompilerParams(
            dimension_semantics=("arbitrary",), vmem_limit_bytes=VMEM_LIMIT),
        name="combine",
    )(n_groups, gdst, route, x1, norm_w, ys3)


def _rotary_tables(seq):
    inv_freq = 1.0 / (ROPE_BASE ** (jnp.arange(0, QK_DIM, 2, dtype=F32) / QK_DIM))
    ang = jnp.arange(seq, dtype=F32)[:, None] * inv_freq[None, :]
    cos, sin = jnp.cos(ang), jnp.sin(ang)
    return jnp.concatenate([cos, cos], axis=1), jnp.concatenate([-sin, sin], axis=1)


def _retention_decay_tables():
    gamma = 1.0 - 2.0 ** (-5.0 - np.arange(N_HEADS, dtype=np.float64))
    idx = np.arange(CHUNK, dtype=np.float64) + 1.0
    dq = gamma[:, None] ** idx[None, :]
    dk = gamma[:, None] ** (-idx[None, :]) * QK_DIM ** -0.5
    bcast = lambda a: jnp.asarray(np.broadcast_to(a[:, :, None], (N_HEADS, CHUNK, QK_DIM)), F32)
    return bcast(dq), bcast(dk), tuple(float(g) for g in gamma ** CHUNK)


def kernel(x, norm_mix_w, w_in, ret_gn_w, w_ret_branch, mlstm_conv_w, mlstm_conv_b, b_igate, b_fgate,
           mlstm_gn_w, w_mlstm_branch, w_out, norm_ffn_w, w_group, b_group, w_expert_router,
           b_expert_router, w_gate, w_up, w_down, norm_final_w):
    assert norm_mix_w.shape[0] == 1, "one layer"
    b, s, d = x.shape
    t = b * s
    assert d == D_MODEL and s % CHUNK == 0

    wi = w_in[0]
    n_pre = 2 * MIX_COLS
    w_big = jnp.concatenate([wi[:, :n_pre], wi[:, n_pre + 2 * N_HEADS:]], axis=1).astype(BF16)
    w_if = jnp.pad(wi[:, n_pre:n_pre + 2 * N_HEADS], ((0, 0), (0, LANES - 2 * N_HEADS))).astype(BF16)
    gate_bias = jnp.pad(jnp.concatenate([b_igate[0], b_fgate[0]]), (0, LANES - 2 * N_HEADS))[None, :]
    w_router = jnp.pad(jnp.concatenate([w_expert_router[0], w_group[0]], axis=1),
                       ((0, 0), (0, LANES - N_EXPERTS - N_GROUPS))).astype(BF16)
    b_router = jnp.pad(jnp.concatenate([b_expert_router[0], b_group[0]]),
                       (0, LANES - N_EXPERTS - N_GROUPS))[None, :]

    cosf, sinf = _rotary_tables(s)
    dq, dk, chunk_decay = _retention_decay_tables()
    tm_merge = min(ROWS_MERGE, t)
    lower = jnp.tril(jnp.ones((tm_merge, tm_merge), F32), -1).astype(BF16)

    x2d = x.reshape(t, d)
    proj, gates = _in_projection(x2d, norm_mix_w, w_big, w_if)
    proj3 = proj.reshape(b, s, N_BIG)
    ret = _retention(proj3, cosf, sinf, dq, dk, ret_gn_w, chunk_decay)
    hm = _mlstm(proj3, gates.reshape(b, s, LANES), mlstm_conv_w[0, :, 0, :], mlstm_conv_b,
                gate_bias, mlstm_gn_w)
    x1, h2p, route, route_t, tiles = _merge_route(
        ret.reshape(t, V_ALL), hm.reshape(t, V_ALL), proj, x2d, w_ret_branch[0].astype(BF16),
        w_mlstm_branch[0].astype(BF16), w_out[0].astype(BF16), norm_ffn_w, w_router, b_router, lower)

    n_tiles = t // tm_merge
    n_slots = 2 * t + N_EXPERTS * (MOE_ROWS + n_tiles * (SUBLANES - 1) // SUBLANES * SUBLANES)
    n_slots = -(-n_slots // MOE_ROWS) * MOE_ROWS
    run = tiles[:, 1, :N_EXPERTS].astype(jnp.int32)
    run_start = tiles[:, 2, :N_EXPERTS].astype(jnp.int32)
    rows_e = jnp.sum(run, axis=0)
    padded = (rows_e + MOE_ROWS - 1) // MOE_ROWS * MOE_ROWS
    expert_ids = jnp.arange(N_EXPERTS, dtype=jnp.int32)
    pstart = jnp.sum(jnp.where(expert_ids[None, :] < expert_ids[:, None], padded[None, :], 0), axis=1)
    pend = pstart + padded
    tile_ids = jnp.arange(n_tiles, dtype=jnp.int32)
    earlier = jnp.sum(jnp.where((tile_ids[None, :] < tile_ids[:, None])[:, :, None], run[None, :, :], 0), axis=1)
    global_start = pstart[None, :] + earlier
    local_row = jnp.arange(LOCAL_GROUPS, dtype=jnp.int32)[None, :, None] * SUBLANES
    in_run = (local_row >= run_start[:, None, :]) & (local_row < (run_start + run)[:, None, :])
    gdst = jnp.sum(jnp.where(in_run, (global_start - run_start)[:, None, :] + local_row, 0), axis=2) // SUBLANES
    gdst = jnp.pad(gdst, ((0, 0), (0, GROUP_TABLE - LOCAL_GROUPS))).reshape(n_tiles * GROUP_TABLE).astype(jnp.int32)
    n_groups = (jnp.sum(run, axis=1) // SUBLANES).astype(jnp.int32)
    blk_start = jnp.arange(n_slots // MOE_ROWS, dtype=jnp.int32) * MOE_ROWS
    blk_e = jnp.minimum(jnp.sum(blk_start[:, None] >= pend[None, :], axis=-1), N_EXPERTS - 1).astype(jnp.int32)
    blk_valid = jnp.clip(pstart[blk_e] + rows_e[blk_e] - blk_start, 0, MOE_ROWS).astype(jnp.int32)
    tail = pend[-1] + jnp.arange((n_slots - 2 * t) // MOE_ROWS, dtype=jnp.int32) * MOE_ROWS
    fill_groups = (jnp.concatenate([jnp.where(padded > 0, pend - MOE_ROWS, -SUBLANES),
                                    jnp.where(tail < n_slots, tail, -SUBLANES)]) // SUBLANES).astype(jnp.int32)

    xs3 = _dispatch(fill_groups, n_groups, gdst, route_t, h2p, n_slots)
    ys = _experts(blk_e, blk_valid, xs3.reshape(n_slots, PACKED), w_gate[0], w_up[0], w_down[0])
    out = _combine(n_groups, gdst, route, x1, norm_final_w[None, :],
                   ys.reshape(n_slots // SUBLANES, SUBLANES, PACKED))
    return out.reshape(b, s, d)
```

```python
import functools

import numpy as np
import jax
import jax.numpy as jnp
from jax import lax
from jax.experimental import pallas as pl
from jax.experimental.pallas import tpu as pltpu

F32 = jnp.float32
BF16 = jnp.bfloat16
U32 = jnp.uint32

D_MODEL = 1024
N_HEADS = 4
QK_DIM = 128
V_DIM = 256
CHUNK = 128
CONV_WIDTH = 4
ROPE_BASE = 10000.0
N_GROUPS = 4
EXPERTS_PER_GROUP = 8
N_EXPERTS = N_GROUPS * EXPERTS_PER_GROUP
D_EXPERT = 512
NORM_EPS = 1e-6
QK_ALL = N_HEADS * QK_DIM
V_ALL = N_HEADS * V_DIM

MIX_COLS = 2 * QK_ALL + 2 * V_ALL
N_BIG = 2 * MIX_COLS + 2 * D_MODEL
LANES = 128
PACKED = D_MODEL // 2

ROWS_PROJ = 2048
COLS_PROJ = 2048
RET_CHUNKS_PER_STEP = 4
MLSTM_CHUNKS_PER_STEP = 4
ROWS_MERGE = 512
ROUTE_ROWS = 8
SUBLANES = 8
LOCAL_ROWS = 2 * ROWS_MERGE + 256
assert LOCAL_ROWS >= 2 * ROWS_MERGE + N_EXPERTS * (SUBLANES - 1) and LOCAL_ROWS % LANES == 0
LOCAL_GROUPS = LOCAL_ROWS // SUBLANES
GROUP_TABLE = 256
assert GROUP_TABLE >= LOCAL_GROUPS
MOE_ROWS = 512
V7X_VMEM_BYTES = 64 * 1024 * 1024
VMEM_LIMIT = V7X_VMEM_BYTES - 8 * 1024 * 1024
VMEM_LIMIT_PROJ = V7X_VMEM_BYTES - 4 * 1024 * 1024


def _rms(x, eps=NORM_EPS):
    return x * lax.rsqrt(jnp.mean(x * x, axis=-1, keepdims=True) + eps)


def _pack_rows(x):
    lo = lax.bitcast_convert_type(x[:, :PACKED].astype(BF16).astype(F32), U32)
    hi = lax.bitcast_convert_type(x[:, PACKED:].astype(BF16).astype(F32), U32)
    return (hi & jnp.uint32(0xFFFF0000)) | (lo >> 16)


def _unpack_rows(w):
    lo = lax.bitcast_convert_type(w << 16, F32)
    hi = lax.bitcast_convert_type(w & jnp.uint32(0xFFFF0000), F32)
    return jnp.concatenate([lo, hi], axis=1)


def _in_proj_kernel(x_ref, nw_ref, w_ref, wif_ref, o_ref, gates_ref, h_scr):
    @pl.when(pl.program_id(1) == 0)
    def _():
        h = (_rms(x_ref[...]) * nw_ref[...]).astype(BF16)
        h_scr[...] = h
        gates_ref[...] = jnp.dot(h, wif_ref[...], preferred_element_type=F32)

    o_ref[...] = jnp.dot(h_scr[...], w_ref[...], preferred_element_type=F32).astype(o_ref.dtype)


def _in_projection(x2d, norm_w, w_big, w_if):
    t = x2d.shape[0]
    tm = min(ROWS_PROJ, t)
    tn = COLS_PROJ
    return pl.pallas_call(
        _in_proj_kernel,
        grid=(t // tm, N_BIG // tn),
        in_specs=[
            pl.BlockSpec((tm, D_MODEL), lambda i, j: (i, 0)),
            pl.BlockSpec((1, D_MODEL), lambda i, j: (0, 0)),
            pl.BlockSpec((D_MODEL, tn), lambda i, j: (0, j)),
            pl.BlockSpec((D_MODEL, LANES), lambda i, j: (0, 0)),
        ],
        out_specs=[
            pl.BlockSpec((tm, tn), lambda i, j: (i, j)),
            pl.BlockSpec((tm, LANES), lambda i, j: (i, 0)),
        ],
        out_shape=[
            jax.ShapeDtypeStruct((t, N_BIG), BF16),
            jax.ShapeDtypeStruct((t, LANES), F32),
        ],
        scratch_shapes=[pltpu.VMEM((tm, D_MODEL), BF16)],
        compiler_params=pltpu.CompilerParams(
            dimension_semantics=("arbitrary", "arbitrary"), vmem_limit_bytes=VMEM_LIMIT_PROJ),
        name="in_projection",
    )(x2d, norm_w, w_big, w_if)


def _retention_kernel(p_ref, cos_ref, sin_ref, dq_ref, dk_ref, gn_ref, o_ref, state_scr, *, chunk_decay):
    L = CHUNK

    @pl.when(pl.program_id(1) == 0)
    def _():
        state_scr[...] = jnp.zeros_like(state_scr)

    row = lax.broadcasted_iota(jnp.int32, (L, L), 0)
    col = lax.broadcasted_iota(jnp.int32, (L, L), 1)
    causal = row >= col
    n_chunks = p_ref.shape[0] // L
    units = [(ci * L, h) for ci in range(n_chunks) for h in range(N_HEADS)]

    qts, kts, scores = [], [], []
    for r0, h in units:
        cosf = cos_ref[r0:r0 + L, :]
        sinf = sin_ref[r0:r0 + L, :]
        q = p_ref[r0:r0 + L, h * QK_DIM:(h + 1) * QK_DIM].astype(F32)
        k = p_ref[r0:r0 + L, QK_ALL + h * QK_DIM:QK_ALL + (h + 1) * QK_DIM].astype(F32)
        qt = ((q * cosf + pltpu.roll(q, QK_DIM // 2, 1) * sinf) * dq_ref[h]).astype(BF16)
        kt = ((k * cosf + pltpu.roll(k, QK_DIM // 2, 1) * sinf) * dk_ref[h]).astype(BF16)
        s = lax.dot_general(qt, kt, (((1,), (1,)), ((), ())), preferred_element_type=F32)
        qts.append(qt)
        kts.append(kt)
        scores.append(jnp.where(causal, s, 0.0).astype(BF16))

    states = [state_scr[h] for h in range(N_HEADS)]
    for (r0, h), qt, kt, s in zip(units, qts, kts, scores):
        v = p_ref[r0:r0 + L, 2 * QK_ALL + h * V_DIM:2 * QK_ALL + (h + 1) * V_DIM]
        g = p_ref[r0:r0 + L, 2 * QK_ALL + V_ALL + h * V_DIM:2 * QK_ALL + V_ALL + (h + 1) * V_DIM].astype(F32)
        lhs = jnp.concatenate([s, qt], axis=1)
        rhs = jnp.concatenate([v, states[h].astype(BF16)], axis=0)
        o = jnp.dot(lhs, rhs, preferred_element_type=F32)
        kv = lax.dot_general(kt, v, (((0,), (0,)), ((), ())), preferred_element_type=F32)
        states[h] = (states[h] + kv) * chunk_decay[h]
        y = _rms(o) * gn_ref[:, h * V_DIM:(h + 1) * V_DIM] * (g * jax.nn.sigmoid(g))
        o_ref[r0:r0 + L, h * V_DIM:(h + 1) * V_DIM] = y.astype(o_ref.dtype)
    for h in range(N_HEADS):
        state_scr[h] = states[h]


def _retention(proj3, cosf, sinf, dq, dk, gn_w, chunk_decay):
    b, s, _ = proj3.shape
    lb = CHUNK * min(RET_CHUNKS_PER_STEP, s // CHUNK)
    return pl.pallas_call(
        functools.partial(_retention_kernel, chunk_decay=chunk_decay),
        grid=(b, s // lb),
        in_specs=[
            pl.BlockSpec((None, lb, MIX_COLS), lambda i, c: (i, c, 0)),
            pl.BlockSpec((lb, QK_DIM), lambda i, c: (c, 0)),
            pl.BlockSpec((lb, QK_DIM), lambda i, c: (c, 0)),
            pl.BlockSpec((N_HEADS, CHUNK, QK_DIM), lambda i, c: (0, 0, 0)),
            pl.BlockSpec((N_HEADS, CHUNK, QK_DIM), lambda i, c: (0, 0, 0)),
            pl.BlockSpec((1, V_ALL), lambda i, c: (0, 0)),
        ],
        out_specs=pl.BlockSpec((None, lb, V_ALL), lambda i, c: (i, c, 0)),
        out_shape=jax.ShapeDtypeStruct((b, s, V_ALL), BF16),
        scratch_shapes=[pltpu.VMEM((N_HEADS, QK_DIM, V_DIM), F32)],
        compiler_params=pltpu.CompilerParams(
            dimension_semantics=("arbitrary", "arbitrary"), vmem_limit_bytes=VMEM_LIMIT),
        name="retention",
    )(proj3, cosf, sinf, dq, dk, gn_w)


def _mlstm_kernel(p_ref, gates_ref, cw_ref, cb_ref, shift_ref, gb_ref, gn_ref, o_ref,
                  c_scr, n_scr, m_scr, tail_scr, act_scr):
    L = CHUNK
    lb = p_ref.shape[0]

    @pl.when(pl.program_id(1) == 0)
    def _():
        c_scr[...] = jnp.zeros_like(c_scr)
        n_scr[...] = jnp.zeros_like(n_scr)
        m_scr[...] = jnp.zeros_like(m_scr)
        tail_scr[...] = jnp.zeros_like(tail_scr)

    ub = p_ref[:, 0:2 * QK_ALL]
    u = ub.astype(F32)
    tail = tail_scr[...]
    row8 = lax.broadcasted_iota(jnp.int32, tail.shape, 0)
    acc = u * cw_ref[CONV_WIDTH - 1:CONV_WIDTH, :] + cb_ref[...]
    head = jnp.zeros_like(tail)
    for d in range(1, CONV_WIDTH):
        w_d = cw_ref[CONV_WIDTH - 1 - d:CONV_WIDTH - d, :]
        acc = acc + jnp.dot(shift_ref[d - 1], ub, preferred_element_type=F32) * w_d
        head = head + jnp.where(row8 < d, pltpu.roll(tail, d, 0), 0.0) * w_d
    acc = jnp.concatenate([acc[0:SUBLANES, :] + head, acc[SUBLANES:, :]], axis=0)
    tail_scr[...] = u[lb - SUBLANES:lb, :]
    act_scr[...] = acc * jax.nn.sigmoid(acc)

    row = lax.broadcasted_iota(jnp.int32, (L, L), 0)
    col = lax.broadcasted_iota(jnp.int32, (L, L), 1)
    causal = row >= col
    k_scale = QK_DIM ** -0.5
    units = [(ci * L, h) for ci in range(lb // L) for h in range(N_HEADS)]

    def load_qk(r0, h):
        q = act_scr[r0:r0 + L, h * QK_DIM:(h + 1) * QK_DIM]
        k = act_scr[r0:r0 + L, QK_ALL + h * QK_DIM:QK_ALL + (h + 1) * QK_DIM] * k_scale
        return q, k

    gate_terms, src_rows = [], []
    lane_t = lax.broadcasted_iota(jnp.int32, (2 * N_HEADS, L), 1)
    for ci in range(lb // L):
        pre = gates_ref[ci * L:(ci + 1) * L, :] + gb_ref[...]
        pre_rows = jnp.transpose(pre)[0:2 * N_HEADS, :]
        b_rows = jnp.minimum(pre_rows, 0.0) - jnp.log1p(jnp.exp(-jnp.abs(pre_rows)))
        shift = 1
        while shift < L:
            b_rows = b_rows + jnp.where(lane_t >= shift, pltpu.roll(b_rows, shift, 1), 0.0)
            shift *= 2
        src_rows.append(pre_rows[0:N_HEADS, :] - b_rows[N_HEADS:2 * N_HEADS, :])
        bcum = jnp.transpose(jnp.concatenate([b_rows, jnp.zeros((L - 2 * N_HEADS, L), F32)], axis=0))
        gate_terms.append((pre, bcum))
    scores = []
    for r0, h in units:
        q, k = load_qk(r0, h)
        scores.append(lax.dot_general(q.astype(BF16), k.astype(BF16), (((1,), (1,)), ((), ())),
                                      preferred_element_type=F32))

    m_state = [m_scr[h:h + 1, :] for h in range(N_HEADS)]
    terms = []
    for (r0, h), qk in zip(units, scores):
        pre, bcum = gate_terms[r0 // L]
        q, k = load_qk(r0, h)
        b_t = jnp.broadcast_to(bcum[:, N_HEADS + h:N_HEADS + h + 1], (L, L))
        i_t = jnp.broadcast_to(pre[:, h:h + 1], (L, L))
        src = jnp.broadcast_to(src_rows[r0 // L][h:h + 1, :], (L, L))
        m_prev = m_state[h]
        a = b_t + m_prev
        dmat = jnp.where(causal, b_t + src, -jnp.inf)
        m_t = jnp.maximum(a, jnp.max(dmat, axis=-1, keepdims=True))
        w_inter = jnp.exp(a - m_t)
        s = qk * jnp.exp(dmat - m_t)
        lhs = jnp.concatenate([s.astype(BF16), (q * w_inter).astype(BF16)], axis=1)
        b_last = b_t[L - 1:L, :]
        gk = b_last - b_t + i_t
        m_new = jnp.maximum(b_last + m_prev, jnp.max(gk, axis=0, keepdims=True))
        wk = k * jnp.exp(gk - m_new)
        m_state[h] = m_new
        terms.append(dict(
            lhs=lhs, s_sum=jnp.sum(s, axis=-1, keepdims=True), w_inter=w_inter, floor=jnp.exp(-m_t),
            keep=jnp.exp(b_last + m_prev - m_new), wk=wk.astype(BF16), wk_sum=jnp.sum(wk, axis=0, keepdims=True)))

    n_state = [n_scr[h:h + 1, :] for h in range(N_HEADS)]
    c_state = [c_scr[h] for h in range(N_HEADS)]
    v_cols = lambda h: slice(2 * QK_ALL + h * V_DIM, 2 * QK_ALL + (h + 1) * V_DIM)
    for ci in range(lb // L):
        r0 = ci * L
        chunk_terms = terms[ci * N_HEADS:(ci + 1) * N_HEADS]
        q_dot_n = [jnp.sum(load_qk(r0, h)[0] * n_state[h], axis=-1, keepdims=True) for h in range(N_HEADS)]
        nums = []
        for h, t in enumerate(chunk_terms):
            rhs = jnp.concatenate([p_ref[r0:r0 + L, v_cols(h)], c_state[h].astype(BF16)], axis=0)
            nums.append(jnp.dot(t["lhs"], rhs, preferred_element_type=F32))
        for h, t in enumerate(chunk_terms):
            c_state[h] = c_state[h] * t["keep"][:, 0:1] + lax.dot_general(
                t["wk"], p_ref[r0:r0 + L, v_cols(h)], (((0,), (0,)), ((), ())), preferred_element_type=F32)
            n_state[h] = n_state[h] * t["keep"] + t["wk_sum"]
        for h, t in enumerate(chunk_terms):
            og = p_ref[r0:r0 + L, V_ALL + v_cols(h).start:V_ALL + v_cols(h).stop].astype(F32)
            den = t["s_sum"] + q_dot_n[h] * t["w_inter"]
            inv = 1.0 / jnp.maximum(jnp.abs(den), t["floor"])
            hh = nums[h] * jnp.concatenate([inv] * (V_DIM // LANES), axis=1)
            y = _rms(hh * jax.nn.sigmoid(og)) * gn_ref[:, h * V_DIM:(h + 1) * V_DIM]
            o_ref[r0:r0 + L, h * V_DIM:(h + 1) * V_DIM] = y.astype(o_ref.dtype)

    for h in range(N_HEADS):
        c_scr[h] = c_state[h]
        n_scr[h:h + 1, :] = n_state[h]
        m_scr[h:h + 1, :] = m_state[h]


def _mlstm(proj3, gates3, conv_w, conv_b, gate_bias, gn_w):
    b, s, _ = proj3.shape
    lb = CHUNK * min(MLSTM_CHUNKS_PER_STEP, s // CHUNK)
    shifts = jnp.stack([jnp.eye(lb, k=-d, dtype=BF16) for d in range(1, CONV_WIDTH)])
    return pl.pallas_call(
        _mlstm_kernel,
        grid=(b, s // lb),
        in_specs=[
            pl.BlockSpec((None, lb, MIX_COLS), lambda i, c: (i, c, 1)),
            pl.BlockSpec((None, lb, LANES), lambda i, c: (i, c, 0)),
            pl.BlockSpec((CONV_WIDTH, 2 * QK_ALL), lambda i, c: (0, 0)),
            pl.BlockSpec((1, 2 * QK_ALL), lambda i, c: (0, 0)),
            pl.BlockSpec((CONV_WIDTH - 1, lb, lb), lambda i, c: (0, 0, 0)),
            pl.BlockSpec((1, LANES), lambda i, c: (0, 0)),
            pl.BlockSpec((1, V_ALL), lambda i, c: (0, 0)),
        ],
        out_specs=pl.BlockSpec((None, lb, V_ALL), lambda i, c: (i, c, 0)),
        out_shape=jax.ShapeDtypeStruct((b, s, V_ALL), BF16),
        scratch_shapes=[
            pltpu.VMEM((N_HEADS, QK_DIM, V_DIM), F32),
            pltpu.VMEM((8, QK_DIM), F32),
            pltpu.VMEM((8, LANES), F32),
            pltpu.VMEM((8, 2 * QK_ALL), F32),
            pltpu.VMEM((lb, 2 * QK_ALL), F32),
        ],
        compiler_params=pltpu.CompilerParams(
            dimension_semantics=("arbitrary", "arbitrary"), vmem_limit_bytes=VMEM_LIMIT),
        name="mlstm",
    )(proj3, gates3, conv_w, conv_b, shifts, gate_bias, gn_w)


def _merge_route_kernel(ret_ref, hm_ref, gr_ref, gm_ref, x_ref, wr_ref, wm_ref, wo_ref, nw_ref,
                        wrt_ref, brt_ref, lower_ref, x1_ref, h2_ref, route_ref, route_t_ref, tile_ref,
                        logits_scr):
    step = pl.program_id(0)

    @pl.when(step == 0)
    def _():
        logits_scr[...] = jnp.zeros_like(logits_scr)

    logits = logits_scr[...]

    y_ret = jnp.dot(ret_ref[...], wr_ref[...], preferred_element_type=F32)
    y_m = jnp.dot(hm_ref[...], wm_ref[...], preferred_element_type=F32)
    merged = (jax.nn.sigmoid(gr_ref[...].astype(F32)) * y_ret
              + jax.nn.sigmoid(gm_ref[...].astype(F32)) * y_m)
    x1 = x_ref[...] + jnp.dot(merged.astype(BF16), wo_ref[...], preferred_element_type=F32)
    x1_ref[...] = x1
    h2 = _rms(x1) * nw_ref[...]
    h2_ref[...] = _pack_rows(h2)
    new_logits = jnp.dot(h2.astype(BF16), wrt_ref[...], preferred_element_type=F32) + brt_ref[...]

    live = jnp.where(step > 0, 1.0, 0.0)
    tm = logits.shape[0]
    lane = lax.broadcasted_iota(jnp.int32, (tm, LANES), 1)
    neg = -jnp.inf
    big = jnp.int32(LANES)
    is_group = (lane >= N_EXPERTS) & (lane < N_EXPERTS + N_GROUPS)
    gl = jnp.where(is_group, logits, neg)
    g_max = jnp.max(gl, axis=-1, keepdims=True)
    g_idx = jnp.min(jnp.where(gl == g_max, lane, big), axis=-1, keepdims=True) - N_EXPERTS
    g_w = 1.0 / jnp.sum(jnp.exp(gl - g_max), axis=-1, keepdims=True)
    in_group = (lane >= g_idx * EXPERTS_PER_GROUP) & (lane < (g_idx + 1) * EXPERTS_PER_GROUP)
    el = jnp.where(in_group, logits, neg)
    l1 = jnp.max(el, axis=-1, keepdims=True)
    e1 = jnp.min(jnp.where(el == l1, lane, big), axis=-1, keepdims=True)
    el2 = jnp.where(lane == e1, neg, el)
    l2 = jnp.max(el2, axis=-1, keepdims=True)
    e2 = jnp.min(jnp.where(el2 == l2, lane, big), axis=-1, keepdims=True)
    t21 = jnp.exp(l2 - l1)
    w1 = g_w / (1.0 + t21)
    w2 = g_w * t21 / (1.0 + t21)

    hit1 = lane == e1
    hit2 = lane == e2
    cnt = jnp.where(hit1 | hit2, live, 0.0)
    before = jnp.dot(lower_ref[...], cnt.astype(BF16), preferred_element_type=F32)
    count = jnp.sum(cnt, axis=0, keepdims=True)
    run = jnp.floor((count + (SUBLANES - 1)) * (1.0 / SUBLANES)) * SUBLANES
    lane1 = lax.broadcasted_iota(jnp.int32, (1, LANES), 1)
    run_end = run
    shift = 1
    while shift < N_EXPERTS:
        run_end = run_end + jnp.where(lane1 >= shift, pltpu.roll(run_end, shift, 1), 0.0)
        shift *= 2
    run_start = run_end - run
    local = before + run_start
    r1 = jnp.sum(jnp.where(hit1, local, 0.0), axis=-1, keepdims=True)
    r2 = jnp.sum(jnp.where(hit2, local, 0.0), axis=-1, keepdims=True)
    sub8 = lax.broadcasted_iota(jnp.int32, (SUBLANES, LANES), 0)
    tile_ref[...] = jnp.where(sub8 == 0, count, jnp.where(sub8 == 1, run, jnp.where(sub8 == 2, run_start, 0.0)))

    fields = (e1.astype(F32), e2.astype(F32), r1, r2, w1, w2)
    packed = jnp.zeros((tm, LANES), F32)
    for idx, val in enumerate(fields):
        packed = jnp.where(lane == idx, val, packed)
    route_ref[...] = packed
    route_t_ref[...] = jnp.transpose(packed)[0:ROUTE_ROWS, :]
    logits_scr[...] = new_logits


def _merge_route(ret, hm, proj, x2d, w_ret, w_m, w_out, norm_w, w_router, b_router, lower):
    t = x2d.shape[0]
    tm = min(ROWS_MERGE, t)
    n_tiles = t // tm
    gate_r_blk = 2 * MIX_COLS // D_MODEL
    tile = lambda i: jnp.minimum(i, n_tiles - 1)
    routed = lambda i: jnp.maximum(i - 1, 0)
    row_blk = lambda i: (tile(i), 0)
    const = lambda i: (0, 0)
    return pl.pallas_call(
        _merge_route_kernel,
        grid=(n_tiles + 1,),
        in_specs=[
            pl.BlockSpec((tm, V_ALL), row_blk),
            pl.BlockSpec((tm, V_ALL), row_blk),
            pl.BlockSpec((tm, D_MODEL), lambda i: (tile(i), gate_r_blk)),
            pl.BlockSpec((tm, D_MODEL), lambda i: (tile(i), gate_r_blk + 1)),
            pl.BlockSpec((tm, D_MODEL), row_blk),
            pl.BlockSpec((V_ALL, D_MODEL), const),
            pl.BlockSpec((V_ALL, D_MODEL), const),
            pl.BlockSpec((D_MODEL, D_MODEL), const),
            pl.BlockSpec((1, D_MODEL), const),
            pl.BlockSpec((D_MODEL, LANES), const),
            pl.BlockSpec((1, LANES), const),
            pl.BlockSpec((tm, tm), const),
        ],
        out_specs=[
            pl.BlockSpec((tm, D_MODEL), row_blk),
            pl.BlockSpec((tm, PACKED), row_blk),
            pl.BlockSpec((tm, LANES), lambda i: (routed(i), 0)),
            pl.BlockSpec((ROUTE_ROWS, tm), lambda i: (0, routed(i))),
            pl.BlockSpec((None, SUBLANES, LANES), lambda i: (routed(i), 0, 0)),
        ],
        out_shape=[
            jax.ShapeDtypeStruct((t, D_MODEL), F32),
            jax.ShapeDtypeStruct((t, PACKED), U32),
            jax.ShapeDtypeStruct((t, LANES), F32),
            jax.ShapeDtypeStruct((ROUTE_ROWS, t), F32),
            jax.ShapeDtypeStruct((n_tiles, SUBLANES, LANES), F32),
        ],
        scratch_shapes=[pltpu.VMEM((tm, LANES), F32)],
        compiler_params=pltpu.CompilerParams(
            dimension_semantics=("arbitrary",), vmem_limit_bytes=VMEM_LIMIT),
        name="merge_route",
    )(ret, hm, proj, proj, x2d, w_ret, w_m, w_out, norm_w, w_router, b_router, lower)


def _dispatch_kernel(fill_ref, n_groups_ref, gdst_ref, route_t_ref, h2_ref, xs_hbm, sorted_scr, zero_scr, sems):
    step = pl.program_id(0)
    last = pl.num_programs(0) - 1
    tm = h2_ref.shape[0]
    slot = step % 2

    def group_copy(which, j):
        return pltpu.make_async_copy(sorted_scr.at[which, pl.ds(j, 1)], xs_hbm.at[pl.ds(gdst_ref[j], 1)],
                                     sems.at[which])

    def wait_groups(which, count):
        def wait(j, carry):
            group_copy(which, 0).wait()
            return carry
        lax.fori_loop(0, count, wait, 0)

    @pl.when(step == 0)
    def _():
        zero_scr[...] = jnp.zeros_like(zero_scr)

        def fill_copy(j):
            g0 = pl.multiple_of(fill_ref[j], MOE_ROWS // SUBLANES)
            return pltpu.make_async_copy(zero_scr, xs_hbm.at[pl.ds(g0, MOE_ROWS // SUBLANES)], sems.at[0])

        def start_fill(j, carry):
            @pl.when(fill_ref[j] >= 0)
            def _():
                fill_copy(j).start()
            return carry

        def wait_fill(j, carry):
            @pl.when(fill_ref[j] >= 0)
            def _():
                fill_copy(j).wait()
            return carry

        lax.fori_loop(0, fill_ref.shape[0], start_fill, 0)
        lax.fori_loop(0, fill_ref.shape[0], wait_fill, 0)

    @pl.when(step >= 2)
    def _():
        wait_groups(slot, n_groups_ref[jnp.maximum(step - 2, 0)])

    pos = lax.broadcasted_iota(jnp.int32, (LOCAL_ROWS, tm), 0)
    row1 = route_t_ref[2:3, :].astype(jnp.int32)
    row2 = route_t_ref[3:4, :].astype(jnp.int32)
    pick = jnp.where((pos == row1) | (pos == row2), 1.0, 0.0).astype(BF16)
    tokens = _unpack_rows(h2_ref[...]).astype(BF16)
    ordered = jnp.dot(pick, tokens, preferred_element_type=F32)
    sorted_scr[slot] = _pack_rows(ordered).reshape(LOCAL_GROUPS, SUBLANES, PACKED)

    def start(j, carry):
        group_copy(slot, j).start()
        return carry

    lax.fori_loop(0, n_groups_ref[step], start, 0)

    @pl.when(step == last)
    def _():
        wait_groups(slot, n_groups_ref[step])

        @pl.when(step >= 1)
        def _():
            wait_groups(1 - slot, n_groups_ref[jnp.maximum(step - 1, 0)])


def _dispatch(fill_groups, n_groups, gdst, route_t, h2p, n_slots):
    t = h2p.shape[0]
    tm = min(ROWS_MERGE, t)
    return pl.pallas_call(
        _dispatch_kernel,
        grid=(t // tm,),
        in_specs=[
            pl.BlockSpec(memory_space=pltpu.SMEM),
            pl.BlockSpec(memory_space=pltpu.SMEM),
            pl.BlockSpec((GROUP_TABLE,), lambda i: (i,), memory_space=pltpu.SMEM),
            pl.BlockSpec((ROUTE_ROWS, tm), lambda i: (0, i)),
            pl.BlockSpec((tm, PACKED), lambda i: (i, 0)),
        ],
        out_specs=pl.BlockSpec(memory_space=pl.ANY),
        out_shape=jax.ShapeDtypeStruct((n_slots // SUBLANES, SUBLANES, PACKED), U32),
        scratch_shapes=[
            pltpu.VMEM((2, LOCAL_GROUPS, SUBLANES, PACKED), U32),
            pltpu.VMEM((MOE_ROWS // SUBLANES, SUBLANES, PACKED), U32),
            pltpu.SemaphoreType.DMA((2,)),
        ],
        compiler_params=pltpu.CompilerParams(
            dimension_semantics=("arbitrary",), vmem_limit_bytes=VMEM_LIMIT),
        name="dispatch",
    )(fill_groups, n_groups, gdst, route_t, h2p)


def _experts_kernel(blk_e_ref, blk_valid_ref, xs_ref, wg_ref, wu_ref, wd_ref, ys_ref, wgu_scr, wd_scr):
    i = pl.program_id(0)
    valid = blk_valid_ref[i]

    @pl.when((i == 0) | (blk_e_ref[i] != blk_e_ref[jnp.maximum(i - 1, 0)]))
    def _():
        wgu_scr[:, :D_EXPERT] = wg_ref[...].astype(BF16)
        wgu_scr[:, D_EXPERT:] = wu_ref[...].astype(BF16)
        wd_scr[...] = wd_ref[...].astype(BF16)

    @pl.when(valid > 0)
    def _():
        xb = _unpack_rows(xs_ref[...]).astype(BF16)
        gu = jnp.dot(xb, wgu_scr[...], preferred_element_type=F32)
        g = gu[:, :D_EXPERT]
        act = (g * jax.nn.sigmoid(g) * gu[:, D_EXPERT:]).astype(BF16)
        ys_ref[...] = _pack_rows(jnp.dot(act, wd_scr[...], preferred_element_type=F32))

    @pl.when(valid <= 0)
    def _():
        ys_ref[...] = jnp.zeros_like(ys_ref)


def _experts(blk_e, blk_valid, xs, w_gate, w_up, w_down):
    p = xs.shape[0]
    per_expert = lambda i, be, bv: (be[i], 0, 0)
    grid_spec = pltpu.PrefetchScalarGridSpec(
        num_scalar_prefetch=2,
        grid=(p // MOE_ROWS,),
        in_specs=[
            pl.BlockSpec((MOE_ROWS, PACKED), lambda i, be, bv: (i, 0)),
            pl.BlockSpec((None, D_MODEL, D_EXPERT), per_expert),
            pl.BlockSpec((None, D_MODEL, D_EXPERT), per_expert),
            pl.BlockSpec((None, D_EXPERT, D_MODEL), per_expert),
        ],
        out_specs=pl.BlockSpec((MOE_ROWS, PACKED), lambda i, be, bv: (i, 0)),
        scratch_shapes=[pltpu.VMEM((D_MODEL, 2 * D_EXPERT), BF16), pltpu.VMEM((D_EXPERT, D_MODEL), BF16)],
    )
    return pl.pallas_call(
        _experts_kernel,
        grid_spec=grid_spec,
        out_shape=jax.ShapeDtypeStruct((p, PACKED), U32),
        compiler_params=pltpu.CompilerParams(
            dimension_semantics=("arbitrary",), vmem_limit_bytes=VMEM_LIMIT),
        name="experts",
    )(blk_e, blk_valid, xs, w_gate, w_up, w_down)


def _combine_kernel(gdst_ref, route_ref, x1_ref, nw_ref, ys_hbm, o_ref, buf, sems):
    step = pl.program_id(0)
    n_tiles = pl.num_programs(0) - 1
    tm = x1_ref.shape[0]
    slot = step % 2
    prev = 1 - slot

    @pl.when(step < n_tiles)
    def _():
        def start(j, carry):
            pltpu.make_async_copy(ys_hbm.at[pl.ds(gdst_ref[j], 1)], buf.at[slot, pl.ds(j, 1)], sems.at[slot]).start()
            return carry
        lax.fori_loop(0, LOCAL_GROUPS, start, 0, unroll=8)

    @pl.when(step > 0)
    def _():
        pltpu.make_async_copy(ys_hbm.at[pl.ds(0, LOCAL_GROUPS)], buf.at[prev], sems.at[prev]).wait()

        rows = _unpack_rows(buf[prev].reshape(LOCAL_ROWS, PACKED)).astype(BF16)
        pos = lax.broadcasted_iota(jnp.int32, (tm, LOCAL_ROWS), 1)
        row1 = route_ref[:, 2:3].astype(jnp.int32)
        row2 = route_ref[:, 3:4].astype(jnp.int32)
        mix = (jnp.where(pos == row1, route_ref[:, 4:5], 0.0)
               + jnp.where(pos == row2, route_ref[:, 5:6], 0.0)).astype(BF16)
        x2 = x1_ref[...] + jnp.dot(mix, rows, preferred_element_type=F32)
        o_ref[...] = _rms(x2) * nw_ref[...]


def _combine(gdst, route, x1, norm_w, ys3):
    t = x1.shape[0]
    tm = min(ROWS_MERGE, t)
    n_tiles = t // tm
    gathered = lambda i: (jnp.minimum(i, n_tiles - 1),)
    finished = lambda i: (jnp.maximum(i - 1, 0), 0)
    return pl.pallas_call(
        _combine_kernel,
        grid=(n_tiles + 1,),
        in_specs=[
            pl.BlockSpec((GROUP_TABLE,), gathered, memory_space=pltpu.SMEM),
            pl.BlockSpec((tm, LANES), finished),
            pl.BlockSpec((tm, D_MODEL), finished),
            pl.BlockSpec((1, D_MODEL), lambda i: (0, 0)),
            pl.BlockSpec(memory_space=pl.ANY),
        ],
        out_specs=pl.BlockSpec((tm, D_MODEL), finished),
        out_shape=jax.ShapeDtypeStruct((t, D_MODEL), F32),
        scratch_shapes=[
            pltpu.VMEM((2, LOCAL_GROUPS, SUBLANES, PACKED), U32),
            pltpu.SemaphoreType.DMA((2,)),
        ],
        compiler_params=pltpu.CompilerParams(
            dimension_semantics=("arbitrary",), vmem_limit_bytes=VMEM_LIMIT),
        name="combine",
    )(gdst, route, x1, norm_w, ys3)


def _rotary_tables(seq):
    inv_freq = 1.0 / (ROPE_BASE ** (jnp.arange(0, QK_DIM, 2, dtype=F32) / QK_DIM))
    ang = jnp.arange(seq, dtype=F32)[:, None] * inv_freq[None, :]
    cos, sin = jnp.cos(ang), jnp.sin(ang)
    return jnp.concatenate([cos, cos], axis=1), jnp.concatenate([-sin, sin], axis=1)


def _retention_decay_tables():
    gamma = 1.0 - 2.0 ** (-5.0 - np.arange(N_HEADS, dtype=np.float64))
    idx = np.arange(CHUNK, dtype=np.float64) + 1.0
    dq = gamma[:, None] ** idx[None, :]
    dk = gamma[:, None] ** (-idx[None, :]) * QK_DIM ** -0.5
    bcast = lambda a: jnp.asarray(np.broadcast_to(a[:, :, None], (N_HEADS, CHUNK, QK_DIM)), F32)
    return bcast(dq), bcast(dk), tuple(float(g) for g in gamma ** CHUNK)


def kernel(x, norm_mix_w, w_in, ret_gn_w, w_ret_branch, mlstm_conv_w, mlstm_conv_b, b_igate, b_fgate,
           mlstm_gn_w, w_mlstm_branch, w_out, norm_ffn_w, w_group, b_group, w_expert_router,
           b_expert_router, w_gate, w_up, w_down, norm_final_w):
    assert norm_mix_w.shape[0] == 1, "one layer"
    b, s, d = x.shape
    t = b * s
    assert d == D_MODEL and s % CHUNK == 0

    wi = w_in[0]
    n_pre = 2 * MIX_COLS
    w_big = jnp.concatenate([wi[:, :n_pre], wi[:, n_pre + 2 * N_HEADS:]], axis=1).astype(BF16)
    w_if = jnp.pad(wi[:, n_pre:n_pre + 2 * N_HEADS], ((0, 0), (0, LANES - 2 * N_HEADS))).astype(BF16)
    gate_bias = jnp.pad(jnp.concatenate([b_igate[0], b_fgate[0]]), (0, LANES - 2 * N_HEADS))[None, :]
    w_router = jnp.pad(jnp.concatenate([w_expert_router[0], w_group[0]], axis=1),
                       ((0, 0), (0, LANES - N_EXPERTS - N_GROUPS))).astype(BF16)
    b_router = jnp.pad(jnp.concatenate([b_expert_router[0], b_group[0]]),
                       (0, LANES - N_EXPERTS - N_GROUPS))[None, :]

    cosf, sinf = _rotary_tables(s)
    dq, dk, chunk_decay = _retention_decay_tables()
    tm_merge = min(ROWS_MERGE, t)
    lower = jnp.tril(jnp.ones((tm_merge, tm_merge), F32), -1).astype(BF16)

    x2d = x.reshape(t, d)
    proj, gates = _in_projection(x2d, norm_mix_w, w_big, w_if)
    proj3 = proj.reshape(b, s, N_BIG)
    ret = _retention(proj3, cosf, sinf, dq, dk, ret_gn_w, chunk_decay)
    hm = _mlstm(proj3, gates.reshape(b, s, LANES), mlstm_conv_w[0, :, 0, :], mlstm_conv_b,
                gate_bias, mlstm_gn_w)
    x1, h2p, route, route_t, tiles = _merge_route(
        ret.reshape(t, V_ALL), hm.reshape(t, V_ALL), proj, x2d, w_ret_branch[0].astype(BF16),
        w_mlstm_branch[0].astype(BF16), w_out[0].astype(BF16), norm_ffn_w, w_router, b_router, lower)

    n_tiles = t // tm_merge
    n_slots = 2 * t + N_EXPERTS * (MOE_ROWS + n_tiles * (SUBLANES - 1) // SUBLANES * SUBLANES)
    n_slots = -(-n_slots // MOE_ROWS) * MOE_ROWS
    run = tiles[:, 1, :N_EXPERTS].astype(jnp.int32)
    run_start = tiles[:, 2, :N_EXPERTS].astype(jnp.int32)
    rows_e = jnp.sum(run, axis=0)
    padded = (rows_e + MOE_ROWS - 1) // MOE_ROWS * MOE_ROWS
    expert_ids = jnp.arange(N_EXPERTS, dtype=jnp.int32)
    pstart = jnp.sum(jnp.where(expert_ids[None, :] < expert_ids[:, None], padded[None, :], 0), axis=1)
    pend = pstart + padded
    tile_ids = jnp.arange(n_tiles, dtype=jnp.int32)
    earlier = jnp.sum(jnp.where((tile_ids[None, :] < tile_ids[:, None])[:, :, None], run[None, :, :], 0), axis=1)
    global_start = pstart[None, :] + earlier
    local_row = jnp.arange(LOCAL_GROUPS, dtype=jnp.int32)[None, :, None] * SUBLANES
    in_run = (local_row >= run_start[:, None, :]) & (local_row < (run_start + run)[:, None, :])
    gdst = jnp.sum(jnp.where(in_run, (global_start - run_start)[:, None, :] + local_row, 0), axis=2) // SUBLANES
    gdst = jnp.pad(gdst, ((0, 0), (0, GROUP_TABLE - LOCAL_GROUPS))).reshape(n_tiles * GROUP_TABLE).astype(jnp.int32)
    n_groups = (jnp.sum(run, axis=1) // SUBLANES).astype(jnp.int32)
    blk_start = jnp.arange(n_slots // MOE_ROWS, dtype=jnp.int32) * MOE_ROWS
    blk_e = jnp.minimum(jnp.sum(blk_start[:, None] >= pend[None, :], axis=-1), N_EXPERTS - 1).astype(jnp.int32)
    blk_valid = jnp.clip(pstart[blk_e] + rows_e[blk_e] - blk_start, 0, MOE_ROWS).astype(jnp.int32)
    tail = pend[-1] + jnp.arange((n_slots - 2 * t) // MOE_ROWS, dtype=jnp.int32) * MOE_ROWS
    fill_groups = (jnp.concatenate([jnp.where(padded > 0, pend - MOE_ROWS, -SUBLANES),
                                    jnp.where(tail < n_slots, tail, -SUBLANES)]) // SUBLANES).astype(jnp.int32)

    xs3 = _dispatch(fill_groups, n_groups, gdst, route_t, h2p, n_slots)
    ys = _experts(blk_e, blk_valid, xs3.reshape(n_slots, PACKED), w_gate[0], w_up[0], w_down[0])
    out = _combine(gdst, route, x1, norm_final_w[None, :],
                   ys.reshape(n_slots // SUBLANES, SUBLANES, PACKED))
    return out.reshape(b, s, d)
```

```python
import functools

import numpy as np
import jax
import jax.numpy as jnp
from jax import lax
from jax.experimental import pallas as pl
from jax.experimental.pallas import tpu as pltpu

F32 = jnp.float32
BF16 = jnp.bfloat16
U32 = jnp.uint32

D_MODEL = 1024
N_HEADS = 4
QK_DIM = 128
V_DIM = 256
CHUNK = 128
CONV_WIDTH = 4
ROPE_BASE = 10000.0
N_GROUPS = 4
EXPERTS_PER_GROUP = 8
N_EXPERTS = N_GROUPS * EXPERTS_PER_GROUP
D_EXPERT = 512
NORM_EPS = 1e-6
QK_ALL = N_HEADS * QK_DIM
V_ALL = N_HEADS * V_DIM

MIX_COLS = 2 * QK_ALL + 2 * V_ALL
N_BIG = 2 * MIX_COLS + 2 * D_MODEL
LANES = 128
PACKED = D_MODEL // 2

ROWS_PROJ = 2048
COLS_PROJ = 2048
RET_CHUNKS_PER_STEP = 4
MLSTM_CHUNKS_PER_STEP = 4
ROWS_MERGE = 512
ROUTE_ROWS = 8
SUBLANES = 8
LOCAL_ROWS = 2 * ROWS_MERGE + 256
assert LOCAL_ROWS >= 2 * ROWS_MERGE + N_EXPERTS * (SUBLANES - 1) and LOCAL_ROWS % LANES == 0
LOCAL_GROUPS = LOCAL_ROWS // SUBLANES
GROUP_TABLE = 256
assert GROUP_TABLE >= LOCAL_GROUPS
MOE_ROWS = 512
V7X_VMEM_BYTES = 64 * 1024 * 1024
VMEM_LIMIT = V7X_VMEM_BYTES - 8 * 1024 * 1024
VMEM_LIMIT_PROJ = V7X_VMEM_BYTES - 4 * 1024 * 1024


def _rms(x, eps=NORM_EPS):
    return x * lax.rsqrt(jnp.mean(x * x, axis=-1, keepdims=True) + eps)


def _pack_rows(x):
    lo = lax.bitcast_convert_type(x[:, :PACKED].astype(BF16).astype(F32), U32)
    hi = lax.bitcast_convert_type(x[:, PACKED:].astype(BF16).astype(F32), U32)
    return (hi & jnp.uint32(0xFFFF0000)) | (lo >> 16)


def _unpack_rows(w):
    lo = lax.bitcast_convert_type(w << 16, F32)
    hi = lax.bitcast_convert_type(w & jnp.uint32(0xFFFF0000), F32)
    return jnp.concatenate([lo, hi], axis=1)


def _in_proj_kernel(x_ref, nw_ref, w_ref, wif_ref, o_ref, gates_ref, h_scr):
    @pl.when(pl.program_id(1) == 0)
    def _():
        h = (_rms(x_ref[...]) * nw_ref[...]).astype(BF16)
        h_scr[...] = h
        gates_ref[...] = jnp.dot(h, wif_ref[...], preferred_element_type=F32)

    o_ref[...] = jnp.dot(h_scr[...], w_ref[...], preferred_element_type=F32).astype(o_ref.dtype)


def _in_projection(x2d, norm_w, w_big, w_if):
    t = x2d.shape[0]
    tm = min(ROWS_PROJ, t)
    tn = COLS_PROJ
    return pl.pallas_call(
        _in_proj_kernel,
        grid=(t // tm, N_BIG // tn),
        in_specs=[
            pl.BlockSpec((tm, D_MODEL), lambda i, j: (i, 0)),
            pl.BlockSpec((1, D_MODEL), lambda i, j: (0, 0)),
            pl.BlockSpec((D_MODEL, tn), lambda i, j: (0, j)),
            pl.BlockSpec((D_MODEL, LANES), lambda i, j: (0, 0)),
        ],
        out_specs=[
            pl.BlockSpec((tm, tn), lambda i, j: (i, j)),
            pl.BlockSpec((tm, LANES), lambda i, j: (i, 0)),
        ],
        out_shape=[
            jax.ShapeDtypeStruct((t, N_BIG), BF16),
            jax.ShapeDtypeStruct((t, LANES), F32),
        ],
        scratch_shapes=[pltpu.VMEM((tm, D_MODEL), BF16)],
        compiler_params=pltpu.CompilerParams(
            dimension_semantics=("arbitrary", "arbitrary"), vmem_limit_bytes=VMEM_LIMIT_PROJ),
        name="in_projection",
    )(x2d, norm_w, w_big, w_if)


def _retention_kernel(p_ref, cos_ref, sin_ref, dq_ref, dk_ref, gn_ref, o_ref, state_scr, *, chunk_decay):
    L = CHUNK

    @pl.when(pl.program_id(1) == 0)
    def _():
        state_scr[...] = jnp.zeros_like(state_scr)

    row = lax.broadcasted_iota(jnp.int32, (L, L), 0)
    col = lax.broadcasted_iota(jnp.int32, (L, L), 1)
    causal = row >= col
    n_chunks = p_ref.shape[0] // L
    units = [(ci * L, h) for ci in range(n_chunks) for h in range(N_HEADS)]

    qts, kts, scores = [], [], []
    for r0, h in units:
        cosf = cos_ref[r0:r0 + L, :]
        sinf = sin_ref[r0:r0 + L, :]
        q = p_ref[r0:r0 + L, h * QK_DIM:(h + 1) * QK_DIM].astype(F32)
        k = p_ref[r0:r0 + L, QK_ALL + h * QK_DIM:QK_ALL + (h + 1) * QK_DIM].astype(F32)
        qt = ((q * cosf + pltpu.roll(q, QK_DIM // 2, 1) * sinf) * dq_ref[h]).astype(BF16)
        kt = ((k * cosf + pltpu.roll(k, QK_DIM // 2, 1) * sinf) * dk_ref[h]).astype(BF16)
        s = lax.dot_general(qt, kt, (((1,), (1,)), ((), ())), preferred_element_type=F32)
        qts.append(qt)
        kts.append(kt)
        scores.append(jnp.where(causal, s, 0.0).astype(BF16))

    states = [state_scr[h] for h in range(N_HEADS)]
    for (r0, h), qt, kt, s in zip(units, qts, kts, scores):
        v = p_ref[r0:r0 + L, 2 * QK_ALL + h * V_DIM:2 * QK_ALL + (h + 1) * V_DIM]
        g = p_ref[r0:r0 + L, 2 * QK_ALL + V_ALL + h * V_DIM:2 * QK_ALL + V_ALL + (h + 1) * V_DIM].astype(F32)
        lhs = jnp.concatenate([s, qt], axis=1)
        rhs = jnp.concatenate([v, states[h].astype(BF16)], axis=0)
        o = jnp.dot(lhs, rhs, preferred_element_type=F32)
        kv = lax.dot_general(kt, v, (((0,), (0,)), ((), ())), preferred_element_type=F32)
        states[h] = (states[h] + kv) * chunk_decay[h]
        y = _rms(o) * gn_ref[:, h * V_DIM:(h + 1) * V_DIM] * (g * jax.nn.sigmoid(g))
        o_ref[r0:r0 + L, h * V_DIM:(h + 1) * V_DIM] = y.astype(o_ref.dtype)
    for h in range(N_HEADS):
        state_scr[h] = states[h]


def _retention(proj3, cosf, sinf, dq, dk, gn_w, chunk_decay):
    b, s, _ = proj3.shape
    lb = CHUNK * min(RET_CHUNKS_PER_STEP, s // CHUNK)
    return pl.pallas_call(
        functools.partial(_retention_kernel, chunk_decay=chunk_decay),
        grid=(b, s // lb),
        in_specs=[
            pl.BlockSpec((None, lb, MIX_COLS), lambda i, c: (i, c, 0)),
            pl.BlockSpec((lb, QK_DIM), lambda i, c: (c, 0)),
            pl.BlockSpec((lb, QK_DIM), lambda i, c: (c, 0)),
            pl.BlockSpec((N_HEADS, CHUNK, QK_DIM), lambda i, c: (0, 0, 0)),
            pl.BlockSpec((N_HEADS, CHUNK, QK_DIM), lambda i, c: (0, 0, 0)),
            pl.BlockSpec((1, V_ALL), lambda i, c: (0, 0)),
        ],
        out_specs=pl.BlockSpec((None, lb, V_ALL), lambda i, c: (i, c, 0)),
        out_shape=jax.ShapeDtypeStruct((b, s, V_ALL), BF16),
        scratch_shapes=[pltpu.VMEM((N_HEADS, QK_DIM, V_DIM), F32)],
        compiler_params=pltpu.CompilerParams(
            dimension_semantics=("arbitrary", "arbitrary"), vmem_limit_bytes=VMEM_LIMIT),
        name="retention",
    )(proj3, cosf, sinf, dq, dk, gn_w)


def _mlstm_kernel(p_ref, gates_ref, cw_ref, cb_ref, shift_ref, gb_ref, gn_ref, o_ref,
                  c_scr, n_scr, m_scr, tail_scr, act_scr):
    L = CHUNK
    lb = p_ref.shape[0]

    @pl.when(pl.program_id(1) == 0)
    def _():
        c_scr[...] = jnp.zeros_like(c_scr)
        n_scr[...] = jnp.zeros_like(n_scr)
        m_scr[...] = jnp.zeros_like(m_scr)
        tail_scr[...] = jnp.zeros_like(tail_scr)

    ub = p_ref[:, 0:2 * QK_ALL]
    u = ub.astype(F32)
    tail = tail_scr[...]
    row8 = lax.broadcasted_iota(jnp.int32, tail.shape, 0)
    acc = u * cw_ref[CONV_WIDTH - 1:CONV_WIDTH, :] + cb_ref[...]
    head = jnp.zeros_like(tail)
    for d in range(1, CONV_WIDTH):
        w_d = cw_ref[CONV_WIDTH - 1 - d:CONV_WIDTH - d, :]
        acc = acc + jnp.dot(shift_ref[d - 1], ub, preferred_element_type=F32) * w_d
        head = head + jnp.where(row8 < d, pltpu.roll(tail, d, 0), 0.0) * w_d
    acc = jnp.concatenate([acc[0:SUBLANES, :] + head, acc[SUBLANES:, :]], axis=0)
    tail_scr[...] = u[lb - SUBLANES:lb, :]
    act_scr[...] = acc * jax.nn.sigmoid(acc)

    row = lax.broadcasted_iota(jnp.int32, (L, L), 0)
    col = lax.broadcasted_iota(jnp.int32, (L, L), 1)
    causal = row >= col
    k_scale = QK_DIM ** -0.5
    units = [(ci * L, h) for ci in range(lb // L) for h in range(N_HEADS)]

    def load_qk(r0, h):
        q = act_scr[r0:r0 + L, h * QK_DIM:(h + 1) * QK_DIM]
        k = act_scr[r0:r0 + L, QK_ALL + h * QK_DIM:QK_ALL + (h + 1) * QK_DIM] * k_scale
        return q, k

    gate_terms, src_rows = [], []
    lane_t = lax.broadcasted_iota(jnp.int32, (2 * N_HEADS, L), 1)
    for ci in range(lb // L):
        pre = gates_ref[ci * L:(ci + 1) * L, :] + gb_ref[...]
        pre_rows = jnp.transpose(pre)[0:2 * N_HEADS, :]
        b_rows = jnp.minimum(pre_rows, 0.0) - jnp.log1p(jnp.exp(-jnp.abs(pre_rows)))
        shift = 1
        while shift < L:
            b_rows = b_rows + jnp.where(lane_t >= shift, pltpu.roll(b_rows, shift, 1), 0.0)
            shift *= 2
        src_rows.append(pre_rows[0:N_HEADS, :] - b_rows[N_HEADS:2 * N_HEADS, :])
        bcum = jnp.transpose(jnp.concatenate([b_rows, jnp.zeros((L - 2 * N_HEADS, L), F32)], axis=0))
        gate_terms.append((pre, bcum))
    scores = []
    for r0, h in units:
        q, k = load_qk(r0, h)
        scores.append(lax.dot_general(q.astype(BF16), k.astype(BF16), (((1,), (1,)), ((), ())),
                                      preferred_element_type=F32))

    m_state = [m_scr[h:h + 1, :] for h in range(N_HEADS)]
    terms = []
    for (r0, h), qk in zip(units, scores):
        pre, bcum = gate_terms[r0 // L]
        q, k = load_qk(r0, h)
        b_t = jnp.broadcast_to(bcum[:, N_HEADS + h:N_HEADS + h + 1], (L, L))
        i_t = jnp.broadcast_to(pre[:, h:h + 1], (L, L))
        src = jnp.broadcast_to(src_rows[r0 // L][h:h + 1, :], (L, L))
        m_prev = m_state[h]
        a = b_t + m_prev
        dmat = jnp.where(causal, b_t + src, -jnp.inf)
        m_t = jnp.maximum(a, jnp.max(dmat, axis=-1, keepdims=True))
        w_inter = jnp.exp(a - m_t)
        s = qk * jnp.exp(dmat - m_t)
        lhs = jnp.concatenate([s.astype(BF16), (q * w_inter).astype(BF16)], axis=1)
        b_last = b_t[L - 1:L, :]
        gk = b_last - b_t + i_t
        m_new = jnp.maximum(b_last + m_prev, jnp.max(gk, axis=0, keepdims=True))
        wk = k * jnp.exp(gk - m_new)
        m_state[h] = m_new
        terms.append(dict(
            lhs=lhs, s_sum=jnp.sum(s, axis=-1, keepdims=True), w_inter=w_inter, floor=jnp.exp(-m_t),
            keep=jnp.exp(b_last + m_prev - m_new), wk=wk.astype(BF16), wk_sum=jnp.sum(wk, axis=0, keepdims=True)))

    n_state = [n_scr[h:h + 1, :] for h in range(N_HEADS)]
    c_state = [c_scr[h] for h in range(N_HEADS)]
    v_cols = lambda h: slice(2 * QK_ALL + h * V_DIM, 2 * QK_ALL + (h + 1) * V_DIM)
    for ci in range(lb // L):
        r0 = ci * L
        chunk_terms = terms[ci * N_HEADS:(ci + 1) * N_HEADS]
        q_dot_n = [jnp.sum(load_qk(r0, h)[0] * n_state[h], axis=-1, keepdims=True) for h in range(N_HEADS)]
        nums = []
        for h, t in enumerate(chunk_terms):
            rhs = jnp.concatenate([p_ref[r0:r0 + L, v_cols(h)], c_state[h].astype(BF16)], axis=0)
            nums.append(jnp.dot(t["lhs"], rhs, preferred_element_type=F32))
        for h, t in enumerate(chunk_terms):
            c_state[h] = c_state[h] * t["keep"][:, 0:1] + lax.dot_general(
                t["wk"], p_ref[r0:r0 + L, v_cols(h)], (((0,), (0,)), ((), ())), preferred_element_type=F32)
            n_state[h] = n_state[h] * t["keep"] + t["wk_sum"]
        for h, t in enumerate(chunk_terms):
            og = p_ref[r0:r0 + L, V_ALL + v_cols(h).start:V_ALL + v_cols(h).stop].astype(F32)
            den = t["s_sum"] + q_dot_n[h] * t["w_inter"]
            inv = 1.0 / jnp.maximum(jnp.abs(den), t["floor"])
            hh = nums[h] * jnp.concatenate([inv] * (V_DIM // LANES), axis=1)
            y = _rms(hh * jax.nn.sigmoid(og)) * gn_ref[:, h * V_DIM:(h + 1) * V_DIM]
            o_ref[r0:r0 + L, h * V_DIM:(h + 1) * V_DIM] = y.astype(o_ref.dtype)

    for h in range(N_HEADS):
        c_scr[h] = c_state[h]
        n_scr[h:h + 1, :] = n_state[h]
        m_scr[h:h + 1, :] = m_state[h]


def _mlstm(proj3, gates3, conv_w, conv_b, gate_bias, gn_w):
    b, s, _ = proj3.shape
    lb = CHUNK * min(MLSTM_CHUNKS_PER_STEP, s // CHUNK)
    shifts = jnp.stack([jnp.eye(lb, k=-d, dtype=BF16) for d in range(1, CONV_WIDTH)])
    return pl.pallas_call(
        _mlstm_kernel,
        grid=(b, s // lb),
        in_specs=[
            pl.BlockSpec((None, lb, MIX_COLS), lambda i, c: (i, c, 1)),
            pl.BlockSpec((None, lb, LANES), lambda i, c: (i, c, 0)),
            pl.BlockSpec((CONV_WIDTH, 2 * QK_ALL), lambda i, c: (0, 0)),
            pl.BlockSpec((1, 2 * QK_ALL), lambda i, c: (0, 0)),
            pl.BlockSpec((CONV_WIDTH - 1, lb, lb), lambda i, c: (0, 0, 0)),
            pl.BlockSpec((1, LANES), lambda i, c: (0, 0)),
            pl.BlockSpec((1, V_ALL), lambda i, c: (0, 0)),
        ],
        out_specs=pl.BlockSpec((None, lb, V_ALL), lambda i, c: (i, c, 0)),
        out_shape=jax.ShapeDtypeStruct((b, s, V_ALL), BF16),
        scratch_shapes=[
            pltpu.VMEM((N_HEADS, QK_DIM, V_DIM), F32),
            pltpu.VMEM((8, QK_DIM), F32),
            pltpu.VMEM((8, LANES), F32),
            pltpu.VMEM((8, 2 * QK_ALL), F32),
            pltpu.VMEM((lb, 2 * QK_ALL), F32),
        ],
        compiler_params=pltpu.CompilerParams(
            dimension_semantics=("arbitrary", "arbitrary"), vmem_limit_bytes=VMEM_LIMIT),
        name="mlstm",
    )(proj3, gates3, conv_w, conv_b, shifts, gate_bias, gn_w)


def _merge_route_kernel(ret_ref, hm_ref, gr_ref, gm_ref, x_ref, wr_ref, wm_ref, wo_ref, nw_ref,
                        wrt_ref, brt_ref, lower_ref, x1_ref, h2_ref, route_ref, route_t_ref, tile_ref,
                        logits_scr):
    step = pl.program_id(0)

    @pl.when(step == 0)
    def _():
        logits_scr[...] = jnp.zeros_like(logits_scr)

    logits = logits_scr[...]

    y_ret = jnp.dot(ret_ref[...], wr_ref[...], preferred_element_type=F32)
    y_m = jnp.dot(hm_ref[...], wm_ref[...], preferred_element_type=F32)
    merged = (jax.nn.sigmoid(gr_ref[...].astype(F32)) * y_ret
              + jax.nn.sigmoid(gm_ref[...].astype(F32)) * y_m)
    x1 = x_ref[...] + jnp.dot(merged.astype(BF16), wo_ref[...], preferred_element_type=F32)
    x1_ref[...] = x1
    h2 = _rms(x1) * nw_ref[...]
    h2_ref[...] = _pack_rows(h2)
    new_logits = jnp.dot(h2.astype(BF16), wrt_ref[...], preferred_element_type=F32) + brt_ref[...]

    live = jnp.where(step > 0, 1.0, 0.0)
    tm = logits.shape[0]
    lane = lax.broadcasted_iota(jnp.int32, (tm, LANES), 1)
    neg = -jnp.inf
    big = jnp.int32(LANES)
    is_group = (lane >= N_EXPERTS) & (lane < N_EXPERTS + N_GROUPS)
    gl = jnp.where(is_group, logits, neg)
    g_max = jnp.max(gl, axis=-1, keepdims=True)
    g_idx = jnp.min(jnp.where(gl == g_max, lane, big), axis=-1, keepdims=True) - N_EXPERTS
    g_w = 1.0 / jnp.sum(jnp.exp(gl - g_max), axis=-1, keepdims=True)
    in_group = (lane >= g_idx * EXPERTS_PER_GROUP) & (lane < (g_idx + 1) * EXPERTS_PER_GROUP)
    el = jnp.where(in_group, logits, neg)
    l1 = jnp.max(el, axis=-1, keepdims=True)
    e1 = jnp.min(jnp.where(el == l1, lane, big), axis=-1, keepdims=True)
    el2 = jnp.where(lane == e1, neg, el)
    l2 = jnp.max(el2, axis=-1, keepdims=True)
    e2 = jnp.min(jnp.where(el2 == l2, lane, big), axis=-1, keepdims=True)
    t21 = jnp.exp(l2 - l1)
    w1 = g_w / (1.0 + t21)
    w2 = g_w * t21 / (1.0 + t21)

    hit1 = lane == e1
    hit2 = lane == e2
    cnt = jnp.where(hit1 | hit2, live, 0.0)
    before = jnp.dot(lower_ref[...], cnt.astype(BF16), preferred_element_type=F32)
    count = jnp.sum(cnt, axis=0, keepdims=True)
    run = jnp.floor((count + (SUBLANES - 1)) * (1.0 / SUBLANES)) * SUBLANES
    lane1 = lax.broadcasted_iota(jnp.int32, (1, LANES), 1)
    run_end = run
    shift = 1
    while shift < N_EXPERTS:
        run_end = run_end + jnp.where(lane1 >= shift, pltpu.roll(run_end, shift, 1), 0.0)
        shift *= 2
    run_start = run_end - run
    local = before + run_start
    r1 = jnp.sum(jnp.where(hit1, local, 0.0), axis=-1, keepdims=True)
    r2 = jnp.sum(jnp.where(hit2, local, 0.0), axis=-1, keepdims=True)
    sub8 = lax.broadcasted_iota(jnp.int32, (SUBLANES, LANES), 0)
    tile_ref[...] = jnp.where(sub8 == 0, count, jnp.where(sub8 == 1, run, jnp.where(sub8 == 2, run_start, 0.0)))

    fields = (e1.astype(F32), e2.astype(F32), r1, r2, w1, w2)
    packed = jnp.zeros((tm, LANES), F32)
    for idx, val in enumerate(fields):
        packed = jnp.where(lane == idx, val, packed)
    route_ref[...] = packed
    route_t_ref[...] = jnp.transpose(packed)[0:ROUTE_ROWS, :]
    logits_scr[...] = new_logits


def _merge_route(ret, hm, proj, x2d, w_ret, w_m, w_out, norm_w, w_router, b_router, lower):
    t = x2d.shape[0]
    tm = min(ROWS_MERGE, t)
    n_tiles = t // tm
    gate_r_blk = 2 * MIX_COLS // D_MODEL
    tile = lambda i: jnp.minimum(i, n_tiles - 1)
    routed = lambda i: jnp.maximum(i - 1, 0)
    row_blk = lambda i: (tile(i), 0)
    const = lambda i: (0, 0)
    return pl.pallas_call(
        _merge_route_kernel,
        grid=(n_tiles + 1,),
        in_specs=[
            pl.BlockSpec((tm, V_ALL), row_blk),
            pl.BlockSpec((tm, V_ALL), row_blk),
            pl.BlockSpec((tm, D_MODEL), lambda i: (tile(i), gate_r_blk)),
            pl.BlockSpec((tm, D_MODEL), lambda i: (tile(i), gate_r_blk + 1)),
            pl.BlockSpec((tm, D_MODEL), row_blk),
            pl.BlockSpec((V_ALL, D_MODEL), const),
            pl.BlockSpec((V_ALL, D_MODEL), const),
            pl.BlockSpec((D_MODEL, D_MODEL), const),
            pl.BlockSpec((1, D_MODEL), const),
            pl.BlockSpec((D_MODEL, LANES), const),
            pl.BlockSpec((1, LANES), const),
            pl.BlockSpec((tm, tm), const),
        ],
        out_specs=[
            pl.BlockSpec((tm, D_MODEL), row_blk),
            pl.BlockSpec((tm, PACKED), row_blk),
            pl.BlockSpec((tm, LANES), lambda i: (routed(i), 0)),
            pl.BlockSpec((ROUTE_ROWS, tm), lambda i: (0, routed(i))),
            pl.BlockSpec((None, SUBLANES, LANES), lambda i: (routed(i), 0, 0)),
        ],
        out_shape=[
            jax.ShapeDtypeStruct((t, D_MODEL), F32),
            jax.ShapeDtypeStruct((t, PACKED), U32),
            jax.ShapeDtypeStruct((t, LANES), F32),
            jax.ShapeDtypeStruct((ROUTE_ROWS, t), F32),
            jax.ShapeDtypeStruct((n_tiles, SUBLANES, LANES), F32),
        ],
        scratch_shapes=[pltpu.VMEM((tm, LANES), F32)],
        compiler_params=pltpu.CompilerParams(
            dimension_semantics=("arbitrary",), vmem_limit_bytes=VMEM_LIMIT),
        name="merge_route",
    )(ret, hm, proj, proj, x2d, w_ret, w_m, w_out, norm_w, w_router, b_router, lower)


def _dispatch_kernel(fill_ref, n_groups_ref, gdst_ref, route_t_ref, h2_ref, xs_hbm, sorted_scr, zero_scr, sems):
    step = pl.program_id(0)
    n_tiles = pl.num_programs(0) - 1
    tm = h2_ref.shape[0]
    slot = step % 2

    def group_copy(which, j):
        return pltpu.make_async_copy(sorted_scr.at[which, pl.ds(j, 1)], xs_hbm.at[pl.ds(gdst_ref[j], 1)],
                                     sems.at[which])

    def wait_groups(which, count):
        def wait(j, carry):
            group_copy(which, 0).wait()
            return carry
        lax.fori_loop(0, count, wait, 0)

    @pl.when(step == 0)
    def _():
        zero_scr[...] = jnp.zeros_like(zero_scr)

        def fill_copy(j):
            g0 = pl.multiple_of(fill_ref[j], MOE_ROWS // SUBLANES)
            return pltpu.make_async_copy(zero_scr, xs_hbm.at[pl.ds(g0, MOE_ROWS // SUBLANES)], sems.at[0])

        def start_fill(j, carry):
            @pl.when(fill_ref[j] >= 0)
            def _():
                fill_copy(j).start()
            return carry

        def wait_fill(j, carry):
            @pl.when(fill_ref[j] >= 0)
            def _():
                fill_copy(j).wait()
            return carry

        lax.fori_loop(0, fill_ref.shape[0], start_fill, 0)
        lax.fori_loop(0, fill_ref.shape[0], wait_fill, 0)

    @pl.when(step >= 1)
    def _():
        def start(j, carry):
            group_copy(1 - slot, j).start()
            return carry
        lax.fori_loop(0, n_groups_ref[jnp.maximum(step - 1, 0)], start, 0)

    @pl.when(step >= 2)
    def _():
        wait_groups(slot, n_groups_ref[jnp.maximum(step - 2, 0)])

    @pl.when(step < n_tiles)
    def _():
        pos = lax.broadcasted_iota(jnp.int32, (LOCAL_ROWS, tm), 0)
        row1 = route_t_ref[2:3, :].astype(jnp.int32)
        row2 = route_t_ref[3:4, :].astype(jnp.int32)
        pick = jnp.where((pos == row1) | (pos == row2), 1.0, 0.0).astype(BF16)
        tokens = _unpack_rows(h2_ref[...]).astype(BF16)
        ordered = jnp.dot(pick, tokens, preferred_element_type=F32)
        sorted_scr[slot] = _pack_rows(ordered).reshape(LOCAL_GROUPS, SUBLANES, PACKED)

    @pl.when(step == n_tiles)
    def _():
        wait_groups(1 - slot, n_groups_ref[jnp.maximum(step - 1, 0)])


def _dispatch(fill_groups, n_groups, gdst, route_t, h2p, n_slots):
    t = h2p.shape[0]
    tm = min(ROWS_MERGE, t)
    n_tiles = t // tm
    sorted_tile = lambda i: jnp.minimum(i, n_tiles - 1)
    return pl.pallas_call(
        _dispatch_kernel,
        grid=(n_tiles + 1,),
        in_specs=[
            pl.BlockSpec(memory_space=pltpu.SMEM),
            pl.BlockSpec(memory_space=pltpu.SMEM),
            pl.BlockSpec((GROUP_TABLE,), lambda i: (jnp.maximum(i - 1, 0),), memory_space=pltpu.SMEM),
            pl.BlockSpec((ROUTE_ROWS, tm), lambda i: (0, sorted_tile(i))),
            pl.BlockSpec((tm, PACKED), lambda i: (sorted_tile(i), 0)),
        ],
        out_specs=pl.BlockSpec(memory_space=pl.ANY),
        out_shape=jax.ShapeDtypeStruct((n_slots // SUBLANES, SUBLANES, PACKED), U32),
        scratch_shapes=[
            pltpu.VMEM((2, LOCAL_GROUPS, SUBLANES, PACKED), U32),
            pltpu.VMEM((MOE_ROWS // SUBLANES, SUBLANES, PACKED), U32),
            pltpu.SemaphoreType.DMA((2,)),
        ],
        compiler_params=pltpu.CompilerParams(
            dimension_semantics=("arbitrary",), vmem_limit_bytes=VMEM_LIMIT),
        name="dispatch",
    )(fill_groups, n_groups, gdst, route_t, h2p)


def _experts_kernel(blk_e_ref, blk_valid_ref, xs_ref, wg_ref, wu_ref, wd_ref, ys_ref, wgu_scr, wd_scr):
    i = pl.program_id(0)
    valid = blk_valid_ref[i]

    @pl.when((i == 0) | (blk_e_ref[i] != blk_e_ref[jnp.maximum(i - 1, 0)]))
    def _():
        wgu_scr[:, :D_EXPERT] = wg_ref[...].astype(BF16)
        wgu_scr[:, D_EXPERT:] = wu_ref[...].astype(BF16)
        wd_scr[...] = wd_ref[...].astype(BF16)

    @pl.when(valid > 0)
    def _():
        xb = _unpack_rows(xs_ref[...]).astype(BF16)
        gu = jnp.dot(xb, wgu_scr[...], preferred_element_type=F32)
        g = gu[:, :D_EXPERT]
        act = (g * jax.nn.sigmoid(g) * gu[:, D_EXPERT:]).astype(BF16)
        ys_ref[...] = _pack_rows(jnp.dot(act, wd_scr[...], preferred_element_type=F32))

    @pl.when(valid <= 0)
    def _():
        ys_ref[...] = jnp.zeros_like(ys_ref)


def _experts(blk_e, blk_valid, xs, w_gate, w_up, w_down):
    p = xs.shape[0]
    per_expert = lambda i, be, bv: (be[i], 0, 0)
    grid_spec = pltpu.PrefetchScalarGridSpec(
        num_scalar_prefetch=2,
        grid=(p // MOE_ROWS,),
        in_specs=[
            pl.BlockSpec((MOE_ROWS, PACKED), lambda i, be, bv: (i, 0)),
            pl.BlockSpec((None, D_MODEL, D_EXPERT), per_expert),
            pl.BlockSpec((None, D_MODEL, D_EXPERT), per_expert),
            pl.BlockSpec((None, D_EXPERT, D_MODEL), per_expert),
        ],
        out_specs=pl.BlockSpec((MOE_ROWS, PACKED), lambda i, be, bv: (i, 0)),
        scratch_shapes=[pltpu.VMEM((D_MODEL, 2 * D_EXPERT), BF16), pltpu.VMEM((D_EXPERT, D_MODEL), BF16)],
    )
    return pl.pallas_call(
        _experts_kernel,
        grid_spec=grid_spec,
        out_shape=jax.ShapeDtypeStruct((p, PACKED), U32),
        compiler_params=pltpu.CompilerParams(
            dimension_semantics=("arbitrary",), vmem_limit_bytes=VMEM_LIMIT),
        name="experts",
    )(blk_e, blk_valid, xs, w_gate, w_up, w_down)


def _combine_kernel(gdst_ref, route_ref, x1_ref, nw_ref, ys_hbm, o_ref, buf, sems):
    step = pl.program_id(0)
    n_tiles = pl.num_programs(0) - 1
    tm = x1_ref.shape[0]
    slot = step % 2
    prev = 1 - slot

    @pl.when(step < n_tiles)
    def _():
        def start(j, carry):
            pltpu.make_async_copy(ys_hbm.at[pl.ds(gdst_ref[j], 1)], buf.at[slot, pl.ds(j, 1)], sems.at[slot]).start()
            return carry
        lax.fori_loop(0, LOCAL_GROUPS, start, 0, unroll=8)

    @pl.when(step > 0)
    def _():
        pltpu.make_async_copy(ys_hbm.at[pl.ds(0, LOCAL_GROUPS)], buf.at[prev], sems.at[prev]).wait()

        rows = _unpack_rows(buf[prev].reshape(LOCAL_ROWS, PACKED)).astype(BF16)
        pos = lax.broadcasted_iota(jnp.int32, (tm, LOCAL_ROWS), 1)
        row1 = route_ref[:, 2:3].astype(jnp.int32)
        row2 = route_ref[:, 3:4].astype(jnp.int32)
        mix = (jnp.where(pos == row1, route_ref[:, 4:5], 0.0)
               + jnp.where(pos == row2, route_ref[:, 5:6], 0.0)).astype(BF16)
        x2 = x1_ref[...] + jnp.dot(mix, rows, preferred_element_type=F32)
        o_ref[...] = _rms(x2) * nw_ref[...]


def _combine(gdst, route, x1, norm_w, ys3):
    t = x1.shape[0]
    tm = min(ROWS_MERGE, t)
    n_tiles = t // tm
    gathered = lambda i: (jnp.minimum(i, n_tiles - 1),)
    finished = lambda i: (jnp.maximum(i - 1, 0), 0)
    return pl.pallas_call(
        _combine_kernel,
        grid=(n_tiles + 1,),
        in_specs=[
            pl.BlockSpec((GROUP_TABLE,), gathered, memory_space=pltpu.SMEM),
            pl.BlockSpec((tm, LANES), finished),
            pl.BlockSpec((tm, D_MODEL), finished),
            pl.BlockSpec((1, D_MODEL), lambda i: (0, 0)),
            pl.BlockSpec(memory_space=pl.ANY),
        ],
        out_specs=pl.BlockSpec((tm, D_MODEL), finished),
        out_shape=jax.ShapeDtypeStruct((t, D_MODEL), F32),
        scratch_shapes=[
            pltpu.VMEM((2, LOCAL_GROUPS, SUBLANES, PACKED), U32),
            pltpu.SemaphoreType.DMA((2,)),
        ],
        compiler_params=pltpu.CompilerParams(
            dimension_semantics=("arbitrary",), vmem_limit_bytes=VMEM_LIMIT),
        name="combine",
    )(gdst, route, x1, norm_w, ys3)


def _rotary_tables(seq):
    inv_freq = 1.0 / (ROPE_BASE ** (jnp.arange(0, QK_DIM, 2, dtype=F32) / QK_DIM))
    ang = jnp.arange(seq, dtype=F32)[:, None] * inv_freq[None, :]
    cos, sin = jnp.cos(ang), jnp.sin(ang)
    return jnp.concatenate([cos, cos], axis=1), jnp.concatenate([-sin, sin], axis=1)


def _retention_decay_tables():
    gamma = 1.0 - 2.0 ** (-5.0 - np.arange(N_HEADS, dtype=np.float64))
    idx = np.arange(CHUNK, dtype=np.float64) + 1.0
    dq = gamma[:, None] ** idx[None, :]
    dk = gamma[:, None] ** (-idx[None, :]) * QK_DIM ** -0.5
    bcast = lambda a: jnp.asarray(np.broadcast_to(a[:, :, None], (N_HEADS, CHUNK, QK_DIM)), F32)
    return bcast(dq), bcast(dk), tuple(float(g) for g in gamma ** CHUNK)


def kernel(x, norm_mix_w, w_in, ret_gn_w, w_ret_branch, mlstm_conv_w, mlstm_conv_b, b_igate, b_fgate,
           mlstm_gn_w, w_mlstm_branch, w_out, norm_ffn_w, w_group, b_group, w_expert_router,
           b_expert_router, w_gate, w_up, w_down, norm_final_w):
    assert norm_mix_w.shape[0] == 1, "one layer"
    b, s, d = x.shape
    t = b * s
    assert d == D_MODEL and s % CHUNK == 0

    wi = w_in[0]
    n_pre = 2 * MIX_COLS
    w_big = jnp.concatenate([wi[:, :n_pre], wi[:, n_pre + 2 * N_HEADS:]], axis=1).astype(BF16)
    w_if = jnp.pad(wi[:, n_pre:n_pre + 2 * N_HEADS], ((0, 0), (0, LANES - 2 * N_HEADS))).astype(BF16)
    gate_bias = jnp.pad(jnp.concatenate([b_igate[0], b_fgate[0]]), (0, LANES - 2 * N_HEADS))[None, :]
    w_router = jnp.pad(jnp.concatenate([w_expert_router[0], w_group[0]], axis=1),
                       ((0, 0), (0, LANES - N_EXPERTS - N_GROUPS))).astype(BF16)
    b_router = jnp.pad(jnp.concatenate([b_expert_router[0], b_group[0]]),
                       (0, LANES - N_EXPERTS - N_GROUPS))[None, :]

    cosf, sinf = _rotary_tables(s)
    dq, dk, chunk_decay = _retention_decay_tables()
    tm_merge = min(ROWS_MERGE, t)
    lower = jnp.tril(jnp.ones((tm_merge, tm_merge), F32), -1).astype(BF16)

    x2d = x.reshape(t, d)
    proj, gates = _in_projection(x2d, norm_mix_w, w_big, w_if)
    proj3 = proj.reshape(b, s, N_BIG)
    ret = _retention(proj3, cosf, sinf, dq, dk, ret_gn_w, chunk_decay)
    hm = _mlstm(proj3, gates.reshape(b, s, LANES), mlstm_conv_w[0, :, 0, :], mlstm_conv_b,
                gate_bias, mlstm_gn_w)
    x1, h2p, route, route_t, tiles = _merge_route(
        ret.reshape(t, V_ALL), hm.reshape(t, V_ALL), proj, x2d, w_ret_branch[0].astype(BF16),
        w_mlstm_branch[0].astype(BF16), w_out[0].astype(BF16), norm_ffn_w, w_router, b_router, lower)

    n_tiles = t // tm_merge
    n_slots = 2 * t + N_EXPERTS * (MOE_ROWS + n_tiles * (SUBLANES - 1) // SUBLANES * SUBLANES)
    n_slots = -(-n_slots // MOE_ROWS) * MOE_ROWS
    run = tiles[:, 1, :N_EXPERTS].astype(jnp.int32)
    run_start = tiles[:, 2, :N_EXPERTS].astype(jnp.int32)
    rows_e = jnp.sum(run, axis=0)
    padded = (rows_e + MOE_ROWS - 1) // MOE_ROWS * MOE_ROWS
    expert_ids = jnp.arange(N_EXPERTS, dtype=jnp.int32)
    pstart = jnp.sum(jnp.where(expert_ids[None, :] < expert_ids[:, None], padded[None, :], 0), axis=1)
    pend = pstart + padded
    tile_ids = jnp.arange(n_tiles, dtype=jnp.int32)
    run_t = run.T
    earlier = jnp.sum(jnp.where((tile_ids[None, :] < tile_ids[:, None])[None], run_t[:, None, :], 0), axis=2).T
    global_start = pstart[None, :] + earlier
    local_row = jnp.arange(GROUP_TABLE, dtype=jnp.int32)[None, None, :] * SUBLANES
    in_run = (local_row >= run_start[:, :, None]) & (local_row < (run_start + run)[:, :, None])
    gdst = jnp.sum(jnp.where(in_run, (global_start - run_start)[:, :, None] + local_row, 0), axis=1) // SUBLANES
    gdst = gdst.reshape(n_tiles * GROUP_TABLE).astype(jnp.int32)
    n_groups = (jnp.sum(run, axis=1) // SUBLANES).astype(jnp.int32)
    blk_start = jnp.arange(n_slots // MOE_ROWS, dtype=jnp.int32) * MOE_ROWS
    blk_e = jnp.minimum(jnp.sum(blk_start[:, None] >= pend[None, :], axis=-1), N_EXPERTS - 1).astype(jnp.int32)
    blk_valid = jnp.clip(pstart[blk_e] + rows_e[blk_e] - blk_start, 0, MOE_ROWS).astype(jnp.int32)
    tail = pend[-1] + jnp.arange((n_slots - 2 * t) // MOE_ROWS, dtype=jnp.int32) * MOE_ROWS
    fill_groups = (jnp.concatenate([jnp.where(padded > 0, pend - MOE_ROWS, -SUBLANES),
                                    jnp.where(tail < n_slots, tail, -SUBLANES)]) // SUBLANES).astype(jnp.int32)

    xs3 = _dispatch(fill_groups, n_groups, gdst, route_t, h2p, n_slots)
    ys = _experts(blk_e, blk_valid, xs3.reshape(n_slots, PACKED), w_gate[0], w_up[0], w_down[0])
    out = _combine(gdst, route, x1, norm_final_w[None, :],
                   ys.reshape(n_slots // SUBLANES, SUBLANES, PACKED))
    return out.reshape(b, s, d)
```

```python
import functools

import numpy as np
import jax
import jax.numpy as jnp
from jax import lax
from jax.experimental import pallas as pl
from jax.experimental.pallas import tpu as pltpu

F32 = jnp.float32
BF16 = jnp.bfloat16
U32 = jnp.uint32

D_MODEL = 1024
N_HEADS = 4
QK_DIM = 128
V_DIM = 256
CHUNK = 128
CONV_WIDTH = 4
ROPE_BASE = 10000.0
N_GROUPS = 4
EXPERTS_PER_GROUP = 8
N_EXPERTS = N_GROUPS * EXPERTS_PER_GROUP
D_EXPERT = 512
NORM_EPS = 1e-6
QK_ALL = N_HEADS * QK_DIM
V_ALL = N_HEADS * V_DIM

MIX_COLS = 2 * QK_ALL + 2 * V_ALL
N_BIG = 2 * MIX_COLS + 2 * D_MODEL
LANES = 128
PACKED = D_MODEL // 2

ROWS_PROJ = 2048
COLS_PROJ = 2048
RET_CHUNKS_PER_STEP = 4
MLSTM_CHUNKS_PER_STEP = 4
ROWS_MERGE = 512
ROUTE_ROWS = 8
SUBLANES = 8
LOCAL_ROWS = 2 * ROWS_MERGE + 256
assert LOCAL_ROWS >= 2 * ROWS_MERGE + N_EXPERTS * (SUBLANES - 1) and LOCAL_ROWS % LANES == 0
LOCAL_GROUPS = LOCAL_ROWS // SUBLANES
DMA_QUEUES = 2
GROUP_TABLE = 256
assert GROUP_TABLE >= LOCAL_GROUPS
MOE_ROWS = 512
V7X_VMEM_BYTES = 64 * 1024 * 1024
VMEM_LIMIT = V7X_VMEM_BYTES - 8 * 1024 * 1024
VMEM_LIMIT_PROJ = V7X_VMEM_BYTES - 4 * 1024 * 1024


def _rms(x, eps=NORM_EPS):
    return x * lax.rsqrt(jnp.mean(x * x, axis=-1, keepdims=True) + eps)


def _pack_rows(x):
    lo = lax.bitcast_convert_type(x[:, :PACKED].astype(BF16).astype(F32), U32)
    hi = lax.bitcast_convert_type(x[:, PACKED:].astype(BF16).astype(F32), U32)
    return (hi & jnp.uint32(0xFFFF0000)) | (lo >> 16)


def _unpack_rows(w):
    lo = lax.bitcast_convert_type(w << 16, F32)
    hi = lax.bitcast_convert_type(w & jnp.uint32(0xFFFF0000), F32)
    return jnp.concatenate([lo, hi], axis=1)


def _in_proj_kernel(x_ref, nw_ref, w_ref, wif_ref, o_ref, gates_ref, h_scr):
    @pl.when(pl.program_id(1) == 0)
    def _():
        h = (_rms(x_ref[...]) * nw_ref[...]).astype(BF16)
        h_scr[...] = h
        gates_ref[...] = jnp.dot(h, wif_ref[...], preferred_element_type=F32)

    o_ref[...] = jnp.dot(h_scr[...], w_ref[...], preferred_element_type=F32).astype(o_ref.dtype)


def _in_projection(x2d, norm_w, w_big, w_if):
    t = x2d.shape[0]
    tm = min(ROWS_PROJ, t)
    tn = COLS_PROJ
    return pl.pallas_call(
        _in_proj_kernel,
        grid=(t // tm, N_BIG // tn),
        in_specs=[
            pl.BlockSpec((tm, D_MODEL), lambda i, j: (i, 0)),
            pl.BlockSpec((1, D_MODEL), lambda i, j: (0, 0)),
            pl.BlockSpec((D_MODEL, tn), lambda i, j: (0, j)),
            pl.BlockSpec((D_MODEL, LANES), lambda i, j: (0, 0)),
        ],
        out_specs=[
            pl.BlockSpec((tm, tn), lambda i, j: (i, j)),
            pl.BlockSpec((tm, LANES), lambda i, j: (i, 0)),
        ],
        out_shape=[
            jax.ShapeDtypeStruct((t, N_BIG), BF16),
            jax.ShapeDtypeStruct((t, LANES), F32),
        ],
        scratch_shapes=[pltpu.VMEM((tm, D_MODEL), BF16)],
        compiler_params=pltpu.CompilerParams(
            dimension_semantics=("arbitrary", "arbitrary"), vmem_limit_bytes=VMEM_LIMIT_PROJ),
        name="in_projection",
    )(x2d, norm_w, w_big, w_if)


def _retention_kernel(p_ref, cos_ref, sin_ref, dq_ref, dk_ref, gn_ref, o_ref, state_scr, *, chunk_decay):
    L = CHUNK

    @pl.when(pl.program_id(1) == 0)
    def _():
        state_scr[...] = jnp.zeros_like(state_scr)

    row = lax.broadcasted_iota(jnp.int32, (L, L), 0)
    col = lax.broadcasted_iota(jnp.int32, (L, L), 1)
    causal = row >= col
    n_chunks = p_ref.shape[0] // L
    units = [(ci * L, h) for ci in range(n_chunks) for h in range(N_HEADS)]

    qts, kts, scores = [], [], []
    for r0, h in units:
        cosf = cos_ref[r0:r0 + L, :]
        sinf = sin_ref[r0:r0 + L, :]
        q = p_ref[r0:r0 + L, h * QK_DIM:(h + 1) * QK_DIM].astype(F32)
        k = p_ref[r0:r0 + L, QK_ALL + h * QK_DIM:QK_ALL + (h + 1) * QK_DIM].astype(F32)
        qt = ((q * cosf + pltpu.roll(q, QK_DIM // 2, 1) * sinf) * dq_ref[h]).astype(BF16)
        kt = ((k * cosf + pltpu.roll(k, QK_DIM // 2, 1) * sinf) * dk_ref[h]).astype(BF16)
        s = lax.dot_general(qt, kt, (((1,), (1,)), ((), ())), preferred_element_type=F32)
        qts.append(qt)
        kts.append(kt)
        scores.append(jnp.where(causal, s, 0.0).astype(BF16))

    states = [state_scr[h] for h in range(N_HEADS)]
    for (r0, h), qt, kt, s in zip(units, qts, kts, scores):
        v = p_ref[r0:r0 + L, 2 * QK_ALL + h * V_DIM:2 * QK_ALL + (h + 1) * V_DIM]
        g = p_ref[r0:r0 + L, 2 * QK_ALL + V_ALL + h * V_DIM:2 * QK_ALL + V_ALL + (h + 1) * V_DIM].astype(F32)
        lhs = jnp.concatenate([s, qt], axis=1)
        rhs = jnp.concatenate([v, states[h].astype(BF16)], axis=0)
        o = jnp.dot(lhs, rhs, preferred_element_type=F32)
        kv = lax.dot_general(kt, v, (((0,), (0,)), ((), ())), preferred_element_type=F32)
        states[h] = (states[h] + kv) * chunk_decay[h]
        y = _rms(o) * gn_ref[:, h * V_DIM:(h + 1) * V_DIM] * (g * jax.nn.sigmoid(g))
        o_ref[r0:r0 + L, h * V_DIM:(h + 1) * V_DIM] = y.astype(o_ref.dtype)
    for h in range(N_HEADS):
        state_scr[h] = states[h]


def _retention(proj3, cosf, sinf, dq, dk, gn_w, chunk_decay):
    b, s, _ = proj3.shape
    lb = CHUNK * min(RET_CHUNKS_PER_STEP, s // CHUNK)
    return pl.pallas_call(
        functools.partial(_retention_kernel, chunk_decay=chunk_decay),
        grid=(b, s // lb),
        in_specs=[
            pl.BlockSpec((None, lb, MIX_COLS), lambda i, c: (i, c, 0)),
            pl.BlockSpec((lb, QK_DIM), lambda i, c: (c, 0)),
            pl.BlockSpec((lb, QK_DIM), lambda i, c: (c, 0)),
            pl.BlockSpec((N_HEADS, CHUNK, QK_DIM), lambda i, c: (0, 0, 0)),
            pl.BlockSpec((N_HEADS, CHUNK, QK_DIM), lambda i, c: (0, 0, 0)),
            pl.BlockSpec((1, V_ALL), lambda i, c: (0, 0)),
        ],
        out_specs=pl.BlockSpec((None, lb, V_ALL), lambda i, c: (i, c, 0)),
        out_shape=jax.ShapeDtypeStruct((b, s, V_ALL), BF16),
        scratch_shapes=[pltpu.VMEM((N_HEADS, QK_DIM, V_DIM), F32)],
        compiler_params=pltpu.CompilerParams(
            dimension_semantics=("arbitrary", "arbitrary"), vmem_limit_bytes=VMEM_LIMIT),
        name="retention",
    )(proj3, cosf, sinf, dq, dk, gn_w)


def _mlstm_kernel(p_ref, gates_ref, cw_ref, cb_ref, shift_ref, gb_ref, gn_ref, o_ref,
                  c_scr, n_scr, m_scr, tail_scr, act_scr):
    L = CHUNK
    lb = p_ref.shape[0]

    @pl.when(pl.program_id(1) == 0)
    def _():
        c_scr[...] = jnp.zeros_like(c_scr)
        n_scr[...] = jnp.zeros_like(n_scr)
        m_scr[...] = jnp.zeros_like(m_scr)
        tail_scr[...] = jnp.zeros_like(tail_scr)

    ub = p_ref[:, 0:2 * QK_ALL]
    u = ub.astype(F32)
    tail = tail_scr[...]
    row8 = lax.broadcasted_iota(jnp.int32, tail.shape, 0)
    acc = u * cw_ref[CONV_WIDTH - 1:CONV_WIDTH, :] + cb_ref[...]
    head = jnp.zeros_like(tail)
    for d in range(1, CONV_WIDTH):
        w_d = cw_ref[CONV_WIDTH - 1 - d:CONV_WIDTH - d, :]
        acc = acc + jnp.dot(shift_ref[d - 1], ub, preferred_element_type=F32) * w_d
        head = head + jnp.where(row8 < d, pltpu.roll(tail, d, 0), 0.0) * w_d
    acc = jnp.concatenate([acc[0:SUBLANES, :] + head, acc[SUBLANES:, :]], axis=0)
    tail_scr[...] = u[lb - SUBLANES:lb, :]
    act_scr[...] = acc * jax.nn.sigmoid(acc)

    row = lax.broadcasted_iota(jnp.int32, (L, L), 0)
    col = lax.broadcasted_iota(jnp.int32, (L, L), 1)
    causal = row >= col
    k_scale = QK_DIM ** -0.5
    units = [(ci * L, h) for ci in range(lb // L) for h in range(N_HEADS)]

    def load_qk(r0, h):
        q = act_scr[r0:r0 + L, h * QK_DIM:(h + 1) * QK_DIM]
        k = act_scr[r0:r0 + L, QK_ALL + h * QK_DIM:QK_ALL + (h + 1) * QK_DIM] * k_scale
        return q, k

    gate_terms, src_rows = [], []
    lane_t = lax.broadcasted_iota(jnp.int32, (2 * N_HEADS, L), 1)
    for ci in range(lb // L):
        pre = gates_ref[ci * L:(ci + 1) * L, :] + gb_ref[...]
        pre_rows = jnp.transpose(pre)[0:2 * N_HEADS, :]
        b_rows = jnp.minimum(pre_rows, 0.0) - jnp.log1p(jnp.exp(-jnp.abs(pre_rows)))
        shift = 1
        while shift < L:
            b_rows = b_rows + jnp.where(lane_t >= shift, pltpu.roll(b_rows, shift, 1), 0.0)
            shift *= 2
        src_rows.append(pre_rows[0:N_HEADS, :] - b_rows[N_HEADS:2 * N_HEADS, :])
        bcum = jnp.transpose(jnp.concatenate([b_rows, jnp.zeros((L - 2 * N_HEADS, L), F32)], axis=0))
        gate_terms.append((pre, bcum))
    scores = []
    for r0, h in units:
        q, k = load_qk(r0, h)
        scores.append(lax.dot_general(q.astype(BF16), k.astype(BF16), (((1,), (1,)), ((), ())),
                                      preferred_element_type=F32))

    m_state = [m_scr[h:h + 1, :] for h in range(N_HEADS)]
    terms = []
    for (r0, h), qk in zip(units, scores):
        pre, bcum = gate_terms[r0 // L]
        q, k = load_qk(r0, h)
        b_t = jnp.broadcast_to(bcum[:, N_HEADS + h:N_HEADS + h + 1], (L, L))
        i_t = jnp.broadcast_to(pre[:, h:h + 1], (L, L))
        src = jnp.broadcast_to(src_rows[r0 // L][h:h + 1, :], (L, L))
        m_prev = m_state[h]
        a = b_t + m_prev
        dmat = jnp.where(causal, b_t + src, -jnp.inf)
        m_t = jnp.maximum(a, jnp.max(dmat, axis=-1, keepdims=True))
        w_inter = jnp.exp(a - m_t)
        s = qk * jnp.exp(dmat - m_t)
        lhs = jnp.concatenate([s.astype(BF16), (q * w_inter).astype(BF16)], axis=1)
        b_last = b_t[L - 1:L, :]
        gk = b_last - b_t + i_t
        m_new = jnp.maximum(b_last + m_prev, jnp.max(gk, axis=0, keepdims=True))
        wk = k * jnp.exp(gk - m_new)
        m_state[h] = m_new
        terms.append(dict(
            lhs=lhs, s_sum=jnp.sum(s, axis=-1, keepdims=True), w_inter=w_inter, floor=jnp.exp(-m_t),
            keep=jnp.exp(b_last + m_prev - m_new), wk=wk.astype(BF16), wk_sum=jnp.sum(wk, axis=0, keepdims=True)))

    n_state = [n_scr[h:h + 1, :] for h in range(N_HEADS)]
    c_state = [c_scr[h] for h in range(N_HEADS)]
    v_cols = lambda h: slice(2 * QK_ALL + h * V_DIM, 2 * QK_ALL + (h + 1) * V_DIM)
    for ci in range(lb // L):
        r0 = ci * L
        chunk_terms = terms[ci * N_HEADS:(ci + 1) * N_HEADS]
        q_dot_n = [jnp.sum(load_qk(r0, h)[0] * n_state[h], axis=-1, keepdims=True) for h in range(N_HEADS)]
        nums = []
        for h, t in enumerate(chunk_terms):
            rhs = jnp.concatenate([p_ref[r0:r0 + L, v_cols(h)], c_state[h].astype(BF16)], axis=0)
            nums.append(jnp.dot(t["lhs"], rhs, preferred_element_type=F32))
        for h, t in enumerate(chunk_terms):
            c_state[h] = c_state[h] * t["keep"][:, 0:1] + lax.dot_general(
                t["wk"], p_ref[r0:r0 + L, v_cols(h)], (((0,), (0,)), ((), ())), preferred_element_type=F32)
            n_state[h] = n_state[h] * t["keep"] + t["wk_sum"]
        for h, t in enumerate(chunk_terms):
            og = p_ref[r0:r0 + L, V_ALL + v_cols(h).start:V_ALL + v_cols(h).stop].astype(F32)
            den = t["s_sum"] + q_dot_n[h] * t["w_inter"]
            inv = 1.0 / jnp.maximum(jnp.abs(den), t["floor"])
            hh = nums[h] * jnp.concatenate([inv] * (V_DIM // LANES), axis=1)
            y = _rms(hh * jax.nn.sigmoid(og)) * gn_ref[:, h * V_DIM:(h + 1) * V_DIM]
            o_ref[r0:r0 + L, h * V_DIM:(h + 1) * V_DIM] = y.astype(o_ref.dtype)

    for h in range(N_HEADS):
        c_scr[h] = c_state[h]
        n_scr[h:h + 1, :] = n_state[h]
        m_scr[h:h + 1, :] = m_state[h]


def _mlstm(proj3, gates3, conv_w, conv_b, gate_bias, gn_w):
    b, s, _ = proj3.shape
    lb = CHUNK * min(MLSTM_CHUNKS_PER_STEP, s // CHUNK)
    shifts = jnp.stack([jnp.eye(lb, k=-d, dtype=BF16) for d in range(1, CONV_WIDTH)])
    return pl.pallas_call(
        _mlstm_kernel,
        grid=(b, s // lb),
        in_specs=[
            pl.BlockSpec((None, lb, MIX_COLS), lambda i, c: (i, c, 1)),
            pl.BlockSpec((None, lb, LANES), lambda i, c: (i, c, 0)),
            pl.BlockSpec((CONV_WIDTH, 2 * QK_ALL), lambda i, c: (0, 0)),
            pl.BlockSpec((1, 2 * QK_ALL), lambda i, c: (0, 0)),
            pl.BlockSpec((CONV_WIDTH - 1, lb, lb), lambda i, c: (0, 0, 0)),
            pl.BlockSpec((1, LANES), lambda i, c: (0, 0)),
            pl.BlockSpec((1, V_ALL), lambda i, c: (0, 0)),
        ],
        out_specs=pl.BlockSpec((None, lb, V_ALL), lambda i, c: (i, c, 0)),
        out_shape=jax.ShapeDtypeStruct((b, s, V_ALL), BF16),
        scratch_shapes=[
            pltpu.VMEM((N_HEADS, QK_DIM, V_DIM), F32),
            pltpu.VMEM((8, QK_DIM), F32),
            pltpu.VMEM((8, LANES), F32),
            pltpu.VMEM((8, 2 * QK_ALL), F32),
            pltpu.VMEM((lb, 2 * QK_ALL), F32),
        ],
        compiler_params=pltpu.CompilerParams(
            dimension_semantics=("arbitrary", "arbitrary"), vmem_limit_bytes=VMEM_LIMIT),
        name="mlstm",
    )(proj3, gates3, conv_w, conv_b, shifts, gate_bias, gn_w)


def _merge_route_kernel(ret_ref, hm_ref, gr_ref, gm_ref, x_ref, wr_ref, wm_ref, wo_ref, nw_ref,
                        wrt_ref, brt_ref, lower_ref, x1_ref, h2_ref, route_ref, route_t_ref, tile_ref,
                        logits_scr):
    step = pl.program_id(0)

    @pl.when(step == 0)
    def _():
        logits_scr[...] = jnp.zeros_like(logits_scr)

    logits = logits_scr[...]

    y_ret = jnp.dot(ret_ref[...], wr_ref[...], preferred_element_type=F32)
    y_m = jnp.dot(hm_ref[...], wm_ref[...], preferred_element_type=F32)
    merged = (jax.nn.sigmoid(gr_ref[...].astype(F32)) * y_ret
              + jax.nn.sigmoid(gm_ref[...].astype(F32)) * y_m)
    x1 = x_ref[...] + jnp.dot(merged.astype(BF16), wo_ref[...], preferred_element_type=F32)
    x1_ref[...] = x1
    h2 = _rms(x1) * nw_ref[...]
    h2_ref[...] = _pack_rows(h2)
    new_logits = jnp.dot(h2.astype(BF16), wrt_ref[...], preferred_element_type=F32) + brt_ref[...]

    live = jnp.where(step > 0, 1.0, 0.0)
    tm = logits.shape[0]
    lane = lax.broadcasted_iota(jnp.int32, (tm, LANES), 1)
    neg = -jnp.inf
    big = jnp.int32(LANES)
    is_group = (lane >= N_EXPERTS) & (lane < N_EXPERTS + N_GROUPS)
    gl = jnp.where(is_group, logits, neg)
    g_max = jnp.max(gl, axis=-1, keepdims=True)
    g_idx = jnp.min(jnp.where(gl == g_max, lane, big), axis=-1, keepdims=True) - N_EXPERTS
    g_w = 1.0 / jnp.sum(jnp.exp(gl - g_max), axis=-1, keepdims=True)
    in_group = (lane >= g_idx * EXPERTS_PER_GROUP) & (lane < (g_idx + 1) * EXPERTS_PER_GROUP)
    el = jnp.where(in_group, logits, neg)
    l1 = jnp.max(el, axis=-1, keepdims=True)
    e1 = jnp.min(jnp.where(el == l1, lane, big), axis=-1, keepdims=True)
    el2 = jnp.where(lane == e1, neg, el)
    l2 = jnp.max(el2, axis=-1, keepdims=True)
    e2 = jnp.min(jnp.where(el2 == l2, lane, big), axis=-1, keepdims=True)
    t21 = jnp.exp(l2 - l1)
    w1 = g_w / (1.0 + t21)
    w2 = g_w * t21 / (1.0 + t21)

    hit1 = lane == e1
    hit2 = lane == e2
    cnt = jnp.where(hit1 | hit2, live, 0.0)
    before = jnp.dot(lower_ref[...], cnt.astype(BF16), preferred_element_type=F32)
    count = jnp.sum(cnt, axis=0, keepdims=True)
    run = jnp.floor((count + (SUBLANES - 1)) * (1.0 / SUBLANES)) * SUBLANES
    lane1 = lax.broadcasted_iota(jnp.int32, (1, LANES), 1)
    run_end = run
    shift = 1
    while shift < N_EXPERTS:
        run_end = run_end + jnp.where(lane1 >= shift, pltpu.roll(run_end, shift, 1), 0.0)
        shift *= 2
    run_start = run_end - run
    local = before + run_start
    r1 = jnp.sum(jnp.where(hit1, local, 0.0), axis=-1, keepdims=True)
    r2 = jnp.sum(jnp.where(hit2, local, 0.0), axis=-1, keepdims=True)
    sub8 = lax.broadcasted_iota(jnp.int32, (SUBLANES, LANES), 0)
    tile_ref[...] = jnp.where(sub8 == 0, count, jnp.where(sub8 == 1, run, jnp.where(sub8 == 2, run_start, 0.0)))

    fields = (e1.astype(F32), e2.astype(F32), r1, r2, w1, w2)
    packed = jnp.zeros((tm, LANES), F32)
    for idx, val in enumerate(fields):
        packed = jnp.where(lane == idx, val, packed)
    route_ref[...] = packed
    route_t_ref[...] = jnp.transpose(packed)[0:ROUTE_ROWS, :]
    logits_scr[...] = new_logits


def _merge_route(ret, hm, proj, x2d, w_ret, w_m, w_out, norm_w, w_router, b_router, lower):
    t = x2d.shape[0]
    tm = min(ROWS_MERGE, t)
    n_tiles = t // tm
    gate_r_blk = 2 * MIX_COLS // D_MODEL
    tile = lambda i: jnp.minimum(i, n_tiles - 1)
    routed = lambda i: jnp.maximum(i - 1, 0)
    row_blk = lambda i: (tile(i), 0)
    const = lambda i: (0, 0)
    return pl.pallas_call(
        _merge_route_kernel,
        grid=(n_tiles + 1,),
        in_specs=[
            pl.BlockSpec((tm, V_ALL), row_blk),
            pl.BlockSpec((tm, V_ALL), row_blk),
            pl.BlockSpec((tm, D_MODEL), lambda i: (tile(i), gate_r_blk)),
            pl.BlockSpec((tm, D_MODEL), lambda i: (tile(i), gate_r_blk + 1)),
            pl.BlockSpec((tm, D_MODEL), row_blk),
            pl.BlockSpec((V_ALL, D_MODEL), const),
            pl.BlockSpec((V_ALL, D_MODEL), const),
            pl.BlockSpec((D_MODEL, D_MODEL), const),
            pl.BlockSpec((1, D_MODEL), const),
            pl.BlockSpec((D_MODEL, LANES), const),
            pl.BlockSpec((1, LANES), const),
            pl.BlockSpec((tm, tm), const),
        ],
        out_specs=[
            pl.BlockSpec((tm, D_MODEL), row_blk),
            pl.BlockSpec((tm, PACKED), row_blk),
            pl.BlockSpec((tm, LANES), lambda i: (routed(i), 0)),
            pl.BlockSpec((ROUTE_ROWS, tm), lambda i: (0, routed(i))),
            pl.BlockSpec((None, SUBLANES, LANES), lambda i: (routed(i), 0, 0)),
        ],
        out_shape=[
            jax.ShapeDtypeStruct((t, D_MODEL), F32),
            jax.ShapeDtypeStruct((t, PACKED), U32),
            jax.ShapeDtypeStruct((t, LANES), F32),
            jax.ShapeDtypeStruct((ROUTE_ROWS, t), F32),
            jax.ShapeDtypeStruct((n_tiles, SUBLANES, LANES), F32),
        ],
        scratch_shapes=[pltpu.VMEM((tm, LANES), F32)],
        compiler_params=pltpu.CompilerParams(
            dimension_semantics=("arbitrary",), vmem_limit_bytes=VMEM_LIMIT),
        name="merge_route",
    )(ret, hm, proj, proj, x2d, w_ret, w_m, w_out, norm_w, w_router, b_router, lower)


def _dispatch_kernel(fill_ref, n_groups_ref, gdst_ref, route_t_ref, h2_ref, xs_hbm, sorted_scr, zero_scr, sems):
    step = pl.program_id(0)
    n_tiles = pl.num_programs(0) - 1
    tm = h2_ref.shape[0]
    slot = step % 2

    def group_copy(which, j):
        return pltpu.make_async_copy(sorted_scr.at[which, pl.ds(j, 1)], xs_hbm.at[pl.ds(gdst_ref[j], 1)],
                                     sems.at[which])

    def wait_groups(which, count):
        def wait(j, carry):
            group_copy(which, 0).wait()
            return carry
        lax.fori_loop(0, count, wait, 0)

    @pl.when(step == 0)
    def _():
        zero_scr[...] = jnp.zeros_like(zero_scr)

        def fill_copy(j):
            g0 = pl.multiple_of(fill_ref[j], MOE_ROWS // SUBLANES)
            return pltpu.make_async_copy(zero_scr, xs_hbm.at[pl.ds(g0, MOE_ROWS // SUBLANES)], sems.at[0])

        def start_fill(j, carry):
            @pl.when(fill_ref[j] >= 0)
            def _():
                fill_copy(j).start()
            return carry

        def wait_fill(j, carry):
            @pl.when(fill_ref[j] >= 0)
            def _():
                fill_copy(j).wait()
            return carry

        lax.fori_loop(0, fill_ref.shape[0], start_fill, 0)
        lax.fori_loop(0, fill_ref.shape[0], wait_fill, 0)

    @pl.when(step >= 1)
    def _():
        count = n_groups_ref[jnp.maximum(step - 1, 0)]
        for queue in range(DMA_QUEUES):
            def start(jq, carry):
                group_copy(1 - slot, DMA_QUEUES * jq + queue).start(priority=queue)
                return carry
            lax.fori_loop(0, (count + DMA_QUEUES - 1 - queue) // DMA_QUEUES, start, 0)

    @pl.when(step >= 2)
    def _():
        wait_groups(slot, n_groups_ref[jnp.maximum(step - 2, 0)])

    @pl.when(step < n_tiles)
    def _():
        pos = lax.broadcasted_iota(jnp.int32, (LOCAL_ROWS, tm), 0)
        row1 = route_t_ref[2:3, :].astype(jnp.int32)
        row2 = route_t_ref[3:4, :].astype(jnp.int32)
        pick = jnp.where((pos == row1) | (pos == row2), 1.0, 0.0).astype(BF16)
        tokens = _unpack_rows(h2_ref[...]).astype(BF16)
        ordered = jnp.dot(pick, tokens, preferred_element_type=F32)
        sorted_scr[slot] = _pack_rows(ordered).reshape(LOCAL_GROUPS, SUBLANES, PACKED)

    @pl.when(step == n_tiles)
    def _():
        wait_groups(1 - slot, n_groups_ref[jnp.maximum(step - 1, 0)])


def _dispatch(fill_groups, n_groups, gdst, route_t, h2p, n_slots):
    t = h2p.shape[0]
    tm = min(ROWS_MERGE, t)
    n_tiles = t // tm
    sorted_tile = lambda i: jnp.minimum(i, n_tiles - 1)
    return pl.pallas_call(
        _dispatch_kernel,
        grid=(n_tiles + 1,),
        in_specs=[
            pl.BlockSpec(memory_space=pltpu.SMEM),
            pl.BlockSpec(memory_space=pltpu.SMEM),
            pl.BlockSpec((GROUP_TABLE,), lambda i: (jnp.maximum(i - 1, 0),), memory_space=pltpu.SMEM),
            pl.BlockSpec((ROUTE_ROWS, tm), lambda i: (0, sorted_tile(i))),
            pl.BlockSpec((tm, PACKED), lambda i: (sorted_tile(i), 0)),
        ],
        out_specs=pl.BlockSpec(memory_space=pl.ANY),
        out_shape=jax.ShapeDtypeStruct((n_slots // SUBLANES, SUBLANES, PACKED), U32),
        scratch_shapes=[
            pltpu.VMEM((2, LOCAL_GROUPS, SUBLANES, PACKED), U32),
            pltpu.VMEM((MOE_ROWS // SUBLANES, SUBLANES, PACKED), U32),
            pltpu.SemaphoreType.DMA((2,)),
        ],
        compiler_params=pltpu.CompilerParams(
            dimension_semantics=("arbitrary",), vmem_limit_bytes=VMEM_LIMIT),
        name="dispatch",
    )(fill_groups, n_groups, gdst, route_t, h2p)


def _experts_kernel(blk_e_ref, blk_valid_ref, xs_ref, wg_ref, wu_ref, wd_ref, ys_ref, wgu_scr, wd_scr):
    i = pl.program_id(0)
    valid = blk_valid_ref[i]

    @pl.when((i == 0) | (blk_e_ref[i] != blk_e_ref[jnp.maximum(i - 1, 0)]))
    def _():
        wgu_scr[:, :D_EXPERT] = wg_ref[...].astype(BF16)
        wgu_scr[:, D_EXPERT:] = wu_ref[...].astype(BF16)
        wd_scr[...] = wd_ref[...].astype(BF16)

    @pl.when(valid > 0)
    def _():
        xb = _unpack_rows(xs_ref[...]).astype(BF16)
        gu = jnp.dot(xb, wgu_scr[...], preferred_element_type=F32)
        g = gu[:, :D_EXPERT]
        act = (g * jax.nn.sigmoid(g) * gu[:, D_EXPERT:]).astype(BF16)
        ys_ref[...] = _pack_rows(jnp.dot(act, wd_scr[...], preferred_element_type=F32))

    @pl.when(valid <= 0)
    def _():
        ys_ref[...] = jnp.zeros_like(ys_ref)


def _experts(blk_e, blk_valid, xs, w_gate, w_up, w_down):
    p = xs.shape[0]
    per_expert = lambda i, be, bv: (be[i], 0, 0)
    grid_spec = pltpu.PrefetchScalarGridSpec(
        num_scalar_prefetch=2,
        grid=(p // MOE_ROWS,),
        in_specs=[
            pl.BlockSpec((MOE_ROWS, PACKED), lambda i, be, bv: (i, 0)),
            pl.BlockSpec((None, D_MODEL, D_EXPERT), per_expert),
            pl.BlockSpec((None, D_MODEL, D_EXPERT), per_expert),
            pl.BlockSpec((None, D_EXPERT, D_MODEL), per_expert),
        ],
        out_specs=pl.BlockSpec((MOE_ROWS, PACKED), lambda i, be, bv: (i, 0)),
        scratch_shapes=[pltpu.VMEM((D_MODEL, 2 * D_EXPERT), BF16), pltpu.VMEM((D_EXPERT, D_MODEL), BF16)],
    )
    return pl.pallas_call(
        _experts_kernel,
        grid_spec=grid_spec,
        out_shape=jax.ShapeDtypeStruct((p, PACKED), U32),
        compiler_params=pltpu.CompilerParams(
            dimension_semantics=("arbitrary",), vmem_limit_bytes=VMEM_LIMIT),
        name="experts",
    )(blk_e, blk_valid, xs, w_gate, w_up, w_down)


def _combine_kernel(gdst_ref, route_ref, x1_ref, nw_ref, ys_hbm, o_ref, buf, sems):
    step = pl.program_id(0)
    n_tiles = pl.num_programs(0) - 1
    tm = x1_ref.shape[0]
    slot = step % 2
    prev = 1 - slot

    @pl.when(step < n_tiles)
    def _():
        def start(jq, carry):
            for queue in range(DMA_QUEUES):
                j = DMA_QUEUES * jq + queue
                pltpu.make_async_copy(ys_hbm.at[pl.ds(gdst_ref[j], 1)], buf.at[slot, pl.ds(j, 1)],
                                      sems.at[slot]).start(priority=queue)
            return carry
        lax.fori_loop(0, LOCAL_GROUPS // DMA_QUEUES, start, 0, unroll=4)

    @pl.when(step > 0)
    def _():
        pltpu.make_async_copy(ys_hbm.at[pl.ds(0, LOCAL_GROUPS)], buf.at[prev], sems.at[prev]).wait()

        rows = _unpack_rows(buf[prev].reshape(LOCAL_ROWS, PACKED)).astype(BF16)
        pos = lax.broadcasted_iota(jnp.int32, (tm, LOCAL_ROWS), 1)
        row1 = route_ref[:, 2:3].astype(jnp.int32)
        row2 = route_ref[:, 3:4].astype(jnp.int32)
        mix = (jnp.where(pos == row1, route_ref[:, 4:5], 0.0)
               + jnp.where(pos == row2, route_ref[:, 5:6], 0.0)).astype(BF16)
        x2 = x1_ref[...] + jnp.dot(mix, rows, preferred_element_type=F32)
        o_ref[...] = _rms(x2) * nw_ref[...]


def _combine(gdst, route, x1, norm_w, ys3):
    t = x1.shape[0]
    tm = min(ROWS_MERGE, t)
    n_tiles = t // tm
    gathered = lambda i: (jnp.minimum(i, n_tiles - 1),)
    finished = lambda i: (jnp.maximum(i - 1, 0), 0)
    return pl.pallas_call(
        _combine_kernel,
        grid=(n_tiles + 1,),
        in_specs=[
            pl.BlockSpec((GROUP_TABLE,), gathered, memory_space=pltpu.SMEM),
            pl.BlockSpec((tm, LANES), finished),
            pl.BlockSpec((tm, D_MODEL), finished),
            pl.BlockSpec((1, D_MODEL), lambda i: (0, 0)),
            pl.BlockSpec(memory_space=pl.ANY),
        ],
        out_specs=pl.BlockSpec((tm, D_MODEL), finished),
        out_shape=jax.ShapeDtypeStruct((t, D_MODEL), F32),
        scratch_shapes=[
            pltpu.VMEM((2, LOCAL_GROUPS, SUBLANES, PACKED), U32),
            pltpu.SemaphoreType.DMA((2,)),
        ],
        compiler_params=pltpu.CompilerParams(
            dimension_semantics=("arbitrary",), vmem_limit_bytes=VMEM_LIMIT),
        name="combine",
    )(gdst, route, x1, norm_w, ys3)


def _rotary_tables(seq):
    inv_freq = 1.0 / (ROPE_BASE ** (jnp.arange(0, QK_DIM, 2, dtype=F32) / QK_DIM))
    ang = jnp.arange(seq, dtype=F32)[:, None] * inv_freq[None, :]
    cos, sin = jnp.cos(ang), jnp.sin(ang)
    return jnp.concatenate([cos, cos], axis=1), jnp.concatenate([-sin, sin], axis=1)


def _retention_decay_tables():
    gamma = 1.0 - 2.0 ** (-5.0 - np.arange(N_HEADS, dtype=np.float64))
    idx = np.arange(CHUNK, dtype=np.float64) + 1.0
    dq = gamma[:, None] ** idx[None, :]
    dk = gamma[:, None] ** (-idx[None, :]) * QK_DIM ** -0.5
    bcast = lambda a: jnp.asarray(np.broadcast_to(a[:, :, None], (N_HEADS, CHUNK, QK_DIM)), F32)
    return bcast(dq), bcast(dk), tuple(float(g) for g in gamma ** CHUNK)


def kernel(x, norm_mix_w, w_in, ret_gn_w, w_ret_branch, mlstm_conv_w, mlstm_conv_b, b_igate, b_fgate,
           mlstm_gn_w, w_mlstm_branch, w_out, norm_ffn_w, w_group, b_group, w_expert_router,
           b_expert_router, w_gate, w_up, w_down, norm_final_w):
    assert norm_mix_w.shape[0] == 1, "one layer"
    b, s, d = x.shape
    t = b * s
    assert d == D_MODEL and s % CHUNK == 0

    wi = w_in[0]
    n_pre = 2 * MIX_COLS
    w_big = jnp.concatenate([wi[:, :n_pre], wi[:, n_pre + 2 * N_HEADS:]], axis=1).astype(BF16)
    w_if = jnp.pad(wi[:, n_pre:n_pre + 2 * N_HEADS], ((0, 0), (0, LANES - 2 * N_HEADS))).astype(BF16)
    gate_bias = jnp.pad(jnp.concatenate([b_igate[0], b_fgate[0]]), (0, LANES - 2 * N_HEADS))[None, :]
    w_router = jnp.pad(jnp.concatenate([w_expert_router[0], w_group[0]], axis=1),
                       ((0, 0), (0, LANES - N_EXPERTS - N_GROUPS))).astype(BF16)
    b_router = jnp.pad(jnp.concatenate([b_expert_router[0], b_group[0]]),
                       (0, LANES - N_EXPERTS - N_GROUPS))[None, :]

    cosf, sinf = _rotary_tables(s)
    dq, dk, chunk_decay = _retention_decay_tables()
    tm_merge = min(ROWS_MERGE, t)
    lower = jnp.tril(jnp.ones((tm_merge, tm_merge), F32), -1).astype(BF16)

    x2d = x.reshape(t, d)
    proj, gates = _in_projection(x2d, norm_mix_w, w_big, w_if)
    proj3 = proj.reshape(b, s, N_BIG)
    ret = _retention(proj3, cosf, sinf, dq, dk, ret_gn_w, chunk_decay)
    hm = _mlstm(proj3, gates.reshape(b, s, LANES), mlstm_conv_w[0, :, 0, :], mlstm_conv_b,
                gate_bias, mlstm_gn_w)
    x1, h2p, route, route_t, tiles = _merge_route(
        ret.reshape(t, V_ALL), hm.reshape(t, V_ALL), proj, x2d, w_ret_branch[0].astype(BF16),
        w_mlstm_branch[0].astype(BF16), w_out[0].astype(BF16), norm_ffn_w, w_router, b_router, lower)

    n_tiles = t // tm_merge
    n_slots = 2 * t + N_EXPERTS * (MOE_ROWS + n_tiles * (SUBLANES - 1) // SUBLANES * SUBLANES)
    n_slots = -(-n_slots // MOE_ROWS) * MOE_ROWS
    run = tiles[:, 1, :N_EXPERTS].astype(jnp.int32)
    run_start = tiles[:, 2, :N_EXPERTS].astype(jnp.int32)
    rows_e = jnp.sum(run, axis=0)
    padded = (rows_e + MOE_ROWS - 1) // MOE_ROWS * MOE_ROWS
    expert_ids = jnp.arange(N_EXPERTS, dtype=jnp.int32)
    pstart = jnp.sum(jnp.where(expert_ids[None, :] < expert_ids[:, None], padded[None, :], 0), axis=1)
    pend = pstart + padded
    tile_ids = jnp.arange(n_tiles, dtype=jnp.int32)
    run_t = run.T
    earlier = jnp.sum(jnp.where((tile_ids[None, :] < tile_ids[:, None])[None], run_t[:, None, :], 0), axis=2).T
    global_start = pstart[None, :] + earlier
    local_row = jnp.arange(GROUP_TABLE, dtype=jnp.int32)[None, None, :] * SUBLANES
    in_run = (local_row >= run_start[:, :, None]) & (local_row < (run_start + run)[:, :, None])
    gdst = jnp.sum(jnp.where(in_run, (global_start - run_start)[:, :, None] + local_row, 0), axis=1) // SUBLANES
    gdst = gdst.reshape(n_tiles * GROUP_TABLE).astype(jnp.int32)
    n_groups = (jnp.sum(run, axis=1) // SUBLANES).astype(jnp.int32)
    blk_start = jnp.arange(n_slots // MOE_ROWS, dtype=jnp.int32) * MOE_ROWS
    blk_e = jnp.minimum(jnp.sum(blk_start[:, None] >= pend[None, :], axis=-1), N_EXPERTS - 1).astype(jnp.int32)
    blk_valid = jnp.clip(pstart[blk_e] + rows_e[blk_e] - blk_start, 0, MOE_ROWS).astype(jnp.int32)
    tail = pend[-1] + jnp.arange((n_slots - 2 * t) // MOE_ROWS, dtype=jnp.int32) * MOE_ROWS
    fill_groups = (jnp.concatenate([jnp.where(padded > 0, pend - MOE_ROWS, -SUBLANES),
                                    jnp.where(tail < n_slots, tail, -SUBLANES)]) // SUBLANES).astype(jnp.int32)

    xs3 = _dispatch(fill_groups, n_groups, gdst, route_t, h2p, n_slots)
    ys = _experts(blk_e, blk_valid, xs3.reshape(n_slots, PACKED), w_gate[0], w_up[0], w_down[0])
    out = _combine(gdst, route, x1, norm_final_w[None, :],
                   ys.reshape(n_slots // SUBLANES, SUBLANES, PACKED))
    return out.reshape(b, s, d)
```

```python
import functools

import numpy as np
import jax
import jax.numpy as jnp
from jax import lax
from jax.experimental import pallas as pl
from jax.experimental.pallas import tpu as pltpu

F32 = jnp.float32
BF16 = jnp.bfloat16
U32 = jnp.uint32

D_MODEL = 1024
N_HEADS = 4
QK_DIM = 128
V_DIM = 256
CHUNK = 128
CONV_WIDTH = 4
ROPE_BASE = 10000.0
N_GROUPS = 4
EXPERTS_PER_GROUP = 8
N_EXPERTS = N_GROUPS * EXPERTS_PER_GROUP
D_EXPERT = 512
NORM_EPS = 1e-6
QK_ALL = N_HEADS * QK_DIM
V_ALL = N_HEADS * V_DIM

MIX_COLS = 2 * QK_ALL + 2 * V_ALL
N_BIG = 2 * MIX_COLS + 2 * D_MODEL
LANES = 128
PACKED = D_MODEL // 2

ROWS_PROJ = 2048
COLS_PROJ = 2048
RET_CHUNKS_PER_STEP = 8
MLSTM_CHUNKS_PER_STEP = 4
ROWS_MERGE = 512
ROUTE_ROWS = 8
SUBLANES = 8
LOCAL_ROWS = 2 * ROWS_MERGE + 256
assert LOCAL_ROWS >= 2 * ROWS_MERGE + N_EXPERTS * (SUBLANES - 1) and LOCAL_ROWS % LANES == 0
LOCAL_GROUPS = LOCAL_ROWS // SUBLANES
GROUP_TABLE = 256
assert GROUP_TABLE >= LOCAL_GROUPS
MOE_ROWS = 512
V7X_VMEM_BYTES = 64 * 1024 * 1024
VMEM_LIMIT = V7X_VMEM_BYTES - 8 * 1024 * 1024
VMEM_LIMIT_PROJ = V7X_VMEM_BYTES - 4 * 1024 * 1024


def _rms(x, eps=NORM_EPS):
    return x * lax.rsqrt(jnp.mean(x * x, axis=-1, keepdims=True) + eps)


def _pack_rows(x):
    lo = lax.bitcast_convert_type(x[:, :PACKED].astype(BF16).astype(F32), U32)
    hi = lax.bitcast_convert_type(x[:, PACKED:].astype(BF16).astype(F32), U32)
    return (hi & jnp.uint32(0xFFFF0000)) | (lo >> 16)


def _unpack_rows(w):
    lo = lax.bitcast_convert_type(w << 16, F32)
    hi = lax.bitcast_convert_type(w & jnp.uint32(0xFFFF0000), F32)
    return jnp.concatenate([lo, hi], axis=1)


def _in_proj_kernel(x_ref, nw_ref, w_ref, wif_ref, o_ref, gates_ref, h_scr):
    @pl.when(pl.program_id(1) == 0)
    def _():
        h = (_rms(x_ref[...]) * nw_ref[...]).astype(BF16)
        h_scr[...] = h
        gates_ref[...] = jnp.dot(h, wif_ref[...], preferred_element_type=F32)

    o_ref[...] = jnp.dot(h_scr[...], w_ref[...], preferred_element_type=F32).astype(o_ref.dtype)


def _in_projection(x2d, norm_w, w_big, w_if):
    t = x2d.shape[0]
    tm = min(ROWS_PROJ, t)
    tn = COLS_PROJ
    return pl.pallas_call(
        _in_proj_kernel,
        grid=(t // tm, N_BIG // tn),
        in_specs=[
            pl.BlockSpec((tm, D_MODEL), lambda i, j: (i, 0)),
            pl.BlockSpec((1, D_MODEL), lambda i, j: (0, 0)),
            pl.BlockSpec((D_MODEL, tn), lambda i, j: (0, j)),
            pl.BlockSpec((D_MODEL, LANES), lambda i, j: (0, 0)),
        ],
        out_specs=[
            pl.BlockSpec((tm, tn), lambda i, j: (i, j)),
            pl.BlockSpec((tm, LANES), lambda i, j: (i, 0)),
        ],
        out_shape=[
            jax.ShapeDtypeStruct((t, N_BIG), BF16),
            jax.ShapeDtypeStruct((t, LANES), F32),
        ],
        scratch_shapes=[pltpu.VMEM((tm, D_MODEL), BF16)],
        compiler_params=pltpu.CompilerParams(
            dimension_semantics=("arbitrary", "arbitrary"), vmem_limit_bytes=VMEM_LIMIT_PROJ),
        name="in_projection",
    )(x2d, norm_w, w_big, w_if)


def _retention_kernel(p_ref, cos_ref, sin_ref, dq_ref, dk_ref, gn_ref, o_ref, state_scr, *, chunk_decay):
    L = CHUNK

    @pl.when(pl.program_id(1) == 0)
    def _():
        state_scr[...] = jnp.zeros_like(state_scr)

    row = lax.broadcasted_iota(jnp.int32, (L, L), 0)
    col = lax.broadcasted_iota(jnp.int32, (L, L), 1)
    causal = row >= col
    n_chunks = p_ref.shape[0] // L
    units = [(ci * L, h) for ci in range(n_chunks) for h in range(N_HEADS)]

    qts, kts, scores = [], [], []
    for r0, h in units:
        cosf = cos_ref[r0:r0 + L, :]
        sinf = sin_ref[r0:r0 + L, :]
        q = p_ref[r0:r0 + L, h * QK_DIM:(h + 1) * QK_DIM].astype(F32)
        k = p_ref[r0:r0 + L, QK_ALL + h * QK_DIM:QK_ALL + (h + 1) * QK_DIM].astype(F32)
        qt = ((q * cosf + pltpu.roll(q, QK_DIM // 2, 1) * sinf) * dq_ref[h]).astype(BF16)
        kt = ((k * cosf + pltpu.roll(k, QK_DIM // 2, 1) * sinf) * dk_ref[h]).astype(BF16)
        s = lax.dot_general(qt, kt, (((1,), (1,)), ((), ())), preferred_element_type=F32)
        qts.append(qt)
        kts.append(kt)
        scores.append(jnp.where(causal, s, 0.0).astype(BF16))

    states = [state_scr[h] for h in range(N_HEADS)]
    for (r0, h), qt, kt, s in zip(units, qts, kts, scores):
        v = p_ref[r0:r0 + L, 2 * QK_ALL + h * V_DIM:2 * QK_ALL + (h + 1) * V_DIM]
        g = p_ref[r0:r0 + L, 2 * QK_ALL + V_ALL + h * V_DIM:2 * QK_ALL + V_ALL + (h + 1) * V_DIM].astype(F32)
        lhs = jnp.concatenate([s, qt], axis=1)
        rhs = jnp.concatenate([v, states[h].astype(BF16)], axis=0)
        o = jnp.dot(lhs, rhs, preferred_element_type=F32)
        kv = lax.dot_general(kt, v, (((0,), (0,)), ((), ())), preferred_element_type=F32)
        states[h] = (states[h] + kv) * chunk_decay[h]
        y = _rms(o) * gn_ref[:, h * V_DIM:(h + 1) * V_DIM] * (g * jax.nn.sigmoid(g))
        o_ref[r0:r0 + L, h * V_DIM:(h + 1) * V_DIM] = y.astype(o_ref.dtype)
    for h in range(N_HEADS):
        state_scr[h] = states[h]


def _retention(proj3, cosf, sinf, dq, dk, gn_w, chunk_decay):
    b, s, _ = proj3.shape
    lb = CHUNK * min(RET_CHUNKS_PER_STEP, s // CHUNK)
    return pl.pallas_call(
        functools.partial(_retention_kernel, chunk_decay=chunk_decay),
        grid=(b, s // lb),
        in_specs=[
            pl.BlockSpec((None, lb, MIX_COLS), lambda i, c: (i, c, 0)),
            pl.BlockSpec((lb, QK_DIM), lambda i, c: (c, 0)),
            pl.BlockSpec((lb, QK_DIM), lambda i, c: (c, 0)),
            pl.BlockSpec((N_HEADS, CHUNK, QK_DIM), lambda i, c: (0, 0, 0)),
            pl.BlockSpec((N_HEADS, CHUNK, QK_DIM), lambda i, c: (0, 0, 0)),
            pl.BlockSpec((1, V_ALL), lambda i, c: (0, 0)),
        ],
        out_specs=pl.BlockSpec((None, lb, V_ALL), lambda i, c: (i, c, 0)),
        out_shape=jax.ShapeDtypeStruct((b, s, V_ALL), BF16),
        scratch_shapes=[pltpu.VMEM((N_HEADS, QK_DIM, V_DIM), F32)],
        compiler_params=pltpu.CompilerParams(
            dimension_semantics=("arbitrary", "arbitrary"), vmem_limit_bytes=VMEM_LIMIT),
        name="retention",
    )(proj3, cosf, sinf, dq, dk, gn_w)


def _mlstm_kernel(p_ref, gates_ref, cw_ref, cb_ref, shift_ref, gb_ref, gn_ref, o_ref,
                  c_scr, n_scr, m_scr, tail_scr, act_scr):
    L = CHUNK
    lb = p_ref.shape[0]

    @pl.when(pl.program_id(1) == 0)
    def _():
        c_scr[...] = jnp.zeros_like(c_scr)
        n_scr[...] = jnp.zeros_like(n_scr)
        m_scr[...] = jnp.zeros_like(m_scr)
        tail_scr[...] = jnp.zeros_like(tail_scr)

    ub = p_ref[:, 0:2 * QK_ALL]
    u = ub.astype(F32)
    tail = tail_scr[...]
    row8 = lax.broadcasted_iota(jnp.int32, tail.shape, 0)
    acc = u * cw_ref[CONV_WIDTH - 1:CONV_WIDTH, :] + cb_ref[...]
    head = jnp.zeros_like(tail)
    for d in range(1, CONV_WIDTH):
        w_d = cw_ref[CONV_WIDTH - 1 - d:CONV_WIDTH - d, :]
        acc = acc + jnp.dot(shift_ref[d - 1], ub, preferred_element_type=F32) * w_d
        head = head + jnp.where(row8 < d, pltpu.roll(tail, d, 0), 0.0) * w_d
    acc = jnp.concatenate([acc[0:SUBLANES, :] + head, acc[SUBLANES:, :]], axis=0)
    tail_scr[...] = u[lb - SUBLANES:lb, :]
    act_scr[...] = acc * jax.nn.sigmoid(acc)

    row = lax.broadcasted_iota(jnp.int32, (L, L), 0)
    col = lax.broadcasted_iota(jnp.int32, (L, L), 1)
    causal = row >= col
    k_scale = QK_DIM ** -0.5
    units = [(ci * L, h) for ci in range(lb // L) for h in range(N_HEADS)]

    def load_qk(r0, h):
        q = act_scr[r0:r0 + L, h * QK_DIM:(h + 1) * QK_DIM]
        k = act_scr[r0:r0 + L, QK_ALL + h * QK_DIM:QK_ALL + (h + 1) * QK_DIM] * k_scale
        return q, k

    gate_terms, src_rows = [], []
    lane_t = lax.broadcasted_iota(jnp.int32, (2 * N_HEADS, L), 1)
    for ci in range(lb // L):
        pre = gates_ref[ci * L:(ci + 1) * L, :] + gb_ref[...]
        pre_rows = jnp.transpose(pre)[0:2 * N_HEADS, :]
        b_rows = jnp.minimum(pre_rows, 0.0) - jnp.log1p(jnp.exp(-jnp.abs(pre_rows)))
        shift = 1
        while shift < L:
            b_rows = b_rows + jnp.where(lane_t >= shift, pltpu.roll(b_rows, shift, 1), 0.0)
            shift *= 2
        src_rows.append(pre_rows[0:N_HEADS, :] - b_rows[N_HEADS:2 * N_HEADS, :])
        bcum = jnp.transpose(jnp.concatenate([b_rows, jnp.zeros((L - 2 * N_HEADS, L), F32)], axis=0))
        gate_terms.append((pre, bcum))
    scores = []
    for r0, h in units:
        q, k = load_qk(r0, h)
        scores.append(lax.dot_general(q.astype(BF16), k.astype(BF16), (((1,), (1,)), ((), ())),
                                      preferred_element_type=F32))

    m_state = [m_scr[h:h + 1, :] for h in range(N_HEADS)]
    terms = []
    for (r0, h), qk in zip(units, scores):
        pre, bcum = gate_terms[r0 // L]
        q, k = load_qk(r0, h)
        b_t = jnp.broadcast_to(bcum[:, N_HEADS + h:N_HEADS + h + 1], (L, L))
        i_t = jnp.broadcast_to(pre[:, h:h + 1], (L, L))
        src = jnp.broadcast_to(src_rows[r0 // L][h:h + 1, :], (L, L))
        m_prev = m_state[h]
        a = b_t + m_prev
        dmat = jnp.where(causal, b_t + src, -jnp.inf)
        m_t = jnp.maximum(a, jnp.max(dmat, axis=-1, keepdims=True))
        w_inter = jnp.exp(a - m_t)
        s = qk * jnp.exp(dmat - m_t)
        lhs = jnp.concatenate([s.astype(BF16), (q * w_inter).astype(BF16)], axis=1)
        b_last = b_t[L - 1:L, :]
        gk = b_last - b_t + i_t
        m_new = jnp.maximum(b_last + m_prev, jnp.max(gk, axis=0, keepdims=True))
        wk = k * jnp.exp(gk - m_new)
        m_state[h] = m_new
        terms.append(dict(
            lhs=lhs, s_sum=jnp.sum(s, axis=-1, keepdims=True), w_inter=w_inter, floor=jnp.exp(-m_t),
            keep=jnp.exp(b_last + m_prev - m_new), wk=wk.astype(BF16), wk_sum=jnp.sum(wk, axis=0, keepdims=True)))

    n_state = [n_scr[h:h + 1, :] for h in range(N_HEADS)]
    c_state = [c_scr[h] for h in range(N_HEADS)]
    v_cols = lambda h: slice(2 * QK_ALL + h * V_DIM, 2 * QK_ALL + (h + 1) * V_DIM)
    for ci in range(lb // L):
        r0 = ci * L
        chunk_terms = terms[ci * N_HEADS:(ci + 1) * N_HEADS]
        q_dot_n = [jnp.sum(load_qk(r0, h)[0] * n_state[h], axis=-1, keepdims=True) for h in range(N_HEADS)]
        nums = []
        for h, t in enumerate(chunk_terms):
            rhs = jnp.concatenate([p_ref[r0:r0 + L, v_cols(h)], c_state[h].astype(BF16)], axis=0)
            nums.append(jnp.dot(t["lhs"], rhs, preferred_element_type=F32))
        for h, t in enumerate(chunk_terms):
            c_state[h] = c_state[h] * t["keep"][:, 0:1] + lax.dot_general(
                t["wk"], p_ref[r0:r0 + L, v_cols(h)], (((0,), (0,)), ((), ())), preferred_element_type=F32)
            n_state[h] = n_state[h] * t["keep"] + t["wk_sum"]
        for h, t in enumerate(chunk_terms):
            og = p_ref[r0:r0 + L, V_ALL + v_cols(h).start:V_ALL + v_cols(h).stop].astype(F32)
            den = t["s_sum"] + q_dot_n[h] * t["w_inter"]
            inv = 1.0 / jnp.maximum(jnp.abs(den), t["floor"])
            hh = nums[h] * jnp.concatenate([inv] * (V_DIM // LANES), axis=1)
            y = _rms(hh * jax.nn.sigmoid(og)) * gn_ref[:, h * V_DIM:(h + 1) * V_DIM]
            o_ref[r0:r0 + L, h * V_DIM:(h + 1) * V_DIM] = y.astype(o_ref.dtype)

    for h in range(N_HEADS):
        c_scr[h] = c_state[h]
        n_scr[h:h + 1, :] = n_state[h]
        m_scr[h:h + 1, :] = m_state[h]


def _mlstm(proj3, gates3, conv_w, conv_b, gate_bias, gn_w):
    b, s, _ = proj3.shape
    lb = CHUNK * min(MLSTM_CHUNKS_PER_STEP, s // CHUNK)
    shifts = jnp.stack([jnp.eye(lb, k=-d, dtype=BF16) for d in range(1, CONV_WIDTH)])
    return pl.pallas_call(
        _mlstm_kernel,
        grid=(b, s // lb),
        in_specs=[
            pl.BlockSpec((None, lb, MIX_COLS), lambda i, c: (i, c, 1)),
            pl.BlockSpec((None, lb, LANES), lambda i, c: (i, c, 0)),
            pl.BlockSpec((CONV_WIDTH, 2 * QK_ALL), lambda i, c: (0, 0)),
            pl.BlockSpec((1, 2 * QK_ALL), lambda i, c: (0, 0)),
            pl.BlockSpec((CONV_WIDTH - 1, lb, lb), lambda i, c: (0, 0, 0)),
            pl.BlockSpec((1, LANES), lambda i, c: (0, 0)),
            pl.BlockSpec((1, V_ALL), lambda i, c: (0, 0)),
        ],
        out_specs=pl.BlockSpec((None, lb, V_ALL), lambda i, c: (i, c, 0)),
        out_shape=jax.ShapeDtypeStruct((b, s, V_ALL), BF16),
        scratch_shapes=[
            pltpu.VMEM((N_HEADS, QK_DIM, V_DIM), F32),
            pltpu.VMEM((8, QK_DIM), F32),
            pltpu.VMEM((8, LANES), F32),
            pltpu.VMEM((8, 2 * QK_ALL), F32),
            pltpu.VMEM((lb, 2 * QK_ALL), F32),
        ],
        compiler_params=pltpu.CompilerParams(
            dimension_semantics=("arbitrary", "arbitrary"), vmem_limit_bytes=VMEM_LIMIT),
        name="mlstm",
    )(proj3, gates3, conv_w, conv_b, shifts, gate_bias, gn_w)


def _merge_route_kernel(ret_ref, hm_ref, gr_ref, gm_ref, x_ref, wr_ref, wm_ref, wo_ref, nw_ref,
                        wrt_ref, brt_ref, lower_ref, x1_ref, h2_ref, route_ref, route_t_ref, tile_ref,
                        logits_scr):
    step = pl.program_id(0)

    @pl.when(step == 0)
    def _():
        logits_scr[...] = jnp.zeros_like(logits_scr)

    logits = logits_scr[...]

    y_ret = jnp.dot(ret_ref[...], wr_ref[...], preferred_element_type=F32)
    y_m = jnp.dot(hm_ref[...], wm_ref[...], preferred_element_type=F32)
    merged = (jax.nn.sigmoid(gr_ref[...].astype(F32)) * y_ret
              + jax.nn.sigmoid(gm_ref[...].astype(F32)) * y_m)
    x1 = x_ref[...] + jnp.dot(merged.astype(BF16), wo_ref[...], preferred_element_type=F32)
    x1_ref[...] = x1
    h2 = _rms(x1) * nw_ref[...]
    h2_ref[...] = _pack_rows(h2)
    new_logits = jnp.dot(h2.astype(BF16), wrt_ref[...], preferred_element_type=F32) + brt_ref[...]

    live = jnp.where(step > 0, 1.0, 0.0)
    tm = logits.shape[0]
    lane = lax.broadcasted_iota(jnp.int32, (tm, LANES), 1)
    neg = -jnp.inf
    big = jnp.int32(LANES)
    is_group = (lane >= N_EXPERTS) & (lane < N_EXPERTS + N_GROUPS)
    gl = jnp.where(is_group, logits, neg)
    g_max = jnp.max(gl, axis=-1, keepdims=True)
    g_idx = jnp.min(jnp.where(gl == g_max, lane, big), axis=-1, keepdims=True) - N_EXPERTS
    g_w = 1.0 / jnp.sum(jnp.exp(gl - g_max), axis=-1, keepdims=True)
    in_group = (lane >= g_idx * EXPERTS_PER_GROUP) & (lane < (g_idx + 1) * EXPERTS_PER_GROUP)
    el = jnp.where(in_group, logits, neg)
    l1 = jnp.max(el, axis=-1, keepdims=True)
    e1 = jnp.min(jnp.where(el == l1, lane, big), axis=-1, keepdims=True)
    el2 = jnp.where(lane == e1, neg, el)
    l2 = jnp.max(el2, axis=-1, keepdims=True)
    e2 = jnp.min(jnp.where(el2 == l2, lane, big), axis=-1, keepdims=True)
    t21 = jnp.exp(l2 - l1)
    w1 = g_w / (1.0 + t21)
    w2 = g_w * t21 / (1.0 + t21)

    hit1 = lane == e1
    hit2 = lane == e2
    cnt = jnp.where(hit1 | hit2, live, 0.0)
    before = jnp.dot(lower_ref[...], cnt.astype(BF16), preferred_element_type=F32)
    count = jnp.sum(cnt, axis=0, keepdims=True)
    run = jnp.floor((count + (SUBLANES - 1)) * (1.0 / SUBLANES)) * SUBLANES
    lane1 = lax.broadcasted_iota(jnp.int32, (1, LANES), 1)
    run_end = run
    shift = 1
    while shift < N_EXPERTS:
        run_end = run_end + jnp.where(lane1 >= shift, pltpu.roll(run_end, shift, 1), 0.0)
        shift *= 2
    run_start = run_end - run
    local = before + run_start
    r1 = jnp.sum(jnp.where(hit1, local, 0.0), axis=-1, keepdims=True)
    r2 = jnp.sum(jnp.where(hit2, local, 0.0), axis=-1, keepdims=True)
    sub8 = lax.broadcasted_iota(jnp.int32, (SUBLANES, LANES), 0)
    tile_ref[...] = jnp.where(sub8 == 0, count, jnp.where(sub8 == 1, run, jnp.where(sub8 == 2, run_start, 0.0)))

    fields = (e1.astype(F32), e2.astype(F32), r1, r2, w1, w2)
    packed = jnp.zeros((tm, LANES), F32)
    for idx, val in enumerate(fields):
        packed = jnp.where(lane == idx, val, packed)
    route_ref[...] = packed
    route_t_ref[...] = jnp.transpose(packed)[0:ROUTE_ROWS, :]
    logits_scr[...] = new_logits


def _merge_route(ret, hm, proj, x2d, w_ret, w_m, w_out, norm_w, w_router, b_router, lower):
    t = x2d.shape[0]
    tm = min(ROWS_MERGE, t)
    n_tiles = t // tm
    gate_r_blk = 2 * MIX_COLS // D_MODEL
    tile = lambda i: jnp.minimum(i, n_tiles - 1)
    routed = lambda i: jnp.maximum(i - 1, 0)
    row_blk = lambda i: (tile(i), 0)
    const = lambda i: (0, 0)
    return pl.pallas_call(
        _merge_route_kernel,
        grid=(n_tiles + 1,),
        in_specs=[
            pl.BlockSpec((tm, V_ALL), row_blk),
            pl.BlockSpec((tm, V_ALL), row_blk),
            pl.BlockSpec((tm, D_MODEL), lambda i: (tile(i), gate_r_blk)),
            pl.BlockSpec((tm, D_MODEL), lambda i: (tile(i), gate_r_blk + 1)),
            pl.BlockSpec((tm, D_MODEL), row_blk),
            pl.BlockSpec((V_ALL, D_MODEL), const),
            pl.BlockSpec((V_ALL, D_MODEL), const),
            pl.BlockSpec((D_MODEL, D_MODEL), const),
            pl.BlockSpec((1, D_MODEL), const),
            pl.BlockSpec((D_MODEL, LANES), const),
            pl.BlockSpec((1, LANES), const),
            pl.BlockSpec((tm, tm), const),
        ],
        out_specs=[
            pl.BlockSpec((tm, D_MODEL), row_blk),
            pl.BlockSpec((tm, PACKED), row_blk),
            pl.BlockSpec((tm, LANES), lambda i: (routed(i), 0)),
            pl.BlockSpec((ROUTE_ROWS, tm), lambda i: (0, routed(i))),
            pl.BlockSpec((None, SUBLANES, LANES), lambda i: (routed(i), 0, 0)),
        ],
        out_shape=[
            jax.ShapeDtypeStruct((t, D_MODEL), F32),
            jax.ShapeDtypeStruct((t, PACKED), U32),
            jax.ShapeDtypeStruct((t, LANES), F32),
            jax.ShapeDtypeStruct((ROUTE_ROWS, t), F32),
            jax.ShapeDtypeStruct((n_tiles, SUBLANES, LANES), F32),
        ],
        scratch_shapes=[pltpu.VMEM((tm, LANES), F32)],
        compiler_params=pltpu.CompilerParams(
            dimension_semantics=("arbitrary",), vmem_limit_bytes=VMEM_LIMIT),
        name="merge_route",
    )(ret, hm, proj, proj, x2d, w_ret, w_m, w_out, norm_w, w_router, b_router, lower)


def _dispatch_kernel(fill_ref, n_groups_ref, gdst_ref, route_t_ref, h2_ref, xs_hbm, sorted_scr, zero_scr, sems):
    step = pl.program_id(0)
    n_tiles = pl.num_programs(0) - 1
    tm = h2_ref.shape[0]
    slot = step % 2

    def group_copy(which, j):
        return pltpu.make_async_copy(sorted_scr.at[which, pl.ds(j, 1)], xs_hbm.at[pl.ds(gdst_ref[j], 1)],
                                     sems.at[which])

    def wait_groups(which, count):
        def wait(j, carry):
            group_copy(which, 0).wait()
            return carry
        lax.fori_loop(0, count, wait, 0)

    @pl.when(step == 0)
    def _():
        zero_scr[...] = jnp.zeros_like(zero_scr)

        def fill_copy(j):
            g0 = pl.multiple_of(fill_ref[j], MOE_ROWS // SUBLANES)
            return pltpu.make_async_copy(zero_scr, xs_hbm.at[pl.ds(g0, MOE_ROWS // SUBLANES)], sems.at[0])

        def start_fill(j, carry):
            @pl.when(fill_ref[j] >= 0)
            def _():
                fill_copy(j).start()
            return carry

        def wait_fill(j, carry):
            @pl.when(fill_ref[j] >= 0)
            def _():
                fill_copy(j).wait()
            return carry

        lax.fori_loop(0, fill_ref.shape[0], start_fill, 0)
        lax.fori_loop(0, fill_ref.shape[0], wait_fill, 0)

    @pl.when(step >= 1)
    def _():
        def start(j, carry):
            group_copy(1 - slot, j).start()
            return carry
        lax.fori_loop(0, n_groups_ref[jnp.maximum(step - 1, 0)], start, 0)

    @pl.when(step >= 2)
    def _():
        wait_groups(slot, n_groups_ref[jnp.maximum(step - 2, 0)])

    @pl.when(step < n_tiles)
    def _():
        pos = lax.broadcasted_iota(jnp.int32, (LOCAL_ROWS, tm), 0)
        row1 = route_t_ref[2:3, :].astype(jnp.int32)
        row2 = route_t_ref[3:4, :].astype(jnp.int32)
        pick = jnp.where((pos == row1) | (pos == row2), 1.0, 0.0).astype(BF16)
        tokens = _unpack_rows(h2_ref[...]).astype(BF16)
        ordered = jnp.dot(pick, tokens, preferred_element_type=F32)
        sorted_scr[slot] = _pack_rows(ordered).reshape(LOCAL_GROUPS, SUBLANES, PACKED)

    @pl.when(step == n_tiles)
    def _():
        wait_groups(1 - slot, n_groups_ref[jnp.maximum(step - 1, 0)])


def _dispatch(fill_groups, n_groups, gdst, route_t, h2p, n_slots):
    t = h2p.shape[0]
    tm = min(ROWS_MERGE, t)
    n_tiles = t // tm
    sorted_tile = lambda i: jnp.minimum(i, n_tiles - 1)
    return pl.pallas_call(
        _dispatch_kernel,
        grid=(n_tiles + 1,),
        in_specs=[
            pl.BlockSpec(memory_space=pltpu.SMEM),
            pl.BlockSpec(memory_space=pltpu.SMEM),
            pl.BlockSpec((GROUP_TABLE,), lambda i: (jnp.maximum(i - 1, 0),), memory_space=pltpu.SMEM),
            pl.BlockSpec((ROUTE_ROWS, tm), lambda i: (0, sorted_tile(i))),
            pl.BlockSpec((tm, PACKED), lambda i: (sorted_tile(i), 0)),
        ],
        out_specs=pl.BlockSpec(memory_space=pl.ANY),
        out_shape=jax.ShapeDtypeStruct((n_slots // SUBLANES, SUBLANES, PACKED), U32),
        scratch_shapes=[
            pltpu.VMEM((2, LOCAL_GROUPS, SUBLANES, PACKED), U32),
            pltpu.VMEM((MOE_ROWS // SUBLANES, SUBLANES, PACKED), U32),
            pltpu.SemaphoreType.DMA((2,)),
        ],
        compiler_params=pltpu.CompilerParams(
            dimension_semantics=("arbitrary",), vmem_limit_bytes=VMEM_LIMIT),
        name="dispatch",
    )(fill_groups, n_groups, gdst, route_t, h2p)


def _experts_kernel(blk_e_ref, blk_valid_ref, xs_ref, wg_ref, wu_ref, wd_ref, ys_ref, wgu_scr, wd_scr):
    i = pl.program_id(0)
    valid = blk_valid_ref[i]

    @pl.when((i == 0) | (blk_e_ref[i] != blk_e_ref[jnp.maximum(i - 1, 0)]))
    def _():
        wgu_scr[:, :D_EXPERT] = wg_ref[...].astype(BF16)
        wgu_scr[:, D_EXPERT:] = wu_ref[...].astype(BF16)
        wd_scr[...] = wd_ref[...].astype(BF16)

    @pl.when(valid > 0)
    def _():
        xb = _unpack_rows(xs_ref[...]).astype(BF16)
        gu = jnp.dot(xb, wgu_scr[...], preferred_element_type=F32)
        g = gu[:, :D_EXPERT]
        act = (g * jax.nn.sigmoid(g) * gu[:, D_EXPERT:]).astype(BF16)
        ys_ref[...] = _pack_rows(jnp.dot(act, wd_scr[...], preferred_element_type=F32))

    @pl.when(valid <= 0)
    def _():
        ys_ref[...] = jnp.zeros_like(ys_ref)


def _experts(blk_e, blk_valid, xs, w_gate, w_up, w_down):
    p = xs.shape[0]
    per_expert = lambda i, be, bv: (be[i], 0, 0)
    grid_spec = pltpu.PrefetchScalarGridSpec(
        num_scalar_prefetch=2,
        grid=(p // MOE_ROWS,),
        in_specs=[
            pl.BlockSpec((MOE_ROWS, PACKED), lambda i, be, bv: (i, 0)),
            pl.BlockSpec((None, D_MODEL, D_EXPERT), per_expert),
            pl.BlockSpec((None, D_MODEL, D_EXPERT), per_expert),
            pl.BlockSpec((None, D_EXPERT, D_MODEL), per_expert),
        ],
        out_specs=pl.BlockSpec((MOE_ROWS, PACKED), lambda i, be, bv: (i, 0)),
        scratch_shapes=[pltpu.VMEM((D_MODEL, 2 * D_EXPERT), BF16), pltpu.VMEM((D_EXPERT, D_MODEL), BF16)],
    )
    return pl.pallas_call(
        _experts_kernel,
        grid_spec=grid_spec,
        out_shape=jax.ShapeDtypeStruct((p, PACKED), U32),
        compiler_params=pltpu.CompilerParams(
            dimension_semantics=("arbitrary",), vmem_limit_bytes=VMEM_LIMIT),
        name="experts",
    )(blk_e, blk_valid, xs, w_gate, w_up, w_down)


def _combine_kernel(gdst_ref, route_ref, x1_ref, nw_ref, ys_hbm, o_ref, buf, sems):
    step = pl.program_id(0)
    n_tiles = pl.num_programs(0) - 1
    tm = x1_ref.shape[0]
    slot = step % 2
    prev = 1 - slot

    @pl.when(step < n_tiles)
    def _():
        def start(j, carry):
            pltpu.make_async_copy(ys_hbm.at[pl.ds(gdst_ref[j], 1)], buf.at[slot, pl.ds(j, 1)], sems.at[slot]).start()
            return carry
        lax.fori_loop(0, LOCAL_GROUPS, start, 0, unroll=8)

    @pl.when(step > 0)
    def _():
        pltpu.make_async_copy(ys_hbm.at[pl.ds(0, LOCAL_GROUPS)], buf.at[prev], sems.at[prev]).wait()

        rows = _unpack_rows(buf[prev].reshape(LOCAL_ROWS, PACKED)).astype(BF16)
        pos = lax.broadcasted_iota(jnp.int32, (tm, LOCAL_ROWS), 1)
        row1 = route_ref[:, 2:3].astype(jnp.int32)
        row2 = route_ref[:, 3:4].astype(jnp.int32)
        mix = (jnp.where(pos == row1, route_ref[:, 4:5], 0.0)
               + jnp.where(pos == row2, route_ref[:, 5:6], 0.0)).astype(BF16)
        x2 = x1_ref[...] + jnp.dot(mix, rows, preferred_element_type=F32)
        o_ref[...] = _rms(x2) * nw_ref[...]


def _combine(gdst, route, x1, norm_w, ys3):
    t = x1.shape[0]
    tm = min(ROWS_MERGE, t)
    n_tiles = t // tm
    gathered = lambda i: (jnp.minimum(i, n_tiles - 1),)
    finished = lambda i: (jnp.maximum(i - 1, 0), 0)
    return pl.pallas_call(
        _combine_kernel,
        grid=(n_tiles + 1,),
        in_specs=[
            pl.BlockSpec((GROUP_TABLE,), gathered, memory_space=pltpu.SMEM),
            pl.BlockSpec((tm, LANES), finished),
            pl.BlockSpec((tm, D_MODEL), finished),
            pl.BlockSpec((1, D_MODEL), lambda i: (0, 0)),
            pl.BlockSpec(memory_space=pl.ANY),
        ],
        out_specs=pl.BlockSpec((tm, D_MODEL), finished),
        out_shape=jax.ShapeDtypeStruct((t, D_MODEL), F32),
        scratch_shapes=[
            pltpu.VMEM((2, LOCAL_GROUPS, SUBLANES, PACKED), U32),
            pltpu.SemaphoreType.DMA((2,)),
        ],
        compiler_params=pltpu.CompilerParams(
            dimension_semantics=("arbitrary",), vmem_limit_bytes=VMEM_LIMIT),
        name="combine",
    )(gdst, route, x1, norm_w, ys3)


def _rotary_tables(seq):
    inv_freq = 1.0 / (ROPE_BASE ** (jnp.arange(0, QK_DIM, 2, dtype=F32) / QK_DIM))
    ang = jnp.arange(seq, dtype=F32)[:, None] * inv_freq[None, :]
    cos, sin = jnp.cos(ang), jnp.sin(ang)
    return jnp.concatenate([cos, cos], axis=1), jnp.concatenate([-sin, sin], axis=1)


def _retention_decay_tables():
    gamma = 1.0 - 2.0 ** (-5.0 - np.arange(N_HEADS, dtype=np.float64))
    idx = np.arange(CHUNK, dtype=np.float64) + 1.0
    dq = gamma[:, None] ** idx[None, :]
    dk = gamma[:, None] ** (-idx[None, :]) * QK_DIM ** -0.5
    bcast = lambda a: jnp.asarray(np.broadcast_to(a[:, :, None], (N_HEADS, CHUNK, QK_DIM)), F32)
    return bcast(dq), bcast(dk), tuple(float(g) for g in gamma ** CHUNK)


def kernel(x, norm_mix_w, w_in, ret_gn_w, w_ret_branch, mlstm_conv_w, mlstm_conv_b, b_igate, b_fgate,
           mlstm_gn_w, w_mlstm_branch, w_out, norm_ffn_w, w_group, b_group, w_expert_router,
           b_expert_router, w_gate, w_up, w_down, norm_final_w):
    assert norm_mix_w.shape[0] == 1, "one layer"
    b, s, d = x.shape
    t = b * s
    assert d == D_MODEL and s % CHUNK == 0

    wi = w_in[0]
    n_pre = 2 * MIX_COLS
    w_big = jnp.concatenate([wi[:, :n_pre], wi[:, n_pre + 2 * N_HEADS:]], axis=1).astype(BF16)
    w_if = jnp.pad(wi[:, n_pre:n_pre + 2 * N_HEADS], ((0, 0), (0, LANES - 2 * N_HEADS))).astype(BF16)
    gate_bias = jnp.pad(jnp.concatenate([b_igate[0], b_fgate[0]]), (0, LANES - 2 * N_HEADS))[None, :]
    w_router = jnp.pad(jnp.concatenate([w_expert_router[0], w_group[0]], axis=1),
                       ((0, 0), (0, LANES - N_EXPERTS - N_GROUPS))).astype(BF16)
    b_router = jnp.pad(jnp.concatenate([b_expert_router[0], b_group[0]]),
                       (0, LANES - N_EXPERTS - N_GROUPS))[None, :]

    cosf, sinf = _rotary_tables(s)
    dq, dk, chunk_decay = _retention_decay_tables()
    tm_merge = min(ROWS_MERGE, t)
    lower = jnp.tril(jnp.ones((tm_merge, tm_merge), F32), -1).astype(BF16)

    x2d = x.reshape(t, d)
    proj, gates = _in_projection(x2d, norm_mix_w, w_big, w_if)
    proj3 = proj.reshape(b, s, N_BIG)
    ret = _retention(proj3, cosf, sinf, dq, dk, ret_gn_w, chunk_decay)
    hm = _mlstm(proj3, gates.reshape(b, s, LANES), mlstm_conv_w[0, :, 0, :], mlstm_conv_b,
                gate_bias, mlstm_gn_w)
    x1, h2p, route, route_t, tiles = _merge_route(
        ret.reshape(t, V_ALL), hm.reshape(t, V_ALL), proj, x2d, w_ret_branch[0].astype(BF16),
        w_mlstm_branch[0].astype(BF16), w_out[0].astype(BF16), norm_ffn_w, w_router, b_router, lower)

    n_tiles = t // tm_merge
    n_slots = 2 * t + N_EXPERTS * (MOE_ROWS + n_tiles * (SUBLANES - 1) // SUBLANES * SUBLANES)
    n_slots = -(-n_slots // MOE_ROWS) * MOE_ROWS
    run = tiles[:, 1, :N_EXPERTS].astype(jnp.int32)
    run_start = tiles[:, 2, :N_EXPERTS].astype(jnp.int32)
    rows_e = jnp.sum(run, axis=0)
    padded = (rows_e + MOE_ROWS - 1) // MOE_ROWS * MOE_ROWS
    expert_ids = jnp.arange(N_EXPERTS, dtype=jnp.int32)
    pstart = jnp.sum(jnp.where(expert_ids[None, :] < expert_ids[:, None], padded[None, :], 0), axis=1)
    pend = pstart + padded
    tile_ids = jnp.arange(n_tiles, dtype=jnp.int32)
    run_t = run.T
    earlier = jnp.sum(jnp.where((tile_ids[None, :] < tile_ids[:, None])[None], run_t[:, None, :], 0), axis=2).T
    global_start = pstart[None, :] + earlier
    local_row = jnp.arange(GROUP_TABLE, dtype=jnp.int32)[None, None, :] * SUBLANES
    in_run = (local_row >= run_start[:, :, None]) & (local_row < (run_start + run)[:, :, None])
    gdst = jnp.sum(jnp.where(in_run, (global_start - run_start)[:, :, None] + local_row, 0), axis=1) // SUBLANES
    gdst = gdst.reshape(n_tiles * GROUP_TABLE).astype(jnp.int32)
    n_groups = (jnp.sum(run, axis=1) // SUBLANES).astype(jnp.int32)
    blk_start = jnp.arange(n_slots // MOE_ROWS, dtype=jnp.int32) * MOE_ROWS
    blk_e = jnp.minimum(jnp.sum(blk_start[:, None] >= pend[None, :], axis=-1), N_EXPERTS - 1).astype(jnp.int32)
    in_expert = (blk_start[:, None] >= pstart[None, :]) & (blk_start[:, None] < pend[None, :])
    rows_end = jnp.sum(jnp.where(in_expert, (pstart + rows_e)[None, :], 0), axis=1)
    blk_valid = jnp.clip(rows_end - blk_start, 0, MOE_ROWS).astype(jnp.int32)
    tail = pend[-1] + jnp.arange((n_slots - 2 * t) // MOE_ROWS, dtype=jnp.int32) * MOE_ROWS
    fill_groups = (jnp.concatenate([jnp.where(padded > 0, pend - MOE_ROWS, -SUBLANES),
                                    jnp.where(tail < n_slots, tail, -SUBLANES)]) // SUBLANES).astype(jnp.int32)

    xs3 = _dispatch(fill_groups, n_groups, gdst, route_t, h2p, n_slots)
    ys = _experts(blk_e, blk_valid, xs3.reshape(n_slots, PACKED), w_gate[0], w_up[0], w_down[0])
    out = _combine(gdst, route, x1, norm_final_w[None, :],
                   ys.reshape(n_slots // SUBLANES, SUBLANES, PACKED))
    return out.reshape(b, s, d)
```

```python
import functools

import numpy as np
import jax
import jax.numpy as jnp
from jax import lax
from jax.experimental import pallas as pl
from jax.experimental.pallas import tpu as pltpu

F32 = jnp.float32
BF16 = jnp.bfloat16
U32 = jnp.uint32

D_MODEL = 1024
N_HEADS = 4
QK_DIM = 128
V_DIM = 256
CHUNK = 128
CONV_WIDTH = 4
ROPE_BASE = 10000.0
N_GROUPS = 4
EXPERTS_PER_GROUP = 8
N_EXPERTS = N_GROUPS * EXPERTS_PER_GROUP
D_EXPERT = 512
NORM_EPS = 1e-6
QK_ALL = N_HEADS * QK_DIM
V_ALL = N_HEADS * V_DIM

MIX_COLS = 2 * QK_ALL + 2 * V_ALL
N_BIG = 2 * MIX_COLS + 2 * D_MODEL
LANES = 128
PACKED = D_MODEL // 2

ROWS_PROJ = 2048
COLS_PROJ = 2048
NORM_ROWS = 512
RET_CHUNKS_PER_STEP = 8
MLSTM_CHUNKS_PER_STEP = 4
ROWS_MERGE = 512
ROUTE_ROWS = 8
SUBLANES = 8
LOCAL_ROWS = 2 * ROWS_MERGE + 256
assert LOCAL_ROWS >= 2 * ROWS_MERGE + N_EXPERTS * (SUBLANES - 1) and LOCAL_ROWS % LANES == 0
LOCAL_GROUPS = LOCAL_ROWS // SUBLANES
GROUP_TABLE = 256
assert GROUP_TABLE >= LOCAL_GROUPS
MOE_ROWS = 512
V7X_VMEM_BYTES = 64 * 1024 * 1024
VMEM_LIMIT = V7X_VMEM_BYTES - 8 * 1024 * 1024
VMEM_LIMIT_PROJ = V7X_VMEM_BYTES - 4 * 1024 * 1024


def _rms(x, eps=NORM_EPS):
    return x * lax.rsqrt(jnp.mean(x * x, axis=-1, keepdims=True) + eps)


def _pack_rows(x):
    lo = lax.bitcast_convert_type(x[:, :PACKED].astype(BF16).astype(F32), U32)
    hi = lax.bitcast_convert_type(x[:, PACKED:].astype(BF16).astype(F32), U32)
    return (hi & jnp.uint32(0xFFFF0000)) | (lo >> 16)


def _unpack_rows(w):
    lo = lax.bitcast_convert_type(w << 16, F32)
    hi = lax.bitcast_convert_type(w & jnp.uint32(0xFFFF0000), F32)
    return jnp.concatenate([lo, hi], axis=1)


def _in_proj_kernel(x_ref, nw_ref, w_ref, wif_ref, o_ref, gates_ref, h_scr):
    @pl.when(pl.program_id(1) == 0)
    def _():
        for r0 in range(0, x_ref.shape[0], NORM_ROWS):
            rows = slice(r0, r0 + NORM_ROWS)
            h = (_rms(x_ref[rows, :]) * nw_ref[...]).astype(BF16)
            h_scr[rows, :] = h
            gates_ref[rows, :] = jnp.dot(h, wif_ref[...], preferred_element_type=F32)
            o_ref[rows, :] = jnp.dot(h, w_ref[...], preferred_element_type=F32).astype(o_ref.dtype)

    @pl.when(pl.program_id(1) != 0)
    def _():
        o_ref[...] = jnp.dot(h_scr[...], w_ref[...], preferred_element_type=F32).astype(o_ref.dtype)


def _in_projection(x2d, norm_w, w_big, w_if):
    t = x2d.shape[0]
    tm = min(ROWS_PROJ, t)
    tn = COLS_PROJ
    return pl.pallas_call(
        _in_proj_kernel,
        grid=(t // tm, N_BIG // tn),
        in_specs=[
            pl.BlockSpec((tm, D_MODEL), lambda i, j: (i, 0)),
            pl.BlockSpec((1, D_MODEL), lambda i, j: (0, 0)),
            pl.BlockSpec((D_MODEL, tn), lambda i, j: (0, j)),
            pl.BlockSpec((D_MODEL, LANES), lambda i, j: (0, 0)),
        ],
        out_specs=[
            pl.BlockSpec((tm, tn), lambda i, j: (i, j)),
            pl.BlockSpec((tm, LANES), lambda i, j: (i, 0)),
        ],
        out_shape=[
            jax.ShapeDtypeStruct((t, N_BIG), BF16),
            jax.ShapeDtypeStruct((t, LANES), F32),
        ],
        scratch_shapes=[pltpu.VMEM((tm, D_MODEL), BF16)],
        compiler_params=pltpu.CompilerParams(
            dimension_semantics=("arbitrary", "arbitrary"), vmem_limit_bytes=VMEM_LIMIT_PROJ),
        name="in_projection",
    )(x2d, norm_w, w_big, w_if)


def _retention_kernel(p_ref, cos_ref, sin_ref, dq_ref, dk_ref, gn_ref, o_ref, state_scr, *, chunk_decay):
    L = CHUNK

    @pl.when(pl.program_id(1) == 0)
    def _():
        state_scr[...] = jnp.zeros_like(state_scr)

    row = lax.broadcasted_iota(jnp.int32, (L, L), 0)
    col = lax.broadcasted_iota(jnp.int32, (L, L), 1)
    causal = row >= col
    n_chunks = p_ref.shape[0] // L
    units = [(ci * L, h) for ci in range(n_chunks) for h in range(N_HEADS)]

    qts, kts, scores = [], [], []
    for r0, h in units:
        cosf = cos_ref[r0:r0 + L, :]
        sinf = sin_ref[r0:r0 + L, :]
        q = p_ref[r0:r0 + L, h * QK_DIM:(h + 1) * QK_DIM].astype(F32)
        k = p_ref[r0:r0 + L, QK_ALL + h * QK_DIM:QK_ALL + (h + 1) * QK_DIM].astype(F32)
        qt = ((q * cosf + pltpu.roll(q, QK_DIM // 2, 1) * sinf) * dq_ref[h]).astype(BF16)
        kt = ((k * cosf + pltpu.roll(k, QK_DIM // 2, 1) * sinf) * dk_ref[h]).astype(BF16)
        s = lax.dot_general(qt, kt, (((1,), (1,)), ((), ())), preferred_element_type=F32)
        qts.append(qt)
        kts.append(kt)
        scores.append(jnp.where(causal, s, 0.0).astype(BF16))

    states = [state_scr[h] for h in range(N_HEADS)]
    for (r0, h), qt, kt, s in zip(units, qts, kts, scores):
        v = p_ref[r0:r0 + L, 2 * QK_ALL + h * V_DIM:2 * QK_ALL + (h + 1) * V_DIM]
        g = p_ref[r0:r0 + L, 2 * QK_ALL + V_ALL + h * V_DIM:2 * QK_ALL + V_ALL + (h + 1) * V_DIM].astype(F32)
        lhs = jnp.concatenate([s, qt], axis=1)
        rhs = jnp.concatenate([v, states[h].astype(BF16)], axis=0)
        o = jnp.dot(lhs, rhs, preferred_element_type=F32)
        kv = lax.dot_general(kt, v, (((0,), (0,)), ((), ())), preferred_element_type=F32)
        states[h] = (states[h] + kv) * chunk_decay[h]
        y = _rms(o) * gn_ref[:, h * V_DIM:(h + 1) * V_DIM] * (g * jax.nn.sigmoid(g))
        o_ref[r0:r0 + L, h * V_DIM:(h + 1) * V_DIM] = y.astype(o_ref.dtype)
    for h in range(N_HEADS):
        state_scr[h] = states[h]


def _retention(proj3, cosf, sinf, dq, dk, gn_w, chunk_decay):
    b, s, _ = proj3.shape
    lb = CHUNK * min(RET_CHUNKS_PER_STEP, s // CHUNK)
    return pl.pallas_call(
        functools.partial(_retention_kernel, chunk_decay=chunk_decay),
        grid=(b, s // lb),
        in_specs=[
            pl.BlockSpec((None, lb, MIX_COLS), lambda i, c: (i, c, 0)),
            pl.BlockSpec((lb, QK_DIM), lambda i, c: (c, 0)),
            pl.BlockSpec((lb, QK_DIM), lambda i, c: (c, 0)),
            pl.BlockSpec((N_HEADS, CHUNK, QK_DIM), lambda i, c: (0, 0, 0)),
            pl.BlockSpec((N_HEADS, CHUNK, QK_DIM), lambda i, c: (0, 0, 0)),
            pl.BlockSpec((1, V_ALL), lambda i, c: (0, 0)),
        ],
        out_specs=pl.BlockSpec((None, lb, V_ALL), lambda i, c: (i, c, 0)),
        out_shape=jax.ShapeDtypeStruct((b, s, V_ALL), BF16),
        scratch_shapes=[pltpu.VMEM((N_HEADS, QK_DIM, V_DIM), F32)],
        compiler_params=pltpu.CompilerParams(
            dimension_semantics=("arbitrary", "arbitrary"), vmem_limit_bytes=VMEM_LIMIT),
        name="retention",
    )(proj3, cosf, sinf, dq, dk, gn_w)


def _mlstm_kernel(p_ref, gates_ref, cw_ref, cb_ref, shift_ref, gb_ref, gn_ref, o_ref,
                  c_scr, n_scr, m_scr, tail_scr, act_scr):
    L = CHUNK
    lb = p_ref.shape[0]

    @pl.when(pl.program_id(1) == 0)
    def _():
        c_scr[...] = jnp.zeros_like(c_scr)
        n_scr[...] = jnp.zeros_like(n_scr)
        m_scr[...] = jnp.zeros_like(m_scr)
        tail_scr[...] = jnp.zeros_like(tail_scr)

    ub = p_ref[:, 0:2 * QK_ALL]
    u = ub.astype(F32)
    tail = tail_scr[...]
    row8 = lax.broadcasted_iota(jnp.int32, tail.shape, 0)
    acc = u * cw_ref[CONV_WIDTH - 1:CONV_WIDTH, :] + cb_ref[...]
    head = jnp.zeros_like(tail)
    for d in range(1, CONV_WIDTH):
        w_d = cw_ref[CONV_WIDTH - 1 - d:CONV_WIDTH - d, :]
        acc = acc + jnp.dot(shift_ref[d - 1], ub, preferred_element_type=F32) * w_d
        head = head + jnp.where(row8 < d, pltpu.roll(tail, d, 0), 0.0) * w_d
    acc = jnp.concatenate([acc[0:SUBLANES, :] + head, acc[SUBLANES:, :]], axis=0)
    tail_scr[...] = u[lb - SUBLANES:lb, :]
    act_scr[...] = acc * jax.nn.sigmoid(acc)

    row = lax.broadcasted_iota(jnp.int32, (L, L), 0)
    col = lax.broadcasted_iota(jnp.int32, (L, L), 1)
    causal = row >= col
    k_scale = QK_DIM ** -0.5
    units = [(ci * L, h) for ci in range(lb // L) for h in range(N_HEADS)]

    def load_qk(r0, h):
        q = act_scr[r0:r0 + L, h * QK_DIM:(h + 1) * QK_DIM]
        k = act_scr[r0:r0 + L, QK_ALL + h * QK_DIM:QK_ALL + (h + 1) * QK_DIM] * k_scale
        return q, k

    gate_terms, src_rows = [], []
    lane_t = lax.broadcasted_iota(jnp.int32, (2 * N_HEADS, L), 1)
    for ci in range(lb // L):
        pre = gates_ref[ci * L:(ci + 1) * L, :] + gb_ref[...]
        pre_rows = jnp.transpose(pre)[0:2 * N_HEADS, :]
        b_rows = jnp.minimum(pre_rows, 0.0) - jnp.log1p(jnp.exp(-jnp.abs(pre_rows)))
        shift = 1
        while shift < L:
            b_rows = b_rows + jnp.where(lane_t >= shift, pltpu.roll(b_rows, shift, 1), 0.0)
            shift *= 2
        src_rows.append(pre_rows[0:N_HEADS, :] - b_rows[N_HEADS:2 * N_HEADS, :])
        bcum = jnp.transpose(jnp.concatenate([b_rows, jnp.zeros((L - 2 * N_HEADS, L), F32)], axis=0))
        gate_terms.append((pre, bcum))
    scores = []
    for r0, h in units:
        q, k = load_qk(r0, h)
        scores.append(lax.dot_general(q.astype(BF16), k.astype(BF16), (((1,), (1,)), ((), ())),
                                      preferred_element_type=F32))

    m_state = [m_scr[h:h + 1, :] for h in range(N_HEADS)]
    terms = []
    for (r0, h), qk in zip(units, scores):
        pre, bcum = gate_terms[r0 // L]
        q, k = load_qk(r0, h)
        b_t = jnp.broadcast_to(bcum[:, N_HEADS + h:N_HEADS + h + 1], (L, L))
        i_t = jnp.broadcast_to(pre[:, h:h + 1], (L, L))
        src = jnp.broadcast_to(src_rows[r0 // L][h:h + 1, :], (L, L))
        m_prev = m_state[h]
        a = b_t + m_prev
        dmat = jnp.where(causal, b_t + src, -jnp.inf)
        m_t = jnp.maximum(a, jnp.max(dmat, axis=-1, keepdims=True))
        w_inter = jnp.exp(a - m_t)
        s = qk * jnp.exp(dmat - m_t)
        lhs = jnp.concatenate([s.astype(BF16), (q * w_inter).astype(BF16)], axis=1)
        b_last = b_t[L - 1:L, :]
        gk = b_last - b_t + i_t
        m_new = jnp.maximum(b_last + m_prev, jnp.max(gk, axis=0, keepdims=True))
        wk = k * jnp.exp(gk - m_new)
        m_state[h] = m_new
        terms.append(dict(
            lhs=lhs, s_sum=jnp.sum(s, axis=-1, keepdims=True), w_inter=w_inter, floor=jnp.exp(-m_t),
            keep=jnp.exp(b_last + m_prev - m_new), wk=wk.astype(BF16), wk_sum=jnp.sum(wk, axis=0, keepdims=True)))

    n_state = [n_scr[h:h + 1, :] for h in range(N_HEADS)]
    c_state = [c_scr[h] for h in range(N_HEADS)]
    v_cols = lambda h: slice(2 * QK_ALL + h * V_DIM, 2 * QK_ALL + (h + 1) * V_DIM)
    for ci in range(lb // L):
        r0 = ci * L
        chunk_terms = terms[ci * N_HEADS:(ci + 1) * N_HEADS]
        q_dot_n = [jnp.sum(load_qk(r0, h)[0] * n_state[h], axis=-1, keepdims=True) for h in range(N_HEADS)]
        nums = []
        for h, t in enumerate(chunk_terms):
            rhs = jnp.concatenate([p_ref[r0:r0 + L, v_cols(h)], c_state[h].astype(BF16)], axis=0)
            nums.append(jnp.dot(t["lhs"], rhs, preferred_element_type=F32))
        for h, t in enumerate(chunk_terms):
            c_state[h] = c_state[h] * t["keep"][:, 0:1] + lax.dot_general(
                t["wk"], p_ref[r0:r0 + L, v_cols(h)], (((0,), (0,)), ((), ())), preferred_element_type=F32)
            n_state[h] = n_state[h] * t["keep"] + t["wk_sum"]
        for h, t in enumerate(chunk_terms):
            og = p_ref[r0:r0 + L, V_ALL + v_cols(h).start:V_ALL + v_cols(h).stop].astype(F32)
            den = t["s_sum"] + q_dot_n[h] * t["w_inter"]
            inv = 1.0 / jnp.maximum(jnp.abs(den), t["floor"])
            hh = nums[h] * jnp.concatenate([inv] * (V_DIM // LANES), axis=1)
            y = _rms(hh * jax.nn.sigmoid(og)) * gn_ref[:, h * V_DIM:(h + 1) * V_DIM]
            o_ref[r0:r0 + L, h * V_DIM:(h + 1) * V_DIM] = y.astype(o_ref.dtype)

    for h in range(N_HEADS):
        c_scr[h] = c_state[h]
        n_scr[h:h + 1, :] = n_state[h]
        m_scr[h:h + 1, :] = m_state[h]


def _mlstm(proj3, gates3, conv_w, conv_b, gate_bias, gn_w):
    b, s, _ = proj3.shape
    lb = CHUNK * min(MLSTM_CHUNKS_PER_STEP, s // CHUNK)
    shifts = jnp.stack([jnp.eye(lb, k=-d, dtype=BF16) for d in range(1, CONV_WIDTH)])
    return pl.pallas_call(
        _mlstm_kernel,
        grid=(b, s // lb),
        in_specs=[
            pl.BlockSpec((None, lb, MIX_COLS), lambda i, c: (i, c, 1)),
            pl.BlockSpec((None, lb, LANES), lambda i, c: (i, c, 0)),
            pl.BlockSpec((CONV_WIDTH, 2 * QK_ALL), lambda i, c: (0, 0)),
            pl.BlockSpec((1, 2 * QK_ALL), lambda i, c: (0, 0)),
            pl.BlockSpec((CONV_WIDTH - 1, lb, lb), lambda i, c: (0, 0, 0)),
            pl.BlockSpec((1, LANES), lambda i, c: (0, 0)),
            pl.BlockSpec((1, V_ALL), lambda i, c: (0, 0)),
        ],
        out_specs=pl.BlockSpec((None, lb, V_ALL), lambda i, c: (i, c, 0)),
        out_shape=jax.ShapeDtypeStruct((b, s, V_ALL), BF16),
        scratch_shapes=[
            pltpu.VMEM((N_HEADS, QK_DIM, V_DIM), F32),
            pltpu.VMEM((8, QK_DIM), F32),
            pltpu.VMEM((8, LANES), F32),
            pltpu.VMEM((8, 2 * QK_ALL), F32),
            pltpu.VMEM((lb, 2 * QK_ALL), F32),
        ],
        compiler_params=pltpu.CompilerParams(
            dimension_semantics=("arbitrary", "arbitrary"), vmem_limit_bytes=VMEM_LIMIT),
        name="mlstm",
    )(proj3, gates3, conv_w, conv_b, shifts, gate_bias, gn_w)


def _merge_route_kernel(ret_ref, hm_ref, gr_ref, gm_ref, x_ref, wr_ref, wm_ref, wo_ref, nw_ref,
                        wrt_ref, brt_ref, lower_ref, x1_ref, h2_ref, route_ref, route_t_ref, tile_ref,
                        logits_scr):
    step = pl.program_id(0)

    @pl.when(step == 0)
    def _():
        logits_scr[...] = jnp.zeros_like(logits_scr)

    logits = logits_scr[...]

    y_ret = jnp.dot(ret_ref[...], wr_ref[...], preferred_element_type=F32)
    y_m = jnp.dot(hm_ref[...], wm_ref[...], preferred_element_type=F32)
    merged = (jax.nn.sigmoid(gr_ref[...].astype(F32)) * y_ret
              + jax.nn.sigmoid(gm_ref[...].astype(F32)) * y_m)
    x1 = x_ref[...] + jnp.dot(merged.astype(BF16), wo_ref[...], preferred_element_type=F32)
    x1_ref[...] = x1
    h2 = _rms(x1) * nw_ref[...]
    h2_ref[...] = _pack_rows(h2)
    new_logits = jnp.dot(h2.astype(BF16), wrt_ref[...], preferred_element_type=F32) + brt_ref[...]

    live = jnp.where(step > 0, 1.0, 0.0)
    tm = logits.shape[0]
    lane = lax.broadcasted_iota(jnp.int32, (tm, LANES), 1)
    neg = -jnp.inf
    big = jnp.int32(LANES)
    is_group = (lane >= N_EXPERTS) & (lane < N_EXPERTS + N_GROUPS)
    gl = jnp.where(is_group, logits, neg)
    g_max = jnp.max(gl, axis=-1, keepdims=True)
    g_idx = jnp.min(jnp.where(gl == g_max, lane, big), axis=-1, keepdims=True) - N_EXPERTS
    g_w = 1.0 / jnp.sum(jnp.exp(gl - g_max), axis=-1, keepdims=True)
    in_group = (lane >= g_idx * EXPERTS_PER_GROUP) & (lane < (g_idx + 1) * EXPERTS_PER_GROUP)
    el = jnp.where(in_group, logits, neg)
    l1 = jnp.max(el, axis=-1, keepdims=True)
    e1 = jnp.min(jnp.where(el == l1, lane, big), axis=-1, keepdims=True)
    el2 = jnp.where(lane == e1, neg, el)
    l2 = jnp.max(el2, axis=-1, keepdims=True)
    e2 = jnp.min(jnp.where(el2 == l2, lane, big), axis=-1, keepdims=True)
    t21 = jnp.exp(l2 - l1)
    w1 = g_w / (1.0 + t21)
    w2 = g_w * t21 / (1.0 + t21)

    hit1 = lane == e1
    hit2 = lane == e2
    cnt = jnp.where(hit1 | hit2, live, 0.0)
    before = jnp.dot(lower_ref[...], cnt.astype(BF16), preferred_element_type=F32)
    count = jnp.sum(cnt, axis=0, keepdims=True)
    run = jnp.floor((count + (SUBLANES - 1)) * (1.0 / SUBLANES)) * SUBLANES
    lane1 = lax.broadcasted_iota(jnp.int32, (1, LANES), 1)
    run_end = run
    shift = 1
    while shift < N_EXPERTS:
        run_end = run_end + jnp.where(lane1 >= shift, pltpu.roll(run_end, shift, 1), 0.0)
        shift *= 2
    run_start = run_end - run
    local = before + run_start
    r1 = jnp.sum(jnp.where(hit1, local, 0.0), axis=-1, keepdims=True)
    r2 = jnp.sum(jnp.where(hit2, local, 0.0), axis=-1, keepdims=True)
    sub8 = lax.broadcasted_iota(jnp.int32, (SUBLANES, LANES), 0)
    tile_ref[...] = jnp.where(sub8 == 0, count, jnp.where(sub8 == 1, run, jnp.where(sub8 == 2, run_start, 0.0)))

    fields = (e1.astype(F32), e2.astype(F32), r1, r2, w1, w2)
    packed = jnp.zeros((tm, LANES), F32)
    for idx, val in enumerate(fields):
        packed = jnp.where(lane == idx, val, packed)
    route_ref[...] = packed
    route_t_ref[...] = jnp.transpose(packed)[0:ROUTE_ROWS, :]
    logits_scr[...] = new_logits


def _merge_route(ret, hm, proj, x2d, w_ret, w_m, w_out, norm_w, w_router, b_router, lower):
    t = x2d.shape[0]
    tm = min(ROWS_MERGE, t)
    n_tiles = t // tm
    gate_r_blk = 2 * MIX_COLS // D_MODEL
    tile = lambda i: jnp.minimum(i, n_tiles - 1)
    routed = lambda i: jnp.maximum(i - 1, 0)
    row_blk = lambda i: (tile(i), 0)
    const = lambda i: (0, 0)
    return pl.pallas_call(
        _merge_route_kernel,
        grid=(n_tiles + 1,),
        in_specs=[
            pl.BlockSpec((tm, V_ALL), row_blk),
            pl.BlockSpec((tm, V_ALL), row_blk),
            pl.BlockSpec((tm, D_MODEL), lambda i: (tile(i), gate_r_blk)),
            pl.BlockSpec((tm, D_MODEL), lambda i: (tile(i), gate_r_blk + 1)),
            pl.BlockSpec((tm, D_MODEL), row_blk),
            pl.BlockSpec((V_ALL, D_MODEL), const),
            pl.BlockSpec((V_ALL, D_MODEL), const),
            pl.BlockSpec((D_MODEL, D_MODEL), const),
            pl.BlockSpec((1, D_MODEL), const),
            pl.BlockSpec((D_MODEL, LANES), const),
            pl.BlockSpec((1, LANES), const),
            pl.BlockSpec((tm, tm), const),
        ],
        out_specs=[
            pl.BlockSpec((tm, D_MODEL), row_blk),
            pl.BlockSpec((tm, PACKED), row_blk),
            pl.BlockSpec((tm, LANES), lambda i: (routed(i), 0)),
            pl.BlockSpec((ROUTE_ROWS, tm), lambda i: (0, routed(i))),
            pl.BlockSpec((None, SUBLANES, LANES), lambda i: (routed(i), 0, 0)),
        ],
        out_shape=[
            jax.ShapeDtypeStruct((t, D_MODEL), F32),
            jax.ShapeDtypeStruct((t, PACKED), U32),
            jax.ShapeDtypeStruct((t, LANES), F32),
            jax.ShapeDtypeStruct((ROUTE_ROWS, t), F32),
            jax.ShapeDtypeStruct((n_tiles, SUBLANES, LANES), F32),
        ],
        scratch_shapes=[pltpu.VMEM((tm, LANES), F32)],
        compiler_params=pltpu.CompilerParams(
            dimension_semantics=("arbitrary",), vmem_limit_bytes=VMEM_LIMIT),
        name="merge_route",
    )(ret, hm, proj, proj, x2d, w_ret, w_m, w_out, norm_w, w_router, b_router, lower)


def _dispatch_kernel(fill_ref, n_groups_ref, gdst_ref, route_t_ref, h2_ref, xs_hbm, sorted_scr, zero_scr, sems):
    step = pl.program_id(0)
    n_tiles = pl.num_programs(0) - 1
    tm = h2_ref.shape[0]
    slot = step % 2

    def group_copy(which, j):
        return pltpu.make_async_copy(sorted_scr.at[which, pl.ds(j, 1)], xs_hbm.at[pl.ds(gdst_ref[j], 1)],
                                     sems.at[which])

    def wait_groups(which, count):
        def wait(j, carry):
            group_copy(which, 0).wait()
            return carry
        lax.fori_loop(0, count, wait, 0)

    @pl.when(step == 0)
    def _():
        zero_scr[...] = jnp.zeros_like(zero_scr)

        def fill_copy(j):
            g0 = pl.multiple_of(fill_ref[j], MOE_ROWS // SUBLANES)
            return pltpu.make_async_copy(zero_scr, xs_hbm.at[pl.ds(g0, MOE_ROWS // SUBLANES)], sems.at[0])

        def start_fill(j, carry):
            @pl.when(fill_ref[j] >= 0)
            def _():
                fill_copy(j).start()
            return carry

        def wait_fill(j, carry):
            @pl.when(fill_ref[j] >= 0)
            def _():
                fill_copy(j).wait()
            return carry

        lax.fori_loop(0, fill_ref.shape[0], start_fill, 0)
        lax.fori_loop(0, fill_ref.shape[0], wait_fill, 0)

    @pl.when(step >= 1)
    def _():
        def start(j, carry):
            group_copy(1 - slot, j).start()
            return carry
        lax.fori_loop(0, n_groups_ref[jnp.maximum(step - 1, 0)], start, 0)

    @pl.when(step >= 2)
    def _():
        wait_groups(slot, n_groups_ref[jnp.maximum(step - 2, 0)])

    @pl.when(step < n_tiles)
    def _():
        pos = lax.broadcasted_iota(jnp.int32, (LOCAL_ROWS, tm), 0)
        row1 = route_t_ref[2:3, :].astype(jnp.int32)
        row2 = route_t_ref[3:4, :].astype(jnp.int32)
        pick = jnp.where((pos == row1) | (pos == row2), 1.0, 0.0).astype(BF16)
        tokens = _unpack_rows(h2_ref[...]).astype(BF16)
        ordered = jnp.dot(pick, tokens, preferred_element_type=F32)
        sorted_scr[slot] = _pack_rows(ordered).reshape(LOCAL_GROUPS, SUBLANES, PACKED)

    @pl.when(step == n_tiles)
    def _():
        wait_groups(1 - slot, n_groups_ref[jnp.maximum(step - 1, 0)])


def _dispatch(fill_groups, n_groups, gdst, route_t, h2p, n_slots):
    t = h2p.shape[0]
    tm = min(ROWS_MERGE, t)
    n_tiles = t // tm
    sorted_tile = lambda i: jnp.minimum(i, n_tiles - 1)
    return pl.pallas_call(
        _dispatch_kernel,
        grid=(n_tiles + 1,),
        in_specs=[
            pl.BlockSpec(memory_space=pltpu.SMEM),
            pl.BlockSpec(memory_space=pltpu.SMEM),
            pl.BlockSpec((GROUP_TABLE,), lambda i: (jnp.maximum(i - 1, 0),), memory_space=pltpu.SMEM),
            pl.BlockSpec((ROUTE_ROWS, tm), lambda i: (0, sorted_tile(i))),
            pl.BlockSpec((tm, PACKED), lambda i: (sorted_tile(i), 0)),
        ],
        out_specs=pl.BlockSpec(memory_space=pl.ANY),
        out_shape=jax.ShapeDtypeStruct((n_slots // SUBLANES, SUBLANES, PACKED), U32),
        scratch_shapes=[
            pltpu.VMEM((2, LOCAL_GROUPS, SUBLANES, PACKED), U32),
            pltpu.VMEM((MOE_ROWS // SUBLANES, SUBLANES, PACKED), U32),
            pltpu.SemaphoreType.DMA((2,)),
        ],
        compiler_params=pltpu.CompilerParams(
            dimension_semantics=("arbitrary",), vmem_limit_bytes=VMEM_LIMIT),
        name="dispatch",
    )(fill_groups, n_groups, gdst, route_t, h2p)


def _experts_kernel(blk_e_ref, blk_valid_ref, xs_ref, wg_ref, wu_ref, wd_ref, ys_ref, wgu_scr, wd_scr):
    i = pl.program_id(0)
    valid = blk_valid_ref[i]

    @pl.when((i == 0) | (blk_e_ref[i] != blk_e_ref[jnp.maximum(i - 1, 0)]))
    def _():
        wgu_scr[:, :D_EXPERT] = wg_ref[...].astype(BF16)
        wgu_scr[:, D_EXPERT:] = wu_ref[...].astype(BF16)
        wd_scr[...] = wd_ref[...].astype(BF16)

    @pl.when(valid > 0)
    def _():
        halves = [pl.ds(h * (MOE_ROWS // 2), MOE_ROWS // 2) for h in range(2)]
        gus = [jnp.dot(_unpack_rows(xs_ref[rows, :]).astype(BF16), wgu_scr[...], preferred_element_type=F32)
               for rows in halves]
        acts = [(gu[:, :D_EXPERT] * jax.nn.sigmoid(gu[:, :D_EXPERT]) * gu[:, D_EXPERT:]).astype(BF16) for gu in gus]
        outs = [jnp.dot(act, wd_scr[...], preferred_element_type=F32) for act in acts]
        for rows, out in zip(halves, outs):
            ys_ref[rows, :] = _pack_rows(out)

    @pl.when(valid <= 0)
    def _():
        ys_ref[...] = jnp.zeros_like(ys_ref)


def _experts(blk_e, blk_valid, xs, w_gate, w_up, w_down):
    p = xs.shape[0]
    per_expert = lambda i, be, bv: (be[i], 0, 0)
    grid_spec = pltpu.PrefetchScalarGridSpec(
        num_scalar_prefetch=2,
        grid=(p // MOE_ROWS,),
        in_specs=[
            pl.BlockSpec((MOE_ROWS, PACKED), lambda i, be, bv: (i, 0)),
            pl.BlockSpec((None, D_MODEL, D_EXPERT), per_expert),
            pl.BlockSpec((None, D_MODEL, D_EXPERT), per_expert),
            pl.BlockSpec((None, D_EXPERT, D_MODEL), per_expert),
        ],
        out_specs=pl.BlockSpec((MOE_ROWS, PACKED), lambda i, be, bv: (i, 0)),
        scratch_shapes=[pltpu.VMEM((D_MODEL, 2 * D_EXPERT), BF16), pltpu.VMEM((D_EXPERT, D_MODEL), BF16)],
    )
    return pl.pallas_call(
        _experts_kernel,
        grid_spec=grid_spec,
        out_shape=jax.ShapeDtypeStruct((p, PACKED), U32),
        compiler_params=pltpu.CompilerParams(
            dimension_semantics=("arbitrary",), vmem_limit_bytes=VMEM_LIMIT),
        name="experts",
    )(blk_e, blk_valid, xs, w_gate, w_up, w_down)


def _combine_kernel(gdst_ref, route_ref, x1_ref, nw_ref, ys_hbm, o_ref, buf, sems):
    step = pl.program_id(0)
    n_tiles = pl.num_programs(0) - 1
    tm = x1_ref.shape[0]
    slot = step % 2
    prev = 1 - slot

    @pl.when(step < n_tiles)
    def _():
        def start(j, carry):
            pltpu.make_async_copy(ys_hbm.at[pl.ds(gdst_ref[j], 1)], buf.at[slot, pl.ds(j, 1)], sems.at[slot]).start()
            return carry
        lax.fori_loop(0, LOCAL_GROUPS, start, 0, unroll=8)

    @pl.when(step > 0)
    def _():
        pltpu.make_async_copy(ys_hbm.at[pl.ds(0, LOCAL_GROUPS)], buf.at[prev], sems.at[prev]).wait()

        rows = _unpack_rows(buf[prev].reshape(LOCAL_ROWS, PACKED)).astype(BF16)
        pos = lax.broadcasted_iota(jnp.int32, (tm, LOCAL_ROWS), 1)
        row1 = route_ref[:, 2:3].astype(jnp.int32)
        row2 = route_ref[:, 3:4].astype(jnp.int32)
        mix = (jnp.where(pos == row1, route_ref[:, 4:5], 0.0)
               + jnp.where(pos == row2, route_ref[:, 5:6], 0.0)).astype(BF16)
        x2 = x1_ref[...] + jnp.dot(mix, rows, preferred_element_type=F32)
        o_ref[...] = _rms(x2) * nw_ref[...]


def _combine(gdst, route, x1, norm_w, ys3):
    t = x1.shape[0]
    tm = min(ROWS_MERGE, t)
    n_tiles = t // tm
    gathered = lambda i: (jnp.minimum(i, n_tiles - 1),)
    finished = lambda i: (jnp.maximum(i - 1, 0), 0)
    return pl.pallas_call(
        _combine_kernel,
        grid=(n_tiles + 1,),
        in_specs=[
            pl.BlockSpec((GROUP_TABLE,), gathered, memory_space=pltpu.SMEM),
            pl.BlockSpec((tm, LANES), finished),
            pl.BlockSpec((tm, D_MODEL), finished),
            pl.BlockSpec((1, D_MODEL), lambda i: (0, 0)),
            pl.BlockSpec(memory_space=pl.ANY),
        ],
        out_specs=pl.BlockSpec((tm, D_MODEL), finished),
        out_shape=jax.ShapeDtypeStruct((t, D_MODEL), F32),
        scratch_shapes=[
            pltpu.VMEM((2, LOCAL_GROUPS, SUBLANES, PACKED), U32),
            pltpu.SemaphoreType.DMA((2,)),
        ],
        compiler_params=pltpu.CompilerParams(
            dimension_semantics=("arbitrary",), vmem_limit_bytes=VMEM_LIMIT),
        name="combine",
    )(gdst, route, x1, norm_w, ys3)


def _rotary_tables(seq):
    inv_freq = 1.0 / (ROPE_BASE ** (jnp.arange(0, QK_DIM, 2, dtype=F32) / QK_DIM))
    ang = jnp.arange(seq, dtype=F32)[:, None] * inv_freq[None, :]
    cos, sin = jnp.cos(ang), jnp.sin(ang)
    return jnp.concatenate([cos, cos], axis=1), jnp.concatenate([-sin, sin], axis=1)


def _retention_decay_tables():
    gamma = 1.0 - 2.0 ** (-5.0 - np.arange(N_HEADS, dtype=np.float64))
    idx = np.arange(CHUNK, dtype=np.float64) + 1.0
    dq = gamma[:, None] ** idx[None, :]
    dk = gamma[:, None] ** (-idx[None, :]) * QK_DIM ** -0.5
    bcast = lambda a: jnp.asarray(np.broadcast_to(a[:, :, None], (N_HEADS, CHUNK, QK_DIM)), F32)
    return bcast(dq), bcast(dk), tuple(float(g) for g in gamma ** CHUNK)


def kernel(x, norm_mix_w, w_in, ret_gn_w, w_ret_branch, mlstm_conv_w, mlstm_conv_b, b_igate, b_fgate,
           mlstm_gn_w, w_mlstm_branch, w_out, norm_ffn_w, w_group, b_group, w_expert_router,
           b_expert_router, w_gate, w_up, w_down, norm_final_w):
    assert norm_mix_w.shape[0] == 1, "one layer"
    b, s, d = x.shape
    t = b * s
    assert d == D_MODEL and s % CHUNK == 0

    wi = w_in[0]
    n_pre = 2 * MIX_COLS
    w_big = jnp.concatenate([wi[:, :n_pre], wi[:, n_pre + 2 * N_HEADS:]], axis=1).astype(BF16)
    w_if = jnp.pad(wi[:, n_pre:n_pre + 2 * N_HEADS], ((0, 0), (0, LANES - 2 * N_HEADS))).astype(BF16)
    gate_bias = jnp.pad(jnp.concatenate([b_igate[0], b_fgate[0]]), (0, LANES - 2 * N_HEADS))[None, :]
    w_router = jnp.pad(jnp.concatenate([w_expert_router[0], w_group[0]], axis=1),
                       ((0, 0), (0, LANES - N_EXPERTS - N_GROUPS))).astype(BF16)
    b_router = jnp.pad(jnp.concatenate([b_expert_router[0], b_group[0]]),
                       (0, LANES - N_EXPERTS - N_GROUPS))[None, :]

    cosf, sinf = _rotary_tables(s)
    dq, dk, chunk_decay = _retention_decay_tables()
    tm_merge = min(ROWS_MERGE, t)
    lower = jnp.tril(jnp.ones((tm_merge, tm_merge), F32), -1).astype(BF16)

    x2d = x.reshape(t, d)
    proj, gates = _in_projection(x2d, norm_mix_w, w_big, w_if)
    proj3 = proj.reshape(b, s, N_BIG)
    ret = _retention(proj3, cosf, sinf, dq, dk, ret_gn_w, chunk_decay)
    hm = _mlstm(proj3, gates.reshape(b, s, LANES), mlstm_conv_w[0, :, 0, :], mlstm_conv_b,
                gate_bias, mlstm_gn_w)
    x1, h2p, route, route_t, tiles = _merge_route(
        ret.reshape(t, V_ALL), hm.reshape(t, V_ALL), proj, x2d, w_ret_branch[0].astype(BF16),
        w_mlstm_branch[0].astype(BF16), w_out[0].astype(BF16), norm_ffn_w, w_router, b_router, lower)

    n_tiles = t // tm_merge
    n_slots = 2 * t + N_EXPERTS * (MOE_ROWS + n_tiles * (SUBLANES - 1) // SUBLANES * SUBLANES)
    n_slots = -(-n_slots // MOE_ROWS) * MOE_ROWS
    run = tiles[:, 1, :N_EXPERTS].astype(jnp.int32)
    run_start = tiles[:, 2, :N_EXPERTS].astype(jnp.int32)
    rows_e = jnp.sum(run, axis=0)
    padded = (rows_e + MOE_ROWS - 1) // MOE_ROWS * MOE_ROWS
    expert_ids = jnp.arange(N_EXPERTS, dtype=jnp.int32)
    pstart = jnp.sum(jnp.where(expert_ids[None, :] < expert_ids[:, None], padded[None, :], 0), axis=1)
    pend = pstart + padded
    tile_ids = jnp.arange(n_tiles, dtype=jnp.int32)
    run_t = run.T
    earlier = jnp.sum(jnp.where((tile_ids[None, :] < tile_ids[:, None])[None], run_t[:, None, :], 0), axis=2).T
    global_start = pstart[None, :] + earlier
    local_row = jnp.arange(GROUP_TABLE, dtype=jnp.int32)[None, None, :] * SUBLANES
    in_run = (local_row >= run_start[:, :, None]) & (local_row < (run_start + run)[:, :, None])
    gdst = jnp.sum(jnp.where(in_run, (global_start - run_start)[:, :, None] + local_row, 0), axis=1) // SUBLANES
    gdst = gdst.reshape(n_tiles * GROUP_TABLE).astype(jnp.int32)
    n_groups = (jnp.sum(run, axis=1) // SUBLANES).astype(jnp.int32)
    blk_start = jnp.arange(n_slots // MOE_ROWS, dtype=jnp.int32) * MOE_ROWS
    blk_e = jnp.minimum(jnp.sum(blk_start[:, None] >= pend[None, :], axis=-1), N_EXPERTS - 1).astype(jnp.int32)
    in_expert = (blk_start[:, None] >= pstart[None, :]) & (blk_start[:, None] < pend[None, :])
    rows_end = jnp.sum(jnp.where(in_expert, (pstart + rows_e)[None, :], 0), axis=1)
    blk_valid = jnp.clip(rows_end - blk_start, 0, MOE_ROWS).astype(jnp.int32)
    tail = pend[-1] + jnp.arange((n_slots - 2 * t) // MOE_ROWS, dtype=jnp.int32) * MOE_ROWS
    fill_groups = (jnp.concatenate([jnp.where(padded > 0, pend - MOE_ROWS, -SUBLANES),
                                    jnp.where(tail < n_slots, tail, -SUBLANES)]) // SUBLANES).astype(jnp.int32)

    xs3 = _dispatch(fill_groups, n_groups, gdst, route_t, h2p, n_slots)
    ys = _experts(blk_e, blk_valid, xs3.reshape(n_slots, PACKED), w_gate[0], w_up[0], w_down[0])
    out = _combine(gdst, route, x1, norm_final_w[None, :],
                   ys.reshape(n_slots // SUBLANES, SUBLANES, PACKED))
    return out.reshape(b, s, d)
```

```python
import functools

import numpy as np
import jax
import jax.numpy as jnp
from jax import lax
from jax.experimental import pallas as pl
from jax.experimental.pallas import tpu as pltpu

F32 = jnp.float32
BF16 = jnp.bfloat16
U32 = jnp.uint32

D_MODEL = 1024
N_HEADS = 4
QK_DIM = 128
V_DIM = 256
CHUNK = 128
CONV_WIDTH = 4
ROPE_BASE = 10000.0
N_GROUPS = 4
EXPERTS_PER_GROUP = 8
N_EXPERTS = N_GROUPS * EXPERTS_PER_GROUP
D_EXPERT = 512
NORM_EPS = 1e-6
QK_ALL = N_HEADS * QK_DIM
V_ALL = N_HEADS * V_DIM

MIX_COLS = 2 * QK_ALL + 2 * V_ALL
N_BIG = 2 * MIX_COLS + 2 * D_MODEL
LANES = 128
PACKED = D_MODEL // 2

ROWS_PROJ = 2048
COLS_PROJ = 2048
NORM_ROWS = 512
RET_CHUNKS_PER_STEP = 8
MLSTM_CHUNKS_PER_STEP = 4
ROWS_MERGE = 512
ROUTE_ROWS = 8
SUBLANES = 8
LOCAL_ROWS = 2 * ROWS_MERGE + 256
assert LOCAL_ROWS >= 2 * ROWS_MERGE + N_EXPERTS * (SUBLANES - 1) and LOCAL_ROWS % LANES == 0
LOCAL_GROUPS = LOCAL_ROWS // SUBLANES
GROUP_TABLE = 256
assert GROUP_TABLE >= LOCAL_GROUPS
MOE_ROWS = 512
V7X_VMEM_BYTES = 64 * 1024 * 1024
VMEM_LIMIT = V7X_VMEM_BYTES - 8 * 1024 * 1024
VMEM_LIMIT_PROJ = V7X_VMEM_BYTES - 4 * 1024 * 1024


def _rms(x, eps=NORM_EPS):
    return x * lax.rsqrt(jnp.mean(x * x, axis=-1, keepdims=True) + eps)


def _pack_rows(x):
    lo = lax.bitcast_convert_type(x[:, :PACKED].astype(BF16).astype(F32), U32)
    hi = lax.bitcast_convert_type(x[:, PACKED:].astype(BF16).astype(F32), U32)
    return (hi & jnp.uint32(0xFFFF0000)) | (lo >> 16)


def _unpack_rows(w):
    lo = lax.bitcast_convert_type(w << 16, F32)
    hi = lax.bitcast_convert_type(w & jnp.uint32(0xFFFF0000), F32)
    return jnp.concatenate([lo, hi], axis=1)


def _in_proj_kernel(x_ref, nw_ref, w_ref, wif_ref, o_ref, gates_ref, h_scr):
    @pl.when(pl.program_id(1) == 0)
    def _():
        for r0 in range(0, x_ref.shape[0], NORM_ROWS):
            rows = slice(r0, r0 + NORM_ROWS)
            h = (_rms(x_ref[rows, :]) * nw_ref[...]).astype(BF16)
            h_scr[rows, :] = h
            gates_ref[rows, :] = jnp.dot(h, wif_ref[...], preferred_element_type=F32)
            o_ref[rows, :] = jnp.dot(h, w_ref[...], preferred_element_type=F32).astype(o_ref.dtype)

    @pl.when(pl.program_id(1) != 0)
    def _():
        o_ref[...] = jnp.dot(h_scr[...], w_ref[...], preferred_element_type=F32).astype(o_ref.dtype)


def _in_projection(x2d, norm_w, w_big, w_if):
    t = x2d.shape[0]
    tm = min(ROWS_PROJ, t)
    tn = COLS_PROJ
    return pl.pallas_call(
        _in_proj_kernel,
        grid=(t // tm, N_BIG // tn),
        in_specs=[
            pl.BlockSpec((tm, D_MODEL), lambda i, j: (i, 0)),
            pl.BlockSpec((1, D_MODEL), lambda i, j: (0, 0)),
            pl.BlockSpec((D_MODEL, tn), lambda i, j: (0, j)),
            pl.BlockSpec((D_MODEL, LANES), lambda i, j: (0, 0)),
        ],
        out_specs=[
            pl.BlockSpec((tm, tn), lambda i, j: (i, j)),
            pl.BlockSpec((tm, LANES), lambda i, j: (i, 0)),
        ],
        out_shape=[
            jax.ShapeDtypeStruct((t, N_BIG), BF16),
            jax.ShapeDtypeStruct((t, LANES), F32),
        ],
        scratch_shapes=[pltpu.VMEM((tm, D_MODEL), BF16)],
        compiler_params=pltpu.CompilerParams(
            dimension_semantics=("arbitrary", "arbitrary"), vmem_limit_bytes=VMEM_LIMIT_PROJ),
        name="in_projection",
    )(x2d, norm_w, w_big, w_if)


def _retention_kernel(p_ref, cos_ref, sin_ref, dq_ref, dk_ref, gn_ref, o_ref, state_scr, *, chunk_decay):
    L = CHUNK

    @pl.when(pl.program_id(1) == 0)
    def _():
        state_scr[...] = jnp.zeros_like(state_scr)

    row = lax.broadcasted_iota(jnp.int32, (L, L), 0)
    col = lax.broadcasted_iota(jnp.int32, (L, L), 1)
    causal = row >= col
    n_chunks = p_ref.shape[0] // L
    units = [(ci * L, h) for ci in range(n_chunks) for h in range(N_HEADS)]

    qts, kts, scores = [], [], []
    for r0, h in units:
        cosf = cos_ref[r0:r0 + L, :]
        sinf = sin_ref[r0:r0 + L, :]
        q = p_ref[r0:r0 + L, h * QK_DIM:(h + 1) * QK_DIM].astype(F32)
        k = p_ref[r0:r0 + L, QK_ALL + h * QK_DIM:QK_ALL + (h + 1) * QK_DIM].astype(F32)
        qt = ((q * cosf + pltpu.roll(q, QK_DIM // 2, 1) * sinf) * dq_ref[h]).astype(BF16)
        kt = ((k * cosf + pltpu.roll(k, QK_DIM // 2, 1) * sinf) * dk_ref[h]).astype(BF16)
        s = lax.dot_general(qt, kt, (((1,), (1,)), ((), ())), preferred_element_type=F32)
        qts.append(qt)
        kts.append(kt)
        scores.append(jnp.where(causal, s, 0.0).astype(BF16))

    states = [state_scr[h] for h in range(N_HEADS)]
    for (r0, h), qt, kt, s in zip(units, qts, kts, scores):
        v = p_ref[r0:r0 + L, 2 * QK_ALL + h * V_DIM:2 * QK_ALL + (h + 1) * V_DIM]
        g = p_ref[r0:r0 + L, 2 * QK_ALL + V_ALL + h * V_DIM:2 * QK_ALL + V_ALL + (h + 1) * V_DIM].astype(F32)
        lhs = jnp.concatenate([s, qt], axis=1)
        rhs = jnp.concatenate([v, states[h].astype(BF16)], axis=0)
        o = jnp.dot(lhs, rhs, preferred_element_type=F32)
        kv = lax.dot_general(kt, v, (((0,), (0,)), ((), ())), preferred_element_type=F32)
        states[h] = (states[h] + kv) * chunk_decay[h]
        y = _rms(o) * gn_ref[:, h * V_DIM:(h + 1) * V_DIM] * (g * jax.nn.sigmoid(g))
        o_ref[r0:r0 + L, h * V_DIM:(h + 1) * V_DIM] = y.astype(o_ref.dtype)
    for h in range(N_HEADS):
        state_scr[h] = states[h]


def _retention(proj3, cosf, sinf, dq, dk, gn_w, chunk_decay):
    b, s, _ = proj3.shape
    lb = CHUNK * min(RET_CHUNKS_PER_STEP, s // CHUNK)
    return pl.pallas_call(
        functools.partial(_retention_kernel, chunk_decay=chunk_decay),
        grid=(b, s // lb),
        in_specs=[
            pl.BlockSpec((None, lb, MIX_COLS), lambda i, c: (i, c, 0)),
            pl.BlockSpec((lb, QK_DIM), lambda i, c: (c, 0)),
            pl.BlockSpec((lb, QK_DIM), lambda i, c: (c, 0)),
            pl.BlockSpec((N_HEADS, CHUNK, QK_DIM), lambda i, c: (0, 0, 0)),
            pl.BlockSpec((N_HEADS, CHUNK, QK_DIM), lambda i, c: (0, 0, 0)),
            pl.BlockSpec((1, V_ALL), lambda i, c: (0, 0)),
        ],
        out_specs=pl.BlockSpec((None, lb, V_ALL), lambda i, c: (i, c, 0)),
        out_shape=jax.ShapeDtypeStruct((b, s, V_ALL), BF16),
        scratch_shapes=[pltpu.VMEM((N_HEADS, QK_DIM, V_DIM), F32)],
        compiler_params=pltpu.CompilerParams(
            dimension_semantics=("arbitrary", "arbitrary"), vmem_limit_bytes=VMEM_LIMIT),
        name="retention",
    )(proj3, cosf, sinf, dq, dk, gn_w)


def _mlstm_kernel(p_ref, gates_ref, cw_ref, cb_ref, shift_ref, gb_ref, gn_ref, o_ref,
                  c_scr, n_scr, m_scr, tail_scr, act_scr):
    L = CHUNK
    lb = p_ref.shape[0]

    @pl.when(pl.program_id(1) == 0)
    def _():
        c_scr[...] = jnp.zeros_like(c_scr)
        n_scr[...] = jnp.zeros_like(n_scr)
        m_scr[...] = jnp.zeros_like(m_scr)
        tail_scr[...] = jnp.zeros_like(tail_scr)

    ub = p_ref[:, 0:2 * QK_ALL]
    u = ub.astype(F32)
    tail = tail_scr[...]
    row8 = lax.broadcasted_iota(jnp.int32, tail.shape, 0)
    acc = u * cw_ref[CONV_WIDTH - 1:CONV_WIDTH, :] + cb_ref[...]
    head = jnp.zeros_like(tail)
    for d in range(1, CONV_WIDTH):
        w_d = cw_ref[CONV_WIDTH - 1 - d:CONV_WIDTH - d, :]
        acc = acc + jnp.dot(shift_ref[d - 1], ub, preferred_element_type=F32) * w_d
        head = head + jnp.where(row8 < d, pltpu.roll(tail, d, 0), 0.0) * w_d
    acc = jnp.concatenate([acc[0:SUBLANES, :] + head, acc[SUBLANES:, :]], axis=0)
    tail_scr[...] = u[lb - SUBLANES:lb, :]
    act_scr[...] = acc * jax.nn.sigmoid(acc)

    row = lax.broadcasted_iota(jnp.int32, (L, L), 0)
    col = lax.broadcasted_iota(jnp.int32, (L, L), 1)
    causal = row >= col
    k_scale = QK_DIM ** -0.5
    units = [(ci * L, h) for ci in range(lb // L) for h in range(N_HEADS)]

    def load_qk(r0, h):
        q = act_scr[r0:r0 + L, h * QK_DIM:(h + 1) * QK_DIM]
        k = act_scr[r0:r0 + L, QK_ALL + h * QK_DIM:QK_ALL + (h + 1) * QK_DIM] * k_scale
        return q, k

    gate_terms, src_rows = [], []
    lane_t = lax.broadcasted_iota(jnp.int32, (2 * N_HEADS, L), 1)
    for ci in range(lb // L):
        pre = gates_ref[ci * L:(ci + 1) * L, :] + gb_ref[...]
        pre_rows = jnp.transpose(pre)[0:2 * N_HEADS, :]
        b_rows = jnp.minimum(pre_rows, 0.0) - jnp.log1p(jnp.exp(-jnp.abs(pre_rows)))
        shift = 1
        while shift < L:
            b_rows = b_rows + jnp.where(lane_t >= shift, pltpu.roll(b_rows, shift, 1), 0.0)
            shift *= 2
        src_rows.append(pre_rows[0:N_HEADS, :] - b_rows[N_HEADS:2 * N_HEADS, :])
        bcum = jnp.transpose(jnp.concatenate([b_rows, jnp.zeros((L - 2 * N_HEADS, L), F32)], axis=0))
        gate_terms.append((pre, bcum))
    scores = []
    for r0, h in units:
        q, k = load_qk(r0, h)
        scores.append(lax.dot_general(q.astype(BF16), k.astype(BF16), (((1,), (1,)), ((), ())),
                                      preferred_element_type=F32))

    m_state = [m_scr[h:h + 1, :] for h in range(N_HEADS)]
    terms = []
    for (r0, h), qk in zip(units, scores):
        pre, bcum = gate_terms[r0 // L]
        q, k = load_qk(r0, h)
        b_t = jnp.broadcast_to(bcum[:, N_HEADS + h:N_HEADS + h + 1], (L, L))
        i_t = jnp.broadcast_to(pre[:, h:h + 1], (L, L))
        src = jnp.broadcast_to(src_rows[r0 // L][h:h + 1, :], (L, L))
        m_prev = m_state[h]
        a = b_t + m_prev
        dmat = jnp.where(causal, b_t + src, -jnp.inf)
        m_t = jnp.maximum(a, jnp.max(dmat, axis=-1, keepdims=True))
        w_inter = jnp.exp(a - m_t)
        s = qk * jnp.exp(dmat - m_t)
        lhs = jnp.concatenate([s.astype(BF16), (q * w_inter).astype(BF16)], axis=1)
        b_last = b_t[L - 1:L, :]
        gk = b_last - b_t + i_t
        m_new = jnp.maximum(b_last + m_prev, jnp.max(gk, axis=0, keepdims=True))
        wk = k * jnp.exp(gk - m_new)
        m_state[h] = m_new
        terms.append(dict(
            lhs=lhs, s_sum=jnp.sum(s, axis=-1, keepdims=True), w_inter=w_inter, floor=jnp.exp(-m_t),
            keep=jnp.exp(b_last + m_prev - m_new), wk=wk.astype(BF16), wk_sum=jnp.sum(wk, axis=0, keepdims=True)))

    n_state = [n_scr[h:h + 1, :] for h in range(N_HEADS)]
    c_state = [c_scr[h] for h in range(N_HEADS)]
    v_cols = lambda h: slice(2 * QK_ALL + h * V_DIM, 2 * QK_ALL + (h + 1) * V_DIM)
    for ci in range(lb // L):
        r0 = ci * L
        chunk_terms = terms[ci * N_HEADS:(ci + 1) * N_HEADS]
        q_dot_n = [jnp.sum(load_qk(r0, h)[0] * n_state[h], axis=-1, keepdims=True) for h in range(N_HEADS)]
        nums = []
        for h, t in enumerate(chunk_terms):
            rhs = jnp.concatenate([p_ref[r0:r0 + L, v_cols(h)], c_state[h].astype(BF16)], axis=0)
            nums.append(jnp.dot(t["lhs"], rhs, preferred_element_type=F32))
        for h, t in enumerate(chunk_terms):
            c_state[h] = c_state[h] * t["keep"][:, 0:1] + lax.dot_general(
                t["wk"], p_ref[r0:r0 + L, v_cols(h)], (((0,), (0,)), ((), ())), preferred_element_type=F32)
            n_state[h] = n_state[h] * t["keep"] + t["wk_sum"]
        for h, t in enumerate(chunk_terms):
            og = p_ref[r0:r0 + L, V_ALL + v_cols(h).start:V_ALL + v_cols(h).stop].astype(F32)
            den = t["s_sum"] + q_dot_n[h] * t["w_inter"]
            inv = 1.0 / jnp.maximum(jnp.abs(den), t["floor"])
            hh = nums[h] * jnp.concatenate([inv] * (V_DIM // LANES), axis=1)
            y = _rms(hh * jax.nn.sigmoid(og)) * gn_ref[:, h * V_DIM:(h + 1) * V_DIM]
            o_ref[r0:r0 + L, h * V_DIM:(h + 1) * V_DIM] = y.astype(o_ref.dtype)

    for h in range(N_HEADS):
        c_scr[h] = c_state[h]
        n_scr[h:h + 1, :] = n_state[h]
        m_scr[h:h + 1, :] = m_state[h]


def _mlstm(proj3, gates3, conv_w, conv_b, gate_bias, gn_w):
    b, s, _ = proj3.shape
    lb = CHUNK * min(MLSTM_CHUNKS_PER_STEP, s // CHUNK)
    shifts = jnp.stack([jnp.eye(lb, k=-d, dtype=BF16) for d in range(1, CONV_WIDTH)])
    return pl.pallas_call(
        _mlstm_kernel,
        grid=(b, s // lb),
        in_specs=[
            pl.BlockSpec((None, lb, MIX_COLS), lambda i, c: (i, c, 1)),
            pl.BlockSpec((None, lb, LANES), lambda i, c: (i, c, 0)),
            pl.BlockSpec((CONV_WIDTH, 2 * QK_ALL), lambda i, c: (0, 0)),
            pl.BlockSpec((1, 2 * QK_ALL), lambda i, c: (0, 0)),
            pl.BlockSpec((CONV_WIDTH - 1, lb, lb), lambda i, c: (0, 0, 0)),
            pl.BlockSpec((1, LANES), lambda i, c: (0, 0)),
            pl.BlockSpec((1, V_ALL), lambda i, c: (0, 0)),
        ],
        out_specs=pl.BlockSpec((None, lb, V_ALL), lambda i, c: (i, c, 0)),
        out_shape=jax.ShapeDtypeStruct((b, s, V_ALL), BF16),
        scratch_shapes=[
            pltpu.VMEM((N_HEADS, QK_DIM, V_DIM), F32),
            pltpu.VMEM((8, QK_DIM), F32),
            pltpu.VMEM((8, LANES), F32),
            pltpu.VMEM((8, 2 * QK_ALL), F32),
            pltpu.VMEM((lb, 2 * QK_ALL), F32),
        ],
        compiler_params=pltpu.CompilerParams(
            dimension_semantics=("arbitrary", "arbitrary"), vmem_limit_bytes=VMEM_LIMIT),
        name="mlstm",
    )(proj3, gates3, conv_w, conv_b, shifts, gate_bias, gn_w)


def _merge_route_kernel(ret_ref, hm_ref, gr_ref, gm_ref, x_ref, wr_ref, wm_ref, wo_ref, nw_ref,
                        wrt_ref, brt_ref, lower_ref, x1_ref, h2_ref, route_ref, route_t_ref, tile_ref,
                        logits_scr):
    step = pl.program_id(0)

    @pl.when(step == 0)
    def _():
        logits_scr[...] = jnp.zeros_like(logits_scr)

    logits = logits_scr[...]

    y_ret = jnp.dot(ret_ref[...], wr_ref[...], preferred_element_type=F32)
    y_m = jnp.dot(hm_ref[...], wm_ref[...], preferred_element_type=F32)
    merged = (jax.nn.sigmoid(gr_ref[...].astype(F32)) * y_ret
              + jax.nn.sigmoid(gm_ref[...].astype(F32)) * y_m)
    x1 = x_ref[...] + jnp.dot(merged.astype(BF16), wo_ref[...], preferred_element_type=F32)
    x1_ref[...] = x1
    h2 = _rms(x1) * nw_ref[...]
    h2_ref[...] = _pack_rows(h2)
    new_logits = jnp.dot(h2.astype(BF16), wrt_ref[...], preferred_element_type=F32) + brt_ref[...]

    live = jnp.where(step > 0, 1.0, 0.0)
    tm = logits.shape[0]
    lane = lax.broadcasted_iota(jnp.int32, (tm, LANES), 1)
    neg = -jnp.inf
    big = jnp.int32(LANES)
    is_group = (lane >= N_EXPERTS) & (lane < N_EXPERTS + N_GROUPS)
    gl = jnp.where(is_group, logits, neg)
    g_max = jnp.max(gl, axis=-1, keepdims=True)
    g_idx = jnp.min(jnp.where(gl == g_max, lane, big), axis=-1, keepdims=True) - N_EXPERTS
    g_w = 1.0 / jnp.sum(jnp.exp(gl - g_max), axis=-1, keepdims=True)
    in_group = (lane >= g_idx * EXPERTS_PER_GROUP) & (lane < (g_idx + 1) * EXPERTS_PER_GROUP)
    el = jnp.where(in_group, logits, neg)
    l1 = jnp.max(el, axis=-1, keepdims=True)
    e1 = jnp.min(jnp.where(el == l1, lane, big), axis=-1, keepdims=True)
    el2 = jnp.where(lane == e1, neg, el)
    l2 = jnp.max(el2, axis=-1, keepdims=True)
    e2 = jnp.min(jnp.where(el2 == l2, lane, big), axis=-1, keepdims=True)
    t21 = jnp.exp(l2 - l1)
    w1 = g_w / (1.0 + t21)
    w2 = g_w * t21 / (1.0 + t21)

    hit1 = lane == e1
    hit2 = lane == e2
    cnt = jnp.where(hit1 | hit2, live, 0.0)
    before = jnp.dot(lower_ref[...], cnt.astype(BF16), preferred_element_type=F32)
    count = jnp.sum(cnt, axis=0, keepdims=True)
    run = jnp.floor((count + (SUBLANES - 1)) * (1.0 / SUBLANES)) * SUBLANES
    lane1 = lax.broadcasted_iota(jnp.int32, (1, LANES), 1)
    run_end = run
    shift = 1
    while shift < N_EXPERTS:
        run_end = run_end + jnp.where(lane1 >= shift, pltpu.roll(run_end, shift, 1), 0.0)
        shift *= 2
    run_start = run_end - run
    local = before + run_start
    r1 = jnp.sum(jnp.where(hit1, local, 0.0), axis=-1, keepdims=True)
    r2 = jnp.sum(jnp.where(hit2, local, 0.0), axis=-1, keepdims=True)
    sub8 = lax.broadcasted_iota(jnp.int32, (SUBLANES, LANES), 0)
    tile_ref[...] = jnp.where(sub8 == 0, count, jnp.where(sub8 == 1, run, jnp.where(sub8 == 2, run_start, 0.0)))

    fields = (e1.astype(F32), e2.astype(F32), r1, r2, w1, w2)
    packed = jnp.zeros((tm, LANES), F32)
    for idx, val in enumerate(fields):
        packed = jnp.where(lane == idx, val, packed)
    route_ref[...] = packed
    route_t_ref[...] = jnp.transpose(packed)[0:ROUTE_ROWS, :]
    logits_scr[...] = new_logits


def _merge_route(ret, hm, proj, x2d, w_ret, w_m, w_out, norm_w, w_router, b_router, lower):
    t = x2d.shape[0]
    tm = min(ROWS_MERGE, t)
    n_tiles = t // tm
    gate_r_blk = 2 * MIX_COLS // D_MODEL
    tile = lambda i: jnp.minimum(i, n_tiles - 1)
    routed = lambda i: jnp.maximum(i - 1, 0)
    row_blk = lambda i: (tile(i), 0)
    const = lambda i: (0, 0)
    return pl.pallas_call(
        _merge_route_kernel,
        grid=(n_tiles + 1,),
        in_specs=[
            pl.BlockSpec((tm, V_ALL), row_blk),
            pl.BlockSpec((tm, V_ALL), row_blk),
            pl.BlockSpec((tm, D_MODEL), lambda i: (tile(i), gate_r_blk)),
            pl.BlockSpec((tm, D_MODEL), lambda i: (tile(i), gate_r_blk + 1)),
            pl.BlockSpec((tm, D_MODEL), row_blk),
            pl.BlockSpec((V_ALL, D_MODEL), const),
            pl.BlockSpec((V_ALL, D_MODEL), const),
            pl.BlockSpec((D_MODEL, D_MODEL), const),
            pl.BlockSpec((1, D_MODEL), const),
            pl.BlockSpec((D_MODEL, LANES), const),
            pl.BlockSpec((1, LANES), const),
            pl.BlockSpec((tm, tm), const),
        ],
        out_specs=[
            pl.BlockSpec((tm, D_MODEL), row_blk),
            pl.BlockSpec((tm, PACKED), row_blk),
            pl.BlockSpec((tm, LANES), lambda i: (routed(i), 0)),
            pl.BlockSpec((ROUTE_ROWS, tm), lambda i: (0, routed(i))),
            pl.BlockSpec((None, SUBLANES, LANES), lambda i: (routed(i), 0, 0)),
        ],
        out_shape=[
            jax.ShapeDtypeStruct((t, D_MODEL), F32),
            jax.ShapeDtypeStruct((t, PACKED), U32),
            jax.ShapeDtypeStruct((t, LANES), F32),
            jax.ShapeDtypeStruct((ROUTE_ROWS, t), F32),
            jax.ShapeDtypeStruct((n_tiles, SUBLANES, LANES), F32),
        ],
        scratch_shapes=[pltpu.VMEM((tm, LANES), F32)],
        compiler_params=pltpu.CompilerParams(
            dimension_semantics=("arbitrary",), vmem_limit_bytes=VMEM_LIMIT),
        name="merge_route",
    )(ret, hm, proj, proj, x2d, w_ret, w_m, w_out, norm_w, w_router, b_router, lower)


def _dispatch_kernel(fill_ref, n_groups_ref, gdst_ref, route_t_ref, h2_ref, xs_hbm, sorted_scr, zero_scr, sems):
    step = pl.program_id(0)
    n_tiles = pl.num_programs(0) - 1
    tm = h2_ref.shape[0]
    slot = step % 2

    def group_copy(which, j):
        return pltpu.make_async_copy(sorted_scr.at[which, pl.ds(j, 1)], xs_hbm.at[pl.ds(gdst_ref[j], 1)],
                                     sems.at[which])

    def wait_groups(which, count):
        def wait(j, carry):
            group_copy(which, 0).wait()
            return carry
        lax.fori_loop(0, count, wait, 0)

    @pl.when(step == 0)
    def _():
        zero_scr[...] = jnp.zeros_like(zero_scr)

        def fill_copy(j):
            g0 = pl.multiple_of(fill_ref[j], MOE_ROWS // SUBLANES)
            return pltpu.make_async_copy(zero_scr, xs_hbm.at[pl.ds(g0, MOE_ROWS // SUBLANES)], sems.at[0])

        def start_fill(j, carry):
            @pl.when(fill_ref[j] >= 0)
            def _():
                fill_copy(j).start()
            return carry

        def wait_fill(j, carry):
            @pl.when(fill_ref[j] >= 0)
            def _():
                fill_copy(j).wait()
            return carry

        lax.fori_loop(0, fill_ref.shape[0], start_fill, 0)
        lax.fori_loop(0, fill_ref.shape[0], wait_fill, 0)

    @pl.when(step >= 1)
    def _():
        def start(j, carry):
            group_copy(1 - slot, j).start()
            return carry
        lax.fori_loop(0, n_groups_ref[jnp.maximum(step - 1, 0)], start, 0)

    @pl.when(step >= 2)
    def _():
        wait_groups(slot, n_groups_ref[jnp.maximum(step - 2, 0)])

    @pl.when(step < n_tiles)
    def _():
        pos = lax.broadcasted_iota(jnp.int32, (LOCAL_ROWS, tm), 0)
        row1 = route_t_ref[2:3, :].astype(jnp.int32)
        row2 = route_t_ref[3:4, :].astype(jnp.int32)
        pick = jnp.where((pos == row1) | (pos == row2), 1.0, 0.0).astype(BF16)
        tokens = _unpack_rows(h2_ref[...]).astype(BF16)
        ordered = jnp.dot(pick, tokens, preferred_element_type=F32)
        sorted_scr[slot] = _pack_rows(ordered).reshape(LOCAL_GROUPS, SUBLANES, PACKED)

    @pl.when(step == n_tiles)
    def _():
        wait_groups(1 - slot, n_groups_ref[jnp.maximum(step - 1, 0)])


def _dispatch(fill_groups, n_groups, gdst, route_t, h2p, n_slots):
    t = h2p.shape[0]
    tm = min(ROWS_MERGE, t)
    n_tiles = t // tm
    sorted_tile = lambda i: jnp.minimum(i, n_tiles - 1)
    return pl.pallas_call(
        _dispatch_kernel,
        grid=(n_tiles + 1,),
        in_specs=[
            pl.BlockSpec(memory_space=pltpu.SMEM),
            pl.BlockSpec(memory_space=pltpu.SMEM),
            pl.BlockSpec((GROUP_TABLE,), lambda i: (jnp.maximum(i - 1, 0),), memory_space=pltpu.SMEM),
            pl.BlockSpec((ROUTE_ROWS, tm), lambda i: (0, sorted_tile(i))),
            pl.BlockSpec((tm, PACKED), lambda i: (sorted_tile(i), 0)),
        ],
        out_specs=pl.BlockSpec(memory_space=pl.ANY),
        out_shape=jax.ShapeDtypeStruct((n_slots // SUBLANES, SUBLANES, PACKED), U32),
        scratch_shapes=[
            pltpu.VMEM((2, LOCAL_GROUPS, SUBLANES, PACKED), U32),
            pltpu.VMEM((MOE_ROWS // SUBLANES, SUBLANES, PACKED), U32),
            pltpu.SemaphoreType.DMA((2,)),
        ],
        compiler_params=pltpu.CompilerParams(
            dimension_semantics=("arbitrary",), vmem_limit_bytes=VMEM_LIMIT),
        name="dispatch",
    )(fill_groups, n_groups, gdst, route_t, h2p)


def _experts_kernel(blk_e_ref, blk_valid_ref, xs_ref, wg_ref, wu_ref, wd_ref, ys_ref, wgu_scr, wd_scr):
    i = pl.program_id(0)
    valid = blk_valid_ref[i]

    @pl.when((i == 0) | (blk_e_ref[i] != blk_e_ref[jnp.maximum(i - 1, 0)]))
    def _():
        wgu_scr[:, :D_EXPERT] = wg_ref[...].astype(BF16)
        wgu_scr[:, D_EXPERT:] = wu_ref[...].astype(BF16)
        wd_scr[...] = wd_ref[...].astype(BF16)

    @pl.when(valid > 0)
    def _():
        xb = _unpack_rows(xs_ref[...]).astype(BF16)
        gu = jnp.dot(xb, wgu_scr[...], preferred_element_type=F32)
        g = gu[:, :D_EXPERT]
        act = (g * jax.nn.sigmoid(g) * gu[:, D_EXPERT:]).astype(BF16)
        ys_ref[...] = _pack_rows(jnp.dot(act, wd_scr[...], preferred_element_type=F32))

    @pl.when(valid <= 0)
    def _():
        ys_ref[...] = jnp.zeros_like(ys_ref)


def _experts(blk_e, blk_valid, xs, w_gate, w_up, w_down):
    p = xs.shape[0]
    per_expert = lambda i, be, bv: (be[i], 0, 0)
    grid_spec = pltpu.PrefetchScalarGridSpec(
        num_scalar_prefetch=2,
        grid=(p // MOE_ROWS,),
        in_specs=[
            pl.BlockSpec((MOE_ROWS, PACKED), lambda i, be, bv: (i, 0)),
            pl.BlockSpec((None, D_MODEL, D_EXPERT), per_expert),
            pl.BlockSpec((None, D_MODEL, D_EXPERT), per_expert),
            pl.BlockSpec((None, D_EXPERT, D_MODEL), per_expert),
        ],
        out_specs=pl.BlockSpec((MOE_ROWS, PACKED), lambda i, be, bv: (i, 0)),
        scratch_shapes=[pltpu.VMEM((D_MODEL, 2 * D_EXPERT), BF16), pltpu.VMEM((D_EXPERT, D_MODEL), BF16)],
    )
    return pl.pallas_call(
        _experts_kernel,
        grid_spec=grid_spec,
        out_shape=jax.ShapeDtypeStruct((p, PACKED), U32),
        compiler_params=pltpu.CompilerParams(
            dimension_semantics=("arbitrary",), vmem_limit_bytes=VMEM_LIMIT),
        name="experts",
    )(blk_e, blk_valid, xs, w_gate, w_up, w_down)


def _combine_kernel(gdst_ref, route_ref, x1_ref, nw_ref, ys_hbm, o_ref, buf, sems):
    step = pl.program_id(0)
    n_tiles = pl.num_programs(0) - 1
    tm = x1_ref.shape[0]
    slot = step % 2
    prev = 1 - slot

    @pl.when(step < n_tiles)
    def _():
        def start(j, carry):
            pltpu.make_async_copy(ys_hbm.at[pl.ds(gdst_ref[j], 1)], buf.at[slot, pl.ds(j, 1)], sems.at[slot]).start()
            return carry
        lax.fori_loop(0, LOCAL_GROUPS, start, 0, unroll=8)

    @pl.when(step > 0)
    def _():
        pltpu.make_async_copy(ys_hbm.at[pl.ds(0, LOCAL_GROUPS)], buf.at[prev], sems.at[prev]).wait()

        rows = _unpack_rows(buf[prev].reshape(LOCAL_ROWS, PACKED)).astype(BF16)
        pos = lax.broadcasted_iota(jnp.int32, (tm, LOCAL_ROWS), 1)
        row1 = route_ref[:, 2:3].astype(jnp.int32)
        row2 = route_ref[:, 3:4].astype(jnp.int32)
        mix = (jnp.where(pos == row1, route_ref[:, 4:5], 0.0)
               + jnp.where(pos == row2, route_ref[:, 5:6], 0.0)).astype(BF16)
        x2 = x1_ref[...] + jnp.dot(mix, rows, preferred_element_type=F32)
        o_ref[...] = _rms(x2) * nw_ref[...]


def _combine(gdst, route, x1, norm_w, ys3):
    t = x1.shape[0]
    tm = min(ROWS_MERGE, t)
    n_tiles = t // tm
    gathered = lambda i: (jnp.minimum(i, n_tiles - 1),)
    finished = lambda i: (jnp.maximum(i - 1, 0), 0)
    return pl.pallas_call(
        _combine_kernel,
        grid=(n_tiles + 1,),
        in_specs=[
            pl.BlockSpec((GROUP_TABLE,), gathered, memory_space=pltpu.SMEM),
            pl.BlockSpec((tm, LANES), finished),
            pl.BlockSpec((tm, D_MODEL), finished),
            pl.BlockSpec((1, D_MODEL), lambda i: (0, 0)),
            pl.BlockSpec(memory_space=pl.ANY),
        ],
        out_specs=pl.BlockSpec((tm, D_MODEL), finished),
        out_shape=jax.ShapeDtypeStruct((t, D_MODEL), F32),
        scratch_shapes=[
            pltpu.VMEM((2, LOCAL_GROUPS, SUBLANES, PACKED), U32),
            pltpu.SemaphoreType.DMA((2,)),
        ],
        compiler_params=pltpu.CompilerParams(
            dimension_semantics=("arbitrary",), vmem_limit_bytes=VMEM_LIMIT),
        name="combine",
    )(gdst, route, x1, norm_w, ys3)


def _rotary_tables(seq):
    inv_freq = 1.0 / (ROPE_BASE ** (jnp.arange(0, QK_DIM, 2, dtype=F32) / QK_DIM))
    ang = jnp.arange(seq, dtype=F32)[:, None] * inv_freq[None, :]
    cos, sin = jnp.cos(ang), jnp.sin(ang)
    return jnp.concatenate([cos, cos], axis=1), jnp.concatenate([-sin, sin], axis=1)


def _retention_decay_tables():
    gamma = 1.0 - 2.0 ** (-5.0 - np.arange(N_HEADS, dtype=np.float64))
    idx = np.arange(CHUNK, dtype=np.float64) + 1.0
    dq = gamma[:, None] ** idx[None, :]
    dk = gamma[:, None] ** (-idx[None, :]) * QK_DIM ** -0.5
    bcast = lambda a: jnp.asarray(np.broadcast_to(a[:, :, None], (N_HEADS, CHUNK, QK_DIM)), F32)
    return bcast(dq), bcast(dk), tuple(float(g) for g in gamma ** CHUNK)


def kernel(x, norm_mix_w, w_in, ret_gn_w, w_ret_branch, mlstm_conv_w, mlstm_conv_b, b_igate, b_fgate,
           mlstm_gn_w, w_mlstm_branch, w_out, norm_ffn_w, w_group, b_group, w_expert_router,
           b_expert_router, w_gate, w_up, w_down, norm_final_w):
    assert norm_mix_w.shape[0] == 1, "one layer"
    b, s, d = x.shape
    t = b * s
    assert d == D_MODEL and s % CHUNK == 0

    wi = w_in[0]
    n_pre = 2 * MIX_COLS
    w_big = jnp.concatenate([wi[:, :n_pre], wi[:, n_pre + 2 * N_HEADS:]], axis=1).astype(BF16)
    w_if = jnp.pad(wi[:, n_pre:n_pre + 2 * N_HEADS], ((0, 0), (0, LANES - 2 * N_HEADS))).astype(BF16)
    gate_bias = jnp.pad(jnp.concatenate([b_igate[0], b_fgate[0]]), (0, LANES - 2 * N_HEADS))[None, :]
    w_router = jnp.pad(jnp.concatenate([w_expert_router[0], w_group[0]], axis=1),
                       ((0, 0), (0, LANES - N_EXPERTS - N_GROUPS))).astype(BF16)
    b_router = jnp.pad(jnp.concatenate([b_expert_router[0], b_group[0]]),
                       (0, LANES - N_EXPERTS - N_GROUPS))[None, :]

    cosf, sinf = _rotary_tables(s)
    dq, dk, chunk_decay = _retention_decay_tables()
    tm_merge = min(ROWS_MERGE, t)
    lower = jnp.tril(jnp.ones((tm_merge, tm_merge), F32), -1).astype(BF16)

    x2d = x.reshape(t, d)
    proj, gates = _in_projection(x2d, norm_mix_w, w_big, w_if)
    proj3 = proj.reshape(b, s, N_BIG)
    ret = _retention(proj3, cosf, sinf, dq, dk, ret_gn_w, chunk_decay)
    hm = _mlstm(proj3, gates.reshape(b, s, LANES), mlstm_conv_w[0, :, 0, :], mlstm_conv_b,
                gate_bias, mlstm_gn_w)
    x1, h2p, route, route_t, tiles = _merge_route(
        ret.reshape(t, V_ALL), hm.reshape(t, V_ALL), proj, x2d, w_ret_branch[0].astype(BF16),
        w_mlstm_branch[0].astype(BF16), w_out[0].astype(BF16), norm_ffn_w, w_router, b_router, lower)

    n_tiles = t // tm_merge
    n_slots = 2 * t + N_EXPERTS * (MOE_ROWS + n_tiles * (SUBLANES - 1) // SUBLANES * SUBLANES)
    n_slots = -(-n_slots // MOE_ROWS) * MOE_ROWS
    run = tiles[:, 1, :N_EXPERTS].astype(jnp.int32)
    run_start = tiles[:, 2, :N_EXPERTS].astype(jnp.int32)
    rows_e = jnp.sum(run, axis=0)
    padded = (rows_e + MOE_ROWS - 1) // MOE_ROWS * MOE_ROWS
    expert_ids = jnp.arange(N_EXPERTS, dtype=jnp.int32)
    pstart = jnp.sum(jnp.where(expert_ids[None, :] < expert_ids[:, None], padded[None, :], 0), axis=1)
    pend = pstart + padded
    tile_ids = jnp.arange(n_tiles, dtype=jnp.int32)
    run_t = run.T
    earlier = jnp.sum(jnp.where((tile_ids[None, :] < tile_ids[:, None])[None], run_t[:, None, :], 0), axis=2).T
    global_start = pstart[None, :] + earlier
    local_row = jnp.arange(GROUP_TABLE, dtype=jnp.int32)[None, None, :] * SUBLANES
    in_run = (local_row >= run_start[:, :, None]) & (local_row < (run_start + run)[:, :, None])
    gdst = jnp.sum(jnp.where(in_run, (global_start - run_start)[:, :, None] + local_row, 0), axis=1) // SUBLANES
    gdst = gdst.reshape(n_tiles * GROUP_TABLE).astype(jnp.int32)
    n_groups = (jnp.sum(run, axis=1) // SUBLANES).astype(jnp.int32)
    blk_start = jnp.arange(n_slots // MOE_ROWS, dtype=jnp.int32) * MOE_ROWS
    blk_e = jnp.minimum(jnp.sum(blk_start[:, None] >= pend[None, :], axis=-1), N_EXPERTS - 1).astype(jnp.int32)
    in_expert = (blk_start[:, None] >= pstart[None, :]) & (blk_start[:, None] < pend[None, :])
    rows_end = jnp.sum(jnp.where(in_expert, (pstart + rows_e)[None, :], 0), axis=1)
    blk_valid = jnp.clip(rows_end - blk_start, 0, MOE_ROWS).astype(jnp.int32)
    tail = pend[-1] + jnp.arange((n_slots - 2 * t) // MOE_ROWS, dtype=jnp.int32) * MOE_ROWS
    fill_groups = (jnp.concatenate([jnp.where(padded > 0, pend - MOE_ROWS, -SUBLANES),
                                    jnp.where(tail < n_slots, tail, -SUBLANES)]) // SUBLANES).astype(jnp.int32)

    xs3 = _dispatch(fill_groups, n_groups, gdst, route_t, h2p, n_slots)
    ys = _experts(blk_e, blk_valid, xs3.reshape(n_slots, PACKED), w_gate[0], w_up[0], w_down[0])
    out = _combine(gdst, route, x1, norm_final_w[None, :],
                   ys.reshape(n_slots // SUBLANES, SUBLANES, PACKED))
    return out.reshape(b, s, d)
```

```python
import functools

import numpy as np
import jax
import jax.numpy as jnp
from jax import lax
from jax.experimental import pallas as pl
from jax.experimental.pallas import tpu as pltpu

F32 = jnp.float32
BF16 = jnp.bfloat16
U32 = jnp.uint32

D_MODEL = 1024
N_HEADS = 4
QK_DIM = 128
V_DIM = 256
CHUNK = 128
CONV_WIDTH = 4
ROPE_BASE = 10000.0
N_GROUPS = 4
EXPERTS_PER_GROUP = 8
N_EXPERTS = N_GROUPS * EXPERTS_PER_GROUP
D_EXPERT = 512
NORM_EPS = 1e-6
QK_ALL = N_HEADS * QK_DIM
V_ALL = N_HEADS * V_DIM

MIX_COLS = 2 * QK_ALL + 2 * V_ALL
N_BIG = 2 * MIX_COLS + 2 * D_MODEL
LANES = 128
PACKED = D_MODEL // 2

ROWS_PROJ = 2048
COLS_PROJ = 2048
NORM_ROWS = 512
RET_CHUNKS_PER_STEP = 16
MLSTM_CHUNKS_PER_STEP = 4
ROWS_MERGE = 512
ROUTE_ROWS = 8
SUBLANES = 8
LOCAL_ROWS = 2 * ROWS_MERGE + 256
assert LOCAL_ROWS >= 2 * ROWS_MERGE + N_EXPERTS * (SUBLANES - 1) and LOCAL_ROWS % LANES == 0
LOCAL_GROUPS = LOCAL_ROWS // SUBLANES
GROUP_TABLE = 256
assert GROUP_TABLE >= LOCAL_GROUPS
MOE_ROWS = 512
V7X_VMEM_BYTES = 64 * 1024 * 1024
VMEM_LIMIT = V7X_VMEM_BYTES - 8 * 1024 * 1024
VMEM_LIMIT_PROJ = V7X_VMEM_BYTES - 4 * 1024 * 1024


def _rms(x, eps=NORM_EPS):
    return x * lax.rsqrt(jnp.mean(x * x, axis=-1, keepdims=True) + eps)


def _pack_rows(x):
    lo = lax.bitcast_convert_type(x[:, :PACKED].astype(BF16).astype(F32), U32)
    hi = lax.bitcast_convert_type(x[:, PACKED:].astype(BF16).astype(F32), U32)
    return (hi & jnp.uint32(0xFFFF0000)) | (lo >> 16)


def _unpack_rows(w):
    lo = lax.bitcast_convert_type(w << 16, F32)
    hi = lax.bitcast_convert_type(w & jnp.uint32(0xFFFF0000), F32)
    return jnp.concatenate([lo, hi], axis=1)


def _in_proj_kernel(x_ref, nw_ref, w_ref, wif_ref, o_ref, gates_ref, h_scr):
    @pl.when(pl.program_id(1) == 0)
    def _():
        for r0 in range(0, x_ref.shape[0], NORM_ROWS):
            rows = slice(r0, r0 + NORM_ROWS)
            h = (_rms(x_ref[rows, :]) * nw_ref[...]).astype(BF16)
            h_scr[rows, :] = h
            gates_ref[rows, :] = jnp.dot(h, wif_ref[...], preferred_element_type=F32)
            o_ref[rows, :] = jnp.dot(h, w_ref[...], preferred_element_type=F32).astype(o_ref.dtype)

    @pl.when(pl.program_id(1) != 0)
    def _():
        o_ref[...] = jnp.dot(h_scr[...], w_ref[...], preferred_element_type=F32).astype(o_ref.dtype)


def _in_projection(x2d, norm_w, w_big, w_if):
    t = x2d.shape[0]
    tm = min(ROWS_PROJ, t)
    tn = COLS_PROJ
    return pl.pallas_call(
        _in_proj_kernel,
        grid=(t // tm, N_BIG // tn),
        in_specs=[
            pl.BlockSpec((tm, D_MODEL), lambda i, j: (i, 0)),
            pl.BlockSpec((1, D_MODEL), lambda i, j: (0, 0)),
            pl.BlockSpec((D_MODEL, tn), lambda i, j: (0, j)),
            pl.BlockSpec((D_MODEL, LANES), lambda i, j: (0, 0)),
        ],
        out_specs=[
            pl.BlockSpec((tm, tn), lambda i, j: (i, j)),
            pl.BlockSpec((tm, LANES), lambda i, j: (i, 0)),
        ],
        out_shape=[
            jax.ShapeDtypeStruct((t, N_BIG), BF16),
            jax.ShapeDtypeStruct((t, LANES), F32),
        ],
        scratch_shapes=[pltpu.VMEM((tm, D_MODEL), BF16)],
        compiler_params=pltpu.CompilerParams(
            dimension_semantics=("arbitrary", "arbitrary"), vmem_limit_bytes=VMEM_LIMIT_PROJ),
        name="in_projection",
    )(x2d, norm_w, w_big, w_if)


def _retention_kernel(p_ref, cos_ref, sin_ref, dq_ref, dk_ref, gn_ref, o_ref, state_scr, *, chunk_decay):
    L = CHUNK

    @pl.when(pl.program_id(1) == 0)
    def _():
        state_scr[...] = jnp.zeros_like(state_scr)

    row = lax.broadcasted_iota(jnp.int32, (L, L), 0)
    col = lax.broadcasted_iota(jnp.int32, (L, L), 1)
    causal = row >= col
    n_chunks = p_ref.shape[0] // L
    units = [(ci * L, h) for ci in range(n_chunks) for h in range(N_HEADS)]

    qts, kts, scores = [], [], []
    for r0, h in units:
        cosf = cos_ref[r0:r0 + L, :]
        sinf = sin_ref[r0:r0 + L, :]
        q = p_ref[r0:r0 + L, h * QK_DIM:(h + 1) * QK_DIM].astype(F32)
        k = p_ref[r0:r0 + L, QK_ALL + h * QK_DIM:QK_ALL + (h + 1) * QK_DIM].astype(F32)
        qt = ((q * cosf + pltpu.roll(q, QK_DIM // 2, 1) * sinf) * dq_ref[h]).astype(BF16)
        kt = ((k * cosf + pltpu.roll(k, QK_DIM // 2, 1) * sinf) * dk_ref[h]).astype(BF16)
        s = lax.dot_general(qt, kt, (((1,), (1,)), ((), ())), preferred_element_type=F32)
        qts.append(qt)
        kts.append(kt)
        scores.append(jnp.where(causal, s, 0.0).astype(BF16))

    states = [state_scr[h] for h in range(N_HEADS)]
    for (r0, h), qt, kt, s in zip(units, qts, kts, scores):
        v = p_ref[r0:r0 + L, 2 * QK_ALL + h * V_DIM:2 * QK_ALL + (h + 1) * V_DIM]
        g = p_ref[r0:r0 + L, 2 * QK_ALL + V_ALL + h * V_DIM:2 * QK_ALL + V_ALL + (h + 1) * V_DIM].astype(F32)
        lhs = jnp.concatenate([s, qt], axis=1)
        rhs = jnp.concatenate([v, states[h].astype(BF16)], axis=0)
        o = jnp.dot(lhs, rhs, preferred_element_type=F32)
        kv = lax.dot_general(kt, v, (((0,), (0,)), ((), ())), preferred_element_type=F32)
        states[h] = (states[h] + kv) * chunk_decay[h]
        y = _rms(o) * gn_ref[:, h * V_DIM:(h + 1) * V_DIM] * (g * jax.nn.sigmoid(g))
        o_ref[r0:r0 + L, h * V_DIM:(h + 1) * V_DIM] = y.astype(o_ref.dtype)
    for h in range(N_HEADS):
        state_scr[h] = states[h]


def _retention(proj3, cosf, sinf, dq, dk, gn_w, chunk_decay):
    b, s, _ = proj3.shape
    lb = CHUNK * min(RET_CHUNKS_PER_STEP, s // CHUNK)
    return pl.pallas_call(
        functools.partial(_retention_kernel, chunk_decay=chunk_decay),
        grid=(b, s // lb),
        in_specs=[
            pl.BlockSpec((None, lb, MIX_COLS), lambda i, c: (i, c, 0)),
            pl.BlockSpec((lb, QK_DIM), lambda i, c: (c, 0)),
            pl.BlockSpec((lb, QK_DIM), lambda i, c: (c, 0)),
            pl.BlockSpec((N_HEADS, CHUNK, QK_DIM), lambda i, c: (0, 0, 0)),
            pl.BlockSpec((N_HEADS, CHUNK, QK_DIM), lambda i, c: (0, 0, 0)),
            pl.BlockSpec((1, V_ALL), lambda i, c: (0, 0)),
        ],
        out_specs=pl.BlockSpec((None, lb, V_ALL), lambda i, c: (i, c, 0)),
        out_shape=jax.ShapeDtypeStruct((b, s, V_ALL), BF16),
        scratch_shapes=[pltpu.VMEM((N_HEADS, QK_DIM, V_DIM), F32)],
        compiler_params=pltpu.CompilerParams(
            dimension_semantics=("arbitrary", "arbitrary"), vmem_limit_bytes=VMEM_LIMIT),
        name="retention",
    )(proj3, cosf, sinf, dq, dk, gn_w)


def _mlstm_kernel(p_ref, gates_ref, cw_ref, cb_ref, shift_ref, gb_ref, gn_ref, o_ref,
                  c_scr, n_scr, m_scr, tail_scr, act_scr):
    L = CHUNK
    lb = p_ref.shape[0]

    @pl.when(pl.program_id(1) == 0)
    def _():
        c_scr[...] = jnp.zeros_like(c_scr)
        n_scr[...] = jnp.zeros_like(n_scr)
        m_scr[...] = jnp.zeros_like(m_scr)
        tail_scr[...] = jnp.zeros_like(tail_scr)

    ub = p_ref[:, 0:2 * QK_ALL]
    u = ub.astype(F32)
    tail = tail_scr[...]
    row8 = lax.broadcasted_iota(jnp.int32, tail.shape, 0)
    acc = u * cw_ref[CONV_WIDTH - 1:CONV_WIDTH, :] + cb_ref[...]
    head = jnp.zeros_like(tail)
    for d in range(1, CONV_WIDTH):
        w_d = cw_ref[CONV_WIDTH - 1 - d:CONV_WIDTH - d, :]
        acc = acc + jnp.dot(shift_ref[d - 1], ub, preferred_element_type=F32) * w_d
        head = head + jnp.where(row8 < d, pltpu.roll(tail, d, 0), 0.0) * w_d
    acc = jnp.concatenate([acc[0:SUBLANES, :] + head, acc[SUBLANES:, :]], axis=0)
    tail_scr[...] = u[lb - SUBLANES:lb, :]
    act_scr[...] = acc * jax.nn.sigmoid(acc)

    row = lax.broadcasted_iota(jnp.int32, (L, L), 0)
    col = lax.broadcasted_iota(jnp.int32, (L, L), 1)
    causal = row >= col
    k_scale = QK_DIM ** -0.5
    units = [(ci * L, h) for ci in range(lb // L) for h in range(N_HEADS)]

    def load_qk(r0, h):
        q = act_scr[r0:r0 + L, h * QK_DIM:(h + 1) * QK_DIM]
        k = act_scr[r0:r0 + L, QK_ALL + h * QK_DIM:QK_ALL + (h + 1) * QK_DIM] * k_scale
        return q, k

    gate_terms, src_rows = [], []
    lane_t = lax.broadcasted_iota(jnp.int32, (2 * N_HEADS, L), 1)
    for ci in range(lb // L):
        pre = gates_ref[ci * L:(ci + 1) * L, :] + gb_ref[...]
        pre_rows = jnp.transpose(pre)[0:2 * N_HEADS, :]
        b_rows = jnp.minimum(pre_rows, 0.0) - jnp.log1p(jnp.exp(-jnp.abs(pre_rows)))
        shift = 1
        while shift < L:
            b_rows = b_rows + jnp.where(lane_t >= shift, pltpu.roll(b_rows, shift, 1), 0.0)
            shift *= 2
        src_rows.append(pre_rows[0:N_HEADS, :] - b_rows[N_HEADS:2 * N_HEADS, :])
        bcum = jnp.transpose(jnp.concatenate([b_rows, jnp.zeros((L - 2 * N_HEADS, L), F32)], axis=0))
        gate_terms.append((pre, bcum))
    scores = []
    for r0, h in units:
        q, k = load_qk(r0, h)
        scores.append(lax.dot_general(q.astype(BF16), k.astype(BF16), (((1,), (1,)), ((), ())),
                                      preferred_element_type=F32))

    m_state = [m_scr[h:h + 1, :] for h in range(N_HEADS)]
    terms = []
    for (r0, h), qk in zip(units, scores):
        pre, bcum = gate_terms[r0 // L]
        q, k = load_qk(r0, h)
        b_t = jnp.broadcast_to(bcum[:, N_HEADS + h:N_HEADS + h + 1], (L, L))
        i_t = jnp.broadcast_to(pre[:, h:h + 1], (L, L))
        src = jnp.broadcast_to(src_rows[r0 // L][h:h + 1, :], (L, L))
        m_prev = m_state[h]
        a = b_t + m_prev
        dmat = jnp.where(causal, b_t + src, -jnp.inf)
        m_t = jnp.maximum(a, jnp.max(dmat, axis=-1, keepdims=True))
        w_inter = jnp.exp(a - m_t)
        s = qk * jnp.exp(dmat - m_t)
        lhs = jnp.concatenate([s.astype(BF16), (q * w_inter).astype(BF16)], axis=1)
        b_last = b_t[L - 1:L, :]
        gk = b_last - b_t + i_t
        m_new = jnp.maximum(b_last + m_prev, jnp.max(gk, axis=0, keepdims=True))
        wk = k * jnp.exp(gk - m_new)
        m_state[h] = m_new
        terms.append(dict(
            lhs=lhs, s_sum=jnp.sum(s, axis=-1, keepdims=True), w_inter=w_inter, floor=jnp.exp(-m_t),
            keep=jnp.exp(b_last + m_prev - m_new), wk=wk.astype(BF16), wk_sum=jnp.sum(wk, axis=0, keepdims=True)))

    n_state = [n_scr[h:h + 1, :] for h in range(N_HEADS)]
    c_state = [c_scr[h] for h in range(N_HEADS)]
    v_cols = lambda h: slice(2 * QK_ALL + h * V_DIM, 2 * QK_ALL + (h + 1) * V_DIM)
    for ci in range(lb // L):
        r0 = ci * L
        chunk_terms = terms[ci * N_HEADS:(ci + 1) * N_HEADS]
        q_dot_n = [jnp.sum(load_qk(r0, h)[0] * n_state[h], axis=-1, keepdims=True) for h in range(N_HEADS)]
        nums = []
        for h, t in enumerate(chunk_terms):
            rhs = jnp.concatenate([p_ref[r0:r0 + L, v_cols(h)], c_state[h].astype(BF16)], axis=0)
            nums.append(jnp.dot(t["lhs"], rhs, preferred_element_type=F32))
        for h, t in enumerate(chunk_terms):
            c_state[h] = c_state[h] * t["keep"][:, 0:1] + lax.dot_general(
                t["wk"], p_ref[r0:r0 + L, v_cols(h)], (((0,), (0,)), ((), ())), preferred_element_type=F32)
            n_state[h] = n_state[h] * t["keep"] + t["wk_sum"]
        for h, t in enumerate(chunk_terms):
            og = p_ref[r0:r0 + L, V_ALL + v_cols(h).start:V_ALL + v_cols(h).stop].astype(F32)
            den = t["s_sum"] + q_dot_n[h] * t["w_inter"]
            inv = 1.0 / jnp.maximum(jnp.abs(den), t["floor"])
            hh = nums[h] * jnp.concatenate([inv] * (V_DIM // LANES), axis=1)
            y = _rms(hh * jax.nn.sigmoid(og)) * gn_ref[:, h * V_DIM:(h + 1) * V_DIM]
            o_ref[r0:r0 + L, h * V_DIM:(h + 1) * V_DIM] = y.astype(o_ref.dtype)

    for h in range(N_HEADS):
        c_scr[h] = c_state[h]
        n_scr[h:h + 1, :] = n_state[h]
        m_scr[h:h + 1, :] = m_state[h]


def _mlstm(proj3, gates3, conv_w, conv_b, gate_bias, gn_w):
    b, s, _ = proj3.shape
    lb = CHUNK * min(MLSTM_CHUNKS_PER_STEP, s // CHUNK)
    shifts = jnp.stack([jnp.eye(lb, k=-d, dtype=BF16) for d in range(1, CONV_WIDTH)])
    return pl.pallas_call(
        _mlstm_kernel,
        grid=(b, s // lb),
        in_specs=[
            pl.BlockSpec((None, lb, MIX_COLS), lambda i, c: (i, c, 1)),
            pl.BlockSpec((None, lb, LANES), lambda i, c: (i, c, 0)),
            pl.BlockSpec((CONV_WIDTH, 2 * QK_ALL), lambda i, c: (0, 0)),
            pl.BlockSpec((1, 2 * QK_ALL), lambda i, c: (0, 0)),
            pl.BlockSpec((CONV_WIDTH - 1, lb, lb), lambda i, c: (0, 0, 0)),
            pl.BlockSpec((1, LANES), lambda i, c: (0, 0)),
            pl.BlockSpec((1, V_ALL), lambda i, c: (0, 0)),
        ],
        out_specs=pl.BlockSpec((None, lb, V_ALL), lambda i, c: (i, c, 0)),
        out_shape=jax.ShapeDtypeStruct((b, s, V_ALL), BF16),
        scratch_shapes=[
            pltpu.VMEM((N_HEADS, QK_DIM, V_DIM), F32),
            pltpu.VMEM((8, QK_DIM), F32),
            pltpu.VMEM((8, LANES), F32),
            pltpu.VMEM((8, 2 * QK_ALL), F32),
            pltpu.VMEM((lb, 2 * QK_ALL), F32),
        ],
        compiler_params=pltpu.CompilerParams(
            dimension_semantics=("arbitrary", "arbitrary"), vmem_limit_bytes=VMEM_LIMIT),
        name="mlstm",
    )(proj3, gates3, conv_w, conv_b, shifts, gate_bias, gn_w)


def _merge_route_kernel(ret_ref, hm_ref, gr_ref, gm_ref, x_ref, wr_ref, wm_ref, wo_ref, nw_ref,
                        wrt_ref, brt_ref, lower_ref, x1_ref, h2_ref, route_ref, route_t_ref, tile_ref,
                        logits_scr):
    step = pl.program_id(0)

    @pl.when(step == 0)
    def _():
        logits_scr[...] = jnp.zeros_like(logits_scr)

    logits = logits_scr[...]

    y_ret = jnp.dot(ret_ref[...], wr_ref[...], preferred_element_type=F32)
    y_m = jnp.dot(hm_ref[...], wm_ref[...], preferred_element_type=F32)
    merged = (jax.nn.sigmoid(gr_ref[...].astype(F32)) * y_ret
              + jax.nn.sigmoid(gm_ref[...].astype(F32)) * y_m)
    x1 = x_ref[...] + jnp.dot(merged.astype(BF16), wo_ref[...], preferred_element_type=F32)
    x1_ref[...] = x1
    h2 = _rms(x1) * nw_ref[...]
    h2_ref[...] = _pack_rows(h2)
    new_logits = jnp.dot(h2.astype(BF16), wrt_ref[...], preferred_element_type=F32) + brt_ref[...]

    live = jnp.where(step > 0, 1.0, 0.0)
    tm = logits.shape[0]
    lane = lax.broadcasted_iota(jnp.int32, (tm, LANES), 1)
    neg = -jnp.inf
    big = jnp.int32(LANES)
    is_group = (lane >= N_EXPERTS) & (lane < N_EXPERTS + N_GROUPS)
    gl = jnp.where(is_group, logits, neg)
    g_max = jnp.max(gl, axis=-1, keepdims=True)
    g_idx = jnp.min(jnp.where(gl == g_max, lane, big), axis=-1, keepdims=True) - N_EXPERTS
    g_w = 1.0 / jnp.sum(jnp.exp(gl - g_max), axis=-1, keepdims=True)
    in_group = (lane >= g_idx * EXPERTS_PER_GROUP) & (lane < (g_idx + 1) * EXPERTS_PER_GROUP)
    el = jnp.where(in_group, logits, neg)
    l1 = jnp.max(el, axis=-1, keepdims=True)
    e1 = jnp.min(jnp.where(el == l1, lane, big), axis=-1, keepdims=True)
    el2 = jnp.where(lane == e1, neg, el)
    l2 = jnp.max(el2, axis=-1, keepdims=True)
    e2 = jnp.min(jnp.where(el2 == l2, lane, big), axis=-1, keepdims=True)
    t21 = jnp.exp(l2 - l1)
    w1 = g_w / (1.0 + t21)
    w2 = g_w * t21 / (1.0 + t21)

    hit1 = lane == e1
    hit2 = lane == e2
    cnt = jnp.where(hit1 | hit2, live, 0.0)
    before = jnp.dot(lower_ref[...], cnt.astype(BF16), preferred_element_type=F32)
    count = jnp.sum(cnt, axis=0, keepdims=True)
    run = jnp.floor((count + (SUBLANES - 1)) * (1.0 / SUBLANES)) * SUBLANES
    lane1 = lax.broadcasted_iota(jnp.int32, (1, LANES), 1)
    run_end = run
    shift = 1
    while shift < N_EXPERTS:
        run_end = run_end + jnp.where(lane1 >= shift, pltpu.roll(run_end, shift, 1), 0.0)
        shift *= 2
    run_start = run_end - run
    local = before + run_start
    r1 = jnp.sum(jnp.where(hit1, local, 0.0), axis=-1, keepdims=True)
    r2 = jnp.sum(jnp.where(hit2, local, 0.0), axis=-1, keepdims=True)
    sub8 = lax.broadcasted_iota(jnp.int32, (SUBLANES, LANES), 0)
    tile_ref[...] = jnp.where(sub8 == 0, count, jnp.where(sub8 == 1, run, jnp.where(sub8 == 2, run_start, 0.0)))

    fields = (e1.astype(F32), e2.astype(F32), r1, r2, w1, w2)
    packed = jnp.zeros((tm, LANES), F32)
    for idx, val in enumerate(fields):
        packed = jnp.where(lane == idx, val, packed)
    route_ref[...] = packed
    route_t_ref[...] = jnp.transpose(packed)[0:ROUTE_ROWS, :]
    logits_scr[...] = new_logits


def _merge_route(ret, hm, proj, x2d, w_ret, w_m, w_out, norm_w, w_router, b_router, lower):
    t = x2d.shape[0]
    tm = min(ROWS_MERGE, t)
    n_tiles = t // tm
    gate_r_blk = 2 * MIX_COLS // D_MODEL
    tile = lambda i: jnp.minimum(i, n_tiles - 1)
    routed = lambda i: jnp.maximum(i - 1, 0)
    row_blk = lambda i: (tile(i), 0)
    const = lambda i: (0, 0)
    return pl.pallas_call(
        _merge_route_kernel,
        grid=(n_tiles + 1,),
        in_specs=[
            pl.BlockSpec((tm, V_ALL), row_blk),
            pl.BlockSpec((tm, V_ALL), row_blk),
            pl.BlockSpec((tm, D_MODEL), lambda i: (tile(i), gate_r_blk)),
            pl.BlockSpec((tm, D_MODEL), lambda i: (tile(i), gate_r_blk + 1)),
            pl.BlockSpec((tm, D_MODEL), row_blk),
            pl.BlockSpec((V_ALL, D_MODEL), const),
            pl.BlockSpec((V_ALL, D_MODEL), const),
            pl.BlockSpec((D_MODEL, D_MODEL), const),
            pl.BlockSpec((1, D_MODEL), const),
            pl.BlockSpec((D_MODEL, LANES), const),
            pl.BlockSpec((1, LANES), const),
            pl.BlockSpec((tm, tm), const),
        ],
        out_specs=[
            pl.BlockSpec((tm, D_MODEL), row_blk),
            pl.BlockSpec((tm, PACKED), row_blk),
            pl.BlockSpec((tm, LANES), lambda i: (routed(i), 0)),
            pl.BlockSpec((ROUTE_ROWS, tm), lambda i: (0, routed(i))),
            pl.BlockSpec((None, SUBLANES, LANES), lambda i: (routed(i), 0, 0)),
        ],
        out_shape=[
            jax.ShapeDtypeStruct((t, D_MODEL), F32),
            jax.ShapeDtypeStruct((t, PACKED), U32),
            jax.ShapeDtypeStruct((t, LANES), F32),
            jax.ShapeDtypeStruct((ROUTE_ROWS, t), F32),
            jax.ShapeDtypeStruct((n_tiles, SUBLANES, LANES), F32),
        ],
        scratch_shapes=[pltpu.VMEM((tm, LANES), F32)],
        compiler_params=pltpu.CompilerParams(
            dimension_semantics=("arbitrary",), vmem_limit_bytes=VMEM_LIMIT),
        name="merge_route",
    )(ret, hm, proj, proj, x2d, w_ret, w_m, w_out, norm_w, w_router, b_router, lower)


def _dispatch_kernel(fill_ref, n_groups_ref, gdst_ref, route_t_ref, h2_ref, xs_hbm, sorted_scr, zero_scr, sems):
    step = pl.program_id(0)
    n_tiles = pl.num_programs(0) - 1
    tm = h2_ref.shape[0]
    slot = step % 2

    def group_copy(which, j):
        return pltpu.make_async_copy(sorted_scr.at[which, pl.ds(j, 1)], xs_hbm.at[pl.ds(gdst_ref[j], 1)],
                                     sems.at[which])

    def wait_groups(which, count):
        def wait(j, carry):
            group_copy(which, 0).wait()
            return carry
        lax.fori_loop(0, count, wait, 0)

    @pl.when(step == 0)
    def _():
        zero_scr[...] = jnp.zeros_like(zero_scr)

        def fill_copy(j):
            g0 = pl.multiple_of(fill_ref[j], MOE_ROWS // SUBLANES)
            return pltpu.make_async_copy(zero_scr, xs_hbm.at[pl.ds(g0, MOE_ROWS // SUBLANES)], sems.at[0])

        def start_fill(j, carry):
            @pl.when(fill_ref[j] >= 0)
            def _():
                fill_copy(j).start()
            return carry

        def wait_fill(j, carry):
            @pl.when(fill_ref[j] >= 0)
            def _():
                fill_copy(j).wait()
            return carry

        lax.fori_loop(0, fill_ref.shape[0], start_fill, 0)
        lax.fori_loop(0, fill_ref.shape[0], wait_fill, 0)

    @pl.when(step >= 1)
    def _():
        def start(j, carry):
            group_copy(1 - slot, j).start()
            return carry
        lax.fori_loop(0, n_groups_ref[jnp.maximum(step - 1, 0)], start, 0)

    @pl.when(step >= 2)
    def _():
        wait_groups(slot, n_groups_ref[jnp.maximum(step - 2, 0)])

    @pl.when(step < n_tiles)
    def _():
        pos = lax.broadcasted_iota(jnp.int32, (LOCAL_ROWS, tm), 0)
        row1 = route_t_ref[2:3, :].astype(jnp.int32)
        row2 = route_t_ref[3:4, :].astype(jnp.int32)
        pick = jnp.where((pos == row1) | (pos == row2), 1.0, 0.0).astype(BF16)
        tokens = _unpack_rows(h2_ref[...]).astype(BF16)
        ordered = jnp.dot(pick, tokens, preferred_element_type=F32)
        sorted_scr[slot] = _pack_rows(ordered).reshape(LOCAL_GROUPS, SUBLANES, PACKED)

    @pl.when(step == n_tiles)
    def _():
        wait_groups(1 - slot, n_groups_ref[jnp.maximum(step - 1, 0)])


def _dispatch(fill_groups, n_groups, gdst, route_t, h2p, n_slots):
    t = h2p.shape[0]
    tm = min(ROWS_MERGE, t)
    n_tiles = t // tm
    sorted_tile = lambda i: jnp.minimum(i, n_tiles - 1)
    return pl.pallas_call(
        _dispatch_kernel,
        grid=(n_tiles + 1,),
        in_specs=[
            pl.BlockSpec(memory_space=pltpu.SMEM),
            pl.BlockSpec(memory_space=pltpu.SMEM),
            pl.BlockSpec((GROUP_TABLE,), lambda i: (jnp.maximum(i - 1, 0),), memory_space=pltpu.SMEM),
            pl.BlockSpec((ROUTE_ROWS, tm), lambda i: (0, sorted_tile(i))),
            pl.BlockSpec((tm, PACKED), lambda i: (sorted_tile(i), 0)),
        ],
        out_specs=pl.BlockSpec(memory_space=pl.ANY),
        out_shape=jax.ShapeDtypeStruct((n_slots // SUBLANES, SUBLANES, PACKED), U32),
        scratch_shapes=[
            pltpu.VMEM((2, LOCAL_GROUPS, SUBLANES, PACKED), U32),
            pltpu.VMEM((MOE_ROWS // SUBLANES, SUBLANES, PACKED), U32),
            pltpu.SemaphoreType.DMA((2,)),
        ],
        compiler_params=pltpu.CompilerParams(
            dimension_semantics=("arbitrary",), vmem_limit_bytes=VMEM_LIMIT),
        name="dispatch",
    )(fill_groups, n_groups, gdst, route_t, h2p)


def _experts_kernel(blk_e_ref, blk_valid_ref, n_used_ref, xs_ref, wg_ref, wu_ref, wd_ref, ys_ref, wgu_scr, wd_scr):
    del n_used_ref
    i = pl.program_id(0)
    valid = blk_valid_ref[i]

    @pl.when((i == 0) | (blk_e_ref[i] != blk_e_ref[jnp.maximum(i - 1, 0)]))
    def _():
        wgu_scr[:, :D_EXPERT] = wg_ref[...].astype(BF16)
        wgu_scr[:, D_EXPERT:] = wu_ref[...].astype(BF16)
        wd_scr[...] = wd_ref[...].astype(BF16)

    @pl.when(valid > 0)
    def _():
        xb = _unpack_rows(xs_ref[...]).astype(BF16)
        gu = jnp.dot(xb, wgu_scr[...], preferred_element_type=F32)
        g = gu[:, :D_EXPERT]
        act = (g * jax.nn.sigmoid(g) * gu[:, D_EXPERT:]).astype(BF16)
        ys_ref[...] = _pack_rows(jnp.dot(act, wd_scr[...], preferred_element_type=F32))

    @pl.when(valid <= 0)
    def _():
        ys_ref[...] = jnp.zeros_like(ys_ref)


def _experts(blk_e, blk_valid, n_used, xs, w_gate, w_up, w_down):
    p = xs.shape[0]
    per_expert = lambda i, be, bv, nu: (be[i], 0, 0)
    grid_spec = pltpu.PrefetchScalarGridSpec(
        num_scalar_prefetch=3,
        grid=(p // MOE_ROWS,),
        in_specs=[
            pl.BlockSpec((MOE_ROWS, PACKED), lambda i, be, bv, nu: (jnp.minimum(i, nu[0] - 1), 0)),
            pl.BlockSpec((None, D_MODEL, D_EXPERT), per_expert),
            pl.BlockSpec((None, D_MODEL, D_EXPERT), per_expert),
            pl.BlockSpec((None, D_EXPERT, D_MODEL), per_expert),
        ],
        out_specs=pl.BlockSpec((MOE_ROWS, PACKED), lambda i, be, bv, nu: (i, 0)),
        scratch_shapes=[pltpu.VMEM((D_MODEL, 2 * D_EXPERT), BF16), pltpu.VMEM((D_EXPERT, D_MODEL), BF16)],
    )
    return pl.pallas_call(
        _experts_kernel,
        grid_spec=grid_spec,
        out_shape=jax.ShapeDtypeStruct((p, PACKED), U32),
        compiler_params=pltpu.CompilerParams(
            dimension_semantics=("arbitrary",), vmem_limit_bytes=VMEM_LIMIT),
        name="experts",
    )(blk_e, blk_valid, n_used, xs, w_gate, w_up, w_down)


def _combine_kernel(gdst_ref, route_ref, x1_ref, nw_ref, ys_hbm, o_ref, buf, sems):
    step = pl.program_id(0)
    n_tiles = pl.num_programs(0) - 1
    tm = x1_ref.shape[0]
    slot = step % 2
    prev = 1 - slot

    @pl.when(step < n_tiles)
    def _():
        def start(j, carry):
            pltpu.make_async_copy(ys_hbm.at[pl.ds(gdst_ref[j], 1)], buf.at[slot, pl.ds(j, 1)], sems.at[slot]).start()
            return carry
        lax.fori_loop(0, LOCAL_GROUPS, start, 0, unroll=8)

    @pl.when(step > 0)
    def _():
        pltpu.make_async_copy(ys_hbm.at[pl.ds(0, LOCAL_GROUPS)], buf.at[prev], sems.at[prev]).wait()

        rows = _unpack_rows(buf[prev].reshape(LOCAL_ROWS, PACKED)).astype(BF16)
        pos = lax.broadcasted_iota(jnp.int32, (tm, LOCAL_ROWS), 1)
        row1 = route_ref[:, 2:3].astype(jnp.int32)
        row2 = route_ref[:, 3:4].astype(jnp.int32)
        mix = (jnp.where(pos == row1, route_ref[:, 4:5], 0.0)
               + jnp.where(pos == row2, route_ref[:, 5:6], 0.0)).astype(BF16)
        x2 = x1_ref[...] + jnp.dot(mix, rows, preferred_element_type=F32)
        o_ref[...] = _rms(x2) * nw_ref[...]


def _combine(gdst, route, x1, norm_w, ys3):
    t = x1.shape[0]
    tm = min(ROWS_MERGE, t)
    n_tiles = t // tm
    gathered = lambda i: (jnp.minimum(i, n_tiles - 1),)
    finished = lambda i: (jnp.maximum(i - 1, 0), 0)
    return pl.pallas_call(
        _combine_kernel,
        grid=(n_tiles + 1,),
        in_specs=[
            pl.BlockSpec((GROUP_TABLE,), gathered, memory_space=pltpu.SMEM),
            pl.BlockSpec((tm, LANES), finished),
            pl.BlockSpec((tm, D_MODEL), finished),
            pl.BlockSpec((1, D_MODEL), lambda i: (0, 0)),
            pl.BlockSpec(memory_space=pl.ANY),
        ],
        out_specs=pl.BlockSpec((tm, D_MODEL), finished),
        out_shape=jax.ShapeDtypeStruct((t, D_MODEL), F32),
        scratch_shapes=[
            pltpu.VMEM((2, LOCAL_GROUPS, SUBLANES, PACKED), U32),
            pltpu.SemaphoreType.DMA((2,)),
        ],
        compiler_params=pltpu.CompilerParams(
            dimension_semantics=("arbitrary",), vmem_limit_bytes=VMEM_LIMIT),
        name="combine",
    )(gdst, route, x1, norm_w, ys3)


def _rotary_tables(seq):
    inv_freq = 1.0 / (ROPE_BASE ** (jnp.arange(0, QK_DIM, 2, dtype=F32) / QK_DIM))
    ang = jnp.arange(seq, dtype=F32)[:, None] * inv_freq[None, :]
    cos, sin = jnp.cos(ang), jnp.sin(ang)
    return jnp.concatenate([cos, cos], axis=1), jnp.concatenate([-sin, sin], axis=1)


def _retention_decay_tables():
    gamma = 1.0 - 2.0 ** (-5.0 - np.arange(N_HEADS, dtype=np.float64))
    idx = np.arange(CHUNK, dtype=np.float64) + 1.0
    dq = gamma[:, None] ** idx[None, :]
    dk = gamma[:, None] ** (-idx[None, :]) * QK_DIM ** -0.5
    bcast = lambda a: jnp.asarray(np.broadcast_to(a[:, :, None], (N_HEADS, CHUNK, QK_DIM)), F32)
    return bcast(dq), bcast(dk), tuple(float(g) for g in gamma ** CHUNK)


def kernel(x, norm_mix_w, w_in, ret_gn_w, w_ret_branch, mlstm_conv_w, mlstm_conv_b, b_igate, b_fgate,
           mlstm_gn_w, w_mlstm_branch, w_out, norm_ffn_w, w_group, b_group, w_expert_router,
           b_expert_router, w_gate, w_up, w_down, norm_final_w):
    assert norm_mix_w.shape[0] == 1, "one layer"
    b, s, d = x.shape
    t = b * s
    assert d == D_MODEL and s % CHUNK == 0

    wi = w_in[0]
    n_pre = 2 * MIX_COLS
    w_big = jnp.concatenate([wi[:, :n_pre], wi[:, n_pre + 2 * N_HEADS:]], axis=1).astype(BF16)
    w_if = jnp.pad(wi[:, n_pre:n_pre + 2 * N_HEADS], ((0, 0), (0, LANES - 2 * N_HEADS))).astype(BF16)
    gate_bias = jnp.pad(jnp.concatenate([b_igate[0], b_fgate[0]]), (0, LANES - 2 * N_HEADS))[None, :]
    w_router = jnp.pad(jnp.concatenate([w_expert_router[0], w_group[0]], axis=1),
                       ((0, 0), (0, LANES - N_EXPERTS - N_GROUPS))).astype(BF16)
    b_router = jnp.pad(jnp.concatenate([b_expert_router[0], b_group[0]]),
                       (0, LANES - N_EXPERTS - N_GROUPS))[None, :]

    cosf, sinf = _rotary_tables(s)
    dq, dk, chunk_decay = _retention_decay_tables()
    tm_merge = min(ROWS_MERGE, t)
    lower = jnp.tril(jnp.ones((tm_merge, tm_merge), F32), -1).astype(BF16)

    x2d = x.reshape(t, d)
    proj, gates = _in_projection(x2d, norm_mix_w, w_big, w_if)
    proj3 = proj.reshape(b, s, N_BIG)
    ret = _retention(proj3, cosf, sinf, dq, dk, ret_gn_w, chunk_decay)
    hm = _mlstm(proj3, gates.reshape(b, s, LANES), mlstm_conv_w[0, :, 0, :], mlstm_conv_b,
                gate_bias, mlstm_gn_w)
    x1, h2p, route, route_t, tiles = _merge_route(
        ret.reshape(t, V_ALL), hm.reshape(t, V_ALL), proj, x2d, w_ret_branch[0].astype(BF16),
        w_mlstm_branch[0].astype(BF16), w_out[0].astype(BF16), norm_ffn_w, w_router, b_router, lower)

    n_tiles = t // tm_merge
    n_slots = 2 * t + N_EXPERTS * (MOE_ROWS + n_tiles * (SUBLANES - 1) // SUBLANES * SUBLANES)
    n_slots = -(-n_slots // MOE_ROWS) * MOE_ROWS
    run = tiles[:, 1, :N_EXPERTS].astype(jnp.int32)
    run_start = tiles[:, 2, :N_EXPERTS].astype(jnp.int32)
    rows_e = jnp.sum(run, axis=0)
    padded = (rows_e + MOE_ROWS - 1) // MOE_ROWS * MOE_ROWS
    expert_ids = jnp.arange(N_EXPERTS, dtype=jnp.int32)
    pstart = jnp.sum(jnp.where(expert_ids[None, :] < expert_ids[:, None], padded[None, :], 0), axis=1)
    pend = pstart + padded
    tile_ids = jnp.arange(n_tiles, dtype=jnp.int32)
    run_t = run.T
    earlier = jnp.sum(jnp.where((tile_ids[None, :] < tile_ids[:, None])[None], run_t[:, None, :], 0), axis=2).T
    global_start = pstart[None, :] + earlier
    local_row = jnp.arange(GROUP_TABLE, dtype=jnp.int32)[None, None, :] * SUBLANES
    in_run = (local_row >= run_start[:, :, None]) & (local_row < (run_start + run)[:, :, None])
    gdst = jnp.sum(jnp.where(in_run, (global_start - run_start)[:, :, None] + local_row, 0), axis=1) // SUBLANES
    gdst = gdst.reshape(n_tiles * GROUP_TABLE).astype(jnp.int32)
    n_groups = (jnp.sum(run, axis=1) // SUBLANES).astype(jnp.int32)
    blk_start = jnp.arange(n_slots // MOE_ROWS, dtype=jnp.int32) * MOE_ROWS
    blk_e = jnp.minimum(jnp.sum(blk_start[:, None] >= pend[None, :], axis=-1), N_EXPERTS - 1).astype(jnp.int32)
    in_expert = (blk_start[:, None] >= pstart[None, :]) & (blk_start[:, None] < pend[None, :])
    rows_end = jnp.sum(jnp.where(in_expert, (pstart + rows_e)[None, :], 0), axis=1)
    blk_valid = jnp.clip(rows_end - blk_start, 0, MOE_ROWS).astype(jnp.int32)
    tail = pend[-1] + jnp.arange((n_slots - 2 * t) // MOE_ROWS, dtype=jnp.int32) * MOE_ROWS
    fill_groups = (jnp.concatenate([jnp.where(padded > 0, pend - MOE_ROWS, -SUBLANES),
                                    jnp.where(tail < n_slots, tail, -SUBLANES)]) // SUBLANES).astype(jnp.int32)

    xs3 = _dispatch(fill_groups, n_groups, gdst, route_t, h2p, n_slots)
    n_used = (pend[-1:] // MOE_ROWS).astype(jnp.int32)
    ys = _experts(blk_e, blk_valid, n_used, xs3.reshape(n_slots, PACKED), w_gate[0], w_up[0], w_down[0])
    out = _combine(gdst, route, x1, norm_final_w[None, :],
                   ys.reshape(n_slots // SUBLANES, SUBLANES, PACKED))
    return out.reshape(b, s, d)
```

```python
import functools

import numpy as np
import jax
import jax.numpy as jnp
from jax import lax
from jax.experimental import pallas as pl
from jax.experimental.pallas import tpu as pltpu

F32 = jnp.float32
BF16 = jnp.bfloat16
U32 = jnp.uint32

D_MODEL = 1024
N_HEADS = 4
QK_DIM = 128
V_DIM = 256
CHUNK = 128
CONV_WIDTH = 4
ROPE_BASE = 10000.0
N_GROUPS = 4
EXPERTS_PER_GROUP = 8
N_EXPERTS = N_GROUPS * EXPERTS_PER_GROUP
D_EXPERT = 512
NORM_EPS = 1e-6
QK_ALL = N_HEADS * QK_DIM
V_ALL = N_HEADS * V_DIM

MIX_COLS = 2 * QK_ALL + 2 * V_ALL
N_BIG = 2 * MIX_COLS + 2 * D_MODEL
LANES = 128
PACKED = D_MODEL // 2

ROWS_PROJ = 2048
COLS_PROJ = 2048
NORM_ROWS = 512
RET_CHUNKS_PER_STEP = 16
MLSTM_CHUNKS_PER_STEP = 4
ROWS_MERGE = 512
ROUTE_ROWS = 8
SUBLANES = 8
LOCAL_ROWS = 2 * ROWS_MERGE + 256
assert LOCAL_ROWS >= 2 * ROWS_MERGE + N_EXPERTS * (SUBLANES - 1) and LOCAL_ROWS % LANES == 0
LOCAL_GROUPS = LOCAL_ROWS // SUBLANES
GROUP_TABLE = 256
assert GROUP_TABLE >= LOCAL_GROUPS
MOE_ROWS = 1024
FILL_ROWS = 512
V7X_VMEM_BYTES = 64 * 1024 * 1024
VMEM_LIMIT = V7X_VMEM_BYTES - 8 * 1024 * 1024
VMEM_LIMIT_PROJ = V7X_VMEM_BYTES - 4 * 1024 * 1024


def _rms(x, eps=NORM_EPS):
    return x * lax.rsqrt(jnp.mean(x * x, axis=-1, keepdims=True) + eps)


def _pack_rows(x):
    lo = lax.bitcast_convert_type(x[:, :PACKED].astype(BF16).astype(F32), U32)
    hi = lax.bitcast_convert_type(x[:, PACKED:].astype(BF16).astype(F32), U32)
    return (hi & jnp.uint32(0xFFFF0000)) | (lo >> 16)


def _unpack_rows(w):
    lo = lax.bitcast_convert_type(w << 16, F32)
    hi = lax.bitcast_convert_type(w & jnp.uint32(0xFFFF0000), F32)
    return jnp.concatenate([lo, hi], axis=1)


def _in_proj_kernel(x_ref, nw_ref, w_ref, wif_ref, o_ref, gates_ref, h_scr):
    @pl.when(pl.program_id(1) == 0)
    def _():
        for r0 in range(0, x_ref.shape[0], NORM_ROWS):
            rows = slice(r0, r0 + NORM_ROWS)
            h = (_rms(x_ref[rows, :]) * nw_ref[...]).astype(BF16)
            h_scr[rows, :] = h
            gates_ref[rows, :] = jnp.dot(h, wif_ref[...], preferred_element_type=F32)
            o_ref[rows, :] = jnp.dot(h, w_ref[...], preferred_element_type=F32).astype(o_ref.dtype)

    @pl.when(pl.program_id(1) != 0)
    def _():
        o_ref[...] = jnp.dot(h_scr[...], w_ref[...], preferred_element_type=F32).astype(o_ref.dtype)


def _in_projection(x2d, norm_w, w_big, w_if):
    t = x2d.shape[0]
    tm = min(ROWS_PROJ, t)
    tn = COLS_PROJ
    return pl.pallas_call(
        _in_proj_kernel,
        grid=(t // tm, N_BIG // tn),
        in_specs=[
            pl.BlockSpec((tm, D_MODEL), lambda i, j: (i, 0)),
            pl.BlockSpec((1, D_MODEL), lambda i, j: (0, 0)),
            pl.BlockSpec((D_MODEL, tn), lambda i, j: (0, j)),
            pl.BlockSpec((D_MODEL, LANES), lambda i, j: (0, 0)),
        ],
        out_specs=[
            pl.BlockSpec((tm, tn), lambda i, j: (i, j)),
            pl.BlockSpec((tm, LANES), lambda i, j: (i, 0)),
        ],
        out_shape=[
            jax.ShapeDtypeStruct((t, N_BIG), BF16),
            jax.ShapeDtypeStruct((t, LANES), F32),
        ],
        scratch_shapes=[pltpu.VMEM((tm, D_MODEL), BF16)],
        compiler_params=pltpu.CompilerParams(
            dimension_semantics=("arbitrary", "arbitrary"), vmem_limit_bytes=VMEM_LIMIT_PROJ),
        name="in_projection",
    )(x2d, norm_w, w_big, w_if)


def _retention_kernel(p_ref, cos_ref, sin_ref, dq_ref, dk_ref, gn_ref, o_ref, state_scr, *, chunk_decay):
    L = CHUNK

    @pl.when(pl.program_id(1) == 0)
    def _():
        state_scr[...] = jnp.zeros_like(state_scr)

    row = lax.broadcasted_iota(jnp.int32, (L, L), 0)
    col = lax.broadcasted_iota(jnp.int32, (L, L), 1)
    causal = row >= col
    n_chunks = p_ref.shape[0] // L
    units = [(ci * L, h) for ci in range(n_chunks) for h in range(N_HEADS)]

    qts, kts, scores = [], [], []
    for r0, h in units:
        cosf = cos_ref[r0:r0 + L, :]
        sinf = sin_ref[r0:r0 + L, :]
        q = p_ref[r0:r0 + L, h * QK_DIM:(h + 1) * QK_DIM].astype(F32)
        k = p_ref[r0:r0 + L, QK_ALL + h * QK_DIM:QK_ALL + (h + 1) * QK_DIM].astype(F32)
        qt = ((q * cosf + pltpu.roll(q, QK_DIM // 2, 1) * sinf) * dq_ref[h]).astype(BF16)
        kt = ((k * cosf + pltpu.roll(k, QK_DIM // 2, 1) * sinf) * dk_ref[h]).astype(BF16)
        s = lax.dot_general(qt, kt, (((1,), (1,)), ((), ())), preferred_element_type=F32)
        qts.append(qt)
        kts.append(kt)
        scores.append(jnp.where(causal, s, 0.0).astype(BF16))

    states = [state_scr[h] for h in range(N_HEADS)]
    for (r0, h), qt, kt, s in zip(units, qts, kts, scores):
        v = p_ref[r0:r0 + L, 2 * QK_ALL + h * V_DIM:2 * QK_ALL + (h + 1) * V_DIM]
        g = p_ref[r0:r0 + L, 2 * QK_ALL + V_ALL + h * V_DIM:2 * QK_ALL + V_ALL + (h + 1) * V_DIM].astype(F32)
        lhs = jnp.concatenate([s, qt], axis=1)
        rhs = jnp.concatenate([v, states[h].astype(BF16)], axis=0)
        o = jnp.dot(lhs, rhs, preferred_element_type=F32)
        kv = lax.dot_general(kt, v, (((0,), (0,)), ((), ())), preferred_element_type=F32)
        states[h] = (states[h] + kv) * chunk_decay[h]
        y = _rms(o) * gn_ref[:, h * V_DIM:(h + 1) * V_DIM] * (g * jax.nn.sigmoid(g))
        o_ref[r0:r0 + L, h * V_DIM:(h + 1) * V_DIM] = y.astype(o_ref.dtype)
    for h in range(N_HEADS):
        state_scr[h] = states[h]


def _retention(proj3, cosf, sinf, dq, dk, gn_w, chunk_decay):
    b, s, _ = proj3.shape
    lb = CHUNK * min(RET_CHUNKS_PER_STEP, s // CHUNK)
    return pl.pallas_call(
        functools.partial(_retention_kernel, chunk_decay=chunk_decay),
        grid=(b, s // lb),
        in_specs=[
            pl.BlockSpec((None, lb, MIX_COLS), lambda i, c: (i, c, 0)),
            pl.BlockSpec((lb, QK_DIM), lambda i, c: (c, 0)),
            pl.BlockSpec((lb, QK_DIM), lambda i, c: (c, 0)),
            pl.BlockSpec((N_HEADS, CHUNK, QK_DIM), lambda i, c: (0, 0, 0)),
            pl.BlockSpec((N_HEADS, CHUNK, QK_DIM), lambda i, c: (0, 0, 0)),
            pl.BlockSpec((1, V_ALL), lambda i, c: (0, 0)),
        ],
        out_specs=pl.BlockSpec((None, lb, V_ALL), lambda i, c: (i, c, 0)),
        out_shape=jax.ShapeDtypeStruct((b, s, V_ALL), BF16),
        scratch_shapes=[pltpu.VMEM((N_HEADS, QK_DIM, V_DIM), F32)],
        compiler_params=pltpu.CompilerParams(
            dimension_semantics=("arbitrary", "arbitrary"), vmem_limit_bytes=VMEM_LIMIT),
        name="retention",
    )(proj3, cosf, sinf, dq, dk, gn_w)


def _mlstm_kernel(p_ref, gates_ref, cw_ref, cb_ref, shift_ref, gb_ref, gn_ref, o_ref,
                  c_scr, n_scr, m_scr, tail_scr, act_scr):
    L = CHUNK
    lb = p_ref.shape[0]

    @pl.when(pl.program_id(1) == 0)
    def _():
        c_scr[...] = jnp.zeros_like(c_scr)
        n_scr[...] = jnp.zeros_like(n_scr)
        m_scr[...] = jnp.zeros_like(m_scr)
        tail_scr[...] = jnp.zeros_like(tail_scr)

    ub = p_ref[:, 0:2 * QK_ALL]
    u = ub.astype(F32)
    tail = tail_scr[...]
    row8 = lax.broadcasted_iota(jnp.int32, tail.shape, 0)
    acc = u * cw_ref[CONV_WIDTH - 1:CONV_WIDTH, :] + cb_ref[...]
    head = jnp.zeros_like(tail)
    for d in range(1, CONV_WIDTH):
        w_d = cw_ref[CONV_WIDTH - 1 - d:CONV_WIDTH - d, :]
        acc = acc + jnp.dot(shift_ref[d - 1], ub, preferred_element_type=F32) * w_d
        head = head + jnp.where(row8 < d, pltpu.roll(tail, d, 0), 0.0) * w_d
    acc = jnp.concatenate([acc[0:SUBLANES, :] + head, acc[SUBLANES:, :]], axis=0)
    tail_scr[...] = u[lb - SUBLANES:lb, :]
    act_scr[...] = acc * jax.nn.sigmoid(acc)

    row = lax.broadcasted_iota(jnp.int32, (L, L), 0)
    col = lax.broadcasted_iota(jnp.int32, (L, L), 1)
    causal = row >= col
    k_scale = QK_DIM ** -0.5
    units = [(ci * L, h) for ci in range(lb // L) for h in range(N_HEADS)]

    def load_qk(r0, h):
        q = act_scr[r0:r0 + L, h * QK_DIM:(h + 1) * QK_DIM]
        k = act_scr[r0:r0 + L, QK_ALL + h * QK_DIM:QK_ALL + (h + 1) * QK_DIM] * k_scale
        return q, k

    gate_terms, src_rows = [], []
    lane_t = lax.broadcasted_iota(jnp.int32, (2 * N_HEADS, L), 1)
    for ci in range(lb // L):
        pre = gates_ref[ci * L:(ci + 1) * L, :] + gb_ref[...]
        pre_rows = jnp.transpose(pre)[0:2 * N_HEADS, :]
        b_rows = jnp.minimum(pre_rows, 0.0) - jnp.log1p(jnp.exp(-jnp.abs(pre_rows)))
        shift = 1
        while shift < L:
            b_rows = b_rows + jnp.where(lane_t >= shift, pltpu.roll(b_rows, shift, 1), 0.0)
            shift *= 2
        src_rows.append(pre_rows[0:N_HEADS, :] - b_rows[N_HEADS:2 * N_HEADS, :])
        bcum = jnp.transpose(jnp.concatenate([b_rows, jnp.zeros((L - 2 * N_HEADS, L), F32)], axis=0))
        gate_terms.append((pre, bcum))
    scores = []
    for r0, h in units:
        q, k = load_qk(r0, h)
        scores.append(lax.dot_general(q.astype(BF16), k.astype(BF16), (((1,), (1,)), ((), ())),
                                      preferred_element_type=F32))

    m_state = [m_scr[h:h + 1, :] for h in range(N_HEADS)]
    terms = []
    for (r0, h), qk in zip(units, scores):
        pre, bcum = gate_terms[r0 // L]
        q, k = load_qk(r0, h)
        b_t = jnp.broadcast_to(bcum[:, N_HEADS + h:N_HEADS + h + 1], (L, L))
        i_t = jnp.broadcast_to(pre[:, h:h + 1], (L, L))
        src = jnp.broadcast_to(src_rows[r0 // L][h:h + 1, :], (L, L))
        m_prev = m_state[h]
        a = b_t + m_prev
        dmat = jnp.where(causal, b_t + src, -jnp.inf)
        m_t = jnp.maximum(a, jnp.max(dmat, axis=-1, keepdims=True))
        w_inter = jnp.exp(a - m_t)
        s = qk * jnp.exp(dmat - m_t)
        lhs = jnp.concatenate([s.astype(BF16), (q * w_inter).astype(BF16)], axis=1)
        b_last = b_t[L - 1:L, :]
        gk = b_last - b_t + i_t
        m_new = jnp.maximum(b_last + m_prev, jnp.max(gk, axis=0, keepdims=True))
        wk = k * jnp.exp(gk - m_new)
        m_state[h] = m_new
        terms.append(dict(
            lhs=lhs, s_sum=jnp.sum(s, axis=-1, keepdims=True), w_inter=w_inter, floor=jnp.exp(-m_t),
            keep=jnp.exp(b_last + m_prev - m_new), wk=wk.astype(BF16), wk_sum=jnp.sum(wk, axis=0, keepdims=True)))

    n_state = [n_scr[h:h + 1, :] for h in range(N_HEADS)]
    c_state = [c_scr[h] for h in range(N_HEADS)]
    v_cols = lambda h: slice(2 * QK_ALL + h * V_DIM, 2 * QK_ALL + (h + 1) * V_DIM)
    for ci in range(lb // L):
        r0 = ci * L
        chunk_terms = terms[ci * N_HEADS:(ci + 1) * N_HEADS]
        q_dot_n = [jnp.sum(load_qk(r0, h)[0] * n_state[h], axis=-1, keepdims=True) for h in range(N_HEADS)]
        nums = []
        for h, t in enumerate(chunk_terms):
            rhs = jnp.concatenate([p_ref[r0:r0 + L, v_cols(h)], c_state[h].astype(BF16)], axis=0)
            nums.append(jnp.dot(t["lhs"], rhs, preferred_element_type=F32))
        for h, t in enumerate(chunk_terms):
            c_state[h] = c_state[h] * t["keep"][:, 0:1] + lax.dot_general(
                t["wk"], p_ref[r0:r0 + L, v_cols(h)], (((0,), (0,)), ((), ())), preferred_element_type=F32)
            n_state[h] = n_state[h] * t["keep"] + t["wk_sum"]
        for h, t in enumerate(chunk_terms):
            og = p_ref[r0:r0 + L, V_ALL + v_cols(h).start:V_ALL + v_cols(h).stop].astype(F32)
            den = t["s_sum"] + q_dot_n[h] * t["w_inter"]
            inv = 1.0 / jnp.maximum(jnp.abs(den), t["floor"])
            hh = nums[h] * jnp.concatenate([inv] * (V_DIM // LANES), axis=1)
            y = _rms(hh * jax.nn.sigmoid(og)) * gn_ref[:, h * V_DIM:(h + 1) * V_DIM]
            o_ref[r0:r0 + L, h * V_DIM:(h + 1) * V_DIM] = y.astype(o_ref.dtype)

    for h in range(N_HEADS):
        c_scr[h] = c_state[h]
        n_scr[h:h + 1, :] = n_state[h]
        m_scr[h:h + 1, :] = m_state[h]


def _mlstm(proj3, gates3, conv_w, conv_b, gate_bias, gn_w):
    b, s, _ = proj3.shape
    lb = CHUNK * min(MLSTM_CHUNKS_PER_STEP, s // CHUNK)
    shifts = jnp.stack([jnp.eye(lb, k=-d, dtype=BF16) for d in range(1, CONV_WIDTH)])
    return pl.pallas_call(
        _mlstm_kernel,
        grid=(b, s // lb),
        in_specs=[
            pl.BlockSpec((None, lb, MIX_COLS), lambda i, c: (i, c, 1)),
            pl.BlockSpec((None, lb, LANES), lambda i, c: (i, c, 0)),
            pl.BlockSpec((CONV_WIDTH, 2 * QK_ALL), lambda i, c: (0, 0)),
            pl.BlockSpec((1, 2 * QK_ALL), lambda i, c: (0, 0)),
            pl.BlockSpec((CONV_WIDTH - 1, lb, lb), lambda i, c: (0, 0, 0)),
            pl.BlockSpec((1, LANES), lambda i, c: (0, 0)),
            pl.BlockSpec((1, V_ALL), lambda i, c: (0, 0)),
        ],
        out_specs=pl.BlockSpec((None, lb, V_ALL), lambda i, c: (i, c, 0)),
        out_shape=jax.ShapeDtypeStruct((b, s, V_ALL), BF16),
        scratch_shapes=[
            pltpu.VMEM((N_HEADS, QK_DIM, V_DIM), F32),
            pltpu.VMEM((8, QK_DIM), F32),
            pltpu.VMEM((8, LANES), F32),
            pltpu.VMEM((8, 2 * QK_ALL), F32),
            pltpu.VMEM((lb, 2 * QK_ALL), F32),
        ],
        compiler_params=pltpu.CompilerParams(
            dimension_semantics=("arbitrary", "arbitrary"), vmem_limit_bytes=VMEM_LIMIT),
        name="mlstm",
    )(proj3, gates3, conv_w, conv_b, shifts, gate_bias, gn_w)


def _merge_route_kernel(ret_ref, hm_ref, gr_ref, gm_ref, x_ref, wr_ref, wm_ref, wo_ref, nw_ref,
                        wrt_ref, brt_ref, lower_ref, x1_ref, h2_ref, route_ref, route_t_ref, tile_ref,
                        logits_scr):
    step = pl.program_id(0)

    @pl.when(step == 0)
    def _():
        logits_scr[...] = jnp.zeros_like(logits_scr)

    logits = logits_scr[...]

    y_ret = jnp.dot(ret_ref[...], wr_ref[...], preferred_element_type=F32)
    y_m = jnp.dot(hm_ref[...], wm_ref[...], preferred_element_type=F32)
    merged = (jax.nn.sigmoid(gr_ref[...].astype(F32)) * y_ret
              + jax.nn.sigmoid(gm_ref[...].astype(F32)) * y_m)
    x1 = x_ref[...] + jnp.dot(merged.astype(BF16), wo_ref[...], preferred_element_type=F32)
    x1_ref[...] = x1
    h2 = _rms(x1) * nw_ref[...]
    h2_ref[...] = _pack_rows(h2)
    new_logits = jnp.dot(h2.astype(BF16), wrt_ref[...], preferred_element_type=F32) + brt_ref[...]

    live = jnp.where(step > 0, 1.0, 0.0)
    tm = logits.shape[0]
    lane = lax.broadcasted_iota(jnp.int32, (tm, LANES), 1)
    neg = -jnp.inf
    big = jnp.int32(LANES)
    is_group = (lane >= N_EXPERTS) & (lane < N_EXPERTS + N_GROUPS)
    gl = jnp.where(is_group, logits, neg)
    g_max = jnp.max(gl, axis=-1, keepdims=True)
    g_idx = jnp.min(jnp.where(gl == g_max, lane, big), axis=-1, keepdims=True) - N_EXPERTS
    g_w = 1.0 / jnp.sum(jnp.exp(gl - g_max), axis=-1, keepdims=True)
    in_group = (lane >= g_idx * EXPERTS_PER_GROUP) & (lane < (g_idx + 1) * EXPERTS_PER_GROUP)
    el = jnp.where(in_group, logits, neg)
    l1 = jnp.max(el, axis=-1, keepdims=True)
    e1 = jnp.min(jnp.where(el == l1, lane, big), axis=-1, keepdims=True)
    el2 = jnp.where(lane == e1, neg, el)
    l2 = jnp.max(el2, axis=-1, keepdims=True)
    e2 = jnp.min(jnp.where(el2 == l2, lane, big), axis=-1, keepdims=True)
    t21 = jnp.exp(l2 - l1)
    w1 = g_w / (1.0 + t21)
    w2 = g_w * t21 / (1.0 + t21)

    hit1 = lane == e1
    hit2 = lane == e2
    cnt = jnp.where(hit1 | hit2, live, 0.0)
    before = jnp.dot(lower_ref[...], cnt.astype(BF16), preferred_element_type=F32)
    count = jnp.sum(cnt, axis=0, keepdims=True)
    run = jnp.floor((count + (SUBLANES - 1)) * (1.0 / SUBLANES)) * SUBLANES
    lane1 = lax.broadcasted_iota(jnp.int32, (1, LANES), 1)
    run_end = run
    shift = 1
    while shift < N_EXPERTS:
        run_end = run_end + jnp.where(lane1 >= shift, pltpu.roll(run_end, shift, 1), 0.0)
        shift *= 2
    run_start = run_end - run
    local = before + run_start
    r1 = jnp.sum(jnp.where(hit1, local, 0.0), axis=-1, keepdims=True)
    r2 = jnp.sum(jnp.where(hit2, local, 0.0), axis=-1, keepdims=True)
    sub8 = lax.broadcasted_iota(jnp.int32, (SUBLANES, LANES), 0)
    tile_ref[...] = jnp.where(sub8 == 0, count, jnp.where(sub8 == 1, run, jnp.where(sub8 == 2, run_start, 0.0)))

    fields = (e1.astype(F32), e2.astype(F32), r1, r2, w1, w2)
    packed = jnp.zeros((tm, LANES), F32)
    for idx, val in enumerate(fields):
        packed = jnp.where(lane == idx, val, packed)
    route_ref[...] = packed
    route_t_ref[...] = jnp.transpose(packed)[0:ROUTE_ROWS, :]
    logits_scr[...] = new_logits


def _merge_route(ret, hm, proj, x2d, w_ret, w_m, w_out, norm_w, w_router, b_router, lower):
    t = x2d.shape[0]
    tm = min(ROWS_MERGE, t)
    n_tiles = t // tm
    gate_r_blk = 2 * MIX_COLS // D_MODEL
    tile = lambda i: jnp.minimum(i, n_tiles - 1)
    routed = lambda i: jnp.maximum(i - 1, 0)
    row_blk = lambda i: (tile(i), 0)
    const = lambda i: (0, 0)
    return pl.pallas_call(
        _merge_route_kernel,
        grid=(n_tiles + 1,),
        in_specs=[
            pl.BlockSpec((tm, V_ALL), row_blk),
            pl.BlockSpec((tm, V_ALL), row_blk),
            pl.BlockSpec((tm, D_MODEL), lambda i: (tile(i), gate_r_blk)),
            pl.BlockSpec((tm, D_MODEL), lambda i: (tile(i), gate_r_blk + 1)),
            pl.BlockSpec((tm, D_MODEL), row_blk),
            pl.BlockSpec((V_ALL, D_MODEL), const),
            pl.BlockSpec((V_ALL, D_MODEL), const),
            pl.BlockSpec((D_MODEL, D_MODEL), const),
            pl.BlockSpec((1, D_MODEL), const),
            pl.BlockSpec((D_MODEL, LANES), const),
            pl.BlockSpec((1, LANES), const),
            pl.BlockSpec((tm, tm), const),
        ],
        out_specs=[
            pl.BlockSpec((tm, D_MODEL), row_blk),
            pl.BlockSpec((tm, PACKED), row_blk),
            pl.BlockSpec((tm, LANES), lambda i: (routed(i), 0)),
            pl.BlockSpec((ROUTE_ROWS, tm), lambda i: (0, routed(i))),
            pl.BlockSpec((None, SUBLANES, LANES), lambda i: (routed(i), 0, 0)),
        ],
        out_shape=[
            jax.ShapeDtypeStruct((t, D_MODEL), F32),
            jax.ShapeDtypeStruct((t, PACKED), U32),
            jax.ShapeDtypeStruct((t, LANES), F32),
            jax.ShapeDtypeStruct((ROUTE_ROWS, t), F32),
            jax.ShapeDtypeStruct((n_tiles, SUBLANES, LANES), F32),
        ],
        scratch_shapes=[pltpu.VMEM((tm, LANES), F32)],
        compiler_params=pltpu.CompilerParams(
            dimension_semantics=("arbitrary",), vmem_limit_bytes=VMEM_LIMIT),
        name="merge_route",
    )(ret, hm, proj, proj, x2d, w_ret, w_m, w_out, norm_w, w_router, b_router, lower)


def _dispatch_kernel(fill_ref, n_groups_ref, gdst_ref, route_t_ref, h2_ref, xs_hbm, sorted_scr, zero_scr, sems):
    step = pl.program_id(0)
    n_tiles = pl.num_programs(0) - 1
    tm = h2_ref.shape[0]
    slot = step % 2

    def group_copy(which, j):
        return pltpu.make_async_copy(sorted_scr.at[which, pl.ds(j, 1)], xs_hbm.at[pl.ds(gdst_ref[j], 1)],
                                     sems.at[which])

    def wait_groups(which, count):
        def wait(j, carry):
            group_copy(which, 0).wait()
            return carry
        lax.fori_loop(0, count, wait, 0)

    @pl.when(step == 0)
    def _():
        zero_scr[...] = jnp.zeros_like(zero_scr)

        def fill_copy(j):
            g0 = pl.multiple_of(fill_ref[j], FILL_ROWS // SUBLANES)
            return pltpu.make_async_copy(zero_scr, xs_hbm.at[pl.ds(g0, FILL_ROWS // SUBLANES)], sems.at[0])

        def start_fill(j, carry):
            @pl.when(fill_ref[j] >= 0)
            def _():
                fill_copy(j).start()
            return carry

        def wait_fill(j, carry):
            @pl.when(fill_ref[j] >= 0)
            def _():
                fill_copy(j).wait()
            return carry

        lax.fori_loop(0, fill_ref.shape[0], start_fill, 0)
        lax.fori_loop(0, fill_ref.shape[0], wait_fill, 0)

    @pl.when(step >= 1)
    def _():
        def start(j, carry):
            group_copy(1 - slot, j).start()
            return carry
        lax.fori_loop(0, n_groups_ref[jnp.maximum(step - 1, 0)], start, 0)

    @pl.when(step >= 2)
    def _():
        wait_groups(slot, n_groups_ref[jnp.maximum(step - 2, 0)])

    @pl.when(step < n_tiles)
    def _():
        pos = lax.broadcasted_iota(jnp.int32, (LOCAL_ROWS, tm), 0)
        row1 = route_t_ref[2:3, :].astype(jnp.int32)
        row2 = route_t_ref[3:4, :].astype(jnp.int32)
        pick = jnp.where((pos == row1) | (pos == row2), 1.0, 0.0).astype(BF16)
        tokens = _unpack_rows(h2_ref[...]).astype(BF16)
        ordered = jnp.dot(pick, tokens, preferred_element_type=F32)
        sorted_scr[slot] = _pack_rows(ordered).reshape(LOCAL_GROUPS, SUBLANES, PACKED)

    @pl.when(step == n_tiles)
    def _():
        wait_groups(1 - slot, n_groups_ref[jnp.maximum(step - 1, 0)])


def _dispatch(fill_groups, n_groups, gdst, route_t, h2p, n_slots):
    t = h2p.shape[0]
    tm = min(ROWS_MERGE, t)
    n_tiles = t // tm
    sorted_tile = lambda i: jnp.minimum(i, n_tiles - 1)
    return pl.pallas_call(
        _dispatch_kernel,
        grid=(n_tiles + 1,),
        in_specs=[
            pl.BlockSpec(memory_space=pltpu.SMEM),
            pl.BlockSpec(memory_space=pltpu.SMEM),
            pl.BlockSpec((GROUP_TABLE,), lambda i: (jnp.maximum(i - 1, 0),), memory_space=pltpu.SMEM),
            pl.BlockSpec((ROUTE_ROWS, tm), lambda i: (0, sorted_tile(i))),
            pl.BlockSpec((tm, PACKED), lambda i: (sorted_tile(i), 0)),
        ],
        out_specs=pl.BlockSpec(memory_space=pl.ANY),
        out_shape=jax.ShapeDtypeStruct((n_slots // SUBLANES, SUBLANES, PACKED), U32),
        scratch_shapes=[
            pltpu.VMEM((2, LOCAL_GROUPS, SUBLANES, PACKED), U32),
            pltpu.VMEM((FILL_ROWS // SUBLANES, SUBLANES, PACKED), U32),
            pltpu.SemaphoreType.DMA((2,)),
        ],
        compiler_params=pltpu.CompilerParams(
            dimension_semantics=("arbitrary",), vmem_limit_bytes=VMEM_LIMIT),
        name="dispatch",
    )(fill_groups, n_groups, gdst, route_t, h2p)


def _experts_kernel(blk_e_ref, blk_valid_ref, n_used_ref, xs_ref, wg_ref, wu_ref, wd_ref, ys_ref, wgu_scr, wd_scr):
    del n_used_ref
    i = pl.program_id(0)
    valid = blk_valid_ref[i]

    @pl.when((i == 0) | (blk_e_ref[i] != blk_e_ref[jnp.maximum(i - 1, 0)]))
    def _():
        wgu_scr[:, :D_EXPERT] = wg_ref[...].astype(BF16)
        wgu_scr[:, D_EXPERT:] = wu_ref[...].astype(BF16)
        wd_scr[...] = wd_ref[...].astype(BF16)

    @pl.when(valid > 0)
    def _():
        xb = _unpack_rows(xs_ref[...]).astype(BF16)
        gu = jnp.dot(xb, wgu_scr[...], preferred_element_type=F32)
        g = gu[:, :D_EXPERT]
        act = (g * jax.nn.sigmoid(g) * gu[:, D_EXPERT:]).astype(BF16)
        ys_ref[...] = _pack_rows(jnp.dot(act, wd_scr[...], preferred_element_type=F32))

    @pl.when(valid <= 0)
    def _():
        ys_ref[...] = jnp.zeros_like(ys_ref)


def _experts(blk_e, blk_valid, n_used, xs, w_gate, w_up, w_down):
    p = xs.shape[0]
    per_expert = lambda i, be, bv, nu: (be[i], 0, 0)
    grid_spec = pltpu.PrefetchScalarGridSpec(
        num_scalar_prefetch=3,
        grid=(p // MOE_ROWS,),
        in_specs=[
            pl.BlockSpec((MOE_ROWS, PACKED), lambda i, be, bv, nu: (jnp.minimum(i, nu[0] - 1), 0)),
            pl.BlockSpec((None, D_MODEL, D_EXPERT), per_expert),
            pl.BlockSpec((None, D_MODEL, D_EXPERT), per_expert),
            pl.BlockSpec((None, D_EXPERT, D_MODEL), per_expert),
        ],
        out_specs=pl.BlockSpec((MOE_ROWS, PACKED), lambda i, be, bv, nu: (i, 0)),
        scratch_shapes=[pltpu.VMEM((D_MODEL, 2 * D_EXPERT), BF16), pltpu.VMEM((D_EXPERT, D_MODEL), BF16)],
    )
    return pl.pallas_call(
        _experts_kernel,
        grid_spec=grid_spec,
        out_shape=jax.ShapeDtypeStruct((p, PACKED), U32),
        compiler_params=pltpu.CompilerParams(
            dimension_semantics=("arbitrary",), vmem_limit_bytes=VMEM_LIMIT),
        name="experts",
    )(blk_e, blk_valid, n_used, xs, w_gate, w_up, w_down)


def _combine_kernel(gdst_ref, route_ref, x1_ref, nw_ref, ys_hbm, o_ref, buf, sems):
    step = pl.program_id(0)
    n_tiles = pl.num_programs(0) - 1
    tm = x1_ref.shape[0]
    slot = step % 2
    prev = 1 - slot

    @pl.when(step < n_tiles)
    def _():
        def start(j, carry):
            pltpu.make_async_copy(ys_hbm.at[pl.ds(gdst_ref[j], 1)], buf.at[slot, pl.ds(j, 1)], sems.at[slot]).start()
            return carry
        lax.fori_loop(0, LOCAL_GROUPS, start, 0, unroll=8)

    @pl.when(step > 0)
    def _():
        pltpu.make_async_copy(ys_hbm.at[pl.ds(0, LOCAL_GROUPS)], buf.at[prev], sems.at[prev]).wait()

        rows = _unpack_rows(buf[prev].reshape(LOCAL_ROWS, PACKED)).astype(BF16)
        pos = lax.broadcasted_iota(jnp.int32, (tm, LOCAL_ROWS), 1)
        row1 = route_ref[:, 2:3].astype(jnp.int32)
        row2 = route_ref[:, 3:4].astype(jnp.int32)
        mix = (jnp.where(pos == row1, route_ref[:, 4:5], 0.0)
               + jnp.where(pos == row2, route_ref[:, 5:6], 0.0)).astype(BF16)
        x2 = x1_ref[...] + jnp.dot(mix, rows, preferred_element_type=F32)
        o_ref[...] = _rms(x2) * nw_ref[...]


def _combine(gdst, route, x1, norm_w, ys3):
    t = x1.shape[0]
    tm = min(ROWS_MERGE, t)
    n_tiles = t // tm
    gathered = lambda i: (jnp.minimum(i, n_tiles - 1),)
    finished = lambda i: (jnp.maximum(i - 1, 0), 0)
    return pl.pallas_call(
        _combine_kernel,
        grid=(n_tiles + 1,),
        in_specs=[
            pl.BlockSpec((GROUP_TABLE,), gathered, memory_space=pltpu.SMEM),
            pl.BlockSpec((tm, LANES), finished),
            pl.BlockSpec((tm, D_MODEL), finished),
            pl.BlockSpec((1, D_MODEL), lambda i: (0, 0)),
            pl.BlockSpec(memory_space=pl.ANY),
        ],
        out_specs=pl.BlockSpec((tm, D_MODEL), finished),
        out_shape=jax.ShapeDtypeStruct((t, D_MODEL), F32),
        scratch_shapes=[
            pltpu.VMEM((2, LOCAL_GROUPS, SUBLANES, PACKED), U32),
            pltpu.SemaphoreType.DMA((2,)),
        ],
        compiler_params=pltpu.CompilerParams(
            dimension_semantics=("arbitrary",), vmem_limit_bytes=VMEM_LIMIT),
        name="combine",
    )(gdst, route, x1, norm_w, ys3)


def _rotary_tables(seq):
    inv_freq = 1.0 / (ROPE_BASE ** (jnp.arange(0, QK_DIM, 2, dtype=F32) / QK_DIM))
    ang = jnp.arange(seq, dtype=F32)[:, None] * inv_freq[None, :]
    cos, sin = jnp.cos(ang), jnp.sin(ang)
    return jnp.concatenate([cos, cos], axis=1), jnp.concatenate([-sin, sin], axis=1)


def _retention_decay_tables():
    gamma = 1.0 - 2.0 ** (-5.0 - np.arange(N_HEADS, dtype=np.float64))
    idx = np.arange(CHUNK, dtype=np.float64) + 1.0
    dq = gamma[:, None] ** idx[None, :]
    dk = gamma[:, None] ** (-idx[None, :]) * QK_DIM ** -0.5
    bcast = lambda a: jnp.asarray(np.broadcast_to(a[:, :, None], (N_HEADS, CHUNK, QK_DIM)), F32)
    return bcast(dq), bcast(dk), tuple(float(g) for g in gamma ** CHUNK)


def kernel(x, norm_mix_w, w_in, ret_gn_w, w_ret_branch, mlstm_conv_w, mlstm_conv_b, b_igate, b_fgate,
           mlstm_gn_w, w_mlstm_branch, w_out, norm_ffn_w, w_group, b_group, w_expert_router,
           b_expert_router, w_gate, w_up, w_down, norm_final_w):
    assert norm_mix_w.shape[0] == 1, "one layer"
    b, s, d = x.shape
    t = b * s
    assert d == D_MODEL and s % CHUNK == 0

    wi = w_in[0]
    n_pre = 2 * MIX_COLS
    w_big = jnp.concatenate([wi[:, :n_pre], wi[:, n_pre + 2 * N_HEADS:]], axis=1).astype(BF16)
    w_if = jnp.pad(wi[:, n_pre:n_pre + 2 * N_HEADS], ((0, 0), (0, LANES - 2 * N_HEADS))).astype(BF16)
    gate_bias = jnp.pad(jnp.concatenate([b_igate[0], b_fgate[0]]), (0, LANES - 2 * N_HEADS))[None, :]
    w_router = jnp.pad(jnp.concatenate([w_expert_router[0], w_group[0]], axis=1),
                       ((0, 0), (0, LANES - N_EXPERTS - N_GROUPS))).astype(BF16)
    b_router = jnp.pad(jnp.concatenate([b_expert_router[0], b_group[0]]),
                       (0, LANES - N_EXPERTS - N_GROUPS))[None, :]

    cosf, sinf = _rotary_tables(s)
    dq, dk, chunk_decay = _retention_decay_tables()
    tm_merge = min(ROWS_MERGE, t)
    lower = jnp.tril(jnp.ones((tm_merge, tm_merge), F32), -1).astype(BF16)

    x2d = x.reshape(t, d)
    proj, gates = _in_projection(x2d, norm_mix_w, w_big, w_if)
    proj3 = proj.reshape(b, s, N_BIG)
    ret = _retention(proj3, cosf, sinf, dq, dk, ret_gn_w, chunk_decay)
    hm = _mlstm(proj3, gates.reshape(b, s, LANES), mlstm_conv_w[0, :, 0, :], mlstm_conv_b,
                gate_bias, mlstm_gn_w)
    x1, h2p, route, route_t, tiles = _merge_route(
        ret.reshape(t, V_ALL), hm.reshape(t, V_ALL), proj, x2d, w_ret_branch[0].astype(BF16),
        w_mlstm_branch[0].astype(BF16), w_out[0].astype(BF16), norm_ffn_w, w_router, b_router, lower)

    n_tiles = t // tm_merge
    n_slots = 2 * t + N_EXPERTS * (MOE_ROWS + n_tiles * (SUBLANES - 1) // SUBLANES * SUBLANES)
    n_slots = -(-n_slots // MOE_ROWS) * MOE_ROWS
    run = tiles[:, 1, :N_EXPERTS].astype(jnp.int32)
    run_start = tiles[:, 2, :N_EXPERTS].astype(jnp.int32)
    rows_e = jnp.sum(run, axis=0)
    padded = (rows_e + MOE_ROWS - 1) // MOE_ROWS * MOE_ROWS
    expert_ids = jnp.arange(N_EXPERTS, dtype=jnp.int32)
    pstart = jnp.sum(jnp.where(expert_ids[None, :] < expert_ids[:, None], padded[None, :], 0), axis=1)
    pend = pstart + padded
    tile_ids = jnp.arange(n_tiles, dtype=jnp.int32)
    run_t = run.T
    earlier = jnp.sum(jnp.where((tile_ids[None, :] < tile_ids[:, None])[None], run_t[:, None, :], 0), axis=2).T
    global_start = pstart[None, :] + earlier
    local_row = jnp.arange(GROUP_TABLE, dtype=jnp.int32)[None, None, :] * SUBLANES
    in_run = (local_row >= run_start[:, :, None]) & (local_row < (run_start + run)[:, :, None])
    gdst = jnp.sum(jnp.where(in_run, (global_start - run_start)[:, :, None] + local_row, 0), axis=1) // SUBLANES
    gdst = gdst.reshape(n_tiles * GROUP_TABLE).astype(jnp.int32)
    n_groups = (jnp.sum(run, axis=1) // SUBLANES).astype(jnp.int32)
    blk_start = jnp.arange(n_slots // MOE_ROWS, dtype=jnp.int32) * MOE_ROWS
    blk_e = jnp.minimum(jnp.sum(blk_start[:, None] >= pend[None, :], axis=-1), N_EXPERTS - 1).astype(jnp.int32)
    in_expert = (blk_start[:, None] >= pstart[None, :]) & (blk_start[:, None] < pend[None, :])
    rows_end = jnp.sum(jnp.where(in_expert, (pstart + rows_e)[None, :], 0), axis=1)
    blk_valid = jnp.clip(rows_end - blk_start, 0, MOE_ROWS).astype(jnp.int32)
    first = (pstart + rows_e) // FILL_ROWS * FILL_ROWS
    tail = pend[-1] + jnp.arange((n_slots - 2 * t) // FILL_ROWS, dtype=jnp.int32) * FILL_ROWS
    fill_groups = (jnp.concatenate([jnp.where(first < pend, first, -SUBLANES),
                                    jnp.where(first + FILL_ROWS < pend, first + FILL_ROWS, -SUBLANES),
                                    jnp.where(tail < n_slots, tail, -SUBLANES)]) // SUBLANES).astype(jnp.int32)

    xs3 = _dispatch(fill_groups, n_groups, gdst, route_t, h2p, n_slots)
    n_used = (pend[-1:] // MOE_ROWS).astype(jnp.int32)
    ys = _experts(blk_e, blk_valid, n_used, xs3.reshape(n_slots, PACKED), w_gate[0], w_up[0], w_down[0])
    out = _combine(gdst, route, x1, norm_final_w[None, :],
                   ys.reshape(n_slots // SUBLANES, SUBLANES, PACKED))
    return out.reshape(b, s, d)
```

```python
import functools

import numpy as np
import jax
import jax.numpy as jnp
from jax import lax
from jax.experimental import pallas as pl
from jax.experimental.pallas import tpu as pltpu

F32 = jnp.float32
BF16 = jnp.bfloat16
U32 = jnp.uint32

D_MODEL = 1024
N_HEADS = 4
QK_DIM = 128
V_DIM = 256
CHUNK = 128
CONV_WIDTH = 4
ROPE_BASE = 10000.0
N_GROUPS = 4
EXPERTS_PER_GROUP = 8
N_EXPERTS = N_GROUPS * EXPERTS_PER_GROUP
D_EXPERT = 512
NORM_EPS = 1e-6
QK_ALL = N_HEADS * QK_DIM
V_ALL = N_HEADS * V_DIM

MIX_COLS = 2 * QK_ALL + 2 * V_ALL
N_BIG = 2 * MIX_COLS + 2 * D_MODEL
LANES = 128
PACKED = D_MODEL // 2

ROWS_PROJ = 2048
COLS_PROJ = 2048
NORM_ROWS = 512
RET_CHUNKS_PER_STEP = 16
MLSTM_CHUNKS_PER_STEP = 4
ROWS_MERGE = 512
ROUTE_ROWS = 8
SUBLANES = 8
LOCAL_ROWS = 2 * ROWS_MERGE + 256
assert LOCAL_ROWS >= 2 * ROWS_MERGE + N_EXPERTS * (SUBLANES - 1) and LOCAL_ROWS % LANES == 0
LOCAL_GROUPS = LOCAL_ROWS // SUBLANES
GROUP_TABLE = 256
assert GROUP_TABLE >= LOCAL_GROUPS
MOE_ROWS = 1024
FILL_ROWS = 512
V7X_VMEM_BYTES = 64 * 1024 * 1024
VMEM_LIMIT = V7X_VMEM_BYTES - 8 * 1024 * 1024
VMEM_LIMIT_PROJ = V7X_VMEM_BYTES - 4 * 1024 * 1024


def _rms(x, eps=NORM_EPS):
    return x * lax.rsqrt(jnp.mean(x * x, axis=-1, keepdims=True) + eps)


def _pack_rows(x):
    lo = lax.bitcast_convert_type(x[:, :PACKED].astype(BF16).astype(F32), U32)
    hi = lax.bitcast_convert_type(x[:, PACKED:].astype(BF16).astype(F32), U32)
    return (hi & jnp.uint32(0xFFFF0000)) | (lo >> 16)


def _unpack_rows(w):
    lo = lax.bitcast_convert_type(w << 16, F32)
    hi = lax.bitcast_convert_type(w & jnp.uint32(0xFFFF0000), F32)
    return jnp.concatenate([lo, hi], axis=1)


def _in_proj_kernel(x_ref, nw_ref, w_ref, wif_ref, o_ref, gates_ref, h_scr):
    @pl.when(pl.program_id(1) == 0)
    def _():
        for r0 in range(0, x_ref.shape[0], NORM_ROWS):
            rows = slice(r0, r0 + NORM_ROWS)
            h = (_rms(x_ref[rows, :]) * nw_ref[...]).astype(BF16)
            h_scr[rows, :] = h
            gates_ref[rows, :] = jnp.dot(h, wif_ref[...], preferred_element_type=F32)
            o_ref[rows, :] = jnp.dot(h, w_ref[...], preferred_element_type=F32).astype(o_ref.dtype)

    @pl.when(pl.program_id(1) != 0)
    def _():
        o_ref[...] = jnp.dot(h_scr[...], w_ref[...], preferred_element_type=F32).astype(o_ref.dtype)


def _in_projection(x2d, norm_w, w_big, w_if):
    t = x2d.shape[0]
    tm = min(ROWS_PROJ, t)
    tn = COLS_PROJ
    return pl.pallas_call(
        _in_proj_kernel,
        grid=(t // tm, N_BIG // tn),
        in_specs=[
            pl.BlockSpec((tm, D_MODEL), lambda i, j: (i, 0)),
            pl.BlockSpec((1, D_MODEL), lambda i, j: (0, 0)),
            pl.BlockSpec((D_MODEL, tn), lambda i, j: (0, j)),
            pl.BlockSpec((D_MODEL, LANES), lambda i, j: (0, 0)),
        ],
        out_specs=[
            pl.BlockSpec((tm, tn), lambda i, j: (i, j)),
            pl.BlockSpec((tm, LANES), lambda i, j: (i, 0)),
        ],
        out_shape=[
            jax.ShapeDtypeStruct((t, N_BIG), BF16),
            jax.ShapeDtypeStruct((t, LANES), F32),
        ],
        scratch_shapes=[pltpu.VMEM((tm, D_MODEL), BF16)],
        compiler_params=pltpu.CompilerParams(
            dimension_semantics=("arbitrary", "arbitrary"), vmem_limit_bytes=VMEM_LIMIT_PROJ),
        name="in_projection",
    )(x2d, norm_w, w_big, w_if)


def _retention_kernel(p_ref, cos_ref, sin_ref, dq_ref, dk_ref, gn_ref, o_ref, state_scr, *, chunk_decay):
    L = CHUNK

    @pl.when(pl.program_id(1) == 0)
    def _():
        state_scr[...] = jnp.zeros_like(state_scr)

    row = lax.broadcasted_iota(jnp.int32, (L, L), 0)
    col = lax.broadcasted_iota(jnp.int32, (L, L), 1)
    causal = row >= col
    n_chunks = p_ref.shape[0] // L
    units = [(ci * L, h) for ci in range(n_chunks) for h in range(N_HEADS)]

    qts, kts, scores = [], [], []
    for r0, h in units:
        cosf = cos_ref[r0:r0 + L, :]
        sinf = sin_ref[r0:r0 + L, :]
        q = p_ref[r0:r0 + L, h * QK_DIM:(h + 1) * QK_DIM].astype(F32)
        k = p_ref[r0:r0 + L, QK_ALL + h * QK_DIM:QK_ALL + (h + 1) * QK_DIM].astype(F32)
        qt = ((q * cosf + pltpu.roll(q, QK_DIM // 2, 1) * sinf) * dq_ref[h]).astype(BF16)
        kt = ((k * cosf + pltpu.roll(k, QK_DIM // 2, 1) * sinf) * dk_ref[h]).astype(BF16)
        s = lax.dot_general(qt, kt, (((1,), (1,)), ((), ())), preferred_element_type=F32)
        qts.append(qt)
        kts.append(kt)
        scores.append(jnp.where(causal, s, 0.0).astype(BF16))

    states = [state_scr[h] for h in range(N_HEADS)]
    for (r0, h), qt, kt, s in zip(units, qts, kts, scores):
        v = p_ref[r0:r0 + L, 2 * QK_ALL + h * V_DIM:2 * QK_ALL + (h + 1) * V_DIM]
        g = p_ref[r0:r0 + L, 2 * QK_ALL + V_ALL + h * V_DIM:2 * QK_ALL + V_ALL + (h + 1) * V_DIM].astype(F32)
        lhs = jnp.concatenate([s, qt], axis=1)
        rhs = jnp.concatenate([v, states[h].astype(BF16)], axis=0)
        o = jnp.dot(lhs, rhs, preferred_element_type=F32)
        kv = lax.dot_general(kt, v, (((0,), (0,)), ((), ())), preferred_element_type=F32)
        states[h] = (states[h] + kv) * chunk_decay[h]
        y = _rms(o) * gn_ref[:, h * V_DIM:(h + 1) * V_DIM] * (g * jax.nn.sigmoid(g))
        o_ref[r0:r0 + L, h * V_DIM:(h + 1) * V_DIM] = y.astype(o_ref.dtype)
    for h in range(N_HEADS):
        state_scr[h] = states[h]


def _retention(proj3, cosf, sinf, dq, dk, gn_w, chunk_decay):
    b, s, _ = proj3.shape
    lb = CHUNK * min(RET_CHUNKS_PER_STEP, s // CHUNK)
    return pl.pallas_call(
        functools.partial(_retention_kernel, chunk_decay=chunk_decay),
        grid=(b, s // lb),
        in_specs=[
            pl.BlockSpec((None, lb, MIX_COLS), lambda i, c: (i, c, 0)),
            pl.BlockSpec((lb, QK_DIM), lambda i, c: (c, 0)),
            pl.BlockSpec((lb, QK_DIM), lambda i, c: (c, 0)),
            pl.BlockSpec((N_HEADS, CHUNK, QK_DIM), lambda i, c: (0, 0, 0)),
            pl.BlockSpec((N_HEADS, CHUNK, QK_DIM), lambda i, c: (0, 0, 0)),
            pl.BlockSpec((1, V_ALL), lambda i, c: (0, 0)),
        ],
        out_specs=pl.BlockSpec((None, lb, V_ALL), lambda i, c: (i, c, 0)),
        out_shape=jax.ShapeDtypeStruct((b, s, V_ALL), BF16),
        scratch_shapes=[pltpu.VMEM((N_HEADS, QK_DIM, V_DIM), F32)],
        compiler_params=pltpu.CompilerParams(
            dimension_semantics=("arbitrary", "arbitrary"), vmem_limit_bytes=VMEM_LIMIT),
        name="retention",
    )(proj3, cosf, sinf, dq, dk, gn_w)


def _mlstm_kernel(p_ref, gates_ref, cw_ref, cb_ref, shift_ref, gb_ref, gn_ref, o_ref,
                  c_scr, n_scr, m_scr, tail_scr, act_scr):
    L = CHUNK
    lb = p_ref.shape[0]

    @pl.when(pl.program_id(1) == 0)
    def _():
        c_scr[...] = jnp.zeros_like(c_scr)
        n_scr[...] = jnp.zeros_like(n_scr)
        m_scr[...] = jnp.zeros_like(m_scr)
        tail_scr[...] = jnp.zeros_like(tail_scr)

    ub = p_ref[:, 0:2 * QK_ALL]
    u = ub.astype(F32)
    tail = tail_scr[...]
    row8 = lax.broadcasted_iota(jnp.int32, tail.shape, 0)
    acc = u * cw_ref[CONV_WIDTH - 1:CONV_WIDTH, :] + cb_ref[...]
    head = jnp.zeros_like(tail)
    for d in range(1, CONV_WIDTH):
        w_d = cw_ref[CONV_WIDTH - 1 - d:CONV_WIDTH - d, :]
        acc = acc + jnp.dot(shift_ref[d - 1], ub, preferred_element_type=F32) * w_d
        head = head + jnp.where(row8 < d, pltpu.roll(tail, d, 0), 0.0) * w_d
    acc = jnp.concatenate([acc[0:SUBLANES, :] + head, acc[SUBLANES:, :]], axis=0)
    tail_scr[...] = u[lb - SUBLANES:lb, :]
    act_scr[...] = acc * jax.nn.sigmoid(acc)

    row = lax.broadcasted_iota(jnp.int32, (L, L), 0)
    col = lax.broadcasted_iota(jnp.int32, (L, L), 1)
    causal = row >= col
    k_scale = QK_DIM ** -0.5
    units = [(ci * L, h) for ci in range(lb // L) for h in range(N_HEADS)]

    def load_qk(r0, h):
        q = act_scr[r0:r0 + L, h * QK_DIM:(h + 1) * QK_DIM]
        k = act_scr[r0:r0 + L, QK_ALL + h * QK_DIM:QK_ALL + (h + 1) * QK_DIM] * k_scale
        return q, k

    gate_terms, src_rows = [], []
    lane_t = lax.broadcasted_iota(jnp.int32, (2 * N_HEADS, L), 1)
    for ci in range(lb // L):
        pre = gates_ref[ci * L:(ci + 1) * L, :] + gb_ref[...]
        pre_rows = jnp.transpose(pre)[0:2 * N_HEADS, :]
        b_rows = jnp.minimum(pre_rows, 0.0) - jnp.log1p(jnp.exp(-jnp.abs(pre_rows)))
        shift = 1
        while shift < L:
            b_rows = b_rows + jnp.where(lane_t >= shift, pltpu.roll(b_rows, shift, 1), 0.0)
            shift *= 2
        src_rows.append(pre_rows[0:N_HEADS, :] - b_rows[N_HEADS:2 * N_HEADS, :])
        bcum = jnp.transpose(jnp.concatenate([b_rows, jnp.zeros((L - 2 * N_HEADS, L), F32)], axis=0))
        gate_terms.append((pre, bcum))
    scores = []
    for r0, h in units:
        q, k = load_qk(r0, h)
        scores.append(lax.dot_general(q.astype(BF16), k.astype(BF16), (((1,), (1,)), ((), ())),
                                      preferred_element_type=F32))

    m_state = [m_scr[h:h + 1, :] for h in range(N_HEADS)]
    terms = []
    for (r0, h), qk in zip(units, scores):
        pre, bcum = gate_terms[r0 // L]
        q, k = load_qk(r0, h)
        b_t = jnp.broadcast_to(bcum[:, N_HEADS + h:N_HEADS + h + 1], (L, L))
        i_t = jnp.broadcast_to(pre[:, h:h + 1], (L, L))
        src = jnp.broadcast_to(src_rows[r0 // L][h:h + 1, :], (L, L))
        m_prev = m_state[h]
        a = b_t + m_prev
        dmat = jnp.where(causal, b_t + src, -jnp.inf)
        m_t = jnp.maximum(a, jnp.max(dmat, axis=-1, keepdims=True))
        w_inter = jnp.exp(a - m_t)
        s = qk * jnp.exp(dmat - m_t)
        lhs = jnp.concatenate([s.astype(BF16), (q * w_inter).astype(BF16)], axis=1)
        b_last = b_t[L - 1:L, :]
        gk = b_last - b_t + i_t
        m_new = jnp.maximum(b_last + m_prev, jnp.max(gk, axis=0, keepdims=True))
        wk = k * jnp.exp(gk - m_new)
        m_state[h] = m_new
        terms.append(dict(
            lhs=lhs, s_sum=jnp.sum(s, axis=-1, keepdims=True), w_inter=w_inter, floor=jnp.exp(-m_t),
            keep=jnp.exp(b_last + m_prev - m_new), wk=wk.astype(BF16), wk_sum=jnp.sum(wk, axis=0, keepdims=True)))

    n_state = [n_scr[h:h + 1, :] for h in range(N_HEADS)]
    c_state = [c_scr[h] for h in range(N_HEADS)]
    v_cols = lambda h: slice(2 * QK_ALL + h * V_DIM, 2 * QK_ALL + (h + 1) * V_DIM)
    for ci in range(lb // L):
        r0 = ci * L
        chunk_terms = terms[ci * N_HEADS:(ci + 1) * N_HEADS]
        q_dot_n = [jnp.sum(load_qk(r0, h)[0] * n_state[h], axis=-1, keepdims=True) for h in range(N_HEADS)]
        nums = []
        for h, t in enumerate(chunk_terms):
            rhs = jnp.concatenate([p_ref[r0:r0 + L, v_cols(h)], c_state[h].astype(BF16)], axis=0)
            nums.append(jnp.dot(t["lhs"], rhs, preferred_element_type=F32))
        for h, t in enumerate(chunk_terms):
            c_state[h] = c_state[h] * t["keep"][:, 0:1] + lax.dot_general(
                t["wk"], p_ref[r0:r0 + L, v_cols(h)], (((0,), (0,)), ((), ())), preferred_element_type=F32)
            n_state[h] = n_state[h] * t["keep"] + t["wk_sum"]
        for h, t in enumerate(chunk_terms):
            og = p_ref[r0:r0 + L, V_ALL + v_cols(h).start:V_ALL + v_cols(h).stop].astype(F32)
            den = t["s_sum"] + q_dot_n[h] * t["w_inter"]
            inv = 1.0 / jnp.maximum(jnp.abs(den), t["floor"])
            hh = nums[h] * jnp.concatenate([inv] * (V_DIM // LANES), axis=1)
            y = _rms(hh * jax.nn.sigmoid(og)) * gn_ref[:, h * V_DIM:(h + 1) * V_DIM]
            o_ref[r0:r0 + L, h * V_DIM:(h + 1) * V_DIM] = y.astype(o_ref.dtype)

    for h in range(N_HEADS):
        c_scr[h] = c_state[h]
        n_scr[h:h + 1, :] = n_state[h]
        m_scr[h:h + 1, :] = m_state[h]


def _mlstm(proj3, gates3, conv_w, conv_b, gate_bias, gn_w):
    b, s, _ = proj3.shape
    lb = CHUNK * min(MLSTM_CHUNKS_PER_STEP, s // CHUNK)
    shifts = jnp.stack([jnp.eye(lb, k=-d, dtype=BF16) for d in range(1, CONV_WIDTH)])
    return pl.pallas_call(
        _mlstm_kernel,
        grid=(b, s // lb),
        in_specs=[
            pl.BlockSpec((None, lb, MIX_COLS), lambda i, c: (i, c, 1)),
            pl.BlockSpec((None, lb, LANES), lambda i, c: (i, c, 0)),
            pl.BlockSpec((CONV_WIDTH, 2 * QK_ALL), lambda i, c: (0, 0)),
            pl.BlockSpec((1, 2 * QK_ALL), lambda i, c: (0, 0)),
            pl.BlockSpec((CONV_WIDTH - 1, lb, lb), lambda i, c: (0, 0, 0)),
            pl.BlockSpec((1, LANES), lambda i, c: (0, 0)),
            pl.BlockSpec((1, V_ALL), lambda i, c: (0, 0)),
        ],
        out_specs=pl.BlockSpec((None, lb, V_ALL), lambda i, c: (i, c, 0)),
        out_shape=jax.ShapeDtypeStruct((b, s, V_ALL), BF16),
        scratch_shapes=[
            pltpu.VMEM((N_HEADS, QK_DIM, V_DIM), F32),
            pltpu.VMEM((8, QK_DIM), F32),
            pltpu.VMEM((8, LANES), F32),
            pltpu.VMEM((8, 2 * QK_ALL), F32),
            pltpu.VMEM((lb, 2 * QK_ALL), F32),
        ],
        compiler_params=pltpu.CompilerParams(
            dimension_semantics=("arbitrary", "arbitrary"), vmem_limit_bytes=VMEM_LIMIT),
        name="mlstm",
    )(proj3, gates3, conv_w, conv_b, shifts, gate_bias, gn_w)


def _merge_route_kernel(ret_ref, hm_ref, gr_ref, gm_ref, x_ref, wr_ref, wm_ref, wo_ref, nw_ref,
                        wrt_ref, brt_ref, lower_ref, x1_ref, h2_ref, route_ref, route_t_ref, tile_ref,
                        logits_scr):
    step = pl.program_id(0)

    @pl.when(step == 0)
    def _():
        logits_scr[...] = jnp.zeros_like(logits_scr)

    logits = logits_scr[...]

    y_ret = jnp.dot(ret_ref[...], wr_ref[...], preferred_element_type=F32)
    y_m = jnp.dot(hm_ref[...], wm_ref[...], preferred_element_type=F32)
    merged = (jax.nn.sigmoid(gr_ref[...].astype(F32)) * y_ret
              + jax.nn.sigmoid(gm_ref[...].astype(F32)) * y_m)
    x1 = x_ref[...] + jnp.dot(merged.astype(BF16), wo_ref[...], preferred_element_type=F32)
    x1_ref[...] = x1
    h2 = _rms(x1) * nw_ref[...]
    h2_ref[...] = _pack_rows(h2)
    new_logits = jnp.dot(h2.astype(BF16), wrt_ref[...], preferred_element_type=F32) + brt_ref[...]

    live = jnp.where(step > 0, 1.0, 0.0)
    tm = logits.shape[0]
    lane = lax.broadcasted_iota(jnp.int32, (tm, LANES), 1)
    neg = -jnp.inf
    big = jnp.int32(LANES)
    is_group = (lane >= N_EXPERTS) & (lane < N_EXPERTS + N_GROUPS)
    gl = jnp.where(is_group, logits, neg)
    g_max = jnp.max(gl, axis=-1, keepdims=True)
    g_idx = jnp.min(jnp.where(gl == g_max, lane, big), axis=-1, keepdims=True) - N_EXPERTS
    g_w = 1.0 / jnp.sum(jnp.exp(gl - g_max), axis=-1, keepdims=True)
    in_group = (lane >= g_idx * EXPERTS_PER_GROUP) & (lane < (g_idx + 1) * EXPERTS_PER_GROUP)
    el = jnp.where(in_group, logits, neg)
    l1 = jnp.max(el, axis=-1, keepdims=True)
    e1 = jnp.min(jnp.where(el == l1, lane, big), axis=-1, keepdims=True)
    el2 = jnp.where(lane == e1, neg, el)
    l2 = jnp.max(el2, axis=-1, keepdims=True)
    e2 = jnp.min(jnp.where(el2 == l2, lane, big), axis=-1, keepdims=True)
    t21 = jnp.exp(l2 - l1)
    w1 = g_w / (1.0 + t21)
    w2 = g_w * t21 / (1.0 + t21)

    hit1 = lane == e1
    hit2 = lane == e2
    cnt = jnp.where(hit1 | hit2, live, 0.0)
    before = jnp.dot(lower_ref[...], cnt.astype(BF16), preferred_element_type=F32)
    count = jnp.sum(cnt, axis=0, keepdims=True)
    run = jnp.floor((count + (SUBLANES - 1)) * (1.0 / SUBLANES)) * SUBLANES
    lane1 = lax.broadcasted_iota(jnp.int32, (1, LANES), 1)
    run_end = run
    shift = 1
    while shift < N_EXPERTS:
        run_end = run_end + jnp.where(lane1 >= shift, pltpu.roll(run_end, shift, 1), 0.0)
        shift *= 2
    run_start = run_end - run
    local = before + run_start
    r1 = jnp.sum(jnp.where(hit1, local, 0.0), axis=-1, keepdims=True)
    r2 = jnp.sum(jnp.where(hit2, local, 0.0), axis=-1, keepdims=True)
    sub8 = lax.broadcasted_iota(jnp.int32, (SUBLANES, LANES), 0)
    tile_ref[...] = jnp.where(sub8 == 0, count, jnp.where(sub8 == 1, run, jnp.where(sub8 == 2, run_start, 0.0)))

    fields = (e1.astype(F32), e2.astype(F32), r1, r2, w1, w2)
    packed = jnp.zeros((tm, LANES), F32)
    for idx, val in enumerate(fields):
        packed = jnp.where(lane == idx, val, packed)
    route_ref[...] = packed
    route_t_ref[...] = jnp.transpose(packed)[0:ROUTE_ROWS, :]
    logits_scr[...] = new_logits


def _merge_route(ret, hm, proj, x2d, w_ret, w_m, w_out, norm_w, w_router, b_router, lower):
    t = x2d.shape[0]
    tm = min(ROWS_MERGE, t)
    n_tiles = t // tm
    gate_r_blk = 2 * MIX_COLS // D_MODEL
    tile = lambda i: jnp.minimum(i, n_tiles - 1)
    routed = lambda i: jnp.maximum(i - 1, 0)
    row_blk = lambda i: (tile(i), 0)
    const = lambda i: (0, 0)
    return pl.pallas_call(
        _merge_route_kernel,
        grid=(n_tiles + 1,),
        in_specs=[
            pl.BlockSpec((tm, V_ALL), row_blk),
            pl.BlockSpec((tm, V_ALL), row_blk),
            pl.BlockSpec((tm, D_MODEL), lambda i: (tile(i), gate_r_blk)),
            pl.BlockSpec((tm, D_MODEL), lambda i: (tile(i), gate_r_blk + 1)),
            pl.BlockSpec((tm, D_MODEL), row_blk),
            pl.BlockSpec((V_ALL, D_MODEL), const),
            pl.BlockSpec((V_ALL, D_MODEL), const),
            pl.BlockSpec((D_MODEL, D_MODEL), const),
            pl.BlockSpec((1, D_MODEL), const),
            pl.BlockSpec((D_MODEL, LANES), const),
            pl.BlockSpec((1, LANES), const),
            pl.BlockSpec((tm, tm), const),
        ],
        out_specs=[
            pl.BlockSpec((tm, D_MODEL), row_blk),
            pl.BlockSpec((tm, PACKED), row_blk),
            pl.BlockSpec((tm, LANES), lambda i: (routed(i), 0)),
            pl.BlockSpec((ROUTE_ROWS, tm), lambda i: (0, routed(i))),
            pl.BlockSpec((None, SUBLANES, LANES), lambda i: (routed(i), 0, 0)),
        ],
        out_shape=[
            jax.ShapeDtypeStruct((t, D_MODEL), F32),
            jax.ShapeDtypeStruct((t, PACKED), U32),
            jax.ShapeDtypeStruct((t, LANES), F32),
            jax.ShapeDtypeStruct((ROUTE_ROWS, t), F32),
            jax.ShapeDtypeStruct((n_tiles, SUBLANES, LANES), F32),
        ],
        scratch_shapes=[pltpu.VMEM((tm, LANES), F32)],
        compiler_params=pltpu.CompilerParams(
            dimension_semantics=("arbitrary",), vmem_limit_bytes=VMEM_LIMIT),
        name="merge_route",
    )(ret, hm, proj, proj, x2d, w_ret, w_m, w_out, norm_w, w_router, b_router, lower)


def _dispatch_kernel(fill_ref, n_groups_ref, gdst_ref, route_t_ref, h2_ref, xs_hbm, sorted_scr, zero_scr, sems):
    step = pl.program_id(0)
    n_tiles = pl.num_programs(0) - 1
    tm = h2_ref.shape[0]
    slot = step % 2

    def group_copy(which, j):
        return pltpu.make_async_copy(sorted_scr.at[which, pl.ds(j, 1)], xs_hbm.at[pl.ds(gdst_ref[j], 1)],
                                     sems.at[which])

    def wait_groups(which, count):
        for bit in range(LOCAL_GROUPS.bit_length()):
            piece = 1 << bit

            @pl.when((lax.shift_right_logical(count, bit) & 1) == 1)
            def _():
                pltpu.make_async_copy(sorted_scr.at[which, pl.ds(0, piece)], xs_hbm.at[pl.ds(0, piece)],
                                      sems.at[which]).wait()

    @pl.when(step == 0)
    def _():
        zero_scr[...] = jnp.zeros_like(zero_scr)

        def fill_copy(j):
            g0 = pl.multiple_of(fill_ref[j], FILL_ROWS // SUBLANES)
            return pltpu.make_async_copy(zero_scr, xs_hbm.at[pl.ds(g0, FILL_ROWS // SUBLANES)], sems.at[0])

        def start_fill(j, carry):
            @pl.when(fill_ref[j] >= 0)
            def _():
                fill_copy(j).start()
            return carry

        def wait_fill(j, carry):
            @pl.when(fill_ref[j] >= 0)
            def _():
                fill_copy(j).wait()
            return carry

        lax.fori_loop(0, fill_ref.shape[0], start_fill, 0)
        lax.fori_loop(0, fill_ref.shape[0], wait_fill, 0)

    @pl.when(step >= 1)
    def _():
        def start(j, carry):
            group_copy(1 - slot, j).start()
            return carry

        def start_four(q, carry):
            for u in range(4):
                group_copy(1 - slot, q * 4 + u).start()
            return carry
        count = n_groups_ref[jnp.maximum(step - 1, 0)]
        fours = lax.shift_right_logical(count, 2)
        lax.fori_loop(0, fours, start_four, 0)
        lax.fori_loop(fours * 4, count, start, 0)

    @pl.when(step >= 2)
    def _():
        wait_groups(slot, n_groups_ref[jnp.maximum(step - 2, 0)])

    @pl.when(step < n_tiles)
    def _():
        pos = lax.broadcasted_iota(jnp.int32, (LOCAL_ROWS, tm), 0)
        row1 = route_t_ref[2:3, :].astype(jnp.int32)
        row2 = route_t_ref[3:4, :].astype(jnp.int32)
        pick = jnp.where((pos == row1) | (pos == row2), 1.0, 0.0).astype(BF16)
        tokens = _unpack_rows(h2_ref[...]).astype(BF16)
        ordered = jnp.dot(pick, tokens, preferred_element_type=F32)
        sorted_scr[slot] = _pack_rows(ordered).reshape(LOCAL_GROUPS, SUBLANES, PACKED)

    @pl.when(step == n_tiles)
    def _():
        wait_groups(1 - slot, n_groups_ref[jnp.maximum(step - 1, 0)])


def _dispatch(fill_groups, n_groups, gdst, route_t, h2p, n_slots):
    t = h2p.shape[0]
    tm = min(ROWS_MERGE, t)
    n_tiles = t // tm
    sorted_tile = lambda i: jnp.minimum(i, n_tiles - 1)
    return pl.pallas_call(
        _dispatch_kernel,
        grid=(n_tiles + 1,),
        in_specs=[
            pl.BlockSpec(memory_space=pltpu.SMEM),
            pl.BlockSpec(memory_space=pltpu.SMEM),
            pl.BlockSpec((GROUP_TABLE,), lambda i: (jnp.maximum(i - 1, 0),), memory_space=pltpu.SMEM),
            pl.BlockSpec((ROUTE_ROWS, tm), lambda i: (0, sorted_tile(i))),
            pl.BlockSpec((tm, PACKED), lambda i: (sorted_tile(i), 0)),
        ],
        out_specs=pl.BlockSpec(memory_space=pl.ANY),
        out_shape=jax.ShapeDtypeStruct((n_slots // SUBLANES, SUBLANES, PACKED), U32),
        scratch_shapes=[
            pltpu.VMEM((2, LOCAL_GROUPS, SUBLANES, PACKED), U32),
            pltpu.VMEM((FILL_ROWS // SUBLANES, SUBLANES, PACKED), U32),
            pltpu.SemaphoreType.DMA((2,)),
        ],
        compiler_params=pltpu.CompilerParams(
            dimension_semantics=("arbitrary",), vmem_limit_bytes=VMEM_LIMIT),
        name="dispatch",
    )(fill_groups, n_groups, gdst, route_t, h2p)


def _experts_kernel(blk_e_ref, blk_valid_ref, n_used_ref, xs_ref, wg_ref, wu_ref, wd_ref, ys_ref, wgu_scr, wd_scr):
    del n_used_ref
    i = pl.program_id(0)
    valid = blk_valid_ref[i]

    @pl.when((i == 0) | (blk_e_ref[i] != blk_e_ref[jnp.maximum(i - 1, 0)]))
    def _():
        wgu_scr[:, :D_EXPERT] = wg_ref[...].astype(BF16)
        wgu_scr[:, D_EXPERT:] = wu_ref[...].astype(BF16)
        wd_scr[...] = wd_ref[...].astype(BF16)

    @pl.when(valid > 0)
    def _():
        xb = _unpack_rows(xs_ref[...]).astype(BF16)
        gu = jnp.dot(xb, wgu_scr[...], preferred_element_type=F32)
        g = gu[:, :D_EXPERT]
        act = (g * jax.nn.sigmoid(g) * gu[:, D_EXPERT:]).astype(BF16)
        ys_ref[...] = _pack_rows(jnp.dot(act, wd_scr[...], preferred_element_type=F32))

    @pl.when(valid <= 0)
    def _():
        ys_ref[...] = jnp.zeros_like(ys_ref)


def _experts(blk_e, blk_valid, n_used, xs, w_gate, w_up, w_down):
    p = xs.shape[0]
    per_expert = lambda i, be, bv, nu: (be[i], 0, 0)
    grid_spec = pltpu.PrefetchScalarGridSpec(
        num_scalar_prefetch=3,
        grid=(p // MOE_ROWS,),
        in_specs=[
            pl.BlockSpec((MOE_ROWS, PACKED), lambda i, be, bv, nu: (jnp.minimum(i, nu[0] - 1), 0)),
            pl.BlockSpec((None, D_MODEL, D_EXPERT), per_expert),
            pl.BlockSpec((None, D_MODEL, D_EXPERT), per_expert),
            pl.BlockSpec((None, D_EXPERT, D_MODEL), per_expert),
        ],
        out_specs=pl.BlockSpec((MOE_ROWS, PACKED), lambda i, be, bv, nu: (i, 0)),
        scratch_shapes=[pltpu.VMEM((D_MODEL, 2 * D_EXPERT), BF16), pltpu.VMEM((D_EXPERT, D_MODEL), BF16)],
    )
    return pl.pallas_call(
        _experts_kernel,
        grid_spec=grid_spec,
        out_shape=jax.ShapeDtypeStruct((p, PACKED), U32),
        compiler_params=pltpu.CompilerParams(
            dimension_semantics=("arbitrary",), vmem_limit_bytes=VMEM_LIMIT),
        name="experts",
    )(blk_e, blk_valid, n_used, xs, w_gate, w_up, w_down)


def _combine_kernel(gdst_ref, route_ref, x1_ref, nw_ref, ys_hbm, o_ref, buf, sems):
    step = pl.program_id(0)
    n_tiles = pl.num_programs(0) - 1
    tm = x1_ref.shape[0]
    slot = step % 2
    prev = 1 - slot

    @pl.when(step < n_tiles)
    def _():
        def start(j, carry):
            pltpu.make_async_copy(ys_hbm.at[pl.ds(gdst_ref[j], 1)], buf.at[slot, pl.ds(j, 1)], sems.at[slot]).start()
            return carry
        lax.fori_loop(0, LOCAL_GROUPS, start, 0, unroll=8)

    @pl.when(step > 0)
    def _():
        pltpu.make_async_copy(ys_hbm.at[pl.ds(0, LOCAL_GROUPS)], buf.at[prev], sems.at[prev]).wait()

        rows = _unpack_rows(buf[prev].reshape(LOCAL_ROWS, PACKED)).astype(BF16)
        pos = lax.broadcasted_iota(jnp.int32, (tm, LOCAL_ROWS), 1)
        row1 = route_ref[:, 2:3].astype(jnp.int32)
        row2 = route_ref[:, 3:4].astype(jnp.int32)
        mix = (jnp.where(pos == row1, route_ref[:, 4:5], 0.0)
               + jnp.where(pos == row2, route_ref[:, 5:6], 0.0)).astype(BF16)
        x2 = x1_ref[...] + jnp.dot(mix, rows, preferred_element_type=F32)
        o_ref[...] = _rms(x2) * nw_ref[...]


def _combine(gdst, route, x1, norm_w, ys3):
    t = x1.shape[0]
    tm = min(ROWS_MERGE, t)
    n_tiles = t // tm
    gathered = lambda i: (jnp.minimum(i, n_tiles - 1),)
    finished = lambda i: (jnp.maximum(i - 1, 0), 0)
    return pl.pallas_call(
        _combine_kernel,
        grid=(n_tiles + 1,),
        in_specs=[
            pl.BlockSpec((GROUP_TABLE,), gathered, memory_space=pltpu.SMEM),
            pl.BlockSpec((tm, LANES), finished),
            pl.BlockSpec((tm, D_MODEL), finished),
            pl.BlockSpec((1, D_MODEL), lambda i: (0, 0)),
            pl.BlockSpec(memory_space=pl.ANY),
        ],
        out_specs=pl.BlockSpec((tm, D_MODEL), finished),
        out_shape=jax.ShapeDtypeStruct((t, D_MODEL), F32),
        scratch_shapes=[
            pltpu.VMEM((2, LOCAL_GROUPS, SUBLANES, PACKED), U32),
            pltpu.SemaphoreType.DMA((2,)),
        ],
        compiler_params=pltpu.CompilerParams(
            dimension_semantics=("arbitrary",), vmem_limit_bytes=VMEM_LIMIT),
        name="combine",
    )(gdst, route, x1, norm_w, ys3)


def _rotary_tables(seq):
    inv_freq = 1.0 / (ROPE_BASE ** (jnp.arange(0, QK_DIM, 2, dtype=F32) / QK_DIM))
    ang = jnp.arange(seq, dtype=F32)[:, None] * inv_freq[None, :]
    cos, sin = jnp.cos(ang), jnp.sin(ang)
    return jnp.concatenate([cos, cos], axis=1), jnp.concatenate([-sin, sin], axis=1)


def _retention_decay_tables():
    gamma = 1.0 - 2.0 ** (-5.0 - np.arange(N_HEADS, dtype=np.float64))
    idx = np.arange(CHUNK, dtype=np.float64) + 1.0
    dq = gamma[:, None] ** idx[None, :]
    dk = gamma[:, None] ** (-idx[None, :]) * QK_DIM ** -0.5
    bcast = lambda a: jnp.asarray(np.broadcast_to(a[:, :, None], (N_HEADS, CHUNK, QK_DIM)), F32)
    return bcast(dq), bcast(dk), tuple(float(g) for g in gamma ** CHUNK)


def kernel(x, norm_mix_w, w_in, ret_gn_w, w_ret_branch, mlstm_conv_w, mlstm_conv_b, b_igate, b_fgate,
           mlstm_gn_w, w_mlstm_branch, w_out, norm_ffn_w, w_group, b_group, w_expert_router,
           b_expert_router, w_gate, w_up, w_down, norm_final_w):
    assert norm_mix_w.shape[0] == 1, "one layer"
    b, s, d = x.shape
    t = b * s
    assert d == D_MODEL and s % CHUNK == 0

    wi = w_in[0]
    n_pre = 2 * MIX_COLS
    w_big = jnp.concatenate([wi[:, :n_pre], wi[:, n_pre + 2 * N_HEADS:]], axis=1).astype(BF16)
    w_if = jnp.pad(wi[:, n_pre:n_pre + 2 * N_HEADS], ((0, 0), (0, LANES - 2 * N_HEADS))).astype(BF16)
    gate_bias = jnp.pad(jnp.concatenate([b_igate[0], b_fgate[0]]), (0, LANES - 2 * N_HEADS))[None, :]
    w_router = jnp.pad(jnp.concatenate([w_expert_router[0], w_group[0]], axis=1),
                       ((0, 0), (0, LANES - N_EXPERTS - N_GROUPS))).astype(BF16)
    b_router = jnp.pad(jnp.concatenate([b_expert_router[0], b_group[0]]),
                       (0, LANES - N_EXPERTS - N_GROUPS))[None, :]

    cosf, sinf = _rotary_tables(s)
    dq, dk, chunk_decay = _retention_decay_tables()
    tm_merge = min(ROWS_MERGE, t)
    lower = jnp.tril(jnp.ones((tm_merge, tm_merge), F32), -1).astype(BF16)

    x2d = x.reshape(t, d)
    proj, gates = _in_projection(x2d, norm_mix_w, w_big, w_if)
    proj3 = proj.reshape(b, s, N_BIG)
    ret = _retention(proj3, cosf, sinf, dq, dk, ret_gn_w, chunk_decay)
    hm = _mlstm(proj3, gates.reshape(b, s, LANES), mlstm_conv_w[0, :, 0, :], mlstm_conv_b,
                gate_bias, mlstm_gn_w)
    x1, h2p, route, route_t, tiles = _merge_route(
        ret.reshape(t, V_ALL), hm.reshape(t, V_ALL), proj, x2d, w_ret_branch[0].astype(BF16),
        w_mlstm_branch[0].astype(BF16), w_out[0].astype(BF16), norm_ffn_w, w_router, b_router, lower)

    n_tiles = t // tm_merge
    n_slots = 2 * t + N_EXPERTS * (MOE_ROWS + n_tiles * (SUBLANES - 1) // SUBLANES * SUBLANES)
    n_slots = -(-n_slots // MOE_ROWS) * MOE_ROWS
    run = tiles[:, 1, :N_EXPERTS].astype(jnp.int32)
    run_start = tiles[:, 2, :N_EXPERTS].astype(jnp.int32)
    rows_e = jnp.sum(run, axis=0)
    padded = (rows_e + MOE_ROWS - 1) // MOE_ROWS * MOE_ROWS
    expert_ids = jnp.arange(N_EXPERTS, dtype=jnp.int32)
    pstart = jnp.sum(jnp.where(expert_ids[None, :] < expert_ids[:, None], padded[None, :], 0), axis=1)
    pend = pstart + padded
    tile_ids = jnp.arange(n_tiles, dtype=jnp.int32)
    run_t = run.T
    earlier = jnp.sum(jnp.where((tile_ids[None, :] < tile_ids[:, None])[None], run_t[:, None, :], 0), axis=2).T
    global_start = pstart[None, :] + earlier
    local_row = jnp.arange(GROUP_TABLE, dtype=jnp.int32)[None, None, :] * SUBLANES
    in_run = (local_row >= run_start[:, :, None]) & (local_row < (run_start + run)[:, :, None])
    gdst = jnp.sum(jnp.where(in_run, (global_start - run_start)[:, :, None] + local_row, 0), axis=1) // SUBLANES
    gdst = gdst.reshape(n_tiles * GROUP_TABLE).astype(jnp.int32)
    n_groups = (jnp.sum(run, axis=1) // SUBLANES).astype(jnp.int32)
    blk_start = jnp.arange(n_slots // MOE_ROWS, dtype=jnp.int32) * MOE_ROWS
    blk_e = jnp.minimum(jnp.sum(blk_start[:, None] >= pend[None, :], axis=-1), N_EXPERTS - 1).astype(jnp.int32)
    in_expert = (blk_start[:, None] >= pstart[None, :]) & (blk_start[:, None] < pend[None, :])
    rows_end = jnp.sum(jnp.where(in_expert, (pstart + rows_e)[None, :], 0), axis=1)
    blk_valid = jnp.clip(rows_end - blk_start, 0, MOE_ROWS).astype(jnp.int32)
    first = (pstart + rows_e) // FILL_ROWS * FILL_ROWS
    tail = pend[-1] + jnp.arange((n_slots - 2 * t) // FILL_ROWS, dtype=jnp.int32) * FILL_ROWS
    fill_groups = (jnp.concatenate([jnp.where(first < pend, first, -SUBLANES),
                                    jnp.where(first + FILL_ROWS < pend, first + FILL_ROWS, -SUBLANES),
                                    jnp.where(tail < n_slots, tail, -SUBLANES)]) // SUBLANES).astype(jnp.int32)

    xs3 = _dispatch(fill_groups, n_groups, gdst, route_t, h2p, n_slots)
    n_used = (pend[-1:] // MOE_ROWS).astype(jnp.int32)
    ys = _experts(blk_e, blk_valid, n_used, xs3.reshape(n_slots, PACKED), w_gate[0], w_up[0], w_down[0])
    out = _combine(gdst, route, x1, norm_final_w[None, :],
                   ys.reshape(n_slots // SUBLANES, SUBLANES, PACKED))
    return out.reshape(b, s, d)
```
